```python
import jax, jax.numpy as jnp
from jax import lax
import numpy as np

D_MODEL = 1024
BATCH = 4
SEQ = 4096
DEPTH = 1
DEC_BATCH = 32
DEC_SEQ = 1
PAST_LEN = 16384
PAGE_SIZE = 128

HEAD_DIM = 64
ROPE_DIM = HEAD_DIM // 4
ROPE_THETA = 500000.0
DSWA_GROUPS = ((128, 1), (512, 4), (2048, 16))
HEADS_PER_GROUP = 4
N_GROUPS = len(DSWA_GROUPS)
A_HEADS = N_GROUPS * HEADS_PER_GROUP
A_QKV = A_HEADS * HEAD_DIM
A_OUT = HEADS_PER_GROUP * HEAD_DIM
BAND = 128
M_HEADS = 4
M_INNER = D_MODEL
M_DK = M_INNER // M_HEADS
M_DV = M_INNER // M_HEADS
CONV_W = 4
M_CHUNK = 64
D_FF = -(-(8 * D_MODEL) // (3 * 256)) * 256
PLE_DIM = 256
EPS = 1e-6
NEG = -1e30

OFF_AQ = 0
OFF_AK = OFF_AQ + A_QKV
OFF_AV = OFF_AK + A_QKV
OFF_MQ = OFF_AV + A_QKV
OFF_MK = OFF_MQ + M_INNER
OFF_MV = OFF_MK + M_INNER
OFF_MO = OFF_MV + M_INNER
OFF_MI = OFF_MO + M_INNER
OFF_MF = OFF_MI + M_HEADS
OFF_GA = OFF_MF + M_HEADS
OFF_GB = OFF_GA + D_MODEL
IN_COLS = OFF_GB + D_MODEL

kernel_name = 'dilated_swa_mlstm_hybrid_step'


def _rmsnorm(x, g):
    xf = x.astype(jnp.float32)
    y = xf * lax.rsqrt(jnp.mean(xf * xf, axis=-1, keepdims=True) + EPS)
    return (y * g.astype(jnp.float32)).astype(x.dtype)


def _rope(x, pos):
    half = ROPE_DIM // 2
    inv = jnp.power(ROPE_THETA, -jnp.arange(half, dtype=jnp.float32) / half)
    ang = pos.astype(jnp.float32)[:, None] * inv[None, :]
    cos = jnp.cos(ang)[None, :, None, :]
    sin = jnp.sin(ang)[None, :, None, :]
    xf = x.astype(jnp.float32)
    x1, x2, rest = xf[..., :half], xf[..., half:ROPE_DIM], xf[..., ROPE_DIM:]
    out = jnp.concatenate([x1 * cos - x2 * sin, x2 * cos + x1 * sin, rest], axis=-1)
    return out.astype(x.dtype)


def _dilated_attn_prompt(q, k, v, dil):
    B, S, H, D = q.shape
    span = dil * BAND
    s_pad = -(-S // span) * span
    nsub = s_pad // dil
    nb = nsub // BAND

    def to_blocks(t):
        t = jnp.pad(t.astype(jnp.float32), ((0, 0), (0, s_pad - S), (0, 0), (0, 0)))
        t = t.reshape(B, nsub, dil, H, D).transpose(0, 2, 1, 3, 4)
        return t.reshape(B, dil, nb, BAND, H, D)

    def with_prev(t):
        prev = jnp.pad(t, ((0, 0), (0, 0), (1, 0), (0, 0), (0, 0), (0, 0)))[:, :, :-1]
        return jnp.concatenate([prev, t], axis=3)

    qb, kb, vb = to_blocks(q), to_blocks(k), to_blocks(v)
    kk, vv = with_prev(kb), with_prev(vb)
    s = jnp.einsum('brnqhd,brnkhd->brnhqk', qb, kk) * (HEAD_DIM ** -0.5)
    qi = jnp.arange(BAND)[:, None]
    kj = jnp.arange(2 * BAND)[None, :]
    dist = BAND + qi - kj
    blk = jnp.arange(nb)[:, None, None]
    valid = (dist >= 0) & (dist <= BAND) & ((blk > 0) | (kj >= BAND))
    s = jnp.where(valid[None, None, :, None], s, -jnp.inf)
    mx = jnp.max(s, axis=-1, keepdims=True)
    e = jnp.exp(s - mx)
    den = jnp.sum(e, axis=-1, keepdims=True)
    o = jnp.einsum('brnhqk,brnkhd->brnqhd', e / den, vv)
    lse = (mx + jnp.log(den))[..., 0]
    o = o.reshape(B, dil, nsub, H, D).transpose(0, 2, 1, 3, 4).reshape(B, s_pad, H, D)[:, :S]
    lse = lse.transpose(0, 1, 2, 4, 3).reshape(B, dil, nsub, H).transpose(0, 2, 1, 3).reshape(B, s_pad, H)[:, :S]
    return o, lse


def _dilated_attn_gather(q, k_ext, v_ext, dil, past):
    B, T, H, D = q.shape
    idx = past + jnp.arange(T)[:, None] - dil * jnp.arange(BAND + 1)[None, :]
    valid = idx >= 0
    flat = jnp.maximum(idx, 0).reshape(-1)
    kg = jnp.take(k_ext, flat, axis=1).reshape(B, T, BAND + 1, H, D).astype(jnp.float32)
    vg = jnp.take(v_ext, flat, axis=1).reshape(B, T, BAND + 1, H, D).astype(jnp.float32)
    s = jnp.einsum('bthd,btkhd->bthk', q.astype(jnp.float32), kg) * (HEAD_DIM ** -0.5)
    s = jnp.where(valid[None, :, None, :], s, -jnp.inf)
    mx = jnp.max(s, axis=-1, keepdims=True)
    e = jnp.exp(s - mx)
    den = jnp.sum(e, axis=-1, keepdims=True)
    o = jnp.einsum('bthk,btkhd->bthd', e / den, vg)
    lse = (mx + jnp.log(den))[..., 0]
    return o, lse


def _causal_conv(x, buf, w, b):
    T = x.shape[1]
    xp = jnp.concatenate([buf.astype(x.dtype), x], axis=1)
    y = b + w[0] * xp[:, 0:T]
    for j in range(1, CONV_W):
        y = y + w[j] * xp[:, j:j + T]
    return y, xp[:, T:]


def _mlstm(q, k, v, ig, lf, C0, n0, m0):
    B, T, H, DK = q.shape
    DV = v.shape[-1]
    L = min(M_CHUNK, T)
    nc = -(-T // L)
    pad = nc * L - T
    f32 = jnp.float32

    def chunks(t, fill):
        t = jnp.pad(t.astype(f32), ((0, 0), (0, pad)) + ((0, 0),) * (t.ndim - 2), constant_values=fill)
        t = t.reshape((B, nc, L) + t.shape[2:])
        return jnp.moveaxis(jnp.moveaxis(t, 1, 0), 2, 3)

    xs = (chunks(q, 0.0), chunks(k, 0.0), chunks(v, 0.0), chunks(ig, NEG), chunks(lf, 0.0))
    tril = jnp.tril(jnp.ones((L, L), dtype=bool))

    def step(carry, xc):
        C, n, m = carry
        qc, kc, vc, ic, fc = xc
        b = jnp.cumsum(fc, axis=-1)
        dmat = jnp.where(tril, b[..., :, None] - b[..., None, :] + ic[..., None, :], -jnp.inf)
        m_inter = b + m[..., None]
        m_t = jnp.maximum(m_inter, jnp.max(dmat, axis=-1))
        w_intra = jnp.einsum('bhtd,bhsd->bhts', qc, kc) * jnp.exp(dmat - m_t[..., None])
        w_inter = jnp.exp(m_inter - m_t)
        num = jnp.einsum('bhts,bhsv->bhtv', w_intra, vc) + w_inter[..., None] * jnp.einsum('bhtd,bhdv->bhtv', qc, C)
        den = jnp.sum(w_intra, axis=-1) + w_inter * jnp.einsum('bhtd,bhd->bht', qc, n)
        h = num / jnp.maximum(jnp.abs(den), jnp.exp(-m_t))[..., None]
        b_last = b[..., -1]
        g = b_last[..., None] - b + ic
        m_new = jnp.maximum(b_last + m, jnp.max(g, axis=-1))
        wk = jnp.exp(g - m_new[..., None])
        decay = jnp.exp(b_last + m - m_new)
        C_new = decay[..., None, None] * C + jnp.einsum('bhs,bhsd,bhsv->bhdv', wk, kc, vc)
        n_new = decay[..., None] * n + jnp.einsum('bhs,bhsd->bhd', wk, kc)
        return (C_new, n_new, m_new), h

    (C1, n1, m1), hs = lax.scan(step, (C0.astype(f32), n0.astype(f32), m0.astype(f32)), xs)
    h = jnp.moveaxis(hs, 0, 1)
    h = jnp.swapaxes(h, 2, 3).reshape(B, nc * L, H, DV)[:, :T]
    return h, C1, n1, m1


def _layer(h, p, pos0, kv_bufs, conv_buf, C0, n0, m0, norm_mix, w_in, conv_w, conv_b, b_igate, b_fgate, mh_norm, w_proj_a, w_proj_b, w_out, norm_ffn, w_gate, w_up, w_down, norm_ple, w_ple_gate, w_ple_proj):
    B, T, _ = h.shape
    f32 = jnp.float32
    xn = _rmsnorm(h, norm_mix)
    z = xn @ w_in
    pos = pos0 + jnp.arange(T, dtype=jnp.int32)

    qa = _rope(z[..., OFF_AQ:OFF_AK].reshape(B, T, A_HEADS, HEAD_DIM), pos)
    ka = _rope(z[..., OFF_AK:OFF_AV].reshape(B, T, A_HEADS, HEAD_DIM), pos)
    va = z[..., OFF_AV:OFF_MQ].reshape(B, T, A_HEADS, HEAD_DIM)
    outs, lses, new_kv = [], [], []
    for g, (win, dil) in enumerate(DSWA_GROUPS):
        sl = slice(g * HEADS_PER_GROUP, (g + 1) * HEADS_PER_GROUP)
        kv_new = jnp.stack([ka[:, :, sl], va[:, :, sl]], axis=2)
        if kv_bufs is None:
            o, lse = _dilated_attn_prompt(qa[:, :, sl], ka[:, :, sl], va[:, :, sl], dil)
            ext = kv_new
        else:
            past = kv_bufs[g].shape[1]
            ext = jnp.concatenate([kv_bufs[g].astype(kv_new.dtype), kv_new], axis=1)
            o, lse = _dilated_attn_gather(qa[:, :, sl], ext[:, :, 0], ext[:, :, 1], dil, past)
        outs.append(o)
        lses.append(lse)
        keep = min(win, ext.shape[1])
        new_kv.append(ext[:, ext.shape[1] - keep:])
    wgt = jax.nn.softmax(jnp.stack(lses, axis=0), axis=0)
    o_a = jnp.einsum('gbth,gbthd->bthd', wgt, jnp.stack(outs, axis=0))
    o_a = o_a.reshape(B, T, A_OUT).astype(h.dtype)

    qk, conv_new = _causal_conv(z[..., OFF_MQ:OFF_MV], conv_buf, conv_w, conv_b)
    qk = jax.nn.silu(qk)
    qm = qk[..., :M_INNER].reshape(B, T, M_HEADS, M_DK)
    km = qk[..., M_INNER:].reshape(B, T, M_HEADS, M_DK) * (M_DK ** -0.5)
    vm = z[..., OFF_MV:OFF_MO].reshape(B, T, M_HEADS, M_DV)
    og = jax.nn.sigmoid(z[..., OFF_MO:OFF_MI])
    ig = z[..., OFF_MI:OFF_MF].astype(f32) + b_igate.astype(f32)
    lf = jax.nn.log_sigmoid(z[..., OFF_MF:OFF_GA].astype(f32) + b_fgate.astype(f32))
    hm, C1, n1, m1 = _mlstm(qm, km, vm, ig, lf, C0, n0, m0)
    hm = hm * lax.rsqrt(jnp.mean(hm * hm, axis=-1, keepdims=True) + EPS) * mh_norm.astype(f32)
    o_b = hm.reshape(B, T, M_INNER).astype(h.dtype) * og

    ga = jax.nn.sigmoid(z[..., OFF_GA:OFF_GB])
    gb = jax.nn.sigmoid(z[..., OFF_GB:IN_COLS])
    mix = ga * (o_a @ w_proj_a) + gb * (o_b @ w_proj_b)
    h = h + mix @ w_out

    xf = _rmsnorm(h, norm_ffn)
    h = h + (jax.nn.silu(xf @ w_gate) * (xf @ w_up)) @ w_down

    xp = _rmsnorm(h, norm_ple)
    h = h + jax.nn.sigmoid(xp @ w_ple_gate) * (p.astype(h.dtype) @ w_ple_proj)
    return h, (new_kv[0], new_kv[1], new_kv[2], conv_new, C1.astype(C0.dtype), n1.astype(n0.dtype), m1.astype(m0.dtype))


def setup_inputs(seed: int = 0) -> dict:
    key = jax.random.key(seed)
    k = jax.random.split(key, 29)
    f32 = jnp.float32

    def nrm(i, shape, scale=1.0):
        return jax.random.normal(k[i], shape, f32) * scale

    def gain(i, shape):
        return 1.0 + 0.02 * jax.random.normal(k[i], shape, f32)

    win_len = [min(w, PAST_LEN) for w, _ in DSWA_GROUPS]

    def kv_shape(n):
        return (DEPTH, DEC_BATCH, n, 2, HEADS_PER_GROUP, HEAD_DIM)

    return {
        'x_prompt': nrm(0, (BATCH, SEQ, D_MODEL)),
        'x_sample': nrm(1, (DEC_BATCH, DEC_SEQ, D_MODEL)),
        'cache_kv_w128': nrm(2, kv_shape(win_len[0])),
        'cache_kv_w512': nrm(3, kv_shape(win_len[1])),
        'cache_kv_w2048': nrm(4, kv_shape(win_len[2])),
        'state_conv': nrm(5, (DEPTH, DEC_BATCH, CONV_W - 1, 2 * M_INNER)),
        'state_C': nrm(6, (DEPTH, DEC_BATCH, M_HEADS, M_DK, M_DV), M_DK ** -0.5),
        'state_n': nrm(7, (DEPTH, DEC_BATCH, M_HEADS, M_DK), M_DK ** -0.5),
        'state_m': nrm(8, (DEPTH, DEC_BATCH, M_HEADS)),
        'p_prompt': nrm(9, (DEPTH, BATCH, SEQ, PLE_DIM)),
        'p_sample': nrm(10, (DEPTH, DEC_BATCH, DEC_SEQ, PLE_DIM)),
        'norm_mix': gain(11, (DEPTH, D_MODEL)),
        'w_in': nrm(12, (DEPTH, D_MODEL, IN_COLS), D_MODEL ** -0.5),
        'conv_w': nrm(13, (DEPTH, CONV_W, 2 * M_INNER), CONV_W ** -0.5),
        'conv_b': nrm(14, (DEPTH, 2 * M_INNER), 0.02),
        'b_igate': nrm(15, (DEPTH, M_HEADS), 0.1),
        'b_fgate': jnp.linspace(3.0, 6.0, M_HEADS, dtype=f32)[None, :] + nrm(16, (DEPTH, M_HEADS), 0.1),
        'mh_norm': gain(17, (DEPTH, M_HEADS, M_DV)),
        'w_proj_a': nrm(18, (DEPTH, A_OUT, D_MODEL), A_OUT ** -0.5),
        'w_proj_b': nrm(19, (DEPTH, M_INNER, D_MODEL), M_INNER ** -0.5),
        'w_out': nrm(20, (DEPTH, D_MODEL, D_MODEL), D_MODEL ** -0.5),
        'norm_ffn': gain(21, (DEPTH, D_MODEL)),
        'w_gate': nrm(22, (DEPTH, D_MODEL, D_FF), D_MODEL ** -0.5),
        'w_up': nrm(23, (DEPTH, D_MODEL, D_FF), D_MODEL ** -0.5),
        'w_down': nrm(24, (DEPTH, D_FF, D_MODEL), D_FF ** -0.5),
        'norm_ple': gain(25, (DEPTH, D_MODEL)),
        'w_ple_gate': nrm(26, (DEPTH, D_MODEL, D_MODEL), D_MODEL ** -0.5),
        'w_ple_proj': nrm(27, (DEPTH, PLE_DIM, D_MODEL), PLE_DIM ** -0.5),
        'norm_final': gain(28, (D_MODEL,)),
    }


def reference(x_prompt, x_sample, cache_kv_w128, cache_kv_w512, cache_kv_w2048, state_conv, state_C, state_n, state_m, p_prompt, p_sample, norm_mix, w_in, conv_w, conv_b, b_igate, b_fgate, mh_norm, w_proj_a, w_proj_b, w_out, norm_ffn, w_gate, w_up, w_down, norm_ple, w_ple_gate, w_ple_proj, norm_final):
    caches = (cache_kv_w128, cache_kv_w512, cache_kv_w2048)
    bp = x_prompt.shape[0]
    hp, hs = x_prompt, x_sample
    st_p, st_s = [], []
    for l in range(DEPTH):
        lw = (norm_mix[l], w_in[l], conv_w[l], conv_b[l], b_igate[l], b_fgate[l], mh_norm[l],
              w_proj_a[l], w_proj_b[l], w_out[l], norm_ffn[l], w_gate[l], w_up[l], w_down[l],
              norm_ple[l], w_ple_gate[l], w_ple_proj[l])
        zc = jnp.zeros((bp, CONV_W - 1, 2 * M_INNER), x_prompt.dtype)
        zC = jnp.zeros((bp, M_HEADS, M_DK, M_DV), state_C.dtype)
        zn = jnp.zeros((bp, M_HEADS, M_DK), state_n.dtype)
        zm = jnp.zeros((bp, M_HEADS), state_m.dtype)
        hp, sp = _layer(hp, p_prompt[l], 0, None, zc, zC, zn, zm, *lw)
        hs, ss = _layer(hs, p_sample[l], PAST_LEN, (caches[0][l], caches[1][l], caches[2][l]),
                        state_conv[l], state_C[l], state_n[l], state_m[l], *lw)
        st_p.append(sp)
        st_s.append(ss)

    def stk(i, sts):
        return jnp.stack([s[i] for s in sts], axis=0)

    y_prompt = _rmsnorm(hp, norm_final)
    y_sample = _rmsnorm(hs, norm_final)
    return (y_prompt, y_sample, stk(0, st_p), stk(0, st_s), stk(1, st_p), stk(1, st_s), stk(2, st_p), stk(2, st_s), stk(3, st_p), stk(3, st_s), stk(4, st_p), stk(4, st_s), stk(5, st_p), stk(5, st_s), stk(6, st_p), stk(6, st_s))
```

```python
import functools

import jax
import jax.numpy as jnp
from jax import lax
from jax.experimental import pallas as pl
from jax.experimental.pallas import tpu as pltpu

F32 = jnp.float32
BF16 = jnp.bfloat16

D_MODEL = 1024
PAST_LEN = 16384
HEAD_DIM = 64
ROPE_DIM = HEAD_DIM // 4
ROPE_HALF = ROPE_DIM // 2
ROPE_THETA = 500000.0
DSWA_GROUPS = ((128, 1), (512, 4), (2048, 16))
HPG = 4
N_GROUPS = len(DSWA_GROUPS)
GROUP_W = HPG * HEAD_DIM
A_QKV = N_GROUPS * GROUP_W
BAND = 128
M_HEADS = 4
M_INNER = D_MODEL
M_DK = M_INNER // M_HEADS
CONV_W = 4
D_FF = 2816
PLE_DIM = 256
EPS = 1e-6
NEG = -1e30

OFF_AQ = 0
OFF_AK = OFF_AQ + A_QKV
OFF_AV = OFF_AK + A_QKV
OFF_MQ = OFF_AV + A_QKV
OFF_MK = OFF_MQ + M_INNER
OFF_MV = OFF_MK + M_INNER
OFF_MO = OFF_MV + M_INNER
OFF_MI = OFF_MO + M_INNER
OFF_MF = OFF_MI + M_HEADS
OFF_GA = OFF_MF + M_HEADS
OFF_GB = OFF_GA + D_MODEL
IN_COLS = OFF_GB + D_MODEL

LANES = 128
SUBLANES = 8
VMEM_LIMIT = 56 * 1024 * 1024

TM_IN = 256
TM_POST = 256
M_CHUNK = 256
ATT_TQ = 512
STEP_NB_ATT = 8
STEP_CACHE_BLOCK = 4 * 1024 * 1024
STEP_NB_MLSTM = 2

_NT = (((1,), (1,)), ((), ()))
_TN = (((0,), (0,)), ((), ()))


def _dot(a, b):
    return jnp.dot(a, b, preferred_element_type=F32)


def _rms(x, g):
    return x * lax.rsqrt(jnp.mean(x * x, axis=-1, keepdims=True) + EPS) * g


def _log_sigmoid(x):
    return jnp.minimum(x, 0.0) - jnp.log1p(jnp.exp(-jnp.abs(x)))


def _rope(t, ra, rm, rp):
    return t * ra + pltpu.roll(t, LANES - ROPE_HALF, 1) * rm + pltpu.roll(t, ROPE_HALF, 1) * rp


def _rope2(t, ra, rm, rp):
    return jnp.concatenate([_rope(t[:, :LANES], ra, rm, rp), _rope(t[:, LANES:], ra, rm, rp)], axis=1)


def _gate_cols(z):
    lane = lax.broadcasted_iota(jnp.int32, z.shape, 1)
    return jnp.where(lane < M_HEADS, z, _log_sigmoid(z))


def _inproj_seq_kernel(x_ref, nmix_ref, waqkv_ref, wmqk_ref, wmv_ref, wmo_ref, wga_ref, wgb_ref,
                       wgc_ref, wgr_ref, bc_ref, br_ref, cw_ref, cb_ref, c0_ref, ra_ref, rm_ref, rp_ref,
                       qa_ref, ka_ref, va_ref, qm_ref, km_ref, vm_ref, og_ref, ga_ref, gb_ref,
                       gc_ref, gr_ref, kv0_ref, kv1_ref, kv2_ref, cn_ref, cbuf, *, tm, ns):
    i = pl.program_id(1)
    xn = _rms(x_ref[0], nmix_ref[...]).astype(BF16)
    ra, rm, rp = ra_ref[...], rm_ref[...], rp_ref[...]
    kv_refs = (kv0_ref, kv1_ref, kv2_ref)

    for g, (win, _) in enumerate(DSWA_GROUPS):
        cs = slice(g * GROUP_W, (g + 1) * GROUP_W)
        zq = _dot(xn, waqkv_ref[:, OFF_AQ + g * GROUP_W:OFF_AQ + (g + 1) * GROUP_W])
        qa_ref[0, :, cs] = (_rope2(zq, ra, rm, rp) * (HEAD_DIM ** -0.5)).astype(BF16)
        zk = _dot(xn, waqkv_ref[:, OFF_AK + g * GROUP_W:OFF_AK + (g + 1) * GROUP_W])
        kr = _rope2(zk, ra, rm, rp)
        ka_ref[0, :, cs] = kr.astype(BF16)
        zv = _dot(xn, waqkv_ref[:, OFF_AV + g * GROUP_W:OFF_AV + (g + 1) * GROUP_W])
        va_ref[0, :, cs] = zv.astype(BF16)
        rows = min(win, tm)
        nblk = win // rows

        @pl.when(i >= ns - nblk)
        def _():
            kv_refs[g][0, :, :GROUP_W] = kr[tm - rows:, :]
            kv_refs[g][0, :, GROUP_W:] = zv[tm - rows:, :]

    @pl.when(i == 0)
    def _():
        cbuf[0:SUBLANES, :] = c0_ref[0]

    for c in range(2 * M_INNER // 256):
        cs = slice(c * 256, (c + 1) * 256)
        pre = _dot(xn, wmqk_ref[:, cs])
        cbuf[SUBLANES:SUBLANES + tm, cs] = pre
        y = cb_ref[:, cs] + cw_ref[CONV_W - 1:CONV_W, cs] * pre
        for j in range(CONV_W - 1):
            off = SUBLANES - (CONV_W - 1) + j
            y = y + cw_ref[j:j + 1, cs] * cbuf[off:off + tm, cs]
        tail = cbuf[tm:tm + SUBLANES, cs]
        cbuf[0:SUBLANES, cs] = tail
        cn_ref[0, :, cs] = tail
        s = y * jax.nn.sigmoid(y)
        if c < M_INNER // 256:
            qm_ref[0, :, cs] = s.astype(BF16)
        else:
            km_ref[0, :, c * 256 - M_INNER:(c + 1) * 256 - M_INNER] = (s * (M_DK ** -0.5)).astype(BF16)

    for c in range(M_INNER // 256):
        cs = slice(c * 256, (c + 1) * 256)
        vm_ref[0, :, cs] = _dot(xn, wmv_ref[:, cs]).astype(BF16)
        og_ref[0, :, cs] = jax.nn.sigmoid(_dot(xn, wmo_ref[:, cs])).astype(BF16)
        ga_ref[0, :, cs] = jax.nn.sigmoid(_dot(xn, wga_ref[:, cs])).astype(BF16)
        gb_ref[0, :, cs] = jax.nn.sigmoid(_dot(xn, wgb_ref[:, cs])).astype(BF16)

    gc_ref[0] = _gate_cols(_dot(xn, wgc_ref[...]) + bc_ref[...])
    zr = lax.dot_general(wgr_ref[...], xn, _NT, preferred_element_type=F32) + br_ref[...]
    sub = lax.broadcasted_iota(jnp.int32, zr.shape, 0)
    gr_ref[0] = jnp.where(sub < M_HEADS, zr, _log_sigmoid(zr))


def _const_spec(shape):
    nd = len(shape)
    return pl.BlockSpec(shape, lambda *_: (0,) * nd, pipeline_mode=pl.Buffered(1))


def _inproj_seq(x, c0, tables, wts):
    B, S, _ = x.shape
    tm = TM_IN
    ns = S // tm
    ra, rm, rp = tables

    def kv_spec(win):
        rows = min(win, tm)
        nblk = win // rows
        return pl.BlockSpec((1, rows, 2 * GROUP_W), lambda b, i: (b, jnp.maximum(i - (ns - nblk), 0), 0))

    tok = lambda w: pl.BlockSpec((1, tm, w), lambda b, i: (b, i, 0))
    tab = pl.BlockSpec((tm, LANES), lambda b, i: (i, 0))
    in_specs = ([tok(D_MODEL)] + [_const_spec(w.shape) for w in wts]
                + [pl.BlockSpec((1, SUBLANES, 2 * M_INNER), lambda b, i: (b, 0, 0)), tab, tab, tab])
    out_shape = ([jax.ShapeDtypeStruct((B, S, A_QKV), BF16)] * 3
                 + [jax.ShapeDtypeStruct((B, S, M_INNER), BF16)] * 6
                 + [jax.ShapeDtypeStruct((B, S, LANES), F32), jax.ShapeDtypeStruct((B, SUBLANES, S), F32)]
                 + [jax.ShapeDtypeStruct((B, win, 2 * GROUP_W), F32) for win, _ in DSWA_GROUPS]
                 + [jax.ShapeDtypeStruct((B, SUBLANES, 2 * M_INNER), F32)])
    out_specs = ([tok(A_QKV)] * 3 + [tok(M_INNER)] * 6
                 + [tok(LANES), pl.BlockSpec((1, SUBLANES, tm), lambda b, i: (b, 0, i))]
                 + [kv_spec(win) for win, _ in DSWA_GROUPS]
                 + [pl.BlockSpec((1, SUBLANES, 2 * M_INNER), lambda b, i: (b, 0, 0))])
    return pl.pallas_call(
        functools.partial(_inproj_seq_kernel, tm=tm, ns=ns),
        grid=(B, ns),
        in_specs=in_specs,
        out_specs=out_specs,
        out_shape=out_shape,
        scratch_shapes=[pltpu.VMEM((tm + SUBLANES, 2 * M_INNER), F32)],
        compiler_params=pltpu.CompilerParams(
            dimension_semantics=("arbitrary", "arbitrary"), vmem_limit_bytes=VMEM_LIMIT),
        name="inproj_seq",
    )(x, *wts, c0, ra, rm, rp)


def _inproj_step_kernel(x_ref, nmix_ref, waqkv_ref, wmqk_ref, wmv_ref, wmo_ref, wga_ref, wgb_ref,
                        wgc_ref, wgr_ref, bc_ref, br_ref, cw_ref, cb_ref, cs_ref, ra_ref, rm_ref, rp_ref,
                        qa_ref, kv0_ref, kv1_ref, kv2_ref, qm_ref, km_ref, vm_ref, og_ref, ga_ref, gb_ref,
                        gc_ref, cn_ref):
    del wgr_ref, br_ref
    xn = _rms(x_ref[...], nmix_ref[...]).astype(BF16)
    ra, rm, rp = ra_ref[...], rm_ref[...], rp_ref[...]
    kv_refs = (kv0_ref, kv1_ref, kv2_ref)
    for g in range(N_GROUPS):
        cs = slice(g * GROUP_W, (g + 1) * GROUP_W)
        zq = _dot(xn, waqkv_ref[:, OFF_AQ + g * GROUP_W:OFF_AQ + (g + 1) * GROUP_W])
        qa_ref[:, cs] = _rope2(zq, ra, rm, rp) * (HEAD_DIM ** -0.5)
        zk = _dot(xn, waqkv_ref[:, OFF_AK + g * GROUP_W:OFF_AK + (g + 1) * GROUP_W])
        kv_refs[g][:, :GROUP_W] = _rope2(zk, ra, rm, rp)
        kv_refs[g][:, GROUP_W:] = _dot(xn, waqkv_ref[:, OFF_AV + g * GROUP_W:OFF_AV + (g + 1) * GROUP_W])

    for c in range(2 * M_INNER // 256):
        cs = slice(c * 256, (c + 1) * 256)
        pre = _dot(xn, wmqk_ref[:, cs])
        y = cb_ref[:, cs] + cw_ref[CONV_W - 1:CONV_W, cs] * pre
        for j in range(CONV_W - 1):
            y = y + cw_ref[j:j + 1, cs] * cs_ref[j, :, cs]
        for j in range(CONV_W - 2):
            cn_ref[j, :, cs] = cs_ref[j + 1, :, cs]
        cn_ref[CONV_W - 2, :, cs] = pre
        s = y * jax.nn.sigmoid(y)
        if c < M_INNER // 256:
            qm_ref[:, cs] = s
        else:
            km_ref[:, c * 256 - M_INNER:(c + 1) * 256 - M_INNER] = s * (M_DK ** -0.5)

    for c in range(M_INNER // 256):
        cs = slice(c * 256, (c + 1) * 256)
        vm_ref[:, cs] = _dot(xn, wmv_ref[:, cs])
        og_ref[:, cs] = jax.nn.sigmoid(_dot(xn, wmo_ref[:, cs]))
        ga_ref[:, cs] = jax.nn.sigmoid(_dot(xn, wga_ref[:, cs]))
        gb_ref[:, cs] = jax.nn.sigmoid(_dot(xn, wgb_ref[:, cs]))
    gc_ref[...] = _gate_cols(_dot(xn, wgc_ref[...]) + bc_ref[...])


def _inproj_step(x, conv_state, tables, wts):
    nb = x.shape[0]
    sds = lambda w: jax.ShapeDtypeStruct((nb, w), F32)
    out_shape = ([sds(A_QKV)] + [sds(2 * GROUP_W)] * N_GROUPS + [sds(M_INNER)] * 6 + [sds(LANES)]
                 + [jax.ShapeDtypeStruct((CONV_W - 1, nb, 2 * M_INNER), F32)])
    return pl.pallas_call(
        _inproj_step_kernel,
        out_shape=out_shape,
        compiler_params=pltpu.CompilerParams(vmem_limit_bytes=VMEM_LIMIT),
        name="inproj_step",
    )(x, *wts, conv_state, *tables)


def _attn_seq_kernel(q_ref, k_ref, v_ref, kp_ref, vp_ref, o_ref, lse_ref, *, tq):
    i = pl.program_id(2)
    lane_head = lax.broadcasted_iota(jnp.int32, (1, GROUP_W), 1) // HEAD_DIM
    head_mask = [(lane_head == h).astype(F32) for h in range(HPG)]
    qi = lax.broadcasted_iota(jnp.int32, (BAND, 2 * BAND), 0)
    kj = lax.broadcasted_iota(jnp.int32, (BAND, 2 * BAND), 1)
    band = (kj >= qi) & (kj <= qi + BAND)
    band_first = band & (kj >= jnp.where(i > 0, 0, BAND))

    for j in range(tq // BAND):
        valid = band_first if j == 0 else band
        q32 = q_ref[0, j * BAND:(j + 1) * BAND, :].astype(F32)
        if j == 0:
            kk = jnp.concatenate([kp_ref[0], k_ref[0, 0:BAND, :]], axis=0)
            vv = jnp.concatenate([vp_ref[0], v_ref[0, 0:BAND, :]], axis=0)
        else:
            kk = k_ref[0, (j - 1) * BAND:(j + 1) * BAND, :]
            vv = v_ref[0, (j - 1) * BAND:(j + 1) * BAND, :]
        o = jnp.zeros((BAND, GROUP_W), F32)
        lse = jnp.zeros((BAND, GROUP_W), F32)
        for h in range(HPG):
            qh = (q32 * head_mask[h]).astype(BF16)
            s = lax.dot_general(qh, kk, _NT, preferred_element_type=F32)
            s = jnp.where(valid, s, NEG)
            mx = jnp.max(s, axis=-1, keepdims=True)
            e = jnp.exp(s - mx)
            den = jnp.sum(e, axis=-1, keepdims=True)
            pv = _dot(e.astype(BF16), vv)
            o = o + (pv * (1.0 / den)) * head_mask[h]
            lse = lse + (mx + jnp.log(den)) * head_mask[h]
        o_ref[0, j * BAND:(j + 1) * BAND, :] = o
        lse_ref[0, j * BAND:(j + 1) * BAND, :] = lse


def _attn_seq(qa, ka, va, g, dil):
    B, S, _ = qa.shape
    nsub = S // dil
    tq = min(ATT_TQ, nsub)
    view = lambda t: t.reshape(B, nsub, dil * A_QKV)
    in_spec = pl.BlockSpec((1, tq, GROUP_W), lambda b, r, i: (b, i, N_GROUPS * r + g))
    prev_spec = pl.BlockSpec((1, BAND, GROUP_W),
                             lambda b, r, i: (b, jnp.maximum(i * (tq // BAND) - 1, 0), N_GROUPS * r + g))
    out_spec = pl.BlockSpec((1, tq, GROUP_W), lambda b, r, i: (b, i, r))
    o, lse = pl.pallas_call(
        functools.partial(_attn_seq_kernel, tq=tq),
        grid=(B, dil, nsub // tq),
        in_specs=[in_spec] * 3 + [prev_spec] * 2,
        out_specs=[out_spec] * 2,
        out_shape=[jax.ShapeDtypeStruct((B, nsub, dil * GROUP_W), F32)] * 2,
        compiler_params=pltpu.CompilerParams(
            dimension_semantics=("arbitrary", "arbitrary", "arbitrary"), vmem_limit_bytes=VMEM_LIMIT),
        name=f"attn_seq_d{dil}",
    )(view(qa), view(ka), view(va), view(ka), view(va))
    return o.reshape(B, S, GROUP_W), lse.reshape(B, S, GROUP_W)


def _attn_step_kernel(q_ref, kvn_ref, c_ref, o_ref, lse_ref, cn_ref, *, nb, dil, w):
    sub = lax.broadcasted_iota(jnp.int32, (SUBLANES, GROUP_W), 0)
    lane_head = lax.broadcasted_iota(jnp.int32, (SUBLANES, GROUP_W), 1) // HEAD_DIM
    own = sub == lane_head
    pos = lax.broadcasted_iota(jnp.int32, (SUBLANES, w), 1)
    valid = (pos % dil) == 0
    eye = (lax.broadcasted_iota(jnp.int32, (GROUP_W, GROUP_W), 0)
           == lax.broadcasted_iota(jnp.int32, (GROUP_W, GROUP_W), 1))
    last = lax.broadcasted_iota(jnp.int32, (GROUP_W, w), 1) == w - 1
    for b in range(nb):
        qrows = jnp.where(own, jnp.broadcast_to(q_ref[b], (SUBLANES, GROUP_W)), 0.0).astype(BF16)
        k_t = c_ref[b, 0]
        v_t = c_ref[b, 1]
        k_row = kvn_ref[b, :, :GROUP_W]
        v_row = kvn_ref[b, :, GROUP_W:]
        k_new = k_row.astype(BF16).astype(F32)
        v_new = v_row.astype(BF16).astype(F32)
        s = jnp.where(valid, _dot(qrows, k_t.astype(BF16)), NEG)
        s_new = jnp.sum(qrows.astype(F32) * k_new, axis=-1, keepdims=True)
        mx = jnp.maximum(jnp.max(s, axis=-1, keepdims=True), s_new)
        e = jnp.exp(s - mx)
        e_new = jnp.exp(s_new - mx)
        den = jnp.sum(e, axis=-1, keepdims=True) + e_new
        orow = (lax.dot_general((e / den).astype(BF16), v_t.astype(BF16), _NT, preferred_element_type=F32)
                + (e_new / den) * v_new)
        lse = mx + jnp.log(den)
        o_ref[b] = jnp.sum(jnp.where(own, orow, 0.0), axis=0, keepdims=True)
        lse_ref[b] = jnp.sum(jnp.where(own, jnp.broadcast_to(lse, (SUBLANES, GROUP_W)), 0.0),
                             axis=0, keepdims=True)
        for j, (old, row) in enumerate(((k_t, k_row), (v_t, v_row))):
            col = jnp.sum(jnp.where(eye, jnp.broadcast_to(row, (GROUP_W, GROUP_W)), 0.0), axis=1, keepdims=True)
            cn_ref[b, j] = jnp.where(last, col, pltpu.roll(old, w - 1, 1))


def _attn_step(q3, kvn3, cache_t, g, dil):
    NB, _, _, w = cache_t.shape
    assert w == BAND * dil, "the cache must hold exactly one full window"
    nb = max(1, min(STEP_NB_ATT, STEP_CACHE_BLOCK // (2 * GROUP_W * w * 4)))
    row = pl.BlockSpec((nb, 1, GROUP_W), lambda b: (b, 0, 0))
    cspec = pl.BlockSpec((nb, 2, GROUP_W, w), lambda b: (b, 0, 0, 0))
    return pl.pallas_call(
        functools.partial(_attn_step_kernel, nb=nb, dil=dil, w=w),
        grid=(NB // nb,),
        in_specs=[pl.BlockSpec((nb, 1, GROUP_W), lambda b: (b, 0, g)),
                  pl.BlockSpec((nb, 1, 2 * GROUP_W), lambda b: (b, 0, 0)),
                  cspec],
        out_specs=[row, row, cspec],
        out_shape=[jax.ShapeDtypeStruct((NB, 1, GROUP_W), F32)] * 2 + [jax.ShapeDtypeStruct(cache_t.shape, F32)],
        compiler_params=pltpu.CompilerParams(dimension_semantics=("arbitrary",), vmem_limit_bytes=VMEM_LIMIT),
        name=f"attn_step_d{dil}",
    )(q3, kvn3, cache_t)


def _mlstm_seq_kernel(q_ref, k_ref, v_ref, og_ref, gc_ref, gr_ref, mhn_ref,
                      ob_ref, c_out, n_out, m_out, c_s, n_s, m_s, *, L):
    c = pl.program_id(1)

    @pl.when(c == 0)
    def _():
        c_s[...] = jnp.zeros(c_s.shape, F32)
        n_s[...] = jnp.zeros(n_s.shape, F32)
        m_s[...] = jnp.zeros(m_s.shape, F32)

    gc = gc_ref[0]
    gr = gr_ref[0]
    ti = lax.broadcasted_iota(jnp.int32, (L, L), 0)
    si = lax.broadcasted_iota(jnp.int32, (L, L), 1)
    tril = ti >= si
    b_cols = jnp.dot(tril.astype(F32), gc, preferred_element_type=F32, precision=lax.Precision.HIGHEST)
    b_rows = jnp.dot(gr, (ti <= si).astype(F32), preferred_element_type=F32, precision=lax.Precision.HIGHEST)
    lane = lax.broadcasted_iota(jnp.int32, (1, LANES), 1)
    m_vec = jnp.zeros((1, LANES), F32)

    for h in range(M_HEADS):
        hs = slice(h * M_DK, (h + 1) * M_DK)
        b_col = b_cols[:, M_HEADS + h:M_HEADS + h + 1]
        i_col = gc[:, h:h + 1]
        b_row = b_rows[M_HEADS + h:M_HEADS + h + 1, :]
        i_row = gr[h:h + 1, :]
        m_prev = m_s[h, 0:1, 0:1]
        qh = q_ref[0, :, hs]
        kh = k_ref[0, :, hs]
        vh = v_ref[0, :, hs]
        c_prev = c_s[h]
        n_prev = n_s[h]

        dmat = jnp.where(tril, b_col - b_row + i_row, -jnp.inf)
        m_inter = b_col + m_prev
        m_t = jnp.maximum(m_inter, jnp.max(dmat, axis=-1, keepdims=True))
        w_intra = lax.dot_general(qh, kh, _NT, preferred_element_type=F32) * jnp.exp(dmat - m_t)
        w_inter = jnp.exp(m_inter - m_t)
        num = _dot(w_intra.astype(BF16), vh) + w_inter * _dot(qh, c_prev.astype(BF16))
        den = (jnp.sum(w_intra, axis=-1, keepdims=True)
               + w_inter * jnp.sum(qh.astype(F32) * n_prev, axis=-1, keepdims=True))
        hh = num / jnp.maximum(jnp.abs(den), jnp.exp(-m_t))
        hn = hh * lax.rsqrt(jnp.mean(hh * hh, axis=-1, keepdims=True) + EPS) * mhn_ref[h:h + 1, :]
        ob_ref[0, :, hs] = (hn * og_ref[0, :, hs].astype(F32)).astype(BF16)

        b_last = b_row[:, L - 1:L]
        m_new = jnp.maximum(b_last + m_prev, jnp.max(b_last - b_row + i_row, axis=-1, keepdims=True))
        wk = jnp.exp(b_last - b_col + i_col - m_new)
        decay = jnp.exp(b_last + m_prev - m_new)
        ks = kh.astype(F32) * wk
        c_new = decay * c_prev + lax.dot_general(ks.astype(BF16), vh, _TN, preferred_element_type=F32)
        n_new = decay * n_prev + jnp.sum(ks, axis=0, keepdims=True)
        c_s[h] = c_new
        n_s[h] = n_new
        m_s[h] = jnp.broadcast_to(m_new, (SUBLANES, LANES))
        c_out[0, h] = c_new
        n_out[0, h:h + 1, :] = n_new
        m_vec = jnp.where(lane == h, m_new, m_vec)
    m_out[0] = m_vec


def _mlstm_seq(qm, km, vm, og, gc, gr, mhn):
    B, S, _ = qm.shape
    L = M_CHUNK
    tok = pl.BlockSpec((1, L, M_INNER), lambda b, c: (b, c, 0))
    return pl.pallas_call(
        functools.partial(_mlstm_seq_kernel, L=L),
        grid=(B, S // L),
        in_specs=[tok, tok, tok, tok,
                  pl.BlockSpec((1, L, LANES), lambda b, c: (b, c, 0)),
                  pl.BlockSpec((1, SUBLANES, L), lambda b, c: (b, 0, c)),
                  pl.BlockSpec((M_HEADS, M_DK), lambda b, c: (0, 0))],
        out_specs=[tok,
                   pl.BlockSpec((1, M_HEADS, M_DK, M_DK), lambda b, c: (b, 0, 0, 0)),
                   pl.BlockSpec((1, M_HEADS, M_DK), lambda b, c: (b, 0, 0)),
                   pl.BlockSpec((1, 1, LANES), lambda b, c: (b, 0, 0))],
        out_shape=[jax.ShapeDtypeStruct((B, S, M_INNER), BF16),
                   jax.ShapeDtypeStruct((B, M_HEADS, M_DK, M_DK), F32),
                   jax.ShapeDtypeStruct((B, M_HEADS, M_DK), F32),
                   jax.ShapeDtypeStruct((B, 1, LANES), F32)],
        scratch_shapes=[pltpu.VMEM((M_HEADS, M_DK, M_DK), F32),
                        pltpu.VMEM((M_HEADS, 1, M_DK), F32),
                        pltpu.VMEM((M_HEADS, SUBLANES, LANES), F32)],
        compiler_params=pltpu.CompilerParams(
            dimension_semantics=("arbitrary", "arbitrary"), vmem_limit_bytes=VMEM_LIMIT),
        name="mlstm_seq",
    )(qm, km, vm, og, gc, gr, mhn)


def _mlstm_step_kernel(q_ref, k_ref, v_ref, og_ref, g_ref, c_ref, n_ref, m_ref, mhn_ref,
                       ob_ref, c_out, n_out, m_out, *, nb):
    row0 = lax.broadcasted_iota(jnp.int32, (SUBLANES, M_DK), 0) == 0
    lane = lax.broadcasted_iota(jnp.int32, (1, LANES), 1)
    for b in range(nb):
        gates = g_ref[b]
        m_all = m_ref[b]
        m_vec = jnp.zeros((1, LANES), F32)
        for h in range(M_HEADS):
            hs = slice(h * M_DK, (h + 1) * M_DK)
            ig = gates[:, h:h + 1]
            lf = gates[:, M_HEADS + h:M_HEADS + h + 1]
            m_prev = m_all[:, h:h + 1]
            qb = q_ref[b, :, hs].astype(BF16)
            kb = k_ref[b, :, hs].astype(BF16)
            vb = v_ref[b, :, hs].astype(BF16)
            qf, kf, vf = qb.astype(F32), kb.astype(F32), vb.astype(F32)
            c_prev = c_ref[b, h]
            n_prev = n_ref[b, h:h + 1, :]

            m_inter = lf + m_prev
            m_t = jnp.maximum(m_inter, ig)
            w_intra = jnp.sum(qf * kf, axis=-1, keepdims=True) * jnp.exp(ig - m_t)
            w_inter = jnp.exp(m_inter - m_t)
            q_c = _dot(jnp.broadcast_to(qb, (SUBLANES, M_DK)), c_prev.astype(BF16))[0:1, :]
            num = w_intra * vf + w_inter * q_c
            den = w_intra + w_inter * jnp.sum(qf * n_prev, axis=-1, keepdims=True)
            hh = num / jnp.maximum(jnp.abs(den), jnp.exp(-m_t))
            hn = hh * lax.rsqrt(jnp.mean(hh * hh, axis=-1, keepdims=True) + EPS) * mhn_ref[h:h + 1, :]
            ob_ref[b, :, hs] = hn * og_ref[b, :, hs]

            ks = kf * jnp.exp(ig - m_t)
            ks8 = jnp.where(row0, jnp.broadcast_to(ks, (SUBLANES, M_DK)), 0.0).astype(BF16)
            outer = lax.dot_general(ks8, jnp.broadcast_to(vb, (SUBLANES, M_DK)), _TN, preferred_element_type=F32)
            c_out[b, h] = w_inter * c_prev + outer
            n_out[b, h:h + 1, :] = w_inter * n_prev + ks
            m_vec = jnp.where(lane == h, m_t, m_vec)
        m_out[b] = m_vec


def _mlstm_step(qm, km, vm, og, gates, c0, n0, m0, mhn):
    NB = qm.shape[0]
    nb = STEP_NB_MLSTM
    tok = pl.BlockSpec((nb, 1, M_INNER), lambda b: (b, 0, 0))
    vec = pl.BlockSpec((nb, 1, LANES), lambda b: (b, 0, 0))
    cspec = pl.BlockSpec((nb, M_HEADS, M_DK, M_DK), lambda b: (b, 0, 0, 0))
    nspec = pl.BlockSpec((nb, M_HEADS, M_DK), lambda b: (b, 0, 0))
    return pl.pallas_call(
        functools.partial(_mlstm_step_kernel, nb=nb),
        grid=(NB // nb,),
        in_specs=[tok, tok, tok, tok, vec, cspec, nspec, vec, pl.BlockSpec((M_HEADS, M_DK), lambda b: (0, 0))],
        out_specs=[tok, cspec, nspec, vec],
        out_shape=[jax.ShapeDtypeStruct((NB, 1, M_INNER), F32),
                   jax.ShapeDtypeStruct(c0.shape, F32),
                   jax.ShapeDtypeStruct(n0.shape, F32),
                   jax.ShapeDtypeStruct((NB, 1, LANES), F32)],
        compiler_params=pltpu.CompilerParams(dimension_semantics=("arbitrary",), vmem_limit_bytes=VMEM_LIMIT),
        name="mlstm_step",
    )(qm, km, vm, og, gates, c0, n0, m0, mhn)


def _post_kernel(x_ref, o0_ref, o1_ref, o2_ref, l0_ref, l1_ref, l2_ref, ob_ref, ga_ref, gb_ref, p_ref,
                 wpa_ref, wpb_ref, wout_ref, nffn_ref, wg_ref, wu_ref, wd_ref, nple_ref, wpg_ref, wpp_ref,
                 nfin_ref, y_ref):
    lead = (0,) * (len(x_ref.shape) - 2)
    rd = lambda r: r[lead] if lead else r[...]
    l0, l1, l2 = rd(l0_ref), rd(l1_ref), rd(l2_ref)
    lmax = jnp.maximum(jnp.maximum(l0, l1), l2)
    e0, e1, e2 = jnp.exp(l0 - lmax), jnp.exp(l1 - lmax), jnp.exp(l2 - lmax)
    o_a = (e0 * rd(o0_ref) + e1 * rd(o1_ref) + e2 * rd(o2_ref)) / (e0 + e1 + e2)

    mix = (rd(ga_ref).astype(F32) * _dot(o_a.astype(BF16), wpa_ref[...])
           + rd(gb_ref).astype(F32) * _dot(rd(ob_ref).astype(BF16), wpb_ref[...]))
    h = rd(x_ref) + _dot(mix.astype(BF16), wout_ref[...])

    xf = _rms(h, nffn_ref[...]).astype(BF16)
    half = D_FF // 2
    for c in range(2):
        cs = slice(c * half, (c + 1) * half)
        gate = _dot(xf, wg_ref[:, cs])
        act = (gate * jax.nn.sigmoid(gate)) * _dot(xf, wu_ref[:, cs])
        h = h + _dot(act.astype(BF16), wd_ref[cs, :])

    xp = _rms(h, nple_ref[...]).astype(BF16)
    h = h + jax.nn.sigmoid(_dot(xp, wpg_ref[...])) * _dot(rd(p_ref).astype(BF16), wpp_ref[...])
    y = _rms(h, nfin_ref[...])
    if lead:
        y_ref[lead] = y
    else:
        y_ref[...] = y


def _post_seq(x, os_, ls_, ob, ga, gb, p, wts):
    B, S, _ = x.shape
    tm = TM_POST
    tok = lambda w: pl.BlockSpec((1, tm, w), lambda b, i: (b, i, 0))
    in_specs = ([tok(D_MODEL)] + [tok(GROUP_W)] * 6 + [tok(M_INNER)] * 3 + [tok(PLE_DIM)]
                + [_const_spec(w.shape) for w in wts])
    return pl.pallas_call(
        _post_kernel,
        grid=(B, S // tm),
        in_specs=in_specs,
        out_specs=tok(D_MODEL),
        out_shape=jax.ShapeDtypeStruct((B, S, D_MODEL), F32),
        compiler_params=pltpu.CompilerParams(
            dimension_semantics=("arbitrary", "arbitrary"), vmem_limit_bytes=VMEM_LIMIT),
        name="post_seq",
    )(x, *os_, *ls_, ob, ga, gb, p, *wts)


def _post_step(x, os_, ls_, ob, ga, gb, p, wts):
    return pl.pallas_call(
        _post_kernel,
        out_shape=jax.ShapeDtypeStruct(x.shape, F32),
        compiler_params=pltpu.CompilerParams(vmem_limit_bytes=VMEM_LIMIT),
        name="post_step",
    )(x, *os_, *ls_, ob, ga, gb, p, *wts)


def _rope_tables(pos):
    t = pos.shape[0]
    inv = jnp.power(ROPE_THETA, -jnp.arange(ROPE_HALF, dtype=F32) / ROPE_HALF)
    ang = pos.astype(F32)[:, None] * inv[None, :]
    cos, sin = jnp.cos(ang), jnp.sin(ang)
    one = jnp.ones((t, HEAD_DIM - ROPE_DIM), F32)
    zero = jnp.zeros((t, HEAD_DIM - ROPE_DIM), F32)
    z8 = jnp.zeros((t, ROPE_HALF), F32)
    a = jnp.concatenate([cos, cos, one], axis=1)
    m = jnp.concatenate([-sin, z8, zero], axis=1)
    p = jnp.concatenate([z8, sin, zero], axis=1)
    rep = LANES // HEAD_DIM
    return tuple(jnp.tile(v, (1, rep)) for v in (a, m, p))


def kernel(x_prompt, x_sample, cache_kv_w128, cache_kv_w512, cache_kv_w2048, state_conv, state_C, state_n,
           state_m, p_prompt, p_sample, norm_mix, w_in, conv_w, conv_b, b_igate, b_fgate, mh_norm, w_proj_a,
           w_proj_b, w_out, norm_ffn, w_gate, w_up, w_down, norm_ple, w_ple_gate, w_ple_proj, norm_final):
    depth = w_in.shape[0]
    assert depth == 1, "single trunk layer"
    B, S, _ = x_prompt.shape
    NB = x_sample.shape[0]
    assert x_sample.shape[1] == 1
    caches = (cache_kv_w128, cache_kv_w512, cache_kv_w2048)
    assert S >= DSWA_GROUPS[-1][0] and S % (BAND * DSWA_GROUPS[-1][1]) == 0
    past = 0
    l = 0

    w = w_in[l]
    gate_w = w[:, OFF_MI:OFF_GA]
    gate_b = jnp.concatenate([b_igate[l], b_fgate[l]])
    in_wts = (
        norm_mix[l][None, :],
        w[:, OFF_AQ:OFF_MQ].astype(BF16),
        w[:, OFF_MQ:OFF_MV].astype(BF16),
        w[:, OFF_MV:OFF_MO].astype(BF16),
        w[:, OFF_MO:OFF_MI].astype(BF16),
        w[:, OFF_GA:OFF_GB].astype(BF16),
        w[:, OFF_GB:IN_COLS].astype(BF16),
        jnp.pad(gate_w, ((0, 0), (0, LANES - 2 * M_HEADS))).astype(BF16),
        gate_w.T.astype(BF16),
        jnp.pad(gate_b, (0, LANES - 2 * M_HEADS))[None, :],
        gate_b[:, None],
        conv_w[l],
        conv_b[l][None, :],
    )
    post_wts = (
        w_proj_a[l].astype(BF16), w_proj_b[l].astype(BF16), w_out[l].astype(BF16), norm_ffn[l][None, :],
        w_gate[l].astype(BF16), w_up[l].astype(BF16), w_down[l].astype(BF16), norm_ple[l][None, :],
        w_ple_gate[l].astype(BF16), w_ple_proj[l].astype(BF16), norm_final[None, :],
    )
    mhn = mh_norm[l]

    c0 = jnp.zeros((B, SUBLANES, 2 * M_INNER), F32)
    tabs_p = _rope_tables(past + jnp.arange(S, dtype=jnp.int32))
    (qa, ka, va, qm, km, vm, og, ga, gb, gc, gr, kv0, kv1, kv2, cn) = _inproj_seq(x_prompt, c0, tabs_p, in_wts)
    os_p, ls_p = [], []
    for g, (_, dil) in enumerate(DSWA_GROUPS):
        o, lse = _attn_seq(qa, ka, va, g, dil)
        os_p.append(o)
        ls_p.append(lse)
    ob, c_p, n_p, m_p = _mlstm_seq(qm, km, vm, og, gc, gr, mhn)
    y_prompt = _post_seq(x_prompt, os_p, ls_p, ob, ga, gb, p_prompt[l], post_wts)

    tabs_s = _rope_tables(jnp.full((NB,), PAST_LEN, jnp.int32))
    conv_s = jnp.swapaxes(state_conv[l], 0, 1)
    (qa_s, kvn0, kvn1, kvn2, qm_s, km_s, vm_s, og_s, ga_s, gb_s, gc_s, cn_s) = _inproj_step(
        x_sample[:, 0, :], conv_s, tabs_s, in_wts)
    kvn = (kvn0, kvn1, kvn2)
    caches_t = [jnp.transpose(c[l], (0, 2, 3, 4, 1)).reshape(NB, 2, GROUP_W, c.shape[2]) for c in caches]
    os_s, ls_s, kv_s = [], [], []
    for g, (_, dil) in enumerate(DSWA_GROUPS):
        o, lse, cache_new = _attn_step(qa_s[:, None, :], kvn[g][:, None, :], caches_t[g], g, dil)
        os_s.append(o[:, 0, :])
        ls_s.append(lse[:, 0, :])
        kv_s.append(jnp.transpose(cache_new.reshape(NB, 2, HPG, HEAD_DIM, -1), (0, 4, 1, 2, 3)))
    m0 = jnp.pad(state_m[l], ((0, 0), (0, LANES - M_HEADS)))[:, None, :]
    ob_s, c_s, n_s, m_s = _mlstm_step(qm_s[:, None, :], km_s[:, None, :], vm_s[:, None, :], og_s[:, None, :],
                                      gc_s[:, None, :], state_C[l], state_n[l], m0, mhn)
    y_sample = _post_step(x_sample[:, 0, :], os_s, ls_s, ob_s[:, 0, :], ga_s, gb_s, p_sample[l][:, 0, :], post_wts)

    kv_shape = lambda n, w: (1, n, w, 2, HPG, HEAD_DIM)
    return (
        y_prompt, y_sample[:, None, :],
        kv0.reshape(kv_shape(B, DSWA_GROUPS[0][0])), kv_s[0].reshape(kv_shape(NB, DSWA_GROUPS[0][0])),
        kv1.reshape(kv_shape(B, DSWA_GROUPS[1][0])), kv_s[1].reshape(kv_shape(NB, DSWA_GROUPS[1][0])),
        kv2.reshape(kv_shape(B, DSWA_GROUPS[2][0])), kv_s[2].reshape(kv_shape(NB, DSWA_GROUPS[2][0])),
        cn[:, SUBLANES - (CONV_W - 1):, :][None], jnp.swapaxes(cn_s, 0, 1)[None],
        c_p[None], c_s[None],
        n_p[None], n_s[None],
        m_p[:, 0, :M_HEADS][None], m_s[:, 0, :M_HEADS][None],
    )
```

```python
import functools

import jax
import jax.numpy as jnp
from jax import lax
from jax.experimental import pallas as pl
from jax.experimental.pallas import tpu as pltpu

F32 = jnp.float32
BF16 = jnp.bfloat16

D_MODEL = 1024
PAST_LEN = 16384
HEAD_DIM = 64
ROPE_DIM = HEAD_DIM // 4
ROPE_HALF = ROPE_DIM // 2
ROPE_THETA = 500000.0
DSWA_GROUPS = ((128, 1), (512, 4), (2048, 16))
HPG = 4
N_GROUPS = len(DSWA_GROUPS)
GROUP_W = HPG * HEAD_DIM
A_QKV = N_GROUPS * GROUP_W
BAND = 128
M_HEADS = 4
M_INNER = D_MODEL
M_DK = M_INNER // M_HEADS
CONV_W = 4
D_FF = 2816
PLE_DIM = 256
EPS = 1e-6
NEG = -1e30

OFF_AQ = 0
OFF_AK = OFF_AQ + A_QKV
OFF_AV = OFF_AK + A_QKV
OFF_MQ = OFF_AV + A_QKV
OFF_MK = OFF_MQ + M_INNER
OFF_MV = OFF_MK + M_INNER
OFF_MO = OFF_MV + M_INNER
OFF_MI = OFF_MO + M_INNER
OFF_MF = OFF_MI + M_HEADS
OFF_GA = OFF_MF + M_HEADS
OFF_GB = OFF_GA + D_MODEL
IN_COLS = OFF_GB + D_MODEL

LANES = 128
SUBLANES = 8
VMEM_LIMIT = 56 * 1024 * 1024

TM_IN = 256
TM_POST = 256
M_CHUNK = 256
ATT_TQ = 512
STEP_NB_ATT = 8
STEP_CACHE_BLOCK = 4 * 1024 * 1024
STEP_NB_MLSTM = 2

_NT = (((1,), (1,)), ((), ()))
_TN = (((0,), (0,)), ((), ()))


def _dot(a, b):
    return jnp.dot(a, b, preferred_element_type=F32)


def _rms(x, g):
    return x * lax.rsqrt(jnp.mean(x * x, axis=-1, keepdims=True) + EPS) * g


def _log_sigmoid(x):
    return jnp.minimum(x, 0.0) - jnp.log1p(jnp.exp(-jnp.abs(x)))


def _rope(t, ra, rm, rp):
    return t * ra + pltpu.roll(t, LANES - ROPE_HALF, 1) * rm + pltpu.roll(t, ROPE_HALF, 1) * rp


def _rope2(t, ra, rm, rp):
    return jnp.concatenate([_rope(t[:, :LANES], ra, rm, rp), _rope(t[:, LANES:], ra, rm, rp)], axis=1)


def _gate_cols(z):
    lane = lax.broadcasted_iota(jnp.int32, z.shape, 1)
    return jnp.where(lane < M_HEADS, z, _log_sigmoid(z))


def _inproj_seq_kernel(x_ref, nmix_ref, waqkv_ref, wmqk_ref, wmv_ref, wmo_ref, wga_ref, wgb_ref,
                       wgc_ref, wgr_ref, bc_ref, br_ref, cw_ref, cb_ref, c0_ref, ra_ref, rm_ref, rp_ref,
                       q0_ref, k0_ref, v0_ref, q1_ref, k1_ref, v1_ref, q2_ref, k2_ref, v2_ref,
                       qm_ref, km_ref, vm_ref, og_ref, ga_ref, gb_ref,
                       gc_ref, gr_ref, kv0_ref, kv1_ref, kv2_ref, cn_ref, cbuf, dbuf, *, tm, ns):
    i = pl.program_id(1)
    xn = _rms(x_ref[0], nmix_ref[...]).astype(BF16)
    ra, rm, rp = ra_ref[...], rm_ref[...], rp_ref[...]
    kv_refs = (kv0_ref, kv1_ref, kv2_ref)
    qkv_refs = ((q0_ref, k0_ref, v0_ref), (q1_ref, k1_ref, v1_ref), (q2_ref, k2_ref, v2_ref))

    def put_dilated(ref, val, dil, slot):
        if dil == 1:
            ref[0] = val.astype(BF16)
            return
        for c in range(GROUP_W // LANES):
            dbuf[slot, c] = val[:, c * LANES:(c + 1) * LANES]
        for r in range(dil):
            for c in range(GROUP_W // LANES):
                lo = r * GROUP_W + c * LANES
                ref[0, :, lo:lo + LANES] = dbuf[slot, c, pl.ds(r, tm // dil, stride=dil), :].astype(BF16)

    for g, (win, dil) in enumerate(DSWA_GROUPS):
        zq = _dot(xn, waqkv_ref[:, OFF_AQ + g * GROUP_W:OFF_AQ + (g + 1) * GROUP_W])
        put_dilated(qkv_refs[g][0], _rope2(zq, ra, rm, rp) * (HEAD_DIM ** -0.5), dil, 3 * g)
        zk = _dot(xn, waqkv_ref[:, OFF_AK + g * GROUP_W:OFF_AK + (g + 1) * GROUP_W])
        kr = _rope2(zk, ra, rm, rp)
        put_dilated(qkv_refs[g][1], kr, dil, 3 * g + 1)
        zv = _dot(xn, waqkv_ref[:, OFF_AV + g * GROUP_W:OFF_AV + (g + 1) * GROUP_W])
        put_dilated(qkv_refs[g][2], zv, dil, 3 * g + 2)
        rows = min(win, tm)
        nblk = win // rows

        @pl.when(i >= ns - nblk)
        def _():
            kv_refs[g][0, :, :GROUP_W] = kr[tm - rows:, :]
            kv_refs[g][0, :, GROUP_W:] = zv[tm - rows:, :]

    @pl.when(i == 0)
    def _():
        cbuf[0:SUBLANES, :] = c0_ref[0]

    for c in range(2 * M_INNER // 256):
        cs = slice(c * 256, (c + 1) * 256)
        pre = _dot(xn, wmqk_ref[:, cs])
        cbuf[SUBLANES:SUBLANES + tm, cs] = pre
        y = cb_ref[:, cs] + cw_ref[CONV_W - 1:CONV_W, cs] * pre
        for j in range(CONV_W - 1):
            off = SUBLANES - (CONV_W - 1) + j
            y = y + cw_ref[j:j + 1, cs] * cbuf[off:off + tm, cs]
        tail = cbuf[tm:tm + SUBLANES, cs]
        cbuf[0:SUBLANES, cs] = tail
        cn_ref[0, :, cs] = tail
        s = y * jax.nn.sigmoid(y)
        if c < M_INNER // 256:
            qm_ref[0, :, cs] = s.astype(BF16)
        else:
            km_ref[0, :, c * 256 - M_INNER:(c + 1) * 256 - M_INNER] = (s * (M_DK ** -0.5)).astype(BF16)

    for c in range(M_INNER // 256):
        cs = slice(c * 256, (c + 1) * 256)
        vm_ref[0, :, cs] = _dot(xn, wmv_ref[:, cs]).astype(BF16)
        og_ref[0, :, cs] = jax.nn.sigmoid(_dot(xn, wmo_ref[:, cs])).astype(BF16)
        ga_ref[0, :, cs] = jax.nn.sigmoid(_dot(xn, wga_ref[:, cs])).astype(BF16)
        gb_ref[0, :, cs] = jax.nn.sigmoid(_dot(xn, wgb_ref[:, cs])).astype(BF16)

    gc_ref[0] = _gate_cols(_dot(xn, wgc_ref[...]) + bc_ref[...])
    zr = lax.dot_general(wgr_ref[...], xn, _NT, preferred_element_type=F32) + br_ref[...]
    sub = lax.broadcasted_iota(jnp.int32, zr.shape, 0)
    gr_ref[0] = jnp.where(sub < M_HEADS, zr, _log_sigmoid(zr))


def _const_spec(shape):
    nd = len(shape)
    return pl.BlockSpec(shape, lambda *_: (0,) * nd, pipeline_mode=pl.Buffered(1))


def _inproj_seq(x, c0, tables, wts):
    B, S, _ = x.shape
    tm = TM_IN
    ns = S // tm
    ra, rm, rp = tables

    def kv_spec(win):
        rows = min(win, tm)
        nblk = win // rows
        return pl.BlockSpec((1, rows, 2 * GROUP_W), lambda b, i: (b, jnp.maximum(i - (ns - nblk), 0), 0))

    tok = lambda w: pl.BlockSpec((1, tm, w), lambda b, i: (b, i, 0))
    tab = pl.BlockSpec((tm, LANES), lambda b, i: (i, 0))
    in_specs = ([tok(D_MODEL)] + [_const_spec(w.shape) for w in wts]
                + [pl.BlockSpec((1, SUBLANES, 2 * M_INNER), lambda b, i: (b, 0, 0)), tab, tab, tab])
    dil_shape = [jax.ShapeDtypeStruct((B, S // dil, dil * GROUP_W), BF16) for _, dil in DSWA_GROUPS for _ in range(3)]
    dil_specs = [pl.BlockSpec((1, tm // dil, dil * GROUP_W), lambda b, i: (b, i, 0))
                 for _, dil in DSWA_GROUPS for _ in range(3)]
    out_shape = (dil_shape
                 + [jax.ShapeDtypeStruct((B, S, M_INNER), BF16)] * 6
                 + [jax.ShapeDtypeStruct((B, S, LANES), F32), jax.ShapeDtypeStruct((B, SUBLANES, S), F32)]
                 + [jax.ShapeDtypeStruct((B, win, 2 * GROUP_W), F32) for win, _ in DSWA_GROUPS]
                 + [jax.ShapeDtypeStruct((B, SUBLANES, 2 * M_INNER), F32)])
    out_specs = (dil_specs + [tok(M_INNER)] * 6
                 + [tok(LANES), pl.BlockSpec((1, SUBLANES, tm), lambda b, i: (b, 0, i))]
                 + [kv_spec(win) for win, _ in DSWA_GROUPS]
                 + [pl.BlockSpec((1, SUBLANES, 2 * M_INNER), lambda b, i: (b, 0, 0))])
    return pl.pallas_call(
        functools.partial(_inproj_seq_kernel, tm=tm, ns=ns),
        grid=(B, ns),
        in_specs=in_specs,
        out_specs=out_specs,
        out_shape=out_shape,
        scratch_shapes=[pltpu.VMEM((tm + SUBLANES, 2 * M_INNER), F32),
                        pltpu.VMEM((3 * N_GROUPS, GROUP_W // LANES, tm, LANES), F32)],
        compiler_params=pltpu.CompilerParams(
            dimension_semantics=("arbitrary", "arbitrary"), vmem_limit_bytes=VMEM_LIMIT),
        name="inproj_seq",
    )(x, *wts, c0, ra, rm, rp)


def _inproj_step_kernel(x_ref, nmix_ref, waqkv_ref, wmqk_ref, wmv_ref, wmo_ref, wga_ref, wgb_ref,
                        wgc_ref, wgr_ref, bc_ref, br_ref, cw_ref, cb_ref, cs_ref, ra_ref, rm_ref, rp_ref,
                        qa_ref, kv0_ref, kv1_ref, kv2_ref, qm_ref, km_ref, vm_ref, og_ref, ga_ref, gb_ref,
                        gc_ref, cn_ref):
    del wgr_ref, br_ref
    xn = _rms(x_ref[...], nmix_ref[...]).astype(BF16)
    ra, rm, rp = ra_ref[...], rm_ref[...], rp_ref[...]
    kv_refs = (kv0_ref, kv1_ref, kv2_ref)
    for g in range(N_GROUPS):
        cs = slice(g * GROUP_W, (g + 1) * GROUP_W)
        zq = _dot(xn, waqkv_ref[:, OFF_AQ + g * GROUP_W:OFF_AQ + (g + 1) * GROUP_W])
        qa_ref[:, cs] = _rope2(zq, ra, rm, rp) * (HEAD_DIM ** -0.5)
        zk = _dot(xn, waqkv_ref[:, OFF_AK + g * GROUP_W:OFF_AK + (g + 1) * GROUP_W])
        kv_refs[g][:, :GROUP_W] = _rope2(zk, ra, rm, rp)
        kv_refs[g][:, GROUP_W:] = _dot(xn, waqkv_ref[:, OFF_AV + g * GROUP_W:OFF_AV + (g + 1) * GROUP_W])

    for c in range(2 * M_INNER // 256):
        cs = slice(c * 256, (c + 1) * 256)
        pre = _dot(xn, wmqk_ref[:, cs])
        y = cb_ref[:, cs] + cw_ref[CONV_W - 1:CONV_W, cs] * pre
        for j in range(CONV_W - 1):
            y = y + cw_ref[j:j + 1, cs] * cs_ref[j, :, cs]
        for j in range(CONV_W - 2):
            cn_ref[j, :, cs] = cs_ref[j + 1, :, cs]
        cn_ref[CONV_W - 2, :, cs] = pre
        s = y * jax.nn.sigmoid(y)
        if c < M_INNER // 256:
            qm_ref[:, cs] = s
        else:
            km_ref[:, c * 256 - M_INNER:(c + 1) * 256 - M_INNER] = s * (M_DK ** -0.5)

    for c in range(M_INNER // 256):
        cs = slice(c * 256, (c + 1) * 256)
        vm_ref[:, cs] = _dot(xn, wmv_ref[:, cs])
        og_ref[:, cs] = jax.nn.sigmoid(_dot(xn, wmo_ref[:, cs]))
        ga_ref[:, cs] = jax.nn.sigmoid(_dot(xn, wga_ref[:, cs]))
        gb_ref[:, cs] = jax.nn.sigmoid(_dot(xn, wgb_ref[:, cs]))
    gc_ref[...] = _gate_cols(_dot(xn, wgc_ref[...]) + bc_ref[...])


def _inproj_step(x, conv_state, tables, wts):
    nb = x.shape[0]
    sds = lambda w: jax.ShapeDtypeStruct((nb, w), F32)
    out_shape = ([sds(A_QKV)] + [sds(2 * GROUP_W)] * N_GROUPS + [sds(M_INNER)] * 6 + [sds(LANES)]
                 + [jax.ShapeDtypeStruct((CONV_W - 1, nb, 2 * M_INNER), F32)])
    return pl.pallas_call(
        _inproj_step_kernel,
        out_shape=out_shape,
        compiler_params=pltpu.CompilerParams(vmem_limit_bytes=VMEM_LIMIT),
        name="inproj_step",
    )(x, *wts, conv_state, *tables)


def _attn_seq_kernel(q_ref, k_ref, v_ref, kp_ref, vp_ref, o_ref, lse_ref, *, tq):
    i = pl.program_id(2)
    lane_head = lax.broadcasted_iota(jnp.int32, (1, GROUP_W), 1) // HEAD_DIM
    head_mask = [(lane_head == h).astype(F32) for h in range(HPG)]
    qi = lax.broadcasted_iota(jnp.int32, (BAND, 2 * BAND), 0)
    kj = lax.broadcasted_iota(jnp.int32, (BAND, 2 * BAND), 1)
    band = (kj >= qi) & (kj <= qi + BAND)
    band_first = band & (kj >= jnp.where(i > 0, 0, BAND))

    for j in range(tq // BAND):
        valid = band_first if j == 0 else band
        q32 = q_ref[0, j * BAND:(j + 1) * BAND, :].astype(F32)
        if j == 0:
            kk = jnp.concatenate([kp_ref[0], k_ref[0, 0:BAND, :]], axis=0)
            vv = jnp.concatenate([vp_ref[0], v_ref[0, 0:BAND, :]], axis=0)
        else:
            kk = k_ref[0, (j - 1) * BAND:(j + 1) * BAND, :]
            vv = v_ref[0, (j - 1) * BAND:(j + 1) * BAND, :]
        o = jnp.zeros((BAND, GROUP_W), F32)
        lse = jnp.zeros((BAND, GROUP_W), F32)
        for h in range(HPG):
            qh = (q32 * head_mask[h]).astype(BF16)
            s = lax.dot_general(qh, kk, _NT, preferred_element_type=F32)
            s = jnp.where(valid, s, NEG)
            mx = jnp.max(s, axis=-1, keepdims=True)
            e = jnp.exp(s - mx)
            den = jnp.sum(e, axis=-1, keepdims=True)
            pv = _dot(e.astype(BF16), vv)
            o = o + (pv * (1.0 / den)) * head_mask[h]
            lse = lse + (mx + jnp.log(den)) * head_mask[h]
        o_ref[0, j * BAND:(j + 1) * BAND, :] = o
        lse_ref[0, j * BAND:(j + 1) * BAND, :] = lse


def _attn_seq(q, k, v, dil):
    B, nsub, _ = q.shape
    tq = min(ATT_TQ, nsub)
    in_spec = pl.BlockSpec((1, tq, GROUP_W), lambda b, r, i: (b, i, r))
    prev_spec = pl.BlockSpec((1, BAND, GROUP_W), lambda b, r, i: (b, jnp.maximum(i * (tq // BAND) - 1, 0), r))
    out_spec = pl.BlockSpec((1, tq, GROUP_W), lambda b, r, i: (b, i, r))
    o, lse = pl.pallas_call(
        functools.partial(_attn_seq_kernel, tq=tq),
        grid=(B, dil, nsub // tq),
        in_specs=[in_spec] * 3 + [prev_spec] * 2,
        out_specs=[out_spec] * 2,
        out_shape=[jax.ShapeDtypeStruct((B, nsub, dil * GROUP_W), F32)] * 2,
        compiler_params=pltpu.CompilerParams(
            dimension_semantics=("arbitrary", "arbitrary", "arbitrary"), vmem_limit_bytes=VMEM_LIMIT),
        name=f"attn_seq_d{dil}",
    )(q, k, v, k, v)
    return o, lse


def _attn_step_kernel(q_ref, kvn_ref, c_ref, o_ref, lse_ref, cn_ref, *, nb, dil, w):
    sub = lax.broadcasted_iota(jnp.int32, (SUBLANES, GROUP_W), 0)
    lane_head = lax.broadcasted_iota(jnp.int32, (SUBLANES, GROUP_W), 1) // HEAD_DIM
    own = sub == lane_head
    pos = lax.broadcasted_iota(jnp.int32, (SUBLANES, w), 1)
    valid = (pos % dil) == 0
    eye = (lax.broadcasted_iota(jnp.int32, (GROUP_W, GROUP_W), 0)
           == lax.broadcasted_iota(jnp.int32, (GROUP_W, GROUP_W), 1))
    last = lax.broadcasted_iota(jnp.int32, (GROUP_W, w), 1) == w - 1
    for b in range(nb):
        qrows = jnp.where(own, jnp.broadcast_to(q_ref[b], (SUBLANES, GROUP_W)), 0.0).astype(BF16)
        k_t = c_ref[b, 0]
        v_t = c_ref[b, 1]
        k_row = kvn_ref[b, :, :GROUP_W]
        v_row = kvn_ref[b, :, GROUP_W:]
        k_new = k_row.astype(BF16).astype(F32)
        v_new = v_row.astype(BF16).astype(F32)
        s = jnp.where(valid, _dot(qrows, k_t.astype(BF16)), NEG)
        s_new = jnp.sum(qrows.astype(F32) * k_new, axis=-1, keepdims=True)
        mx = jnp.maximum(jnp.max(s, axis=-1, keepdims=True), s_new)
        e = jnp.exp(s - mx)
        e_new = jnp.exp(s_new - mx)
        den = jnp.sum(e, axis=-1, keepdims=True) + e_new
        orow = (lax.dot_general((e / den).astype(BF16), v_t.astype(BF16), _NT, preferred_element_type=F32)
                + (e_new / den) * v_new)
        lse = mx + jnp.log(den)
        o_ref[b] = jnp.sum(jnp.where(own, orow, 0.0), axis=0, keepdims=True)
        lse_ref[b] = jnp.sum(jnp.where(own, jnp.broadcast_to(lse, (SUBLANES, GROUP_W)), 0.0),
                             axis=0, keepdims=True)
        for j, (old, row) in enumerate(((k_t, k_row), (v_t, v_row))):
            col = jnp.sum(jnp.where(eye, jnp.broadcast_to(row, (GROUP_W, GROUP_W)), 0.0), axis=1, keepdims=True)
            cn_ref[b, j] = jnp.where(last, col, pltpu.roll(old, w - 1, 1))


def _attn_step(q3, kvn3, cache_t, g, dil):
    NB, _, _, w = cache_t.shape
    assert w == BAND * dil, "the cache must hold exactly one full window"
    nb = max(1, min(STEP_NB_ATT, STEP_CACHE_BLOCK // (2 * GROUP_W * w * 4)))
    row = pl.BlockSpec((nb, 1, GROUP_W), lambda b: (b, 0, 0))
    cspec = pl.BlockSpec((nb, 2, GROUP_W, w), lambda b: (b, 0, 0, 0))
    return pl.pallas_call(
        functools.partial(_attn_step_kernel, nb=nb, dil=dil, w=w),
        grid=(NB // nb,),
        in_specs=[pl.BlockSpec((nb, 1, GROUP_W), lambda b: (b, 0, g)),
                  pl.BlockSpec((nb, 1, 2 * GROUP_W), lambda b: (b, 0, 0)),
                  cspec],
        out_specs=[row, row, cspec],
        out_shape=[jax.ShapeDtypeStruct((NB, 1, GROUP_W), F32)] * 2 + [jax.ShapeDtypeStruct(cache_t.shape, F32)],
        compiler_params=pltpu.CompilerParams(dimension_semantics=("arbitrary",), vmem_limit_bytes=VMEM_LIMIT),
        name=f"attn_step_d{dil}",
    )(q3, kvn3, cache_t)


def _mlstm_seq_kernel(q_ref, k_ref, v_ref, og_ref, gc_ref, gr_ref, mhn_ref,
                      ob_ref, c_out, n_out, m_out, c_s, n_s, m_s, *, L):
    c = pl.program_id(1)

    @pl.when(c == 0)
    def _():
        c_s[...] = jnp.zeros(c_s.shape, F32)
        n_s[...] = jnp.zeros(n_s.shape, F32)
        m_s[...] = jnp.zeros(m_s.shape, F32)

    gc = gc_ref[0]
    gr = gr_ref[0]
    ti = lax.broadcasted_iota(jnp.int32, (L, L), 0)
    si = lax.broadcasted_iota(jnp.int32, (L, L), 1)
    tril = ti >= si
    b_cols = jnp.dot(tril.astype(F32), gc, preferred_element_type=F32, precision=lax.Precision.HIGHEST)
    b_rows = jnp.dot(gr, (ti <= si).astype(F32), preferred_element_type=F32, precision=lax.Precision.HIGHEST)
    lane = lax.broadcasted_iota(jnp.int32, (1, LANES), 1)
    m_vec = jnp.zeros((1, LANES), F32)

    for h in range(M_HEADS):
        hs = slice(h * M_DK, (h + 1) * M_DK)
        b_col = b_cols[:, M_HEADS + h:M_HEADS + h + 1]
        i_col = gc[:, h:h + 1]
        b_row = b_rows[M_HEADS + h:M_HEADS + h + 1, :]
        i_row = gr[h:h + 1, :]
        m_prev = m_s[h, 0:1, 0:1]
        qh = q_ref[0, :, hs]
        kh = k_ref[0, :, hs]
        vh = v_ref[0, :, hs]
        c_prev = c_s[h]
        n_prev = n_s[h]

        dmat = jnp.where(tril, b_col - b_row + i_row, -jnp.inf)
        m_inter = b_col + m_prev
        m_t = jnp.maximum(m_inter, jnp.max(dmat, axis=-1, keepdims=True))
        w_intra = lax.dot_general(qh, kh, _NT, preferred_element_type=F32) * jnp.exp(dmat - m_t)
        w_inter = jnp.exp(m_inter - m_t)
        num = _dot(w_intra.astype(BF16), vh) + w_inter * _dot(qh, c_prev.astype(BF16))
        den = (jnp.sum(w_intra, axis=-1, keepdims=True)
               + w_inter * jnp.sum(qh.astype(F32) * n_prev, axis=-1, keepdims=True))
        hh = num / jnp.maximum(jnp.abs(den), jnp.exp(-m_t))
        hn = hh * lax.rsqrt(jnp.mean(hh * hh, axis=-1, keepdims=True) + EPS) * mhn_ref[h:h + 1, :]
        ob_ref[0, :, hs] = (hn * og_ref[0, :, hs].astype(F32)).astype(BF16)

        b_last = b_row[:, L - 1:L]
        m_new = jnp.maximum(b_last + m_prev, jnp.max(b_last - b_row + i_row, axis=-1, keepdims=True))
        wk = jnp.exp(b_last - b_col + i_col - m_new)
        decay = jnp.exp(b_last + m_prev - m_new)
        ks = kh.astype(F32) * wk
        c_new = decay * c_prev + lax.dot_general(ks.astype(BF16), vh, _TN, preferred_element_type=F32)
        n_new = decay * n_prev + jnp.sum(ks, axis=0, keepdims=True)
        c_s[h] = c_new
        n_s[h] = n_new
        m_s[h] = jnp.broadcast_to(m_new, (SUBLANES, LANES))
        c_out[0, h] = c_new
        n_out[0, h:h + 1, :] = n_new
        m_vec = jnp.where(lane == h, m_new, m_vec)
    m_out[0] = m_vec


def _mlstm_seq(qm, km, vm, og, gc, gr, mhn):
    B, S, _ = qm.shape
    L = M_CHUNK
    tok = pl.BlockSpec((1, L, M_INNER), lambda b, c: (b, c, 0))
    return pl.pallas_call(
        functools.partial(_mlstm_seq_kernel, L=L),
        grid=(B, S // L),
        in_specs=[tok, tok, tok, tok,
                  pl.BlockSpec((1, L, LANES), lambda b, c: (b, c, 0)),
                  pl.BlockSpec((1, SUBLANES, L), lambda b, c: (b, 0, c)),
                  pl.BlockSpec((M_HEADS, M_DK), lambda b, c: (0, 0))],
        out_specs=[tok,
                   pl.BlockSpec((1, M_HEADS, M_DK, M_DK), lambda b, c: (b, 0, 0, 0)),
                   pl.BlockSpec((1, M_HEADS, M_DK), lambda b, c: (b, 0, 0)),
                   pl.BlockSpec((1, 1, LANES), lambda b, c: (b, 0, 0))],
        out_shape=[jax.ShapeDtypeStruct((B, S, M_INNER), BF16),
                   jax.ShapeDtypeStruct((B, M_HEADS, M_DK, M_DK), F32),
                   jax.ShapeDtypeStruct((B, M_HEADS, M_DK), F32),
                   jax.ShapeDtypeStruct((B, 1, LANES), F32)],
        scratch_shapes=[pltpu.VMEM((M_HEADS, M_DK, M_DK), F32),
                        pltpu.VMEM((M_HEADS, 1, M_DK), F32),
                        pltpu.VMEM((M_HEADS, SUBLANES, LANES), F32)],
        compiler_params=pltpu.CompilerParams(
            dimension_semantics=("arbitrary", "arbitrary"), vmem_limit_bytes=VMEM_LIMIT),
        name="mlstm_seq",
    )(qm, km, vm, og, gc, gr, mhn)


def _mlstm_step_kernel(q_ref, k_ref, v_ref, og_ref, g_ref, c_ref, n_ref, m_ref, mhn_ref,
                       ob_ref, c_out, n_out, m_out, *, nb):
    row0 = lax.broadcasted_iota(jnp.int32, (SUBLANES, M_DK), 0) == 0
    lane = lax.broadcasted_iota(jnp.int32, (1, LANES), 1)
    for b in range(nb):
        gates = g_ref[b]
        m_all = m_ref[b]
        m_vec = jnp.zeros((1, LANES), F32)
        for h in range(M_HEADS):
            hs = slice(h * M_DK, (h + 1) * M_DK)
            ig = gates[:, h:h + 1]
            lf = gates[:, M_HEADS + h:M_HEADS + h + 1]
            m_prev = m_all[:, h:h + 1]
            qb = q_ref[b, :, hs].astype(BF16)
            kb = k_ref[b, :, hs].astype(BF16)
            vb = v_ref[b, :, hs].astype(BF16)
            qf, kf, vf = qb.astype(F32), kb.astype(F32), vb.astype(F32)
            c_prev = c_ref[b, h]
            n_prev = n_ref[b, h:h + 1, :]

            m_inter = lf + m_prev
            m_t = jnp.maximum(m_inter, ig)
            w_intra = jnp.sum(qf * kf, axis=-1, keepdims=True) * jnp.exp(ig - m_t)
            w_inter = jnp.exp(m_inter - m_t)
            q_c = _dot(jnp.broadcast_to(qb, (SUBLANES, M_DK)), c_prev.astype(BF16))[0:1, :]
            num = w_intra * vf + w_inter * q_c
            den = w_intra + w_inter * jnp.sum(qf * n_prev, axis=-1, keepdims=True)
            hh = num / jnp.maximum(jnp.abs(den), jnp.exp(-m_t))
            hn = hh * lax.rsqrt(jnp.mean(hh * hh, axis=-1, keepdims=True) + EPS) * mhn_ref[h:h + 1, :]
            ob_ref[b, :, hs] = hn * og_ref[b, :, hs]

            ks = kf * jnp.exp(ig - m_t)
            ks8 = jnp.where(row0, jnp.broadcast_to(ks, (SUBLANES, M_DK)), 0.0).astype(BF16)
            outer = lax.dot_general(ks8, jnp.broadcast_to(vb, (SUBLANES, M_DK)), _TN, preferred_element_type=F32)
            c_out[b, h] = w_inter * c_prev + outer
            n_out[b, h:h + 1, :] = w_inter * n_prev + ks
            m_vec = jnp.where(lane == h, m_t, m_vec)
        m_out[b] = m_vec


def _mlstm_step(qm, km, vm, og, gates, c0, n0, m0, mhn):
    NB = qm.shape[0]
    nb = STEP_NB_MLSTM
    tok = pl.BlockSpec((nb, 1, M_INNER), lambda b: (b, 0, 0))
    vec = pl.BlockSpec((nb, 1, LANES), lambda b: (b, 0, 0))
    cspec = pl.BlockSpec((nb, M_HEADS, M_DK, M_DK), lambda b: (b, 0, 0, 0))
    nspec = pl.BlockSpec((nb, M_HEADS, M_DK), lambda b: (b, 0, 0))
    return pl.pallas_call(
        functools.partial(_mlstm_step_kernel, nb=nb),
        grid=(NB // nb,),
        in_specs=[tok, tok, tok, tok, vec, cspec, nspec, vec, pl.BlockSpec((M_HEADS, M_DK), lambda b: (0, 0))],
        out_specs=[tok, cspec, nspec, vec],
        out_shape=[jax.ShapeDtypeStruct((NB, 1, M_INNER), F32),
                   jax.ShapeDtypeStruct(c0.shape, F32),
                   jax.ShapeDtypeStruct(n0.shape, F32),
                   jax.ShapeDtypeStruct((NB, 1, LANES), F32)],
        compiler_params=pltpu.CompilerParams(dimension_semantics=("arbitrary",), vmem_limit_bytes=VMEM_LIMIT),
        name="mlstm_step",
    )(qm, km, vm, og, gates, c0, n0, m0, mhn)


def _post_kernel(x_ref, o0_ref, o1_ref, o2_ref, l0_ref, l1_ref, l2_ref, ob_ref, ga_ref, gb_ref, p_ref,
                 wpa_ref, wpb_ref, wout_ref, nffn_ref, wg_ref, wu_ref, wd_ref, nple_ref, wpg_ref, wpp_ref,
                 nfin_ref, y_ref, *ubuf, dils):
    lead = (0,) * (len(x_ref.shape) - 2)
    rd = lambda r: r[lead] if lead else r[...]
    tm = x_ref.shape[-2]

    def undilate(ref, dil, slot):
        if dil == 1:
            return rd(ref)
        halves = range(GROUP_W // LANES)
        for r in range(dil):
            for c in halves:
                lo = r * GROUP_W + c * LANES
                ubuf[0][slot, c, pl.ds(r, tm // dil, stride=dil), :] = ref[0, :, lo:lo + LANES]
        return jnp.concatenate([ubuf[0][slot, c] for c in halves], axis=1)

    l0, l1, l2 = (undilate(r, d, 2 * g) for g, (r, d) in enumerate(zip((l0_ref, l1_ref, l2_ref), dils)))
    o0, o1, o2 = (undilate(r, d, 2 * g + 1) for g, (r, d) in enumerate(zip((o0_ref, o1_ref, o2_ref), dils)))
    lmax = jnp.maximum(jnp.maximum(l0, l1), l2)
    e0, e1, e2 = jnp.exp(l0 - lmax), jnp.exp(l1 - lmax), jnp.exp(l2 - lmax)
    o_a = (e0 * o0 + e1 * o1 + e2 * o2) / (e0 + e1 + e2)

    mix = (rd(ga_ref).astype(F32) * _dot(o_a.astype(BF16), wpa_ref[...])
           + rd(gb_ref).astype(F32) * _dot(rd(ob_ref).astype(BF16), wpb_ref[...]))
    h = rd(x_ref) + _dot(mix.astype(BF16), wout_ref[...])

    xf = _rms(h, nffn_ref[...]).astype(BF16)
    half = D_FF // 2
    for c in range(2):
        cs = slice(c * half, (c + 1) * half)
        gate = _dot(xf, wg_ref[:, cs])
        act = (gate * jax.nn.sigmoid(gate)) * _dot(xf, wu_ref[:, cs])
        h = h + _dot(act.astype(BF16), wd_ref[cs, :])

    xp = _rms(h, nple_ref[...]).astype(BF16)
    h = h + jax.nn.sigmoid(_dot(xp, wpg_ref[...])) * _dot(rd(p_ref).astype(BF16), wpp_ref[...])
    y = _rms(h, nfin_ref[...])
    if lead:
        y_ref[lead] = y
    else:
        y_ref[...] = y


def _post_seq(x, os_, ls_, ob, ga, gb, p, wts):
    B, S, _ = x.shape
    tm = TM_POST
    tok = lambda w: pl.BlockSpec((1, tm, w), lambda b, i: (b, i, 0))
    dils = tuple(dil for _, dil in DSWA_GROUPS)
    dil_specs = [pl.BlockSpec((1, tm // dil, dil * GROUP_W), lambda b, i: (b, i, 0)) for dil in dils]
    in_specs = ([tok(D_MODEL)] + dil_specs * 2 + [tok(M_INNER)] * 3 + [tok(PLE_DIM)]
                + [_const_spec(w.shape) for w in wts])
    return pl.pallas_call(
        functools.partial(_post_kernel, dils=dils),
        grid=(B, S // tm),
        in_specs=in_specs,
        out_specs=tok(D_MODEL),
        out_shape=jax.ShapeDtypeStruct((B, S, D_MODEL), F32),
        scratch_shapes=[pltpu.VMEM((2 * N_GROUPS, GROUP_W // LANES, tm, LANES), F32)],
        compiler_params=pltpu.CompilerParams(
            dimension_semantics=("arbitrary", "arbitrary"), vmem_limit_bytes=VMEM_LIMIT),
        name="post_seq",
    )(x, *os_, *ls_, ob, ga, gb, p, *wts)


def _post_step(x, os_, ls_, ob, ga, gb, p, wts):
    return pl.pallas_call(
        functools.partial(_post_kernel, dils=(1,) * N_GROUPS),
        out_shape=jax.ShapeDtypeStruct(x.shape, F32),
        compiler_params=pltpu.CompilerParams(vmem_limit_bytes=VMEM_LIMIT),
        name="post_step",
    )(x, *os_, *ls_, ob, ga, gb, p, *wts)


def _rope_tables(pos):
    t = pos.shape[0]
    inv = jnp.power(ROPE_THETA, -jnp.arange(ROPE_HALF, dtype=F32) / ROPE_HALF)
    ang = pos.astype(F32)[:, None] * inv[None, :]
    cos, sin = jnp.cos(ang), jnp.sin(ang)
    one = jnp.ones((t, HEAD_DIM - ROPE_DIM), F32)
    zero = jnp.zeros((t, HEAD_DIM - ROPE_DIM), F32)
    z8 = jnp.zeros((t, ROPE_HALF), F32)
    a = jnp.concatenate([cos, cos, one], axis=1)
    m = jnp.concatenate([-sin, z8, zero], axis=1)
    p = jnp.concatenate([z8, sin, zero], axis=1)
    rep = LANES // HEAD_DIM
    return tuple(jnp.tile(v, (1, rep)) for v in (a, m, p))


def kernel(x_prompt, x_sample, cache_kv_w128, cache_kv_w512, cache_kv_w2048, state_conv, state_C, state_n,
           state_m, p_prompt, p_sample, norm_mix, w_in, conv_w, conv_b, b_igate, b_fgate, mh_norm, w_proj_a,
           w_proj_b, w_out, norm_ffn, w_gate, w_up, w_down, norm_ple, w_ple_gate, w_ple_proj, norm_final):
    depth = w_in.shape[0]
    assert depth == 1, "single trunk layer"
    B, S, _ = x_prompt.shape
    NB = x_sample.shape[0]
    assert x_sample.shape[1] == 1
    caches = (cache_kv_w128, cache_kv_w512, cache_kv_w2048)
    assert S >= DSWA_GROUPS[-1][0] and S % (BAND * DSWA_GROUPS[-1][1]) == 0
    past = 0
    l = 0

    w = w_in[l]
    gate_w = w[:, OFF_MI:OFF_GA]
    gate_b = jnp.concatenate([b_igate[l], b_fgate[l]])
    in_wts = (
        norm_mix[l][None, :],
        w[:, OFF_AQ:OFF_MQ].astype(BF16),
        w[:, OFF_MQ:OFF_MV].astype(BF16),
        w[:, OFF_MV:OFF_MO].astype(BF16),
        w[:, OFF_MO:OFF_MI].astype(BF16),
        w[:, OFF_GA:OFF_GB].astype(BF16),
        w[:, OFF_GB:IN_COLS].astype(BF16),
        jnp.pad(gate_w, ((0, 0), (0, LANES - 2 * M_HEADS))).astype(BF16),
        gate_w.T.astype(BF16),
        jnp.pad(gate_b, (0, LANES - 2 * M_HEADS))[None, :],
        gate_b[:, None],
        conv_w[l],
        conv_b[l][None, :],
    )
    post_wts = (
        w_proj_a[l].astype(BF16), w_proj_b[l].astype(BF16), w_out[l].astype(BF16), norm_ffn[l][None, :],
        w_gate[l].astype(BF16), w_up[l].astype(BF16), w_down[l].astype(BF16), norm_ple[l][None, :],
        w_ple_gate[l].astype(BF16), w_ple_proj[l].astype(BF16), norm_final[None, :],
    )
    mhn = mh_norm[l]

    c0 = jnp.zeros((B, SUBLANES, 2 * M_INNER), F32)
    tabs_p = _rope_tables(past + jnp.arange(S, dtype=jnp.int32))
    (*qkv, qm, km, vm, og, ga, gb, gc, gr, kv0, kv1, kv2, cn) = _inproj_seq(x_prompt, c0, tabs_p, in_wts)
    os_p, ls_p = [], []
    for g, (_, dil) in enumerate(DSWA_GROUPS):
        o, lse = _attn_seq(*qkv[3 * g:3 * g + 3], dil)
        os_p.append(o)
        ls_p.append(lse)
    ob, c_p, n_p, m_p = _mlstm_seq(qm, km, vm, og, gc, gr, mhn)
    y_prompt = _post_seq(x_prompt, os_p, ls_p, ob, ga, gb, p_prompt[l], post_wts)

    tabs_s = _rope_tables(jnp.full((NB,), PAST_LEN, jnp.int32))
    conv_s = jnp.swapaxes(state_conv[l], 0, 1)
    (qa_s, kvn0, kvn1, kvn2, qm_s, km_s, vm_s, og_s, ga_s, gb_s, gc_s, cn_s) = _inproj_step(
        x_sample[:, 0, :], conv_s, tabs_s, in_wts)
    kvn = (kvn0, kvn1, kvn2)
    caches_t = [jnp.transpose(c[l], (0, 2, 3, 4, 1)).reshape(NB, 2, GROUP_W, c.shape[2]) for c in caches]
    os_s, ls_s, kv_s = [], [], []
    for g, (_, dil) in enumerate(DSWA_GROUPS):
        o, lse, cache_new = _attn_step(qa_s[:, None, :], kvn[g][:, None, :], caches_t[g], g, dil)
        os_s.append(o[:, 0, :])
        ls_s.append(lse[:, 0, :])
        kv_s.append(jnp.transpose(cache_new.reshape(NB, 2, HPG, HEAD_DIM, -1), (0, 4, 1, 2, 3)))
    m0 = jnp.pad(state_m[l], ((0, 0), (0, LANES - M_HEADS)))[:, None, :]
    ob_s, c_s, n_s, m_s = _mlstm_step(qm_s[:, None, :], km_s[:, None, :], vm_s[:, None, :], og_s[:, None, :],
                                      gc_s[:, None, :], state_C[l], state_n[l], m0, mhn)
    y_sample = _post_step(x_sample[:, 0, :], os_s, ls_s, ob_s[:, 0, :], ga_s, gb_s, p_sample[l][:, 0, :], post_wts)

    kv_shape = lambda n, w: (1, n, w, 2, HPG, HEAD_DIM)
    return (
        y_prompt, y_sample[:, None, :],
        kv0.reshape(kv_shape(B, DSWA_GROUPS[0][0])), kv_s[0].reshape(kv_shape(NB, DSWA_GROUPS[0][0])),
        kv1.reshape(kv_shape(B, DSWA_GROUPS[1][0])), kv_s[1].reshape(kv_shape(NB, DSWA_GROUPS[1][0])),
        kv2.reshape(kv_shape(B, DSWA_GROUPS[2][0])), kv_s[2].reshape(kv_shape(NB, DSWA_GROUPS[2][0])),
        cn[:, SUBLANES - (CONV_W - 1):, :][None], jnp.swapaxes(cn_s, 0, 1)[None],
        c_p[None], c_s[None],
        n_p[None], n_s[None],
        m_p[:, 0, :M_HEADS][None], m_s[:, 0, :M_HEADS][None],
    )
```

```python
import functools

import jax
import jax.numpy as jnp
from jax import lax
from jax.experimental import pallas as pl
from jax.experimental.pallas import tpu as pltpu

F32 = jnp.float32
BF16 = jnp.bfloat16

D_MODEL = 1024
PAST_LEN = 16384
HEAD_DIM = 64
ROPE_DIM = HEAD_DIM // 4
ROPE_HALF = ROPE_DIM // 2
ROPE_THETA = 500000.0
DSWA_GROUPS = ((128, 1), (512, 4), (2048, 16))
HPG = 4
N_GROUPS = len(DSWA_GROUPS)
GROUP_W = HPG * HEAD_DIM
A_QKV = N_GROUPS * GROUP_W
BAND = 128
M_HEADS = 4
M_INNER = D_MODEL
M_DK = M_INNER // M_HEADS
CONV_W = 4
D_FF = 2816
PLE_DIM = 256
EPS = 1e-6
NEG = -1e30

OFF_AQ = 0
OFF_AK = OFF_AQ + A_QKV
OFF_AV = OFF_AK + A_QKV
OFF_MQ = OFF_AV + A_QKV
OFF_MK = OFF_MQ + M_INNER
OFF_MV = OFF_MK + M_INNER
OFF_MO = OFF_MV + M_INNER
OFF_MI = OFF_MO + M_INNER
OFF_MF = OFF_MI + M_HEADS
OFF_GA = OFF_MF + M_HEADS
OFF_GB = OFF_GA + D_MODEL
IN_COLS = OFF_GB + D_MODEL

LANES = 128
SUBLANES = 8
VMEM_LIMIT = 56 * 1024 * 1024

TM_IN = 512
TM_POST = 256
M_CHUNK = 256
ATT_TQ = 512
STEP_NB_ATT = 8
STEP_CACHE_BLOCK = 4 * 1024 * 1024
STEP_NB_MLSTM = 2

_NT = (((1,), (1,)), ((), ()))
_TN = (((0,), (0,)), ((), ()))


def _dot(a, b):
    return jnp.dot(a, b, preferred_element_type=F32)


def _rms(x, g):
    return x * lax.rsqrt(jnp.mean(x * x, axis=-1, keepdims=True) + EPS) * g


def _log_sigmoid(x):
    return jnp.minimum(x, 0.0) - jnp.log1p(jnp.exp(-jnp.abs(x)))


def _rope(t, ra, rm, rp):
    return t * ra + pltpu.roll(t, LANES - ROPE_HALF, 1) * rm + pltpu.roll(t, ROPE_HALF, 1) * rp


def _rope2(t, ra, rm, rp):
    return jnp.concatenate([_rope(t[:, :LANES], ra, rm, rp), _rope(t[:, LANES:], ra, rm, rp)], axis=1)


def _gate_cols(z):
    lane = lax.broadcasted_iota(jnp.int32, z.shape, 1)
    return jnp.where(lane < M_HEADS, z, _log_sigmoid(z))


def _inproj_seq_kernel(x_ref, nmix_ref, waqkv_ref, wmqk_ref, wmv_ref, wmo_ref, wga_ref, wgb_ref,
                       wgc_ref, wgr_ref, bc_ref, br_ref, cw_ref, cb_ref, c0_ref, ra_ref, rm_ref, rp_ref,
                       q0_ref, k0_ref, v0_ref, q1_ref, k1_ref, v1_ref, q2_ref, k2_ref, v2_ref,
                       qm_ref, km_ref, vm_ref, og_ref, ga_ref, gb_ref,
                       gc_ref, gr_ref, kv0_ref, kv1_ref, kv2_ref, cn_ref, cbuf, dbuf, *, tm, ns):
    i = pl.program_id(1)
    xn = _rms(x_ref[0], nmix_ref[...]).astype(BF16)
    ra, rm, rp = ra_ref[...], rm_ref[...], rp_ref[...]
    kv_refs = (kv0_ref, kv1_ref, kv2_ref)
    qkv_refs = ((q0_ref, k0_ref, v0_ref), (q1_ref, k1_ref, v1_ref), (q2_ref, k2_ref, v2_ref))

    def put_dilated(ref, val, dil, slot):
        if dil == 1:
            ref[0] = val.astype(BF16)
            return
        for c in range(GROUP_W // LANES):
            dbuf[slot, c] = val[:, c * LANES:(c + 1) * LANES]
        for r in range(dil):
            for c in range(GROUP_W // LANES):
                lo = r * GROUP_W + c * LANES
                ref[0, :, lo:lo + LANES] = dbuf[slot, c, pl.ds(r, tm // dil, stride=dil), :].astype(BF16)

    for g, (win, dil) in enumerate(DSWA_GROUPS):
        zq = _dot(xn, waqkv_ref[:, OFF_AQ + g * GROUP_W:OFF_AQ + (g + 1) * GROUP_W])
        put_dilated(qkv_refs[g][0], _rope2(zq, ra, rm, rp) * (HEAD_DIM ** -0.5), dil, 3 * g)
        zk = _dot(xn, waqkv_ref[:, OFF_AK + g * GROUP_W:OFF_AK + (g + 1) * GROUP_W])
        kr = _rope2(zk, ra, rm, rp)
        put_dilated(qkv_refs[g][1], kr, dil, 3 * g + 1)
        zv = _dot(xn, waqkv_ref[:, OFF_AV + g * GROUP_W:OFF_AV + (g + 1) * GROUP_W])
        put_dilated(qkv_refs[g][2], zv, dil, 3 * g + 2)
        rows = min(win, tm)
        nblk = win // rows

        @pl.when(i >= ns - nblk)
        def _():
            kv_refs[g][0, :, :GROUP_W] = kr[tm - rows:, :]
            kv_refs[g][0, :, GROUP_W:] = zv[tm - rows:, :]

    @pl.when(i == 0)
    def _():
        cbuf[...] = c0_ref[0]

    sub3 = lax.broadcasted_iota(jnp.int32, (tm // SUBLANES, SUBLANES, 256), 1)

    for c in range(2 * M_INNER // 256):
        cs = slice(c * 256, (c + 1) * 256)
        pre = _dot(xn, wmqk_ref[:, cs])
        ext = jnp.concatenate([cbuf[:, cs], pre], axis=0).reshape(tm // SUBLANES + 1, SUBLANES, 256)
        y = cb_ref[:, cs] + cw_ref[CONV_W - 1:CONV_W, cs] * pre
        for d in range(1, CONV_W):
            rot = pltpu.roll(ext, d, 1)
            back = jnp.where(sub3 < d, rot[:-1], rot[1:]).reshape(tm, 256)
            y = y + cw_ref[CONV_W - 1 - d:CONV_W - d, cs] * back
        tail = pre[tm - SUBLANES:, :]
        cbuf[:, cs] = tail
        cn_ref[0, :, cs] = tail
        s = y * jax.nn.sigmoid(y)
        if c < M_INNER // 256:
            qm_ref[0, :, cs] = s.astype(BF16)
        else:
            km_ref[0, :, c * 256 - M_INNER:(c + 1) * 256 - M_INNER] = (s * (M_DK ** -0.5)).astype(BF16)

    for c in range(M_INNER // 256):
        cs = slice(c * 256, (c + 1) * 256)
        vm_ref[0, :, cs] = _dot(xn, wmv_ref[:, cs]).astype(BF16)
        og_ref[0, :, cs] = jax.nn.sigmoid(_dot(xn, wmo_ref[:, cs])).astype(BF16)
        ga_ref[0, :, cs] = jax.nn.sigmoid(_dot(xn, wga_ref[:, cs])).astype(BF16)
        gb_ref[0, :, cs] = jax.nn.sigmoid(_dot(xn, wgb_ref[:, cs])).astype(BF16)

    gc_ref[0] = _gate_cols(_dot(xn, wgc_ref[...]) + bc_ref[...])
    zr = lax.dot_general(wgr_ref[...], xn, _NT, preferred_element_type=F32) + br_ref[...]
    sub = lax.broadcasted_iota(jnp.int32, zr.shape, 0)
    gr_ref[0] = jnp.where(sub < M_HEADS, zr, _log_sigmoid(zr))


def _const_spec(shape):
    nd = len(shape)
    return pl.BlockSpec(shape, lambda *_: (0,) * nd, pipeline_mode=pl.Buffered(1))


def _inproj_seq(x, c0, tables, wts):
    B, S, _ = x.shape
    tm = TM_IN
    ns = S // tm
    ra, rm, rp = tables

    def kv_spec(win):
        rows = min(win, tm)
        nblk = win // rows
        return pl.BlockSpec((1, rows, 2 * GROUP_W), lambda b, i: (b, jnp.maximum(i - (ns - nblk), 0), 0))

    tok = lambda w: pl.BlockSpec((1, tm, w), lambda b, i: (b, i, 0))
    tab = pl.BlockSpec((tm, LANES), lambda b, i: (i, 0))
    in_specs = ([tok(D_MODEL)] + [_const_spec(w.shape) for w in wts]
                + [pl.BlockSpec((1, SUBLANES, 2 * M_INNER), lambda b, i: (b, 0, 0)), tab, tab, tab])
    dil_shape = [jax.ShapeDtypeStruct((B, S // dil, dil * GROUP_W), BF16) for _, dil in DSWA_GROUPS for _ in range(3)]
    dil_specs = [pl.BlockSpec((1, tm // dil, dil * GROUP_W), lambda b, i: (b, i, 0))
                 for _, dil in DSWA_GROUPS for _ in range(3)]
    out_shape = (dil_shape
                 + [jax.ShapeDtypeStruct((B, S, M_INNER), BF16)] * 6
                 + [jax.ShapeDtypeStruct((B, S, LANES), F32), jax.ShapeDtypeStruct((B, SUBLANES, S), F32)]
                 + [jax.ShapeDtypeStruct((B, win, 2 * GROUP_W), F32) for win, _ in DSWA_GROUPS]
                 + [jax.ShapeDtypeStruct((B, SUBLANES, 2 * M_INNER), F32)])
    out_specs = (dil_specs + [tok(M_INNER)] * 6
                 + [tok(LANES), pl.BlockSpec((1, SUBLANES, tm), lambda b, i: (b, 0, i))]
                 + [kv_spec(win) for win, _ in DSWA_GROUPS]
                 + [pl.BlockSpec((1, SUBLANES, 2 * M_INNER), lambda b, i: (b, 0, 0))])
    return pl.pallas_call(
        functools.partial(_inproj_seq_kernel, tm=tm, ns=ns),
        grid=(B, ns),
        in_specs=in_specs,
        out_specs=out_specs,
        out_shape=out_shape,
        scratch_shapes=[pltpu.VMEM((SUBLANES, 2 * M_INNER), F32),
                        pltpu.VMEM((3 * N_GROUPS, GROUP_W // LANES, tm, LANES), F32)],
        compiler_params=pltpu.CompilerParams(
            dimension_semantics=("arbitrary", "arbitrary"), vmem_limit_bytes=VMEM_LIMIT),
        name="inproj_seq",
    )(x, *wts, c0, ra, rm, rp)


def _inproj_step_kernel(x_ref, nmix_ref, waqkv_ref, wmqk_ref, wmv_ref, wmo_ref, wga_ref, wgb_ref,
                        wgc_ref, wgr_ref, bc_ref, br_ref, cw_ref, cb_ref, cs_ref, ra_ref, rm_ref, rp_ref,
                        qa_ref, kv0_ref, kv1_ref, kv2_ref, qm_ref, km_ref, vm_ref, og_ref, ga_ref, gb_ref,
                        gc_ref, cn_ref):
    del wgr_ref, br_ref
    xn = _rms(x_ref[...], nmix_ref[...]).astype(BF16)
    ra, rm, rp = ra_ref[...], rm_ref[...], rp_ref[...]
    kv_refs = (kv0_ref, kv1_ref, kv2_ref)
    for g in range(N_GROUPS):
        cs = slice(g * GROUP_W, (g + 1) * GROUP_W)
        zq = _dot(xn, waqkv_ref[:, OFF_AQ + g * GROUP_W:OFF_AQ + (g + 1) * GROUP_W])
        qa_ref[:, cs] = _rope2(zq, ra, rm, rp) * (HEAD_DIM ** -0.5)
        zk = _dot(xn, waqkv_ref[:, OFF_AK + g * GROUP_W:OFF_AK + (g + 1) * GROUP_W])
        kv_refs[g][:, :GROUP_W] = _rope2(zk, ra, rm, rp)
        kv_refs[g][:, GROUP_W:] = _dot(xn, waqkv_ref[:, OFF_AV + g * GROUP_W:OFF_AV + (g + 1) * GROUP_W])

    for c in range(2 * M_INNER // 256):
        cs = slice(c * 256, (c + 1) * 256)
        pre = _dot(xn, wmqk_ref[:, cs])
        y = cb_ref[:, cs] + cw_ref[CONV_W - 1:CONV_W, cs] * pre
        for j in range(CONV_W - 1):
            y = y + cw_ref[j:j + 1, cs] * cs_ref[j, :, cs]
        for j in range(CONV_W - 2):
            cn_ref[j, :, cs] = cs_ref[j + 1, :, cs]
        cn_ref[CONV_W - 2, :, cs] = pre
        s = y * jax.nn.sigmoid(y)
        if c < M_INNER // 256:
            qm_ref[:, cs] = s
        else:
            km_ref[:, c * 256 - M_INNER:(c + 1) * 256 - M_INNER] = s * (M_DK ** -0.5)

    for c in range(M_INNER // 256):
        cs = slice(c * 256, (c + 1) * 256)
        vm_ref[:, cs] = _dot(xn, wmv_ref[:, cs])
        og_ref[:, cs] = jax.nn.sigmoid(_dot(xn, wmo_ref[:, cs]))
        ga_ref[:, cs] = jax.nn.sigmoid(_dot(xn, wga_ref[:, cs]))
        gb_ref[:, cs] = jax.nn.sigmoid(_dot(xn, wgb_ref[:, cs]))
    gc_ref[...] = _gate_cols(_dot(xn, wgc_ref[...]) + bc_ref[...])


def _inproj_step(x, conv_state, tables, wts):
    nb = x.shape[0]
    sds = lambda w: jax.ShapeDtypeStruct((nb, w), F32)
    out_shape = ([sds(A_QKV)] + [sds(2 * GROUP_W)] * N_GROUPS + [sds(M_INNER)] * 6 + [sds(LANES)]
                 + [jax.ShapeDtypeStruct((CONV_W - 1, nb, 2 * M_INNER), F32)])
    return pl.pallas_call(
        _inproj_step_kernel,
        out_shape=out_shape,
        compiler_params=pltpu.CompilerParams(vmem_limit_bytes=VMEM_LIMIT),
        name="inproj_step",
    )(x, *wts, conv_state, *tables)


def _attn_seq_kernel(q_ref, k_ref, v_ref, kp_ref, vp_ref, o_ref, lse_ref, *, tq):
    i = pl.program_id(2)
    lane_head = lax.broadcasted_iota(jnp.int32, (1, GROUP_W), 1) // HEAD_DIM
    head_mask = [jnp.where(lane_head == h, 1.0, 0.0).astype(BF16) for h in range(HPG)]
    lane_head_full = lax.broadcasted_iota(jnp.int32, (BAND, GROUP_W), 1) // HEAD_DIM
    qi = lax.broadcasted_iota(jnp.int32, (BAND, 2 * BAND), 0)
    kj = lax.broadcasted_iota(jnp.int32, (BAND, 2 * BAND), 1)
    band = (kj >= qi) & (kj <= qi + BAND)
    first_lo = jnp.where(i > 0, 0, BAND)
    bias = jnp.where(band, 0.0, NEG)
    bias_first = jnp.where(band & (kj >= first_lo), 0.0, NEG)

    for j in range(tq // BAND):
        b1 = bias_first if j == 0 else bias
        q = q_ref[0, j * BAND:(j + 1) * BAND, :]
        if j == 0:
            kk = jnp.concatenate([kp_ref[0], k_ref[0, 0:BAND, :]], axis=0)
            vv = jnp.concatenate([vp_ref[0], v_ref[0, 0:BAND, :]], axis=0)
        else:
            kk = k_ref[0, (j - 1) * BAND:(j + 1) * BAND, :]
            vv = v_ref[0, (j - 1) * BAND:(j + 1) * BAND, :]
        qblk = jnp.concatenate([q * head_mask[h] for h in range(HPG)], axis=0)
        s = lax.dot_general(qblk, kk, _NT, preferred_element_type=F32) + jnp.concatenate([b1] * HPG, axis=0)
        mx = jnp.max(s, axis=-1, keepdims=True)
        e = jnp.exp(s - mx)
        den = jnp.sum(e, axis=-1, keepdims=True)
        pv = _dot(e.astype(BF16), vv) * (1.0 / den)
        lse_rows = mx + jnp.log(den)
        o = pv[0:BAND]
        lse = jnp.broadcast_to(lse_rows[0:BAND], (BAND, GROUP_W))
        for h in range(1, HPG):
            own = lane_head_full == h
            o = jnp.where(own, pv[h * BAND:(h + 1) * BAND], o)
            lse = jnp.where(own, lse_rows[h * BAND:(h + 1) * BAND], lse)
        o_ref[0, j * BAND:(j + 1) * BAND, :] = o
        lse_ref[0, j * BAND:(j + 1) * BAND, :] = lse


def _attn_seq(q, k, v, dil):
    B, nsub, _ = q.shape
    tq = min(ATT_TQ, nsub)
    in_spec = pl.BlockSpec((1, tq, GROUP_W), lambda b, r, i: (b, i, r))
    prev_spec = pl.BlockSpec((1, BAND, GROUP_W), lambda b, r, i: (b, jnp.maximum(i * (tq // BAND) - 1, 0), r))
    out_spec = pl.BlockSpec((1, tq, GROUP_W), lambda b, r, i: (b, i, r))
    o, lse = pl.pallas_call(
        functools.partial(_attn_seq_kernel, tq=tq),
        grid=(B, dil, nsub // tq),
        in_specs=[in_spec] * 3 + [prev_spec] * 2,
        out_specs=[out_spec] * 2,
        out_shape=[jax.ShapeDtypeStruct((B, nsub, dil * GROUP_W), F32)] * 2,
        compiler_params=pltpu.CompilerParams(
            dimension_semantics=("arbitrary", "arbitrary", "arbitrary"), vmem_limit_bytes=VMEM_LIMIT),
        name=f"attn_seq_d{dil}",
    )(q, k, v, k, v)
    return o, lse


def _attn_step_kernel(q_ref, kvn_ref, c_ref, o_ref, lse_ref, cn_ref, *, nb, dil, w):
    sub = lax.broadcasted_iota(jnp.int32, (SUBLANES, GROUP_W), 0)
    lane_head = lax.broadcasted_iota(jnp.int32, (SUBLANES, GROUP_W), 1) // HEAD_DIM
    own = sub == lane_head
    pos = lax.broadcasted_iota(jnp.int32, (SUBLANES, w), 1)
    valid = (pos % dil) == 0
    eye = (lax.broadcasted_iota(jnp.int32, (GROUP_W, GROUP_W), 0)
           == lax.broadcasted_iota(jnp.int32, (GROUP_W, GROUP_W), 1))
    last = lax.broadcasted_iota(jnp.int32, (GROUP_W, w), 1) == w - 1
    for b in range(nb):
        qrows = jnp.where(own, jnp.broadcast_to(q_ref[b], (SUBLANES, GROUP_W)), 0.0).astype(BF16)
        k_t = c_ref[b, 0]
        v_t = c_ref[b, 1]
        k_row = kvn_ref[b, :, :GROUP_W]
        v_row = kvn_ref[b, :, GROUP_W:]
        k_new = k_row.astype(BF16).astype(F32)
        v_new = v_row.astype(BF16).astype(F32)
        s = jnp.where(valid, _dot(qrows, k_t.astype(BF16)), NEG)
        s_new = jnp.sum(qrows.astype(F32) * k_new, axis=-1, keepdims=True)
        mx = jnp.maximum(jnp.max(s, axis=-1, keepdims=True), s_new)
        e = jnp.exp(s - mx)
        e_new = jnp.exp(s_new - mx)
        den = jnp.sum(e, axis=-1, keepdims=True) + e_new
        orow = (lax.dot_general((e / den).astype(BF16), v_t.astype(BF16), _NT, preferred_element_type=F32)
                + (e_new / den) * v_new)
        lse = mx + jnp.log(den)
        o_ref[b] = jnp.sum(jnp.where(own, orow, 0.0), axis=0, keepdims=True)
        lse_ref[b] = jnp.sum(jnp.where(own, jnp.broadcast_to(lse, (SUBLANES, GROUP_W)), 0.0),
                             axis=0, keepdims=True)
        for j, (old, row) in enumerate(((k_t, k_row), (v_t, v_row))):
            col = jnp.sum(jnp.where(eye, jnp.broadcast_to(row, (GROUP_W, GROUP_W)), 0.0), axis=1, keepdims=True)
            cn_ref[b, j] = jnp.where(last, col, pltpu.roll(old, w - 1, 1))


def _attn_step(q3, kvn3, cache_t, g, dil):
    NB, _, _, w = cache_t.shape
    assert w == BAND * dil, "the cache must hold exactly one full window"
    nb = max(1, min(STEP_NB_ATT, STEP_CACHE_BLOCK // (2 * GROUP_W * w * 4)))
    row = pl.BlockSpec((nb, 1, GROUP_W), lambda b: (b, 0, 0))
    cspec = pl.BlockSpec((nb, 2, GROUP_W, w), lambda b: (b, 0, 0, 0))
    return pl.pallas_call(
        functools.partial(_attn_step_kernel, nb=nb, dil=dil, w=w),
        grid=(NB // nb,),
        in_specs=[pl.BlockSpec((nb, 1, GROUP_W), lambda b: (b, 0, g)),
                  pl.BlockSpec((nb, 1, 2 * GROUP_W), lambda b: (b, 0, 0)),
                  cspec],
        out_specs=[row, row, cspec],
        out_shape=[jax.ShapeDtypeStruct((NB, 1, GROUP_W), F32)] * 2 + [jax.ShapeDtypeStruct(cache_t.shape, F32)],
        compiler_params=pltpu.CompilerParams(dimension_semantics=("arbitrary",), vmem_limit_bytes=VMEM_LIMIT),
        name=f"attn_step_d{dil}",
    )(q3, kvn3, cache_t)


def _mlstm_seq_kernel(q_ref, k_ref, v_ref, og_ref, gc_ref, gr_ref, mhn_ref,
                      ob_ref, c_out, n_out, m_out, c_s, n_s, m_s, *, L):
    c = pl.program_id(1)

    @pl.when(c == 0)
    def _():
        c_s[...] = jnp.zeros(c_s.shape, F32)
        n_s[...] = jnp.zeros(n_s.shape, F32)
        m_s[...] = jnp.zeros(m_s.shape, F32)

    gc = gc_ref[0]
    gr = gr_ref[0]
    ti = lax.broadcasted_iota(jnp.int32, (L, L), 0)
    si = lax.broadcasted_iota(jnp.int32, (L, L), 1)
    tril = ti >= si
    b_cols = jnp.dot(tril.astype(F32), gc, preferred_element_type=F32, precision=lax.Precision.HIGHEST)
    b_rows = jnp.dot(gr, (ti <= si).astype(F32), preferred_element_type=F32, precision=lax.Precision.HIGHEST)
    lane = lax.broadcasted_iota(jnp.int32, (1, LANES), 1)
    m_vec = jnp.zeros((1, LANES), F32)

    for h in range(M_HEADS):
        hs = slice(h * M_DK, (h + 1) * M_DK)
        b_col = b_cols[:, M_HEADS + h:M_HEADS + h + 1]
        i_col = gc[:, h:h + 1]
        b_row = b_rows[M_HEADS + h:M_HEADS + h + 1, :]
        i_row = gr[h:h + 1, :]
        m_prev = m_s[h, 0:1, 0:1]
        qh = q_ref[0, :, hs]
        kh = k_ref[0, :, hs]
        vh = v_ref[0, :, hs]
        c_prev = c_s[h]
        n_prev = n_s[h]

        dmat = jnp.where(tril, b_col - b_row + i_row, -jnp.inf)
        m_inter = b_col + m_prev
        m_t = jnp.maximum(m_inter, jnp.max(dmat, axis=-1, keepdims=True))
        w_intra = lax.dot_general(qh, kh, _NT, preferred_element_type=F32) * jnp.exp(dmat - m_t)
        w_inter = jnp.exp(m_inter - m_t)
        num = _dot(w_intra.astype(BF16), vh) + w_inter * _dot(qh, c_prev.astype(BF16))
        den = (jnp.sum(w_intra, axis=-1, keepdims=True)
               + w_inter * jnp.sum(qh.astype(F32) * n_prev, axis=-1, keepdims=True))
        hh = num / jnp.maximum(jnp.abs(den), jnp.exp(-m_t))
        hn = hh * lax.rsqrt(jnp.mean(hh * hh, axis=-1, keepdims=True) + EPS) * mhn_ref[h:h + 1, :]
        ob_ref[0, :, hs] = (hn * og_ref[0, :, hs].astype(F32)).astype(BF16)

        b_last = b_row[:, L - 1:L]
        m_new = jnp.maximum(b_last + m_prev, jnp.max(b_last - b_row + i_row, axis=-1, keepdims=True))
        wk = jnp.exp(b_last - b_col + i_col - m_new)
        decay = jnp.exp(b_last + m_prev - m_new)
        ks = kh.astype(F32) * wk
        c_new = decay * c_prev + lax.dot_general(ks.astype(BF16), vh, _TN, preferred_element_type=F32)
        n_new = decay * n_prev + jnp.sum(ks, axis=0, keepdims=True)
        c_s[h] = c_new
        n_s[h] = n_new
        m_s[h] = jnp.broadcast_to(m_new, (SUBLANES, LANES))
        c_out[0, h] = c_new
        n_out[0, h:h + 1, :] = n_new
        m_vec = jnp.where(lane == h, m_new, m_vec)
    m_out[0] = m_vec


def _mlstm_seq(qm, km, vm, og, gc, gr, mhn):
    B, S, _ = qm.shape
    L = M_CHUNK
    tok = pl.BlockSpec((1, L, M_INNER), lambda b, c: (b, c, 0))
    return pl.pallas_call(
        functools.partial(_mlstm_seq_kernel, L=L),
        grid=(B, S // L),
        in_specs=[tok, tok, tok, tok,
                  pl.BlockSpec((1, L, LANES), lambda b, c: (b, c, 0)),
                  pl.BlockSpec((1, SUBLANES, L), lambda b, c: (b, 0, c)),
                  pl.BlockSpec((M_HEADS, M_DK), lambda b, c: (0, 0))],
        out_specs=[tok,
                   pl.BlockSpec((1, M_HEADS, M_DK, M_DK), lambda b, c: (b, 0, 0, 0)),
                   pl.BlockSpec((1, M_HEADS, M_DK), lambda b, c: (b, 0, 0)),
                   pl.BlockSpec((1, 1, LANES), lambda b, c: (b, 0, 0))],
        out_shape=[jax.ShapeDtypeStruct((B, S, M_INNER), BF16),
                   jax.ShapeDtypeStruct((B, M_HEADS, M_DK, M_DK), F32),
                   jax.ShapeDtypeStruct((B, M_HEADS, M_DK), F32),
                   jax.ShapeDtypeStruct((B, 1, LANES), F32)],
        scratch_shapes=[pltpu.VMEM((M_HEADS, M_DK, M_DK), F32),
                        pltpu.VMEM((M_HEADS, 1, M_DK), F32),
                        pltpu.VMEM((M_HEADS, SUBLANES, LANES), F32)],
        compiler_params=pltpu.CompilerParams(
            dimension_semantics=("arbitrary", "arbitrary"), vmem_limit_bytes=VMEM_LIMIT),
        name="mlstm_seq",
    )(qm, km, vm, og, gc, gr, mhn)


def _mlstm_step_kernel(q_ref, k_ref, v_ref, og_ref, g_ref, c_ref, n_ref, m_ref, mhn_ref,
                       ob_ref, c_out, n_out, m_out, *, nb):
    row0 = lax.broadcasted_iota(jnp.int32, (SUBLANES, M_DK), 0) == 0
    lane = lax.broadcasted_iota(jnp.int32, (1, LANES), 1)
    for b in range(nb):
        gates = g_ref[b]
        m_all = m_ref[b]
        m_vec = jnp.zeros((1, LANES), F32)
        for h in range(M_HEADS):
            hs = slice(h * M_DK, (h + 1) * M_DK)
            ig = gates[:, h:h + 1]
            lf = gates[:, M_HEADS + h:M_HEADS + h + 1]
            m_prev = m_all[:, h:h + 1]
            qb = q_ref[b, :, hs].astype(BF16)
            kb = k_ref[b, :, hs].astype(BF16)
            vb = v_ref[b, :, hs].astype(BF16)
            qf, kf, vf = qb.astype(F32), kb.astype(F32), vb.astype(F32)
            c_prev = c_ref[b, h]
            n_prev = n_ref[b, h:h + 1, :]

            m_inter = lf + m_prev
            m_t = jnp.maximum(m_inter, ig)
            w_intra = jnp.sum(qf * kf, axis=-1, keepdims=True) * jnp.exp(ig - m_t)
            w_inter = jnp.exp(m_inter - m_t)
            q_c = _dot(jnp.broadcast_to(qb, (SUBLANES, M_DK)), c_prev.astype(BF16))[0:1, :]
            num = w_intra * vf + w_inter * q_c
            den = w_intra + w_inter * jnp.sum(qf * n_prev, axis=-1, keepdims=True)
            hh = num / jnp.maximum(jnp.abs(den), jnp.exp(-m_t))
            hn = hh * lax.rsqrt(jnp.mean(hh * hh, axis=-1, keepdims=True) + EPS) * mhn_ref[h:h + 1, :]
            ob_ref[b, :, hs] = hn * og_ref[b, :, hs]

            ks = kf * jnp.exp(ig - m_t)
            ks8 = jnp.where(row0, jnp.broadcast_to(ks, (SUBLANES, M_DK)), 0.0).astype(BF16)
            outer = lax.dot_general(ks8, jnp.broadcast_to(vb, (SUBLANES, M_DK)), _TN, preferred_element_type=F32)
            c_out[b, h] = w_inter * c_prev + outer
            n_out[b, h:h + 1, :] = w_inter * n_prev + ks
            m_vec = jnp.where(lane == h, m_t, m_vec)
        m_out[b] = m_vec


def _mlstm_step(qm, km, vm, og, gates, c0, n0, m0, mhn):
    NB = qm.shape[0]
    nb = STEP_NB_MLSTM
    tok = pl.BlockSpec((nb, 1, M_INNER), lambda b: (b, 0, 0))
    vec = pl.BlockSpec((nb, 1, LANES), lambda b: (b, 0, 0))
    cspec = pl.BlockSpec((nb, M_HEADS, M_DK, M_DK), lambda b: (b, 0, 0, 0))
    nspec = pl.BlockSpec((nb, M_HEADS, M_DK), lambda b: (b, 0, 0))
    return pl.pallas_call(
        functools.partial(_mlstm_step_kernel, nb=nb),
        grid=(NB // nb,),
        in_specs=[tok, tok, tok, tok, vec, cspec, nspec, vec, pl.BlockSpec((M_HEADS, M_DK), lambda b: (0, 0))],
        out_specs=[tok, cspec, nspec, vec],
        out_shape=[jax.ShapeDtypeStruct((NB, 1, M_INNER), F32),
                   jax.ShapeDtypeStruct(c0.shape, F32),
                   jax.ShapeDtypeStruct(n0.shape, F32),
                   jax.ShapeDtypeStruct((NB, 1, LANES), F32)],
        compiler_params=pltpu.CompilerParams(dimension_semantics=("arbitrary",), vmem_limit_bytes=VMEM_LIMIT),
        name="mlstm_step",
    )(qm, km, vm, og, gates, c0, n0, m0, mhn)


def _post_kernel(x_ref, o0_ref, o1_ref, o2_ref, l0_ref, l1_ref, l2_ref, ob_ref, ga_ref, gb_ref, p_ref,
                 wpa_ref, wpb_ref, wout_ref, nffn_ref, wg_ref, wu_ref, wd_ref, nple_ref, wpg_ref, wpp_ref,
                 nfin_ref, y_ref, *ubuf, dils):
    lead = (0,) * (len(x_ref.shape) - 2)
    rd = lambda r: r[lead] if lead else r[...]
    tm = x_ref.shape[-2]

    def undilate(ref, dil, slot):
        if dil == 1:
            return rd(ref)
        halves = range(GROUP_W // LANES)
        for r in range(dil):
            for c in halves:
                lo = r * GROUP_W + c * LANES
                ubuf[0][slot, c, pl.ds(r, tm // dil, stride=dil), :] = ref[0, :, lo:lo + LANES]
        return jnp.concatenate([ubuf[0][slot, c] for c in halves], axis=1)

    l0, l1, l2 = (undilate(r, d, 2 * g) for g, (r, d) in enumerate(zip((l0_ref, l1_ref, l2_ref), dils)))
    o0, o1, o2 = (undilate(r, d, 2 * g + 1) for g, (r, d) in enumerate(zip((o0_ref, o1_ref, o2_ref), dils)))
    lmax = jnp.maximum(jnp.maximum(l0, l1), l2)
    e0, e1, e2 = jnp.exp(l0 - lmax), jnp.exp(l1 - lmax), jnp.exp(l2 - lmax)
    o_a = (e0 * o0 + e1 * o1 + e2 * o2) / (e0 + e1 + e2)

    mix = (rd(ga_ref).astype(F32) * _dot(o_a.astype(BF16), wpa_ref[...])
           + rd(gb_ref).astype(F32) * _dot(rd(ob_ref).astype(BF16), wpb_ref[...]))
    h = rd(x_ref) + _dot(mix.astype(BF16), wout_ref[...])

    xf = _rms(h, nffn_ref[...]).astype(BF16)
    half = D_FF // 2
    for c in range(2):
        cs = slice(c * half, (c + 1) * half)
        gate = _dot(xf, wg_ref[:, cs])
        act = (gate * jax.nn.sigmoid(gate)) * _dot(xf, wu_ref[:, cs])
        h = h + _dot(act.astype(BF16), wd_ref[cs, :])

    xp = _rms(h, nple_ref[...]).astype(BF16)
    h = h + jax.nn.sigmoid(_dot(xp, wpg_ref[...])) * _dot(rd(p_ref).astype(BF16), wpp_ref[...])
    y = _rms(h, nfin_ref[...])
    if lead:
        y_ref[lead] = y
    else:
        y_ref[...] = y


def _post_seq(x, os_, ls_, ob, ga, gb, p, wts):
    B, S, _ = x.shape
    tm = TM_POST
    tok = lambda w: pl.BlockSpec((1, tm, w), lambda b, i: (b, i, 0))
    dils = tuple(dil for _, dil in DSWA_GROUPS)
    dil_specs = [pl.BlockSpec((1, tm // dil, dil * GROUP_W), lambda b, i: (b, i, 0)) for dil in dils]
    in_specs = ([tok(D_MODEL)] + dil_specs * 2 + [tok(M_INNER)] * 3 + [tok(PLE_DIM)]
                + [_const_spec(w.shape) for w in wts])
    return pl.pallas_call(
        functools.partial(_post_kernel, dils=dils),
        grid=(B, S // tm),
        in_specs=in_specs,
        out_specs=tok(D_MODEL),
        out_shape=jax.ShapeDtypeStruct((B, S, D_MODEL), F32),
        scratch_shapes=[pltpu.VMEM((2 * N_GROUPS, GROUP_W // LANES, tm, LANES), F32)],
        compiler_params=pltpu.CompilerParams(
            dimension_semantics=("arbitrary", "arbitrary"), vmem_limit_bytes=VMEM_LIMIT),
        name="post_seq",
    )(x, *os_, *ls_, ob, ga, gb, p, *wts)


def _post_step(x, os_, ls_, ob, ga, gb, p, wts):
    return pl.pallas_call(
        functools.partial(_post_kernel, dils=(1,) * N_GROUPS),
        out_shape=jax.ShapeDtypeStruct(x.shape, F32),
        compiler_params=pltpu.CompilerParams(vmem_limit_bytes=VMEM_LIMIT),
        name="post_step",
    )(x, *os_, *ls_, ob, ga, gb, p, *wts)


def _rope_tables(pos):
    t = pos.shape[0]
    inv = jnp.power(ROPE_THETA, -jnp.arange(ROPE_HALF, dtype=F32) / ROPE_HALF)
    ang = pos.astype(F32)[:, None] * inv[None, :]
    cos, sin = jnp.cos(ang), jnp.sin(ang)
    one = jnp.ones((t, HEAD_DIM - ROPE_DIM), F32)
    zero = jnp.zeros((t, HEAD_DIM - ROPE_DIM), F32)
    z8 = jnp.zeros((t, ROPE_HALF), F32)
    a = jnp.concatenate([cos, cos, one], axis=1)
    m = jnp.concatenate([-sin, z8, zero], axis=1)
    p = jnp.concatenate([z8, sin, zero], axis=1)
    rep = LANES // HEAD_DIM
    return tuple(jnp.tile(v, (1, rep)) for v in (a, m, p))


def kernel(x_prompt, x_sample, cache_kv_w128, cache_kv_w512, cache_kv_w2048, state_conv, state_C, state_n,
           state_m, p_prompt, p_sample, norm_mix, w_in, conv_w, conv_b, b_igate, b_fgate, mh_norm, w_proj_a,
           w_proj_b, w_out, norm_ffn, w_gate, w_up, w_down, norm_ple, w_ple_gate, w_ple_proj, norm_final):
    depth = w_in.shape[0]
    assert depth == 1, "single trunk layer"
    B, S, _ = x_prompt.shape
    NB = x_sample.shape[0]
    assert x_sample.shape[1] == 1
    caches = (cache_kv_w128, cache_kv_w512, cache_kv_w2048)
    assert S >= DSWA_GROUPS[-1][0] and S % (BAND * DSWA_GROUPS[-1][1]) == 0
    past = 0
    l = 0

    w = w_in[l]
    gate_w = w[:, OFF_MI:OFF_GA]
    gate_b = jnp.concatenate([b_igate[l], b_fgate[l]])
    in_wts = (
        norm_mix[l][None, :],
        w[:, OFF_AQ:OFF_MQ].astype(BF16),
        w[:, OFF_MQ:OFF_MV].astype(BF16),
        w[:, OFF_MV:OFF_MO].astype(BF16),
        w[:, OFF_MO:OFF_MI].astype(BF16),
        w[:, OFF_GA:OFF_GB].astype(BF16),
        w[:, OFF_GB:IN_COLS].astype(BF16),
        jnp.pad(gate_w, ((0, 0), (0, LANES - 2 * M_HEADS))).astype(BF16),
        gate_w.T.astype(BF16),
        jnp.pad(gate_b, (0, LANES - 2 * M_HEADS))[None, :],
        gate_b[:, None],
        conv_w[l],
        conv_b[l][None, :],
    )
    post_wts = (
        w_proj_a[l].astype(BF16), w_proj_b[l].astype(BF16), w_out[l].astype(BF16), norm_ffn[l][None, :],
        w_gate[l].astype(BF16), w_up[l].astype(BF16), w_down[l].astype(BF16), norm_ple[l][None, :],
        w_ple_gate[l].astype(BF16), w_ple_proj[l].astype(BF16), norm_final[None, :],
    )
    mhn = mh_norm[l]

    c0 = jnp.zeros((B, SUBLANES, 2 * M_INNER), F32)
    tabs_p = _rope_tables(past + jnp.arange(S, dtype=jnp.int32))
    (*qkv, qm, km, vm, og, ga, gb, gc, gr, kv0, kv1, kv2, cn) = _inproj_seq(x_prompt, c0, tabs_p, in_wts)
    os_p, ls_p = [], []
    for g, (_, dil) in enumerate(DSWA_GROUPS):
        o, lse = _attn_seq(*qkv[3 * g:3 * g + 3], dil)
        os_p.append(o)
        ls_p.append(lse)
    ob, c_p, n_p, m_p = _mlstm_seq(qm, km, vm, og, gc, gr, mhn)
    y_prompt = _post_seq(x_prompt, os_p, ls_p, ob, ga, gb, p_prompt[l], post_wts)

    tabs_s = _rope_tables(jnp.full((NB,), PAST_LEN, jnp.int32))
    conv_s = jnp.swapaxes(state_conv[l], 0, 1)
    (qa_s, kvn0, kvn1, kvn2, qm_s, km_s, vm_s, og_s, ga_s, gb_s, gc_s, cn_s) = _inproj_step(
        x_sample[:, 0, :], conv_s, tabs_s, in_wts)
    kvn = (kvn0, kvn1, kvn2)
    caches_t = [jnp.transpose(c[l], (0, 2, 3, 4, 1)).reshape(NB, 2, GROUP_W, c.shape[2]) for c in caches]
    os_s, ls_s, kv_s = [], [], []
    for g, (_, dil) in enumerate(DSWA_GROUPS):
        o, lse, cache_new = _attn_step(qa_s[:, None, :], kvn[g][:, None, :], caches_t[g], g, dil)
        os_s.append(o[:, 0, :])
        ls_s.append(lse[:, 0, :])
        kv_s.append(jnp.transpose(cache_new.reshape(NB, 2, HPG, HEAD_DIM, -1), (0, 4, 1, 2, 3)))
    m0 = jnp.pad(state_m[l], ((0, 0), (0, LANES - M_HEADS)))[:, None, :]
    ob_s, c_s, n_s, m_s = _mlstm_step(qm_s[:, None, :], km_s[:, None, :], vm_s[:, None, :], og_s[:, None, :],
                                      gc_s[:, None, :], state_C[l], state_n[l], m0, mhn)
    y_sample = _post_step(x_sample[:, 0, :], os_s, ls_s, ob_s[:, 0, :], ga_s, gb_s, p_sample[l][:, 0, :], post_wts)

    kv_shape = lambda n, w: (1, n, w, 2, HPG, HEAD_DIM)
    return (
        y_prompt, y_sample[:, None, :],
        kv0.reshape(kv_shape(B, DSWA_GROUPS[0][0])), kv_s[0].reshape(kv_shape(NB, DSWA_GROUPS[0][0])),
        kv1.reshape(kv_shape(B, DSWA_GROUPS[1][0])), kv_s[1].reshape(kv_shape(NB, DSWA_GROUPS[1][0])),
        kv2.reshape(kv_shape(B, DSWA_GROUPS[2][0])), kv_s[2].reshape(kv_shape(NB, DSWA_GROUPS[2][0])),
        cn[:, SUBLANES - (CONV_W - 1):, :][None], jnp.swapaxes(cn_s, 0, 1)[None],
        c_p[None], c_s[None],
        n_p[None], n_s[None],
        m_p[:, 0, :M_HEADS][None], m_s[:, 0, :M_HEADS][None],
    )
```

```python
import functools

import jax
import jax.numpy as jnp
from jax import lax
from jax.experimental import pallas as pl
from jax.experimental.pallas import tpu as pltpu

F32 = jnp.float32
BF16 = jnp.bfloat16

D_MODEL = 1024
PAST_LEN = 16384
HEAD_DIM = 64
ROPE_DIM = HEAD_DIM // 4
ROPE_HALF = ROPE_DIM // 2
ROPE_THETA = 500000.0
DSWA_GROUPS = ((128, 1), (512, 4), (2048, 16))
HPG = 4
N_GROUPS = len(DSWA_GROUPS)
GROUP_W = HPG * HEAD_DIM
A_QKV = N_GROUPS * GROUP_W
BAND = 128
M_HEADS = 4
M_INNER = D_MODEL
M_DK = M_INNER // M_HEADS
CONV_W = 4
D_FF = 2816
PLE_DIM = 256
EPS = 1e-6
NEG = -1e30

OFF_AQ = 0
OFF_AK = OFF_AQ + A_QKV
OFF_AV = OFF_AK + A_QKV
OFF_MQ = OFF_AV + A_QKV
OFF_MK = OFF_MQ + M_INNER
OFF_MV = OFF_MK + M_INNER
OFF_MO = OFF_MV + M_INNER
OFF_MI = OFF_MO + M_INNER
OFF_MF = OFF_MI + M_HEADS
OFF_GA = OFF_MF + M_HEADS
OFF_GB = OFF_GA + D_MODEL
IN_COLS = OFF_GB + D_MODEL

LANES = 128
SUBLANES = 8
VMEM_LIMIT = 56 * 1024 * 1024

TM_IN = 512
TM_POST = 256
M_CHUNK = 256
ATT_TQ = 512
STEP_NB_ATT = 8
STEP_CACHE_BLOCK = 4 * 1024 * 1024
STEP_NB_MLSTM = 2

_NT = (((1,), (1,)), ((), ()))
_TN = (((0,), (0,)), ((), ()))


def _dot(a, b):
    return jnp.dot(a, b, preferred_element_type=F32)


def _rms(x, g):
    return x * lax.rsqrt(jnp.mean(x * x, axis=-1, keepdims=True) + EPS) * g


def _bf16_pieces(x):
    hi = x.astype(BF16)
    r1 = x - hi.astype(F32)
    mid = r1.astype(BF16)
    lo = (r1 - mid.astype(F32)).astype(BF16)
    return hi, mid, lo


def _sigmoid(x):
    return 0.5 * jnp.tanh(0.5 * x) + 0.5


def _log_sigmoid(x):
    return jnp.minimum(x, 0.0) - jnp.log1p(jnp.exp(-jnp.abs(x)))


def _rope(t, ra, rm, rp):
    return t * ra + pltpu.roll(t, LANES - ROPE_HALF, 1) * rm + pltpu.roll(t, ROPE_HALF, 1) * rp


def _rope2(t, ra, rm, rp):
    return jnp.concatenate([_rope(t[:, :LANES], ra, rm, rp), _rope(t[:, LANES:], ra, rm, rp)], axis=1)


def _gate_cols(z):
    lane = lax.broadcasted_iota(jnp.int32, z.shape, 1)
    return jnp.where(lane < M_HEADS, z, _log_sigmoid(z))


def _inproj_seq_kernel(x_ref, nmix_ref, waqkv_ref, wmqk_ref, wmv_ref, wmo_ref, wga_ref, wgb_ref,
                       wgc_ref, wgr_ref, bc_ref, br_ref, cw_ref, cb_ref, c0_ref, ra_ref, rm_ref, rp_ref,
                       q0_ref, k0_ref, v0_ref, q1_ref, k1_ref, v1_ref, q2_ref, k2_ref, v2_ref,
                       qm_ref, km_ref, vm_ref, og_ref, ga_ref, gb_ref,
                       gc_ref, gr_ref, kv0_ref, kv1_ref, kv2_ref, cn_ref, cbuf, dbuf, *, tm, ns):
    i = pl.program_id(1)
    xn = _rms(x_ref[0], nmix_ref[...]).astype(BF16)
    ra, rm, rp = ra_ref[...], rm_ref[...], rp_ref[...]
    kv_refs = (kv0_ref, kv1_ref, kv2_ref)
    qkv_refs = ((q0_ref, k0_ref, v0_ref), (q1_ref, k1_ref, v1_ref), (q2_ref, k2_ref, v2_ref))

    def put_dilated(ref, val, dil, slot):
        if dil == 1:
            ref[0] = val.astype(BF16)
            return
        for c in range(GROUP_W // LANES):
            dbuf[slot, c] = val[:, c * LANES:(c + 1) * LANES]
        for r in range(dil):
            for c in range(GROUP_W // LANES):
                lo = r * GROUP_W + c * LANES
                ref[0, :, lo:lo + LANES] = dbuf[slot, c, pl.ds(r, tm // dil, stride=dil), :].astype(BF16)

    for g, (win, dil) in enumerate(DSWA_GROUPS):
        zq = _dot(xn, waqkv_ref[:, OFF_AQ + g * GROUP_W:OFF_AQ + (g + 1) * GROUP_W])
        put_dilated(qkv_refs[g][0], _rope2(zq, ra, rm, rp) * (HEAD_DIM ** -0.5), dil, 3 * g)
        zk = _dot(xn, waqkv_ref[:, OFF_AK + g * GROUP_W:OFF_AK + (g + 1) * GROUP_W])
        kr = _rope2(zk, ra, rm, rp)
        put_dilated(qkv_refs[g][1], kr, dil, 3 * g + 1)
        zv = _dot(xn, waqkv_ref[:, OFF_AV + g * GROUP_W:OFF_AV + (g + 1) * GROUP_W])
        put_dilated(qkv_refs[g][2], zv, dil, 3 * g + 2)
        rows = min(win, tm)
        nblk = win // rows

        @pl.when(i >= ns - nblk)
        def _():
            kv_refs[g][0, :, :GROUP_W] = kr[tm - rows:, :]
            kv_refs[g][0, :, GROUP_W:] = zv[tm - rows:, :]

    @pl.when(i == 0)
    def _():
        for blk in range(2 * M_INNER // LANES):
            cbuf[blk, 0:SUBLANES, :] = c0_ref[0, :, blk * LANES:(blk + 1) * LANES]

    for c in range(2 * M_INNER // 256):
        pre = _dot(xn, wmqk_ref[:, c * 256:(c + 1) * 256])
        halves = []
        for hf in range(256 // LANES):
            blk = c * (256 // LANES) + hf
            ls = slice(blk * LANES, (blk + 1) * LANES)
            pre_h = pre[:, hf * LANES:(hf + 1) * LANES]
            cbuf[blk, SUBLANES:SUBLANES + tm, :] = pre_h
            y = cb_ref[:, ls] + cw_ref[CONV_W - 1:CONV_W, ls] * pre_h
            for d in range(1, CONV_W):
                y = y + cw_ref[CONV_W - 1 - d:CONV_W - d, ls] * cbuf[blk, pl.ds(SUBLANES - d, tm), :]
            tail = pre_h[tm - SUBLANES:, :]
            cbuf[blk, 0:SUBLANES, :] = tail
            cn_ref[0, :, ls] = tail
            halves.append(y * _sigmoid(y))
        s = jnp.concatenate(halves, axis=1)
        if c < M_INNER // 256:
            qm_ref[0, :, c * 256:(c + 1) * 256] = s.astype(BF16)
        else:
            km_ref[0, :, c * 256 - M_INNER:(c + 1) * 256 - M_INNER] = (s * (M_DK ** -0.5)).astype(BF16)

    for c in range(M_INNER // 256):
        cs = slice(c * 256, (c + 1) * 256)
        vm_ref[0, :, cs] = _dot(xn, wmv_ref[:, cs]).astype(BF16)
        og_ref[0, :, cs] = _sigmoid(_dot(xn, wmo_ref[:, cs])).astype(BF16)
        ga_ref[0, :, cs] = _sigmoid(_dot(xn, wga_ref[:, cs])).astype(BF16)
        gb_ref[0, :, cs] = _sigmoid(_dot(xn, wgb_ref[:, cs])).astype(BF16)

    gcv = _gate_cols(_dot(xn, wgc_ref[...]) + bc_ref[...])
    zr = lax.dot_general(wgr_ref[...], xn, _NT, preferred_element_type=F32) + br_ref[...]
    sub = lax.broadcasted_iota(jnp.int32, zr.shape, 0)
    grv = jnp.where(sub < M_HEADS, zr, _log_sigmoid(zr))
    L = M_CHUNK
    ti = lax.broadcasted_iota(jnp.int32, (L, L), 0)
    si = lax.broadcasted_iota(jnp.int32, (L, L), 1)
    lower = jnp.where(ti >= si, 1.0, 0.0).astype(BF16)
    upper = jnp.where(ti <= si, 1.0, 0.0).astype(BF16)
    lane = lax.broadcasted_iota(jnp.int32, (L, LANES), 1)
    sub_l = lax.broadcasted_iota(jnp.int32, (SUBLANES, L), 0)
    for ch in range(tm // L):
        rs = slice(ch * L, (ch + 1) * L)
        cum_c = sum(_dot(lower, p) for p in _bf16_pieces(gcv[rs, :]))
        cum_r = sum(_dot(p, upper) for p in _bf16_pieces(grv[:, rs]))
        gc_ref[0, rs, :] = jnp.where(lane < M_HEADS, gcv[rs, :], cum_c)
        gr_ref[0, :, rs] = jnp.where(sub_l < M_HEADS, grv[:, rs], cum_r)


def _const_spec(shape):
    nd = len(shape)
    return pl.BlockSpec(shape, lambda *_: (0,) * nd, pipeline_mode=pl.Buffered(1))


def _inproj_seq(x, c0, tables, wts):
    B, S, _ = x.shape
    tm = TM_IN
    ns = S // tm
    ra, rm, rp = tables

    def kv_spec(win):
        rows = min(win, tm)
        nblk = win // rows
        return pl.BlockSpec((1, rows, 2 * GROUP_W), lambda b, i: (b, jnp.maximum(i - (ns - nblk), 0), 0))

    tok = lambda w: pl.BlockSpec((1, tm, w), lambda b, i: (b, i, 0))
    tab = pl.BlockSpec((tm, LANES), lambda b, i: (i, 0))
    in_specs = ([tok(D_MODEL)] + [_const_spec(w.shape) for w in wts]
                + [pl.BlockSpec((1, SUBLANES, 2 * M_INNER), lambda b, i: (b, 0, 0)), tab, tab, tab])
    dil_shape = [jax.ShapeDtypeStruct((B, S // dil, dil * GROUP_W), BF16) for _, dil in DSWA_GROUPS for _ in range(3)]
    dil_specs = [pl.BlockSpec((1, tm // dil, dil * GROUP_W), lambda b, i: (b, i, 0))
                 for _, dil in DSWA_GROUPS for _ in range(3)]
    out_shape = (dil_shape
                 + [jax.ShapeDtypeStruct((B, S, M_INNER), BF16)] * 6
                 + [jax.ShapeDtypeStruct((B, S, LANES), F32), jax.ShapeDtypeStruct((B, SUBLANES, S), F32)]
                 + [jax.ShapeDtypeStruct((B, win, 2 * GROUP_W), F32) for win, _ in DSWA_GROUPS]
                 + [jax.ShapeDtypeStruct((B, SUBLANES, 2 * M_INNER), F32)])
    out_specs = (dil_specs + [tok(M_INNER)] * 6
                 + [tok(LANES), pl.BlockSpec((1, SUBLANES, tm), lambda b, i: (b, 0, i))]
                 + [kv_spec(win) for win, _ in DSWA_GROUPS]
                 + [pl.BlockSpec((1, SUBLANES, 2 * M_INNER), lambda b, i: (b, 0, 0))])
    return pl.pallas_call(
        functools.partial(_inproj_seq_kernel, tm=tm, ns=ns),
        grid=(B, ns),
        in_specs=in_specs,
        out_specs=out_specs,
        out_shape=out_shape,
        scratch_shapes=[pltpu.VMEM((2 * M_INNER // LANES, SUBLANES + tm, LANES), F32),
                        pltpu.VMEM((3 * N_GROUPS, GROUP_W // LANES, tm, LANES), F32)],
        compiler_params=pltpu.CompilerParams(
            dimension_semantics=("arbitrary", "arbitrary"), vmem_limit_bytes=VMEM_LIMIT),
        name="inproj_seq",
    )(x, *wts, c0, ra, rm, rp)


def _inproj_step_kernel(x_ref, nmix_ref, waqkv_ref, wmqk_ref, wmv_ref, wmo_ref, wga_ref, wgb_ref,
                        wgc_ref, wgr_ref, bc_ref, br_ref, cw_ref, cb_ref, cs_ref, ra_ref, rm_ref, rp_ref,
                        qa_ref, kv0_ref, kv1_ref, kv2_ref, qm_ref, km_ref, vm_ref, og_ref, ga_ref, gb_ref,
                        gc_ref, cn_ref):
    del wgr_ref, br_ref
    xn = _rms(x_ref[...], nmix_ref[...]).astype(BF16)
    ra, rm, rp = ra_ref[...], rm_ref[...], rp_ref[...]
    kv_refs = (kv0_ref, kv1_ref, kv2_ref)
    for g in range(N_GROUPS):
        cs = slice(g * GROUP_W, (g + 1) * GROUP_W)
        zq = _dot(xn, waqkv_ref[:, OFF_AQ + g * GROUP_W:OFF_AQ + (g + 1) * GROUP_W])
        qa_ref[:, cs] = _rope2(zq, ra, rm, rp) * (HEAD_DIM ** -0.5)
        zk = _dot(xn, waqkv_ref[:, OFF_AK + g * GROUP_W:OFF_AK + (g + 1) * GROUP_W])
        kv_refs[g][:, :GROUP_W] = _rope2(zk, ra, rm, rp)
        kv_refs[g][:, GROUP_W:] = _dot(xn, waqkv_ref[:, OFF_AV + g * GROUP_W:OFF_AV + (g + 1) * GROUP_W])

    for c in range(2 * M_INNER // 256):
        cs = slice(c * 256, (c + 1) * 256)
        pre = _dot(xn, wmqk_ref[:, cs])
        y = cb_ref[:, cs] + cw_ref[CONV_W - 1:CONV_W, cs] * pre
        for j in range(CONV_W - 1):
            y = y + cw_ref[j:j + 1, cs] * cs_ref[j, :, cs]
        for j in range(CONV_W - 2):
            cn_ref[j, :, cs] = cs_ref[j + 1, :, cs]
        cn_ref[CONV_W - 2, :, cs] = pre
        s = y * _sigmoid(y)
        if c < M_INNER // 256:
            qm_ref[:, cs] = s
        else:
            km_ref[:, c * 256 - M_INNER:(c + 1) * 256 - M_INNER] = s * (M_DK ** -0.5)

    for c in range(M_INNER // 256):
        cs = slice(c * 256, (c + 1) * 256)
        vm_ref[:, cs] = _dot(xn, wmv_ref[:, cs])
        og_ref[:, cs] = _sigmoid(_dot(xn, wmo_ref[:, cs]))
        ga_ref[:, cs] = _sigmoid(_dot(xn, wga_ref[:, cs]))
        gb_ref[:, cs] = _sigmoid(_dot(xn, wgb_ref[:, cs]))
    gc_ref[...] = _gate_cols(_dot(xn, wgc_ref[...]) + bc_ref[...])


def _inproj_step(x, conv_state, tables, wts):
    nb = x.shape[0]
    sds = lambda w: jax.ShapeDtypeStruct((nb, w), F32)
    out_shape = ([sds(A_QKV)] + [sds(2 * GROUP_W)] * N_GROUPS + [sds(M_INNER)] * 6 + [sds(LANES)]
                 + [jax.ShapeDtypeStruct((CONV_W - 1, nb, 2 * M_INNER), F32)])
    return pl.pallas_call(
        _inproj_step_kernel,
        out_shape=out_shape,
        compiler_params=pltpu.CompilerParams(vmem_limit_bytes=VMEM_LIMIT),
        name="inproj_step",
    )(x, *wts, conv_state, *tables)


def _attn_seq_kernel(q_ref, k_ref, v_ref, kp_ref, vp_ref, o_ref, lse_ref, *, tq):
    i = pl.program_id(2)
    lane_head = lax.broadcasted_iota(jnp.int32, (1, GROUP_W), 1) // HEAD_DIM
    head_mask = [jnp.where(lane_head == h, 1.0, 0.0).astype(BF16) for h in range(HPG)]
    lane_head_full = lax.broadcasted_iota(jnp.int32, (BAND, GROUP_W), 1) // HEAD_DIM
    qi = lax.broadcasted_iota(jnp.int32, (BAND, 2 * BAND), 0)
    kj = lax.broadcasted_iota(jnp.int32, (BAND, 2 * BAND), 1)
    band = (kj >= qi) & (kj <= qi + BAND)
    first_lo = jnp.where(i > 0, 0, BAND)
    bias = jnp.where(band, 0.0, NEG)
    bias_first = jnp.where(band & (kj >= first_lo), 0.0, NEG)

    for j in range(tq // BAND):
        b1 = bias_first if j == 0 else bias
        q = q_ref[0, j * BAND:(j + 1) * BAND, :]
        if j == 0:
            kk = jnp.concatenate([kp_ref[0], k_ref[0, 0:BAND, :]], axis=0)
            vv = jnp.concatenate([vp_ref[0], v_ref[0, 0:BAND, :]], axis=0)
        else:
            kk = k_ref[0, (j - 1) * BAND:(j + 1) * BAND, :]
            vv = v_ref[0, (j - 1) * BAND:(j + 1) * BAND, :]
        qblk = jnp.concatenate([q * head_mask[h] for h in range(HPG)], axis=0)
        s = lax.dot_general(qblk, kk, _NT, preferred_element_type=F32) + jnp.concatenate([b1] * HPG, axis=0)
        mx = jnp.max(s, axis=-1, keepdims=True)
        e = jnp.exp(s - mx)
        den = jnp.sum(e, axis=-1, keepdims=True)
        pv = _dot(e.astype(BF16), vv) * (1.0 / den)
        lse_rows = mx + jnp.log(den)
        o = pv[0:BAND]
        lse = jnp.broadcast_to(lse_rows[0:BAND], (BAND, GROUP_W))
        for h in range(1, HPG):
            own = lane_head_full == h
            o = jnp.where(own, pv[h * BAND:(h + 1) * BAND], o)
            lse = jnp.where(own, lse_rows[h * BAND:(h + 1) * BAND], lse)
        o_ref[0, j * BAND:(j + 1) * BAND, :] = o
        lse_ref[0, j * BAND:(j + 1) * BAND, :] = lse


def _attn_seq(q, k, v, dil):
    B, nsub, _ = q.shape
    tq = min(ATT_TQ, nsub)
    in_spec = pl.BlockSpec((1, tq, GROUP_W), lambda b, r, i: (b, i, r))
    prev_spec = pl.BlockSpec((1, BAND, GROUP_W), lambda b, r, i: (b, jnp.maximum(i * (tq // BAND) - 1, 0), r))
    out_spec = pl.BlockSpec((1, tq, GROUP_W), lambda b, r, i: (b, i, r))
    o, lse = pl.pallas_call(
        functools.partial(_attn_seq_kernel, tq=tq),
        grid=(B, dil, nsub // tq),
        in_specs=[in_spec] * 3 + [prev_spec] * 2,
        out_specs=[out_spec] * 2,
        out_shape=[jax.ShapeDtypeStruct((B, nsub, dil * GROUP_W), F32)] * 2,
        compiler_params=pltpu.CompilerParams(
            dimension_semantics=("arbitrary", "arbitrary", "arbitrary"), vmem_limit_bytes=VMEM_LIMIT),
        name=f"attn_seq_d{dil}",
    )(q, k, v, k, v)
    return o, lse


def _attn_step_kernel(q_ref, kvn_ref, c_ref, o_ref, lse_ref, cn_ref, *, nb, dil, w):
    sub = lax.broadcasted_iota(jnp.int32, (SUBLANES, GROUP_W), 0)
    lane_head = lax.broadcasted_iota(jnp.int32, (SUBLANES, GROUP_W), 1) // HEAD_DIM
    own = sub == lane_head
    pos = lax.broadcasted_iota(jnp.int32, (SUBLANES, w), 1)
    valid = (pos % dil) == 0
    eye = (lax.broadcasted_iota(jnp.int32, (GROUP_W, GROUP_W), 0)
           == lax.broadcasted_iota(jnp.int32, (GROUP_W, GROUP_W), 1))
    last = lax.broadcasted_iota(jnp.int32, (GROUP_W, w), 1) == w - 1
    for b in range(nb):
        qrows = jnp.where(own, jnp.broadcast_to(q_ref[b], (SUBLANES, GROUP_W)), 0.0).astype(BF16)
        k_t = c_ref[b, 0]
        v_t = c_ref[b, 1]
        k_row = kvn_ref[b, :, :GROUP_W]
        v_row = kvn_ref[b, :, GROUP_W:]
        k_new = k_row.astype(BF16).astype(F32)
        v_new = v_row.astype(BF16).astype(F32)
        s = jnp.where(valid, _dot(qrows, k_t.astype(BF16)), NEG)
        s_new = jnp.sum(qrows.astype(F32) * k_new, axis=-1, keepdims=True)
        mx = jnp.maximum(jnp.max(s, axis=-1, keepdims=True), s_new)
        e = jnp.exp(s - mx)
        e_new = jnp.exp(s_new - mx)
        den = jnp.sum(e, axis=-1, keepdims=True) + e_new
        orow = (lax.dot_general((e / den).astype(BF16), v_t.astype(BF16), _NT, preferred_element_type=F32)
                + (e_new / den) * v_new)
        lse = mx + jnp.log(den)
        o_ref[b] = jnp.sum(jnp.where(own, orow, 0.0), axis=0, keepdims=True)
        lse_ref[b] = jnp.sum(jnp.where(own, jnp.broadcast_to(lse, (SUBLANES, GROUP_W)), 0.0),
                             axis=0, keepdims=True)
        for j, (old, row) in enumerate(((k_t, k_row), (v_t, v_row))):
            col = jnp.sum(jnp.where(eye, jnp.broadcast_to(row, (GROUP_W, GROUP_W)), 0.0), axis=1, keepdims=True)
            cn_ref[b, j] = jnp.where(last, col, pltpu.roll(old, w - 1, 1))


def _attn_step(q3, kvn3, cache_t, g, dil):
    NB, _, _, w = cache_t.shape
    assert w == BAND * dil, "the cache must hold exactly one full window"
    nb = max(1, min(STEP_NB_ATT, STEP_CACHE_BLOCK // (2 * GROUP_W * w * 4)))
    row = pl.BlockSpec((nb, 1, GROUP_W), lambda b: (b, 0, 0))
    cspec = pl.BlockSpec((nb, 2, GROUP_W, w), lambda b: (b, 0, 0, 0))
    return pl.pallas_call(
        functools.partial(_attn_step_kernel, nb=nb, dil=dil, w=w),
        grid=(NB // nb,),
        in_specs=[pl.BlockSpec((nb, 1, GROUP_W), lambda b: (b, 0, g)),
                  pl.BlockSpec((nb, 1, 2 * GROUP_W), lambda b: (b, 0, 0)),
                  cspec],
        out_specs=[row, row, cspec],
        out_shape=[jax.ShapeDtypeStruct((NB, 1, GROUP_W), F32)] * 2 + [jax.ShapeDtypeStruct(cache_t.shape, F32)],
        compiler_params=pltpu.CompilerParams(dimension_semantics=("arbitrary",), vmem_limit_bytes=VMEM_LIMIT),
        name=f"attn_step_d{dil}",
    )(q3, kvn3, cache_t)


def _mlstm_seq_kernel(q_ref, k_ref, v_ref, og_ref, gc_ref, gr_ref, mhn_ref,
                      ob_ref, c_out, n_out, m_out, c_s, n_s, m_s, *, L):
    c = pl.program_id(1)

    @pl.when(c == 0)
    def _():
        c_s[...] = jnp.zeros(c_s.shape, F32)
        n_s[...] = jnp.zeros(n_s.shape, F32)
        m_s[...] = jnp.zeros(m_s.shape, F32)

    gc = gc_ref[0]
    gr = gr_ref[0]
    ti = lax.broadcasted_iota(jnp.int32, (L, L), 0)
    si = lax.broadcasted_iota(jnp.int32, (L, L), 1)
    tril = ti >= si
    lane = lax.broadcasted_iota(jnp.int32, (1, LANES), 1)
    m_vec = jnp.zeros((1, LANES), F32)

    for h in range(M_HEADS):
        hs = slice(h * M_DK, (h + 1) * M_DK)
        b_col = gc[:, M_HEADS + h:M_HEADS + h + 1]
        i_col = gc[:, h:h + 1]
        b_row = gr[M_HEADS + h:M_HEADS + h + 1, :]
        i_row = gr[h:h + 1, :]
        m_prev = m_s[h, 0:1, 0:1]
        qh = q_ref[0, :, hs]
        kh = k_ref[0, :, hs]
        vh = v_ref[0, :, hs]
        c_prev = c_s[h]
        n_prev = n_s[h]

        dmat = jnp.where(tril, b_col - b_row + i_row, -jnp.inf)
        m_inter = b_col + m_prev
        m_t = jnp.maximum(m_inter, jnp.max(dmat, axis=-1, keepdims=True))
        w_intra = lax.dot_general(qh, kh, _NT, preferred_element_type=F32) * jnp.exp(dmat - m_t)
        w_inter = jnp.exp(m_inter - m_t)
        num = _dot(w_intra.astype(BF16), vh) + w_inter * _dot(qh, c_prev.astype(BF16))
        den = (jnp.sum(w_intra, axis=-1, keepdims=True)
               + w_inter * jnp.sum(qh.astype(F32) * n_prev, axis=-1, keepdims=True))
        hh = num / jnp.maximum(jnp.abs(den), jnp.exp(-m_t))
        hn = hh * lax.rsqrt(jnp.mean(hh * hh, axis=-1, keepdims=True) + EPS) * mhn_ref[h:h + 1, :]
        ob_ref[0, :, hs] = (hn * og_ref[0, :, hs].astype(F32)).astype(BF16)

        b_last = b_row[:, L - 1:L]
        m_new = jnp.maximum(b_last + m_prev, jnp.max(b_last - b_row + i_row, axis=-1, keepdims=True))
        wk = jnp.exp(b_last - b_col + i_col - m_new)
        decay = jnp.exp(b_last + m_prev - m_new)
        ks = kh.astype(F32) * wk
        c_new = decay * c_prev + lax.dot_general(ks.astype(BF16), vh, _TN, preferred_element_type=F32)
        n_new = decay * n_prev + jnp.sum(ks, axis=0, keepdims=True)
        c_s[h] = c_new
        n_s[h] = n_new
        m_s[h] = jnp.broadcast_to(m_new, (SUBLANES, LANES))
        c_out[0, h] = c_new
        n_out[0, h:h + 1, :] = n_new
        m_vec = jnp.where(lane == h, m_new, m_vec)
    m_out[0] = m_vec


def _mlstm_seq(qm, km, vm, og, gc, gr, mhn):
    B, S, _ = qm.shape
    L = M_CHUNK
    tok = pl.BlockSpec((1, L, M_INNER), lambda b, c: (b, c, 0))
    return pl.pallas_call(
        functools.partial(_mlstm_seq_kernel, L=L),
        grid=(B, S // L),
        in_specs=[tok, tok, tok, tok,
                  pl.BlockSpec((1, L, LANES), lambda b, c: (b, c, 0)),
                  pl.BlockSpec((1, SUBLANES, L), lambda b, c: (b, 0, c)),
                  pl.BlockSpec((M_HEADS, M_DK), lambda b, c: (0, 0))],
        out_specs=[tok,
                   pl.BlockSpec((1, M_HEADS, M_DK, M_DK), lambda b, c: (b, 0, 0, 0)),
                   pl.BlockSpec((1, M_HEADS, M_DK), lambda b, c: (b, 0, 0)),
                   pl.BlockSpec((1, 1, LANES), lambda b, c: (b, 0, 0))],
        out_shape=[jax.ShapeDtypeStruct((B, S, M_INNER), BF16),
                   jax.ShapeDtypeStruct((B, M_HEADS, M_DK, M_DK), F32),
                   jax.ShapeDtypeStruct((B, M_HEADS, M_DK), F32),
                   jax.ShapeDtypeStruct((B, 1, LANES), F32)],
        scratch_shapes=[pltpu.VMEM((M_HEADS, M_DK, M_DK), F32),
                        pltpu.VMEM((M_HEADS, 1, M_DK), F32),
                        pltpu.VMEM((M_HEADS, SUBLANES, LANES), F32)],
        compiler_params=pltpu.CompilerParams(
            dimension_semantics=("arbitrary", "arbitrary"), vmem_limit_bytes=VMEM_LIMIT),
        name="mlstm_seq",
    )(qm, km, vm, og, gc, gr, mhn)


def _mlstm_step_kernel(q_ref, k_ref, v_ref, og_ref, g_ref, c_ref, n_ref, m_ref, mhn_ref,
                       ob_ref, c_out, n_out, m_out, *, nb):
    row0 = lax.broadcasted_iota(jnp.int32, (SUBLANES, M_DK), 0) == 0
    lane = lax.broadcasted_iota(jnp.int32, (1, LANES), 1)
    for b in range(nb):
        gates = g_ref[b]
        m_all = m_ref[b]
        m_vec = jnp.zeros((1, LANES), F32)
        for h in range(M_HEADS):
            hs = slice(h * M_DK, (h + 1) * M_DK)
            ig = gates[:, h:h + 1]
            lf = gates[:, M_HEADS + h:M_HEADS + h + 1]
            m_prev = m_all[:, h:h + 1]
            qb = q_ref[b, :, hs].astype(BF16)
            kb = k_ref[b, :, hs].astype(BF16)
            vb = v_ref[b, :, hs].astype(BF16)
            qf, kf, vf = qb.astype(F32), kb.astype(F32), vb.astype(F32)
            c_prev = c_ref[b, h]
            n_prev = n_ref[b, h:h + 1, :]

            m_inter = lf + m_prev
            m_t = jnp.maximum(m_inter, ig)
            w_intra = jnp.sum(qf * kf, axis=-1, keepdims=True) * jnp.exp(ig - m_t)
            w_inter = jnp.exp(m_inter - m_t)
            q_c = _dot(jnp.broadcast_to(qb, (SUBLANES, M_DK)), c_prev.astype(BF16))[0:1, :]
            num = w_intra * vf + w_inter * q_c
            den = w_intra + w_inter * jnp.sum(qf * n_prev, axis=-1, keepdims=True)
            hh = num / jnp.maximum(jnp.abs(den), jnp.exp(-m_t))
            hn = hh * lax.rsqrt(jnp.mean(hh * hh, axis=-1, keepdims=True) + EPS) * mhn_ref[h:h + 1, :]
            ob_ref[b, :, hs] = hn * og_ref[b, :, hs]

            ks = kf * jnp.exp(ig - m_t)
            ks8 = jnp.where(row0, jnp.broadcast_to(ks, (SUBLANES, M_DK)), 0.0).astype(BF16)
            outer = lax.dot_general(ks8, jnp.broadcast_to(vb, (SUBLANES, M_DK)), _TN, preferred_element_type=F32)
            c_out[b, h] = w_inter * c_prev + outer
            n_out[b, h:h + 1, :] = w_inter * n_prev + ks
            m_vec = jnp.where(lane == h, m_t, m_vec)
        m_out[b] = m_vec


def _mlstm_step(qm, km, vm, og, gates, c0, n0, m0, mhn):
    NB = qm.shape[0]
    nb = STEP_NB_MLSTM
    tok = pl.BlockSpec((nb, 1, M_INNER), lambda b: (b, 0, 0))
    vec = pl.BlockSpec((nb, 1, LANES), lambda b: (b, 0, 0))
    cspec = pl.BlockSpec((nb, M_HEADS, M_DK, M_DK), lambda b: (b, 0, 0, 0))
    nspec = pl.BlockSpec((nb, M_HEADS, M_DK), lambda b: (b, 0, 0))
    return pl.pallas_call(
        functools.partial(_mlstm_step_kernel, nb=nb),
        grid=(NB // nb,),
        in_specs=[tok, tok, tok, tok, vec, cspec, nspec, vec, pl.BlockSpec((M_HEADS, M_DK), lambda b: (0, 0))],
        out_specs=[tok, cspec, nspec, vec],
        out_shape=[jax.ShapeDtypeStruct((NB, 1, M_INNER), F32),
                   jax.ShapeDtypeStruct(c0.shape, F32),
                   jax.ShapeDtypeStruct(n0.shape, F32),
                   jax.ShapeDtypeStruct((NB, 1, LANES), F32)],
        compiler_params=pltpu.CompilerParams(dimension_semantics=("arbitrary",), vmem_limit_bytes=VMEM_LIMIT),
        name="mlstm_step",
    )(qm, km, vm, og, gates, c0, n0, m0, mhn)


def _post_kernel(x_ref, o0_ref, o1_ref, o2_ref, l0_ref, l1_ref, l2_ref, ob_ref, ga_ref, gb_ref, p_ref,
                 wpa_ref, wpb_ref, wout_ref, nffn_ref, wg_ref, wu_ref, wd_ref, nple_ref, wpg_ref, wpp_ref,
                 nfin_ref, y_ref, *ubuf, dils):
    lead = (0,) * (len(x_ref.shape) - 2)
    rd = lambda r: r[lead] if lead else r[...]
    tm = x_ref.shape[-2]

    def undilate(ref, dil, slot):
        if dil == 1:
            return rd(ref)
        halves = range(GROUP_W // LANES)
        for r in range(dil):
            for c in halves:
                lo = r * GROUP_W + c * LANES
                ubuf[0][slot, c, pl.ds(r, tm // dil, stride=dil), :] = ref[0, :, lo:lo + LANES]
        return jnp.concatenate([ubuf[0][slot, c] for c in halves], axis=1)

    l0, l1, l2 = (undilate(r, d, 2 * g) for g, (r, d) in enumerate(zip((l0_ref, l1_ref, l2_ref), dils)))
    o0, o1, o2 = (undilate(r, d, 2 * g + 1) for g, (r, d) in enumerate(zip((o0_ref, o1_ref, o2_ref), dils)))
    lmax = jnp.maximum(jnp.maximum(l0, l1), l2)
    e0, e1, e2 = jnp.exp(l0 - lmax), jnp.exp(l1 - lmax), jnp.exp(l2 - lmax)
    o_a = (e0 * o0 + e1 * o1 + e2 * o2) / (e0 + e1 + e2)

    mix = (rd(ga_ref).astype(F32) * _dot(o_a.astype(BF16), wpa_ref[...])
           + rd(gb_ref).astype(F32) * _dot(rd(ob_ref).astype(BF16), wpb_ref[...]))
    h = rd(x_ref) + _dot(mix.astype(BF16), wout_ref[...])

    xf = _rms(h, nffn_ref[...]).astype(BF16)
    half = D_FF // 2
    for c in range(2):
        cs = slice(c * half, (c + 1) * half)
        gate = _dot(xf, wg_ref[:, cs])
        act = (gate * _sigmoid(gate)) * _dot(xf, wu_ref[:, cs])
        h = h + _dot(act.astype(BF16), wd_ref[cs, :])

    xp = _rms(h, nple_ref[...]).astype(BF16)
    h = h + _sigmoid(_dot(xp, wpg_ref[...])) * _dot(rd(p_ref).astype(BF16), wpp_ref[...])
    y = _rms(h, nfin_ref[...])
    if lead:
        y_ref[lead] = y
    else:
        y_ref[...] = y


def _post_seq(x, os_, ls_, ob, ga, gb, p, wts):
    B, S, _ = x.shape
    tm = TM_POST
    tok = lambda w: pl.BlockSpec((1, tm, w), lambda b, i: (b, i, 0))
    dils = tuple(dil for _, dil in DSWA_GROUPS)
    dil_specs = [pl.BlockSpec((1, tm // dil, dil * GROUP_W), lambda b, i: (b, i, 0)) for dil in dils]
    in_specs = ([tok(D_MODEL)] + dil_specs * 2 + [tok(M_INNER)] * 3 + [tok(PLE_DIM)]
                + [_const_spec(w.shape) for w in wts])
    return pl.pallas_call(
        functools.partial(_post_kernel, dils=dils),
        grid=(B, S // tm),
        in_specs=in_specs,
        out_specs=tok(D_MODEL),
        out_shape=jax.ShapeDtypeStruct((B, S, D_MODEL), F32),
        scratch_shapes=[pltpu.VMEM((2 * N_GROUPS, GROUP_W // LANES, tm, LANES), F32)],
        compiler_params=pltpu.CompilerParams(
            dimension_semantics=("arbitrary", "arbitrary"), vmem_limit_bytes=VMEM_LIMIT),
        name="post_seq",
    )(x, *os_, *ls_, ob, ga, gb, p, *wts)


def _post_step(x, os_, ls_, ob, ga, gb, p, wts):
    return pl.pallas_call(
        functools.partial(_post_kernel, dils=(1,) * N_GROUPS),
        out_shape=jax.ShapeDtypeStruct(x.shape, F32),
        compiler_params=pltpu.CompilerParams(vmem_limit_bytes=VMEM_LIMIT),
        name="post_step",
    )(x, *os_, *ls_, ob, ga, gb, p, *wts)


def _rope_tables(pos):
    t = pos.shape[0]
    inv = jnp.power(ROPE_THETA, -jnp.arange(ROPE_HALF, dtype=F32) / ROPE_HALF)
    ang = pos.astype(F32)[:, None] * inv[None, :]
    cos, sin = jnp.cos(ang), jnp.sin(ang)
    one = jnp.ones((t, HEAD_DIM - ROPE_DIM), F32)
    zero = jnp.zeros((t, HEAD_DIM - ROPE_DIM), F32)
    z8 = jnp.zeros((t, ROPE_HALF), F32)
    a = jnp.concatenate([cos, cos, one], axis=1)
    m = jnp.concatenate([-sin, z8, zero], axis=1)
    p = jnp.concatenate([z8, sin, zero], axis=1)
    rep = LANES // HEAD_DIM
    return tuple(jnp.tile(v, (1, rep)) for v in (a, m, p))


def kernel(x_prompt, x_sample, cache_kv_w128, cache_kv_w512, cache_kv_w2048, state_conv, state_C, state_n,
           state_m, p_prompt, p_sample, norm_mix, w_in, conv_w, conv_b, b_igate, b_fgate, mh_norm, w_proj_a,
           w_proj_b, w_out, norm_ffn, w_gate, w_up, w_down, norm_ple, w_ple_gate, w_ple_proj, norm_final):
    depth = w_in.shape[0]
    assert depth == 1, "single trunk layer"
    B, S, _ = x_prompt.shape
    NB = x_sample.shape[0]
    assert x_sample.shape[1] == 1
    caches = (cache_kv_w128, cache_kv_w512, cache_kv_w2048)
    assert S >= DSWA_GROUPS[-1][0] and S % (BAND * DSWA_GROUPS[-1][1]) == 0
    past = 0
    l = 0

    w = w_in[l]
    gate_w = w[:, OFF_MI:OFF_GA]
    gate_b = jnp.concatenate([b_igate[l], b_fgate[l]])
    in_wts = (
        norm_mix[l][None, :],
        w[:, OFF_AQ:OFF_MQ].astype(BF16),
        w[:, OFF_MQ:OFF_MV].astype(BF16),
        w[:, OFF_MV:OFF_MO].astype(BF16),
        w[:, OFF_MO:OFF_MI].astype(BF16),
        w[:, OFF_GA:OFF_GB].astype(BF16),
        w[:, OFF_GB:IN_COLS].astype(BF16),
        jnp.pad(gate_w, ((0, 0), (0, LANES - 2 * M_HEADS))).astype(BF16),
        gate_w.T.astype(BF16),
        jnp.pad(gate_b, (0, LANES - 2 * M_HEADS))[None, :],
        gate_b[:, None],
        conv_w[l],
        conv_b[l][None, :],
    )
    post_wts = (
        w_proj_a[l].astype(BF16), w_proj_b[l].astype(BF16), w_out[l].astype(BF16), norm_ffn[l][None, :],
        w_gate[l].astype(BF16), w_up[l].astype(BF16), w_down[l].astype(BF16), norm_ple[l][None, :],
        w_ple_gate[l].astype(BF16), w_ple_proj[l].astype(BF16), norm_final[None, :],
    )
    mhn = mh_norm[l]

    c0 = jnp.zeros((B, SUBLANES, 2 * M_INNER), F32)
    tabs_p = _rope_tables(past + jnp.arange(S, dtype=jnp.int32))
    (*qkv, qm, km, vm, og, ga, gb, gc, gr, kv0, kv1, kv2, cn) = _inproj_seq(x_prompt, c0, tabs_p, in_wts)
    os_p, ls_p = [], []
    for g, (_, dil) in enumerate(DSWA_GROUPS):
        o, lse = _attn_seq(*qkv[3 * g:3 * g + 3], dil)
        os_p.append(o)
        ls_p.append(lse)
    ob, c_p, n_p, m_p = _mlstm_seq(qm, km, vm, og, gc, gr, mhn)
    y_prompt = _post_seq(x_prompt, os_p, ls_p, ob, ga, gb, p_prompt[l], post_wts)

    tabs_s = _rope_tables(jnp.full((NB,), PAST_LEN, jnp.int32))
    conv_s = jnp.swapaxes(state_conv[l], 0, 1)
    (qa_s, kvn0, kvn1, kvn2, qm_s, km_s, vm_s, og_s, ga_s, gb_s, gc_s, cn_s) = _inproj_step(
        x_sample[:, 0, :], conv_s, tabs_s, in_wts)
    kvn = (kvn0, kvn1, kvn2)
    caches_t = [jnp.transpose(c[l], (0, 2, 3, 4, 1)).reshape(NB, 2, GROUP_W, c.shape[2]) for c in caches]
    os_s, ls_s, kv_s = [], [], []
    for g, (_, dil) in enumerate(DSWA_GROUPS):
        o, lse, cache_new = _attn_step(qa_s[:, None, :], kvn[g][:, None, :], caches_t[g], g, dil)
        os_s.append(o[:, 0, :])
        ls_s.append(lse[:, 0, :])
        kv_s.append(jnp.transpose(cache_new.reshape(NB, 2, HPG, HEAD_DIM, -1), (0, 4, 1, 2, 3)))
    m0 = jnp.pad(state_m[l], ((0, 0), (0, LANES - M_HEADS)))[:, None, :]
    ob_s, c_s, n_s, m_s = _mlstm_step(qm_s[:, None, :], km_s[:, None, :], vm_s[:, None, :], og_s[:, None, :],
                                      gc_s[:, None, :], state_C[l], state_n[l], m0, mhn)
    y_sample = _post_step(x_sample[:, 0, :], os_s, ls_s, ob_s[:, 0, :], ga_s, gb_s, p_sample[l][:, 0, :], post_wts)

    kv_shape = lambda n, w: (1, n, w, 2, HPG, HEAD_DIM)
    return (
        y_prompt, y_sample[:, None, :],
        kv0.reshape(kv_shape(B, DSWA_GROUPS[0][0])), kv_s[0].reshape(kv_shape(NB, DSWA_GROUPS[0][0])),
        kv1.reshape(kv_shape(B, DSWA_GROUPS[1][0])), kv_s[1].reshape(kv_shape(NB, DSWA_GROUPS[1][0])),
        kv2.reshape(kv_shape(B, DSWA_GROUPS[2][0])), kv_s[2].reshape(kv_shape(NB, DSWA_GROUPS[2][0])),
        cn[:, SUBLANES - (CONV_W - 1):, :][None], jnp.swapaxes(cn_s, 0, 1)[None],
        c_p[None], c_s[None],
        n_p[None], n_s[None],
        m_p[:, 0, :M_HEADS][None], m_s[:, 0, :M_HEADS][None],
    )
```

```python
import functools

import jax
import jax.numpy as jnp
import numpy as np
from jax import lax
from jax.experimental import pallas as pl
from jax.experimental.pallas import tpu as pltpu

F32 = jnp.float32
BF16 = jnp.bfloat16

D_MODEL = 1024
PAST_LEN = 16384
HEAD_DIM = 64
ROPE_DIM = HEAD_DIM // 4
ROPE_HALF = ROPE_DIM // 2
ROPE_THETA = 500000.0
DSWA_GROUPS = ((128, 1), (512, 4), (2048, 16))
HPG = 4
N_GROUPS = len(DSWA_GROUPS)
GROUP_W = HPG * HEAD_DIM
A_QKV = N_GROUPS * GROUP_W
BAND = 128
M_HEADS = 4
M_INNER = D_MODEL
M_DK = M_INNER // M_HEADS
CONV_W = 4
D_FF = 2816
PLE_DIM = 256
EPS = 1e-6
NEG = -1e30

OFF_AQ = 0
OFF_AK = OFF_AQ + A_QKV
OFF_AV = OFF_AK + A_QKV
OFF_MQ = OFF_AV + A_QKV
OFF_MK = OFF_MQ + M_INNER
OFF_MV = OFF_MK + M_INNER
OFF_MO = OFF_MV + M_INNER
OFF_MI = OFF_MO + M_INNER
OFF_MF = OFF_MI + M_HEADS
OFF_GA = OFF_MF + M_HEADS
OFF_GB = OFF_GA + D_MODEL
IN_COLS = OFF_GB + D_MODEL

LANES = 128
SUBLANES = 8
VMEM_LIMIT = 56 * 1024 * 1024

TM_IN = 512
TM_POST = 256
M_CHUNK = 256
ATT_TQ = 512
STEP_NB_ATT = 8
STEP_CACHE_BLOCK = 4 * 1024 * 1024
STEP_NB_MLSTM = 2

_NT = (((1,), (1,)), ((), ()))
_TN = (((0,), (0,)), ((), ()))


def _dot(a, b):
    return jnp.dot(a, b, preferred_element_type=F32)


def _wdot(a, wt_ref, lo, hi):
    return lax.dot_general(a, wt_ref[lo:hi, :], _NT, preferred_element_type=F32)


def _rms(x, g):
    return x * lax.rsqrt(jnp.mean(x * x, axis=-1, keepdims=True) + EPS) * g


def _bf16_pieces(x):
    hi = x.astype(BF16)
    r1 = x - hi.astype(F32)
    mid = r1.astype(BF16)
    lo = (r1 - mid.astype(F32)).astype(BF16)
    return hi, mid, lo


def _sigmoid(x):
    return 0.5 * jnp.tanh(0.5 * x) + 0.5


def _log_sigmoid(x):
    return jnp.minimum(x, 0.0) - jnp.log1p(jnp.exp(-jnp.abs(x)))


def _rope(t, ra, rm, rp):
    return t * ra + pltpu.roll(t, LANES - ROPE_HALF, 1) * rm + pltpu.roll(t, ROPE_HALF, 1) * rp


def _rope2(t, ra, rm, rp):
    return jnp.concatenate([_rope(t[:, :LANES], ra, rm, rp), _rope(t[:, LANES:], ra, rm, rp)], axis=1)


def _gate_cols(z):
    lane = lax.broadcasted_iota(jnp.int32, z.shape, 1)
    return jnp.where(lane < M_HEADS, z, _log_sigmoid(z))


def _inproj_seq_kernel(x_ref, nmix_ref, waqkv_ref, wmqk_ref, wmv_ref, wmo_ref, wga_ref, wgb_ref,
                       wgc_ref, wgr_ref, bc_ref, br_ref, cw_ref, cb_ref, c0_ref, ra_ref, rm_ref, rp_ref,
                       q0_ref, k0_ref, v0_ref, q1_ref, k1_ref, v1_ref, q2_ref, k2_ref, v2_ref,
                       qm_ref, km_ref, vm_ref, og_ref, ga_ref, gb_ref,
                       gc_ref, gr_ref, kv0_ref, kv1_ref, kv2_ref, cn_ref, cbuf, dbuf, *, tm, ns):
    i = pl.program_id(1)
    xn = _rms(x_ref[0], nmix_ref[...]).astype(BF16)
    ra, rm, rp = ra_ref[...], rm_ref[...], rp_ref[...]
    kv_refs = (kv0_ref, kv1_ref, kv2_ref)
    qkv_refs = ((q0_ref, k0_ref, v0_ref), (q1_ref, k1_ref, v1_ref), (q2_ref, k2_ref, v2_ref))

    def put_dilated(ref, val, dil, slot):
        if dil == 1:
            ref[0] = val.astype(BF16)
            return
        for c in range(GROUP_W // LANES):
            dbuf[slot, c] = val[:, c * LANES:(c + 1) * LANES]
        for r in range(dil):
            for c in range(GROUP_W // LANES):
                lo = r * GROUP_W + c * LANES
                ref[0, :, lo:lo + LANES] = dbuf[slot, c, pl.ds(r, tm // dil, stride=dil), :].astype(BF16)

    for g, (win, dil) in enumerate(DSWA_GROUPS):
        zq = _wdot(xn, waqkv_ref, OFF_AQ + g * GROUP_W, OFF_AQ + (g + 1) * GROUP_W)
        put_dilated(qkv_refs[g][0], _rope2(zq, ra, rm, rp) * (HEAD_DIM ** -0.5), dil, 3 * g)
        zk = _wdot(xn, waqkv_ref, OFF_AK + g * GROUP_W, OFF_AK + (g + 1) * GROUP_W)
        kr = _rope2(zk, ra, rm, rp)
        put_dilated(qkv_refs[g][1], kr, dil, 3 * g + 1)
        zv = _wdot(xn, waqkv_ref, OFF_AV + g * GROUP_W, OFF_AV + (g + 1) * GROUP_W)
        put_dilated(qkv_refs[g][2], zv, dil, 3 * g + 2)
        rows = min(win, tm)
        nblk = win // rows

        @pl.when(i >= ns - nblk)
        def _():
            kv_refs[g][0, :, :GROUP_W] = kr[tm - rows:, :]
            kv_refs[g][0, :, GROUP_W:] = zv[tm - rows:, :]

    @pl.when(i == 0)
    def _():
        for blk in range(2 * M_INNER // LANES):
            cbuf[blk, 0:SUBLANES, :] = c0_ref[0, :, blk * LANES:(blk + 1) * LANES]

    for c in range(2 * M_INNER // 256):
        pre = _wdot(xn, wmqk_ref, c * 256, (c + 1) * 256)
        halves = []
        for hf in range(256 // LANES):
            blk = c * (256 // LANES) + hf
            ls = slice(blk * LANES, (blk + 1) * LANES)
            pre_h = pre[:, hf * LANES:(hf + 1) * LANES]
            cbuf[blk, SUBLANES:SUBLANES + tm, :] = pre_h
            y = cb_ref[:, ls] + cw_ref[CONV_W - 1:CONV_W, ls] * pre_h
            for d in range(1, CONV_W):
                y = y + cw_ref[CONV_W - 1 - d:CONV_W - d, ls] * cbuf[blk, pl.ds(SUBLANES - d, tm), :]
            tail = pre_h[tm - SUBLANES:, :]
            cbuf[blk, 0:SUBLANES, :] = tail
            cn_ref[0, :, ls] = tail
            halves.append(y * _sigmoid(y))
        s = jnp.concatenate(halves, axis=1)
        if c < M_INNER // 256:
            qm_ref[0, :, c * 256:(c + 1) * 256] = s.astype(BF16)
        else:
            km_ref[0, :, c * 256 - M_INNER:(c + 1) * 256 - M_INNER] = (s * (M_DK ** -0.5)).astype(BF16)

    for c in range(M_INNER // 256):
        cs = slice(c * 256, (c + 1) * 256)
        vm_ref[0, :, cs] = _wdot(xn, wmv_ref, cs.start, cs.stop).astype(BF16)
        og_ref[0, :, cs] = _sigmoid(_wdot(xn, wmo_ref, cs.start, cs.stop)).astype(BF16)
        ga_ref[0, :, cs] = _sigmoid(_wdot(xn, wga_ref, cs.start, cs.stop)).astype(BF16)
        gb_ref[0, :, cs] = _sigmoid(_wdot(xn, wgb_ref, cs.start, cs.stop)).astype(BF16)

    gcv = _gate_cols(_wdot(xn, wgc_ref, 0, LANES) + bc_ref[...])
    zr = lax.dot_general(wgr_ref[...], xn, _NT, preferred_element_type=F32) + br_ref[...]
    sub = lax.broadcasted_iota(jnp.int32, zr.shape, 0)
    grv = jnp.where(sub < M_HEADS, zr, _log_sigmoid(zr))
    L = M_CHUNK
    ti = lax.broadcasted_iota(jnp.int32, (L, L), 0)
    si = lax.broadcasted_iota(jnp.int32, (L, L), 1)
    lower = jnp.where(ti >= si, 1.0, 0.0).astype(BF16)
    upper = jnp.where(ti <= si, 1.0, 0.0).astype(BF16)
    lane = lax.broadcasted_iota(jnp.int32, (L, LANES), 1)
    sub_l = lax.broadcasted_iota(jnp.int32, (SUBLANES, L), 0)
    for ch in range(tm // L):
        rs = slice(ch * L, (ch + 1) * L)
        cum_c = sum(_dot(lower, p) for p in _bf16_pieces(gcv[rs, :]))
        cum_r = sum(_dot(p, upper) for p in _bf16_pieces(grv[:, rs]))
        gc_ref[0, rs, :] = jnp.where(lane < M_HEADS, gcv[rs, :], cum_c)
        gr_ref[0, :, rs] = jnp.where(sub_l < M_HEADS, grv[:, rs], cum_r)


def _const_spec(shape):
    nd = len(shape)
    return pl.BlockSpec(shape, lambda *_: (0,) * nd, pipeline_mode=pl.Buffered(1))


def _inproj_seq(x, c0, tables, wts):
    B, S, _ = x.shape
    tm = TM_IN
    ns = S // tm
    ra, rm, rp = tables

    def kv_spec(win):
        rows = min(win, tm)
        nblk = win // rows
        return pl.BlockSpec((1, rows, 2 * GROUP_W), lambda b, i: (b, jnp.maximum(i - (ns - nblk), 0), 0))

    tok = lambda w: pl.BlockSpec((1, tm, w), lambda b, i: (b, i, 0))
    tab = pl.BlockSpec((tm, LANES), lambda b, i: (i, 0))
    in_specs = ([tok(D_MODEL)] + [_const_spec(w.shape) for w in wts]
                + [pl.BlockSpec((1, SUBLANES, 2 * M_INNER), lambda b, i: (b, 0, 0)), tab, tab, tab])
    dil_shape = [jax.ShapeDtypeStruct((B, S // dil, dil * GROUP_W), BF16) for _, dil in DSWA_GROUPS for _ in range(3)]
    dil_specs = [pl.BlockSpec((1, tm // dil, dil * GROUP_W), lambda b, i: (b, i, 0))
                 for _, dil in DSWA_GROUPS for _ in range(3)]
    out_shape = (dil_shape
                 + [jax.ShapeDtypeStruct((B, S, M_INNER), BF16)] * 6
                 + [jax.ShapeDtypeStruct((B, S, LANES), F32), jax.ShapeDtypeStruct((B, SUBLANES, S), F32)]
                 + [jax.ShapeDtypeStruct((B, win, 2 * GROUP_W), F32) for win, _ in DSWA_GROUPS]
                 + [jax.ShapeDtypeStruct((B, SUBLANES, 2 * M_INNER), F32)])
    out_specs = (dil_specs + [tok(M_INNER)] * 6
                 + [tok(LANES), pl.BlockSpec((1, SUBLANES, tm), lambda b, i: (b, 0, i))]
                 + [kv_spec(win) for win, _ in DSWA_GROUPS]
                 + [pl.BlockSpec((1, SUBLANES, 2 * M_INNER), lambda b, i: (b, 0, 0))])
    return pl.pallas_call(
        functools.partial(_inproj_seq_kernel, tm=tm, ns=ns),
        grid=(B, ns),
        in_specs=in_specs,
        out_specs=out_specs,
        out_shape=out_shape,
        scratch_shapes=[pltpu.VMEM((2 * M_INNER // LANES, SUBLANES + tm, LANES), F32),
                        pltpu.VMEM((3 * N_GROUPS, GROUP_W // LANES, tm, LANES), F32)],
        compiler_params=pltpu.CompilerParams(
            dimension_semantics=("arbitrary", "arbitrary"), vmem_limit_bytes=VMEM_LIMIT),
        name="inproj_seq",
    )(x, *wts, c0, ra, rm, rp)


def _inproj_step_kernel(x_ref, nmix_ref, waqkv_ref, wmqk_ref, wmv_ref, wmo_ref, wga_ref, wgb_ref,
                        wgc_ref, wgr_ref, bc_ref, br_ref, cw_ref, cb_ref, cs_ref, ra_ref, rm_ref, rp_ref,
                        qa_ref, kv0_ref, kv1_ref, kv2_ref, qm_ref, km_ref, vm_ref, og_ref, ga_ref, gb_ref,
                        gc_ref, cn_ref):
    del wgr_ref, br_ref
    xn = _rms(x_ref[...], nmix_ref[...]).astype(BF16)
    ra, rm, rp = ra_ref[...], rm_ref[...], rp_ref[...]
    kv_refs = (kv0_ref, kv1_ref, kv2_ref)
    for g in range(N_GROUPS):
        cs = slice(g * GROUP_W, (g + 1) * GROUP_W)
        zq = _wdot(xn, waqkv_ref, OFF_AQ + g * GROUP_W, OFF_AQ + (g + 1) * GROUP_W)
        qa_ref[:, cs] = _rope2(zq, ra, rm, rp) * (HEAD_DIM ** -0.5)
        zk = _wdot(xn, waqkv_ref, OFF_AK + g * GROUP_W, OFF_AK + (g + 1) * GROUP_W)
        kv_refs[g][:, :GROUP_W] = _rope2(zk, ra, rm, rp)
        kv_refs[g][:, GROUP_W:] = _wdot(xn, waqkv_ref, OFF_AV + g * GROUP_W, OFF_AV + (g + 1) * GROUP_W)

    for c in range(2 * M_INNER // 256):
        cs = slice(c * 256, (c + 1) * 256)
        pre = _wdot(xn, wmqk_ref, cs.start, cs.stop)
        y = cb_ref[:, cs] + cw_ref[CONV_W - 1:CONV_W, cs] * pre
        for j in range(CONV_W - 1):
            y = y + cw_ref[j:j + 1, cs] * cs_ref[j, :, cs]
        for j in range(CONV_W - 2):
            cn_ref[j, :, cs] = cs_ref[j + 1, :, cs]
        cn_ref[CONV_W - 2, :, cs] = pre
        s = y * _sigmoid(y)
        if c < M_INNER // 256:
            qm_ref[:, cs] = s
        else:
            km_ref[:, c * 256 - M_INNER:(c + 1) * 256 - M_INNER] = s * (M_DK ** -0.5)

    for c in range(M_INNER // 256):
        cs = slice(c * 256, (c + 1) * 256)
        vm_ref[:, cs] = _wdot(xn, wmv_ref, cs.start, cs.stop)
        og_ref[:, cs] = _sigmoid(_wdot(xn, wmo_ref, cs.start, cs.stop))
        ga_ref[:, cs] = _sigmoid(_wdot(xn, wga_ref, cs.start, cs.stop))
        gb_ref[:, cs] = _sigmoid(_wdot(xn, wgb_ref, cs.start, cs.stop))
    gc_ref[...] = _gate_cols(_wdot(xn, wgc_ref, 0, LANES) + bc_ref[...])


def _inproj_step(x, conv_state, tables, wts):
    nb = x.shape[0]
    sds = lambda w: jax.ShapeDtypeStruct((nb, w), F32)
    out_shape = ([sds(A_QKV)] + [sds(2 * GROUP_W)] * N_GROUPS + [sds(M_INNER)] * 6 + [sds(LANES)]
                 + [jax.ShapeDtypeStruct((CONV_W - 1, nb, 2 * M_INNER), F32)])
    return pl.pallas_call(
        _inproj_step_kernel,
        out_shape=out_shape,
        compiler_params=pltpu.CompilerParams(vmem_limit_bytes=VMEM_LIMIT),
        name="inproj_step",
    )(x, *wts, conv_state, *tables)


def _attn_seq_kernel(q_ref, k_ref, v_ref, kp_ref, vp_ref, o_ref, lse_ref, *, tq):
    i = pl.program_id(2)
    lane_head = lax.broadcasted_iota(jnp.int32, (1, GROUP_W), 1) // HEAD_DIM
    head_mask = [jnp.where(lane_head == h, 1.0, 0.0).astype(BF16) for h in range(HPG)]
    lane_head_full = lax.broadcasted_iota(jnp.int32, (BAND, GROUP_W), 1) // HEAD_DIM
    qi = lax.broadcasted_iota(jnp.int32, (BAND, 2 * BAND), 0)
    kj = lax.broadcasted_iota(jnp.int32, (BAND, 2 * BAND), 1)
    band = (kj >= qi) & (kj <= qi + BAND)
    first_lo = jnp.where(i > 0, 0, BAND)
    bias = jnp.where(band, 0.0, NEG)
    bias_first = jnp.where(band & (kj >= first_lo), 0.0, NEG)

    for j in range(tq // BAND):
        b1 = bias_first if j == 0 else bias
        q = q_ref[0, j * BAND:(j + 1) * BAND, :]
        if j == 0:
            kk = jnp.concatenate([kp_ref[0], k_ref[0, 0:BAND, :]], axis=0)
            vv = jnp.concatenate([vp_ref[0], v_ref[0, 0:BAND, :]], axis=0)
        else:
            kk = k_ref[0, (j - 1) * BAND:(j + 1) * BAND, :]
            vv = v_ref[0, (j - 1) * BAND:(j + 1) * BAND, :]
        qblk = jnp.concatenate([q * head_mask[h] for h in range(HPG)], axis=0)
        s = lax.dot_general(qblk, kk, _NT, preferred_element_type=F32) + jnp.concatenate([b1] * HPG, axis=0)
        mx = jnp.max(s, axis=-1, keepdims=True)
        e = jnp.exp(s - mx)
        den = jnp.sum(e, axis=-1, keepdims=True)
        pv = _dot(e.astype(BF16), vv) * (1.0 / den)
        lse_rows = mx + jnp.log(den)
        o = pv[0:BAND]
        lse = jnp.broadcast_to(lse_rows[0:BAND], (BAND, GROUP_W))
        for h in range(1, HPG):
            own = lane_head_full == h
            o = jnp.where(own, pv[h * BAND:(h + 1) * BAND], o)
            lse = jnp.where(own, lse_rows[h * BAND:(h + 1) * BAND], lse)
        o_ref[0, j * BAND:(j + 1) * BAND, :] = o
        lse_ref[0, j * BAND:(j + 1) * BAND, :] = lse


def _attn_seq(q, k, v, dil):
    B, nsub, _ = q.shape
    tq = min(ATT_TQ, nsub)
    in_spec = pl.BlockSpec((1, tq, GROUP_W), lambda b, r, i: (b, i, r))
    prev_spec = pl.BlockSpec((1, BAND, GROUP_W), lambda b, r, i: (b, jnp.maximum(i * (tq // BAND) - 1, 0), r))
    out_spec = pl.BlockSpec((1, tq, GROUP_W), lambda b, r, i: (b, i, r))
    o, lse = pl.pallas_call(
        functools.partial(_attn_seq_kernel, tq=tq),
        grid=(B, dil, nsub // tq),
        in_specs=[in_spec] * 3 + [prev_spec] * 2,
        out_specs=[out_spec] * 2,
        out_shape=[jax.ShapeDtypeStruct((B, nsub, dil * GROUP_W), F32)] * 2,
        compiler_params=pltpu.CompilerParams(
            dimension_semantics=("arbitrary", "arbitrary", "arbitrary"), vmem_limit_bytes=VMEM_LIMIT),
        name=f"attn_seq_d{dil}",
    )(q, k, v, k, v)
    return o, lse


def _attn_step_kernel(q_ref, kvn_ref, c_ref, o_ref, lse_ref, cn_ref, *, nb, dil, w):
    sub = lax.broadcasted_iota(jnp.int32, (SUBLANES, GROUP_W), 0)
    lane_head = lax.broadcasted_iota(jnp.int32, (SUBLANES, GROUP_W), 1) // HEAD_DIM
    own = sub == lane_head
    pos = lax.broadcasted_iota(jnp.int32, (SUBLANES, w), 1)
    valid = (pos % dil) == 0
    eye = (lax.broadcasted_iota(jnp.int32, (GROUP_W, GROUP_W), 0)
           == lax.broadcasted_iota(jnp.int32, (GROUP_W, GROUP_W), 1))
    last = lax.broadcasted_iota(jnp.int32, (GROUP_W, w), 1) == w - 1
    for b in range(nb):
        qrows = jnp.where(own, jnp.broadcast_to(q_ref[b], (SUBLANES, GROUP_W)), 0.0).astype(BF16)
        k_t = c_ref[b, 0]
        v_t = c_ref[b, 1]
        k_row = kvn_ref[b, :, :GROUP_W]
        v_row = kvn_ref[b, :, GROUP_W:]
        k_new = k_row.astype(BF16).astype(F32)
        v_new = v_row.astype(BF16).astype(F32)
        s = jnp.where(valid, _dot(qrows, k_t.astype(BF16)), NEG)
        s_new = jnp.sum(qrows.astype(F32) * k_new, axis=-1, keepdims=True)
        mx = jnp.maximum(jnp.max(s, axis=-1, keepdims=True), s_new)
        e = jnp.exp(s - mx)
        e_new = jnp.exp(s_new - mx)
        den = jnp.sum(e, axis=-1, keepdims=True) + e_new
        orow = (lax.dot_general((e / den).astype(BF16), v_t.astype(BF16), _NT, preferred_element_type=F32)
                + (e_new / den) * v_new)
        lse = mx + jnp.log(den)
        o_ref[b] = jnp.sum(jnp.where(own, orow, 0.0), axis=0, keepdims=True)
        lse_ref[b] = jnp.sum(jnp.where(own, jnp.broadcast_to(lse, (SUBLANES, GROUP_W)), 0.0),
                             axis=0, keepdims=True)
        for j, (old, row) in enumerate(((k_t, k_row), (v_t, v_row))):
            col = jnp.sum(jnp.where(eye, jnp.broadcast_to(row, (GROUP_W, GROUP_W)), 0.0), axis=1, keepdims=True)
            cn_ref[b, j] = jnp.where(last, col, pltpu.roll(old, w - 1, 1))


def _attn_step(q3, kvn3, cache_t, g, dil):
    NB, _, _, w = cache_t.shape
    assert w == BAND * dil, "the cache must hold exactly one full window"
    nb = max(1, min(STEP_NB_ATT, STEP_CACHE_BLOCK // (2 * GROUP_W * w * 4)))
    row = pl.BlockSpec((nb, 1, GROUP_W), lambda b: (b, 0, 0))
    cspec = pl.BlockSpec((nb, 2, GROUP_W, w), lambda b: (b, 0, 0, 0))
    return pl.pallas_call(
        functools.partial(_attn_step_kernel, nb=nb, dil=dil, w=w),
        grid=(NB // nb,),
        in_specs=[pl.BlockSpec((nb, 1, GROUP_W), lambda b: (b, 0, g)),
                  pl.BlockSpec((nb, 1, 2 * GROUP_W), lambda b: (b, 0, 0)),
                  cspec],
        out_specs=[row, row, cspec],
        out_shape=[jax.ShapeDtypeStruct((NB, 1, GROUP_W), F32)] * 2 + [jax.ShapeDtypeStruct(cache_t.shape, F32)],
        compiler_params=pltpu.CompilerParams(dimension_semantics=("arbitrary",), vmem_limit_bytes=VMEM_LIMIT),
        name=f"attn_step_d{dil}",
    )(q3, kvn3, cache_t)


def _mlstm_seq_kernel(q_ref, k_ref, v_ref, og_ref, gc_ref, gr_ref, mhn_ref,
                      ob_ref, c_out, n_out, m_out, c_s, n_s, m_s, *, L):
    c = pl.program_id(1)

    @pl.when(c == 0)
    def _():
        c_s[...] = jnp.zeros(c_s.shape, F32)
        n_s[...] = jnp.zeros(n_s.shape, F32)
        m_s[...] = jnp.zeros(m_s.shape, F32)

    gc = gc_ref[0]
    gr = gr_ref[0]
    ti = lax.broadcasted_iota(jnp.int32, (L, L), 0)
    si = lax.broadcasted_iota(jnp.int32, (L, L), 1)
    tril = ti >= si
    lane = lax.broadcasted_iota(jnp.int32, (1, LANES), 1)
    m_vec = jnp.zeros((1, LANES), F32)

    for h in range(M_HEADS):
        hs = slice(h * M_DK, (h + 1) * M_DK)
        b_col = gc[:, M_HEADS + h:M_HEADS + h + 1]
        i_col = gc[:, h:h + 1]
        b_row = gr[M_HEADS + h:M_HEADS + h + 1, :]
        i_row = gr[h:h + 1, :]
        m_prev = m_s[h, 0:1, 0:1]
        qh = q_ref[0, :, hs]
        kh = k_ref[0, :, hs]
        vh = v_ref[0, :, hs]
        c_prev = c_s[h]
        n_prev = n_s[h]

        dmat = jnp.where(tril, b_col - b_row + i_row, -jnp.inf)
        m_inter = b_col + m_prev
        m_t = jnp.maximum(m_inter, jnp.max(dmat, axis=-1, keepdims=True))
        w_intra = lax.dot_general(qh, kh, _NT, preferred_element_type=F32) * jnp.exp(dmat - m_t)
        w_inter = jnp.exp(m_inter - m_t)
        num = _dot(w_intra.astype(BF16), vh) + w_inter * _dot(qh, c_prev.astype(BF16))
        den = (jnp.sum(w_intra, axis=-1, keepdims=True)
               + w_inter * jnp.sum(qh.astype(F32) * n_prev, axis=-1, keepdims=True))
        hh = num / jnp.maximum(jnp.abs(den), jnp.exp(-m_t))
        hn = hh * lax.rsqrt(jnp.mean(hh * hh, axis=-1, keepdims=True) + EPS) * mhn_ref[h:h + 1, :]
        ob_ref[0, :, hs] = (hn * og_ref[0, :, hs].astype(F32)).astype(BF16)

        b_last = b_row[:, L - 1:L]
        m_new = jnp.maximum(b_last + m_prev, jnp.max(b_last - b_row + i_row, axis=-1, keepdims=True))
        wk = jnp.exp(b_last - b_col + i_col - m_new)
        decay = jnp.exp(b_last + m_prev - m_new)
        ks = kh.astype(F32) * wk
        c_new = decay * c_prev + lax.dot_general(ks.astype(BF16), vh, _TN, preferred_element_type=F32)
        n_new = decay * n_prev + jnp.sum(ks, axis=0, keepdims=True)
        c_s[h] = c_new
        n_s[h] = n_new
        m_s[h] = jnp.broadcast_to(m_new, (SUBLANES, LANES))
        c_out[0, h] = c_new
        n_out[0, h:h + 1, :] = n_new
        m_vec = jnp.where(lane == h, m_new, m_vec)
    m_out[0] = m_vec


def _mlstm_seq(qm, km, vm, og, gc, gr, mhn):
    B, S, _ = qm.shape
    L = M_CHUNK
    tok = pl.BlockSpec((1, L, M_INNER), lambda b, c: (b, c, 0))
    return pl.pallas_call(
        functools.partial(_mlstm_seq_kernel, L=L),
        grid=(B, S // L),
        in_specs=[tok, tok, tok, tok,
                  pl.BlockSpec((1, L, LANES), lambda b, c: (b, c, 0)),
                  pl.BlockSpec((1, SUBLANES, L), lambda b, c: (b, 0, c)),
                  pl.BlockSpec((M_HEADS, M_DK), lambda b, c: (0, 0))],
        out_specs=[tok,
                   pl.BlockSpec((1, M_HEADS, M_DK, M_DK), lambda b, c: (b, 0, 0, 0)),
                   pl.BlockSpec((1, M_HEADS, M_DK), lambda b, c: (b, 0, 0)),
                   pl.BlockSpec((1, 1, LANES), lambda b, c: (b, 0, 0))],
        out_shape=[jax.ShapeDtypeStruct((B, S, M_INNER), BF16),
                   jax.ShapeDtypeStruct((B, M_HEADS, M_DK, M_DK), F32),
                   jax.ShapeDtypeStruct((B, M_HEADS, M_DK), F32),
                   jax.ShapeDtypeStruct((B, 1, LANES), F32)],
        scratch_shapes=[pltpu.VMEM((M_HEADS, M_DK, M_DK), F32),
                        pltpu.VMEM((M_HEADS, 1, M_DK), F32),
                        pltpu.VMEM((M_HEADS, SUBLANES, LANES), F32)],
        compiler_params=pltpu.CompilerParams(
            dimension_semantics=("arbitrary", "arbitrary"), vmem_limit_bytes=VMEM_LIMIT),
        name="mlstm_seq",
    )(qm, km, vm, og, gc, gr, mhn)


def _mlstm_step_kernel(q_ref, k_ref, v_ref, og_ref, g_ref, c_ref, n_ref, m_ref, mhn_ref,
                       ob_ref, c_out, n_out, m_out, *, nb):
    row0 = lax.broadcasted_iota(jnp.int32, (SUBLANES, M_DK), 0) == 0
    lane = lax.broadcasted_iota(jnp.int32, (1, LANES), 1)
    for b in range(nb):
        gates = g_ref[b]
        m_all = m_ref[b]
        m_vec = jnp.zeros((1, LANES), F32)
        for h in range(M_HEADS):
            hs = slice(h * M_DK, (h + 1) * M_DK)
            ig = gates[:, h:h + 1]
            lf = gates[:, M_HEADS + h:M_HEADS + h + 1]
            m_prev = m_all[:, h:h + 1]
            qb = q_ref[b, :, hs].astype(BF16)
            kb = k_ref[b, :, hs].astype(BF16)
            vb = v_ref[b, :, hs].astype(BF16)
            qf, kf, vf = qb.astype(F32), kb.astype(F32), vb.astype(F32)
            c_prev = c_ref[b, h]
            n_prev = n_ref[b, h:h + 1, :]

            m_inter = lf + m_prev
            m_t = jnp.maximum(m_inter, ig)
            w_intra = jnp.sum(qf * kf, axis=-1, keepdims=True) * jnp.exp(ig - m_t)
            w_inter = jnp.exp(m_inter - m_t)
            q_c = _dot(jnp.broadcast_to(qb, (SUBLANES, M_DK)), c_prev.astype(BF16))[0:1, :]
            num = w_intra * vf + w_inter * q_c
            den = w_intra + w_inter * jnp.sum(qf * n_prev, axis=-1, keepdims=True)
            hh = num / jnp.maximum(jnp.abs(den), jnp.exp(-m_t))
            hn = hh * lax.rsqrt(jnp.mean(hh * hh, axis=-1, keepdims=True) + EPS) * mhn_ref[h:h + 1, :]
            ob_ref[b, :, hs] = hn * og_ref[b, :, hs]

            ks = kf * jnp.exp(ig - m_t)
            ks8 = jnp.where(row0, jnp.broadcast_to(ks, (SUBLANES, M_DK)), 0.0).astype(BF16)
            outer = lax.dot_general(ks8, jnp.broadcast_to(vb, (SUBLANES, M_DK)), _TN, preferred_element_type=F32)
            c_out[b, h] = w_inter * c_prev + outer
            n_out[b, h:h + 1, :] = w_inter * n_prev + ks
            m_vec = jnp.where(lane == h, m_t, m_vec)
        m_out[b] = m_vec


def _mlstm_step(qm, km, vm, og, gates, c0, n0, m0, mhn):
    NB = qm.shape[0]
    nb = STEP_NB_MLSTM
    tok = pl.BlockSpec((nb, 1, M_INNER), lambda b: (b, 0, 0))
    vec = pl.BlockSpec((nb, 1, LANES), lambda b: (b, 0, 0))
    cspec = pl.BlockSpec((nb, M_HEADS, M_DK, M_DK), lambda b: (b, 0, 0, 0))
    nspec = pl.BlockSpec((nb, M_HEADS, M_DK), lambda b: (b, 0, 0))
    return pl.pallas_call(
        functools.partial(_mlstm_step_kernel, nb=nb),
        grid=(NB // nb,),
        in_specs=[tok, tok, tok, tok, vec, cspec, nspec, vec, pl.BlockSpec((M_HEADS, M_DK), lambda b: (0, 0))],
        out_specs=[tok, cspec, nspec, vec],
        out_shape=[jax.ShapeDtypeStruct((NB, 1, M_INNER), F32),
                   jax.ShapeDtypeStruct(c0.shape, F32),
                   jax.ShapeDtypeStruct(n0.shape, F32),
                   jax.ShapeDtypeStruct((NB, 1, LANES), F32)],
        compiler_params=pltpu.CompilerParams(dimension_semantics=("arbitrary",), vmem_limit_bytes=VMEM_LIMIT),
        name="mlstm_step",
    )(qm, km, vm, og, gates, c0, n0, m0, mhn)


def _post_kernel(x_ref, o0_ref, o1_ref, o2_ref, l0_ref, l1_ref, l2_ref, ob_ref, ga_ref, gb_ref, p_ref,
                 wpa_ref, wpb_ref, wout_ref, nffn_ref, wg_ref, wu_ref, wd_ref, nple_ref, wpg_ref, wpp_ref,
                 nfin_ref, y_ref, *ubuf, dils):
    lead = (0,) * (len(x_ref.shape) - 2)
    rd = lambda r: r[lead] if lead else r[...]
    tm = x_ref.shape[-2]

    def undilate(ref, dil, slot):
        if dil == 1:
            return rd(ref)
        halves = range(GROUP_W // LANES)
        for r in range(dil):
            for c in halves:
                lo = r * GROUP_W + c * LANES
                ubuf[0][slot, c, pl.ds(r, tm // dil, stride=dil), :] = ref[0, :, lo:lo + LANES]
        return jnp.concatenate([ubuf[0][slot, c] for c in halves], axis=1)

    l0, l1, l2 = (undilate(r, d, 2 * g) for g, (r, d) in enumerate(zip((l0_ref, l1_ref, l2_ref), dils)))
    o0, o1, o2 = (undilate(r, d, 2 * g + 1) for g, (r, d) in enumerate(zip((o0_ref, o1_ref, o2_ref), dils)))
    lmax = jnp.maximum(jnp.maximum(l0, l1), l2)
    e0, e1, e2 = jnp.exp(l0 - lmax), jnp.exp(l1 - lmax), jnp.exp(l2 - lmax)
    o_a = (e0 * o0 + e1 * o1 + e2 * o2) / (e0 + e1 + e2)

    mix = (rd(ga_ref).astype(F32) * _dot(o_a.astype(BF16), wpa_ref[...])
           + rd(gb_ref).astype(F32) * _dot(rd(ob_ref).astype(BF16), wpb_ref[...]))
    h = rd(x_ref) + _dot(mix.astype(BF16), wout_ref[...])

    xf = _rms(h, nffn_ref[...]).astype(BF16)
    half = D_FF // 2
    for c in range(2):
        cs = slice(c * half, (c + 1) * half)
        gate = _dot(xf, wg_ref[:, cs])
        act = (gate * _sigmoid(gate)) * _dot(xf, wu_ref[:, cs])
        h = h + _dot(act.astype(BF16), wd_ref[cs, :])

    xp = _rms(h, nple_ref[...]).astype(BF16)
    h = h + _sigmoid(_dot(xp, wpg_ref[...])) * _dot(rd(p_ref).astype(BF16), wpp_ref[...])
    y = _rms(h, nfin_ref[...])
    if lead:
        y_ref[lead] = y
    else:
        y_ref[...] = y


def _post_seq(x, os_, ls_, ob, ga, gb, p, wts):
    B, S, _ = x.shape
    tm = TM_POST
    tok = lambda w: pl.BlockSpec((1, tm, w), lambda b, i: (b, i, 0))
    dils = tuple(dil for _, dil in DSWA_GROUPS)
    dil_specs = [pl.BlockSpec((1, tm // dil, dil * GROUP_W), lambda b, i: (b, i, 0)) for dil in dils]
    in_specs = ([tok(D_MODEL)] + dil_specs * 2 + [tok(M_INNER)] * 3 + [tok(PLE_DIM)]
                + [_const_spec(w.shape) for w in wts])
    return pl.pallas_call(
        functools.partial(_post_kernel, dils=dils),
        grid=(B, S // tm),
        in_specs=in_specs,
        out_specs=tok(D_MODEL),
        out_shape=jax.ShapeDtypeStruct((B, S, D_MODEL), F32),
        scratch_shapes=[pltpu.VMEM((2 * N_GROUPS, GROUP_W // LANES, tm, LANES), F32)],
        compiler_params=pltpu.CompilerParams(
            dimension_semantics=("arbitrary", "arbitrary"), vmem_limit_bytes=VMEM_LIMIT),
        name="post_seq",
    )(x, *os_, *ls_, ob, ga, gb, p, *wts)


def _post_step(x, os_, ls_, ob, ga, gb, p, wts):
    return pl.pallas_call(
        functools.partial(_post_kernel, dils=(1,) * N_GROUPS),
        out_shape=jax.ShapeDtypeStruct(x.shape, F32),
        compiler_params=pltpu.CompilerParams(vmem_limit_bytes=VMEM_LIMIT),
        name="post_step",
    )(x, *os_, *ls_, ob, ga, gb, p, *wts)


def _rope_tables(pos):
    pos = np.asarray(pos, np.float64)
    t = pos.shape[0]
    inv = np.power(ROPE_THETA, -np.arange(ROPE_HALF, dtype=np.float64) / ROPE_HALF)
    ang = pos[:, None] * inv[None, :]
    cos, sin = np.cos(ang), np.sin(ang)
    one = np.ones((t, HEAD_DIM - ROPE_DIM))
    zero = np.zeros((t, HEAD_DIM - ROPE_DIM))
    z8 = np.zeros((t, ROPE_HALF))
    a = np.concatenate([cos, cos, one], axis=1)
    m = np.concatenate([-sin, z8, zero], axis=1)
    p = np.concatenate([z8, sin, zero], axis=1)
    rep = LANES // HEAD_DIM
    return tuple(jnp.asarray(np.tile(v, (1, rep)).astype(np.float32)) for v in (a, m, p))


def kernel(x_prompt, x_sample, cache_kv_w128, cache_kv_w512, cache_kv_w2048, state_conv, state_C, state_n,
           state_m, p_prompt, p_sample, norm_mix, w_in, conv_w, conv_b, b_igate, b_fgate, mh_norm, w_proj_a,
           w_proj_b, w_out, norm_ffn, w_gate, w_up, w_down, norm_ple, w_ple_gate, w_ple_proj, norm_final):
    depth = w_in.shape[0]
    assert depth == 1, "single trunk layer"
    B, S, _ = x_prompt.shape
    NB = x_sample.shape[0]
    assert x_sample.shape[1] == 1
    caches = (cache_kv_w128, cache_kv_w512, cache_kv_w2048)
    assert S >= DSWA_GROUPS[-1][0] and S % (BAND * DSWA_GROUPS[-1][1]) == 0
    past = 0
    l = 0

    wt = w_in[l].T.astype(BF16)
    gate_wt = wt[OFF_MI:OFF_GA]
    gate_b = jnp.concatenate([b_igate[l], b_fgate[l]])
    in_wts = (
        norm_mix[l][None, :],
        wt[OFF_AQ:OFF_MQ],
        wt[OFF_MQ:OFF_MV],
        wt[OFF_MV:OFF_MO],
        wt[OFF_MO:OFF_MI],
        wt[OFF_GA:OFF_GB],
        wt[OFF_GB:IN_COLS],
        jnp.pad(gate_wt, ((0, LANES - 2 * M_HEADS), (0, 0))),
        gate_wt,
        jnp.pad(gate_b, (0, LANES - 2 * M_HEADS))[None, :],
        gate_b[:, None],
        conv_w[l],
        conv_b[l][None, :],
    )
    post_wts = (
        w_proj_a[l].astype(BF16), w_proj_b[l].astype(BF16), w_out[l].astype(BF16), norm_ffn[l][None, :],
        w_gate[l].astype(BF16), w_up[l].astype(BF16), w_down[l].astype(BF16), norm_ple[l][None, :],
        w_ple_gate[l].astype(BF16), w_ple_proj[l].astype(BF16), norm_final[None, :],
    )
    mhn = mh_norm[l]

    c0 = jnp.zeros((B, SUBLANES, 2 * M_INNER), F32)
    tabs_p = _rope_tables(past + np.arange(S))
    (*qkv, qm, km, vm, og, ga, gb, gc, gr, kv0, kv1, kv2, cn) = _inproj_seq(x_prompt, c0, tabs_p, in_wts)
    os_p, ls_p = [], []
    for g, (_, dil) in enumerate(DSWA_GROUPS):
        o, lse = _attn_seq(*qkv[3 * g:3 * g + 3], dil)
        os_p.append(o)
        ls_p.append(lse)
    ob, c_p, n_p, m_p = _mlstm_seq(qm, km, vm, og, gc, gr, mhn)
    y_prompt = _post_seq(x_prompt, os_p, ls_p, ob, ga, gb, p_prompt[l], post_wts)

    tabs_s = _rope_tables(np.full((NB,), PAST_LEN))
    conv_s = jnp.swapaxes(state_conv[l], 0, 1)
    (qa_s, kvn0, kvn1, kvn2, qm_s, km_s, vm_s, og_s, ga_s, gb_s, gc_s, cn_s) = _inproj_step(
        x_sample[:, 0, :], conv_s, tabs_s, in_wts)
    kvn = (kvn0, kvn1, kvn2)
    caches_t = [jnp.transpose(c[l], (0, 2, 3, 4, 1)).reshape(NB, 2, GROUP_W, c.shape[2]) for c in caches]
    os_s, ls_s, kv_s = [], [], []
    for g, (_, dil) in enumerate(DSWA_GROUPS):
        o, lse, cache_new = _attn_step(qa_s[:, None, :], kvn[g][:, None, :], caches_t[g], g, dil)
        os_s.append(o[:, 0, :])
        ls_s.append(lse[:, 0, :])
        kv_s.append(jnp.transpose(cache_new.reshape(NB, 2, HPG, HEAD_DIM, -1), (0, 4, 1, 2, 3)))
    m0 = jnp.pad(state_m[l], ((0, 0), (0, LANES - M_HEADS)))[:, None, :]
    ob_s, c_s, n_s, m_s = _mlstm_step(qm_s[:, None, :], km_s[:, None, :], vm_s[:, None, :], og_s[:, None, :],
                                      gc_s[:, None, :], state_C[l], state_n[l], m0, mhn)
    y_sample = _post_step(x_sample[:, 0, :], os_s, ls_s, ob_s[:, 0, :], ga_s, gb_s, p_sample[l][:, 0, :], post_wts)

    kv_shape = lambda n, w: (1, n, w, 2, HPG, HEAD_DIM)
    return (
        y_prompt, y_sample[:, None, :],
        kv0.reshape(kv_shape(B, DSWA_GROUPS[0][0])), kv_s[0].reshape(kv_shape(NB, DSWA_GROUPS[0][0])),
        kv1.reshape(kv_shape(B, DSWA_GROUPS[1][0])), kv_s[1].reshape(kv_shape(NB, DSWA_GROUPS[1][0])),
        kv2.reshape(kv_shape(B, DSWA_GROUPS[2][0])), kv_s[2].reshape(kv_shape(NB, DSWA_GROUPS[2][0])),
        cn[:, SUBLANES - (CONV_W - 1):, :][None], jnp.swapaxes(cn_s, 0, 1)[None],
        c_p[None], c_s[None],
        n_p[None], n_s[None],
        m_p[:, 0, :M_HEADS][None], m_s[:, 0, :M_HEADS][None],
    )
```

```python
import functools

import jax
import jax.numpy as jnp
import numpy as np
from jax import lax
from jax.experimental import pallas as pl
from jax.experimental.pallas import tpu as pltpu

F32 = jnp.float32
BF16 = jnp.bfloat16

D_MODEL = 1024
PAST_LEN = 16384
HEAD_DIM = 64
ROPE_DIM = HEAD_DIM // 4
ROPE_HALF = ROPE_DIM // 2
ROPE_THETA = 500000.0
DSWA_GROUPS = ((128, 1), (512, 4), (2048, 16))
HPG = 4
N_GROUPS = len(DSWA_GROUPS)
GROUP_W = HPG * HEAD_DIM
A_QKV = N_GROUPS * GROUP_W
BAND = 128
M_HEADS = 4
M_INNER = D_MODEL
M_DK = M_INNER // M_HEADS
CONV_W = 4
D_FF = 2816
PLE_DIM = 256
EPS = 1e-6
NEG = -1e30

OFF_AQ = 0
OFF_AK = OFF_AQ + A_QKV
OFF_AV = OFF_AK + A_QKV
OFF_MQ = OFF_AV + A_QKV
OFF_MK = OFF_MQ + M_INNER
OFF_MV = OFF_MK + M_INNER
OFF_MO = OFF_MV + M_INNER
OFF_MI = OFF_MO + M_INNER
OFF_MF = OFF_MI + M_HEADS
OFF_GA = OFF_MF + M_HEADS
OFF_GB = OFF_GA + D_MODEL
IN_COLS = OFF_GB + D_MODEL

LANES = 128
SUBLANES = 8
VMEM_LIMIT = 56 * 1024 * 1024

TM_IN = 512
TM_POST = 256
M_CHUNK = 256
ATT_TQ = 512

_NT = (((1,), (1,)), ((), ()))
_TN = (((0,), (0,)), ((), ()))


def _dot(a, b):
    return jnp.dot(a, b, preferred_element_type=F32)


def _wdot(a, wt_ref, lo, hi):
    return lax.dot_general(a, wt_ref[lo:hi, :], _NT, preferred_element_type=F32)


def _rms(x, g):
    return x * lax.rsqrt(jnp.mean(x * x, axis=-1, keepdims=True) + EPS) * g


def _bf16_pieces(x):
    hi = x.astype(BF16)
    r1 = x - hi.astype(F32)
    mid = r1.astype(BF16)
    lo = (r1 - mid.astype(F32)).astype(BF16)
    return hi, mid, lo


def _sigmoid(x):
    return 0.5 * jnp.tanh(0.5 * x) + 0.5


def _log_sigmoid(x):
    return jnp.minimum(x, 0.0) - jnp.log1p(jnp.exp(-jnp.abs(x)))


def _rope(t, ra, rm, rp):
    return t * ra + pltpu.roll(t, LANES - ROPE_HALF, 1) * rm + pltpu.roll(t, ROPE_HALF, 1) * rp


def _rope2(t, ra, rm, rp):
    return jnp.concatenate([_rope(t[:, :LANES], ra, rm, rp), _rope(t[:, LANES:], ra, rm, rp)], axis=1)


def _gate_cols(z):
    lane = lax.broadcasted_iota(jnp.int32, z.shape, 1)
    return jnp.where(lane < M_HEADS, z, _log_sigmoid(z))


def _inproj_seq_kernel(x_ref, nmix_ref, waqkv_ref, wmqk_ref, wmv_ref, wmo_ref, wga_ref, wgb_ref,
                       wgc_ref, wgr_ref, bc_ref, br_ref, cw_ref, cb_ref, c0_ref, ra_ref, rm_ref, rp_ref,
                       q0_ref, k0_ref, v0_ref, q1_ref, k1_ref, v1_ref, q2_ref, k2_ref, v2_ref,
                       qm_ref, km_ref, vm_ref, og_ref, ga_ref, gb_ref,
                       gc_ref, gr_ref, kv0_ref, kv1_ref, kv2_ref, cn_ref, cbuf, dbuf, *, tm, ns):
    i = pl.program_id(1)
    xn = _rms(x_ref[0], nmix_ref[...]).astype(BF16)
    ra, rm, rp = ra_ref[...], rm_ref[...], rp_ref[...]
    kv_refs = (kv0_ref, kv1_ref, kv2_ref)
    qkv_refs = ((q0_ref, k0_ref, v0_ref), (q1_ref, k1_ref, v1_ref), (q2_ref, k2_ref, v2_ref))

    def put_dilated(ref, val, dil, slot):
        if dil == 1:
            ref[0] = val.astype(BF16)
            return
        for c in range(GROUP_W // LANES):
            dbuf[slot, c] = val[:, c * LANES:(c + 1) * LANES]
        for r in range(dil):
            for c in range(GROUP_W // LANES):
                lo = r * GROUP_W + c * LANES
                ref[0, :, lo:lo + LANES] = dbuf[slot, c, pl.ds(r, tm // dil, stride=dil), :].astype(BF16)

    for g, (win, dil) in enumerate(DSWA_GROUPS):
        zq = _wdot(xn, waqkv_ref, OFF_AQ + g * GROUP_W, OFF_AQ + (g + 1) * GROUP_W)
        put_dilated(qkv_refs[g][0], _rope2(zq, ra, rm, rp) * (HEAD_DIM ** -0.5), dil, 3 * g)
        zk = _wdot(xn, waqkv_ref, OFF_AK + g * GROUP_W, OFF_AK + (g + 1) * GROUP_W)
        kr = _rope2(zk, ra, rm, rp)
        put_dilated(qkv_refs[g][1], kr, dil, 3 * g + 1)
        zv = _wdot(xn, waqkv_ref, OFF_AV + g * GROUP_W, OFF_AV + (g + 1) * GROUP_W)
        put_dilated(qkv_refs[g][2], zv, dil, 3 * g + 2)
        rows = min(win, tm)
        nblk = win // rows

        @pl.when(i >= ns - nblk)
        def _():
            kv_refs[g][0, :, :GROUP_W] = kr[tm - rows:, :]
            kv_refs[g][0, :, GROUP_W:] = zv[tm - rows:, :]

    @pl.when(i == 0)
    def _():
        for blk in range(2 * M_INNER // LANES):
            cbuf[blk, 0:SUBLANES, :] = c0_ref[0, :, blk * LANES:(blk + 1) * LANES]

    for c in range(2 * M_INNER // 256):
        pre = _wdot(xn, wmqk_ref, c * 256, (c + 1) * 256)
        halves = []
        for hf in range(256 // LANES):
            blk = c * (256 // LANES) + hf
            ls = slice(blk * LANES, (blk + 1) * LANES)
            pre_h = pre[:, hf * LANES:(hf + 1) * LANES]
            cbuf[blk, SUBLANES:SUBLANES + tm, :] = pre_h
            y = cb_ref[:, ls] + cw_ref[CONV_W - 1:CONV_W, ls] * pre_h
            for d in range(1, CONV_W):
                y = y + cw_ref[CONV_W - 1 - d:CONV_W - d, ls] * cbuf[blk, pl.ds(SUBLANES - d, tm), :]
            tail = pre_h[tm - SUBLANES:, :]
            cbuf[blk, 0:SUBLANES, :] = tail
            cn_ref[0, :, ls] = tail
            halves.append(y * _sigmoid(y))
        s = jnp.concatenate(halves, axis=1)
        if c < M_INNER // 256:
            qm_ref[0, :, c * 256:(c + 1) * 256] = s.astype(BF16)
        else:
            km_ref[0, :, c * 256 - M_INNER:(c + 1) * 256 - M_INNER] = (s * (M_DK ** -0.5)).astype(BF16)

    for c in range(M_INNER // 256):
        cs = slice(c * 256, (c + 1) * 256)
        vm_ref[0, :, cs] = _wdot(xn, wmv_ref, cs.start, cs.stop).astype(BF16)
        og_ref[0, :, cs] = _sigmoid(_wdot(xn, wmo_ref, cs.start, cs.stop)).astype(BF16)
        ga_ref[0, :, cs] = _sigmoid(_wdot(xn, wga_ref, cs.start, cs.stop)).astype(BF16)
        gb_ref[0, :, cs] = _sigmoid(_wdot(xn, wgb_ref, cs.start, cs.stop)).astype(BF16)

    gcv = _gate_cols(_wdot(xn, wgc_ref, 0, LANES) + bc_ref[...])
    zr = lax.dot_general(wgr_ref[...], xn, _NT, preferred_element_type=F32) + br_ref[...]
    sub = lax.broadcasted_iota(jnp.int32, zr.shape, 0)
    grv = jnp.where(sub < M_HEADS, zr, _log_sigmoid(zr))
    L = M_CHUNK
    ti = lax.broadcasted_iota(jnp.int32, (L, L), 0)
    si = lax.broadcasted_iota(jnp.int32, (L, L), 1)
    lower = jnp.where(ti >= si, 1.0, 0.0).astype(BF16)
    upper = jnp.where(ti <= si, 1.0, 0.0).astype(BF16)
    lane = lax.broadcasted_iota(jnp.int32, (L, LANES), 1)
    sub_l = lax.broadcasted_iota(jnp.int32, (SUBLANES, L), 0)
    for ch in range(tm // L):
        rs = slice(ch * L, (ch + 1) * L)
        cum_c = sum(_dot(lower, p) for p in _bf16_pieces(gcv[rs, :]))
        cum_r = sum(_dot(p, upper) for p in _bf16_pieces(grv[:, rs]))
        gc_ref[0, rs, :] = jnp.where(lane < M_HEADS, gcv[rs, :], cum_c)
        gr_ref[0, :, rs] = jnp.where(sub_l < M_HEADS, grv[:, rs], cum_r)


def _const_spec(shape):
    nd = len(shape)
    return pl.BlockSpec(shape, lambda *_: (0,) * nd, pipeline_mode=pl.Buffered(1))


def _inproj_seq(x, c0, tables, wts):
    B, S, _ = x.shape
    tm = TM_IN
    ns = S // tm
    ra, rm, rp = tables

    def kv_spec(win):
        rows = min(win, tm)
        nblk = win // rows
        return pl.BlockSpec((1, rows, 2 * GROUP_W), lambda b, i: (b, jnp.maximum(i - (ns - nblk), 0), 0))

    tok = lambda w: pl.BlockSpec((1, tm, w), lambda b, i: (b, i, 0))
    tab = pl.BlockSpec((tm, LANES), lambda b, i: (i, 0))
    in_specs = ([tok(D_MODEL)] + [_const_spec(w.shape) for w in wts]
                + [pl.BlockSpec((1, SUBLANES, 2 * M_INNER), lambda b, i: (b, 0, 0)), tab, tab, tab])
    dil_shape = [jax.ShapeDtypeStruct((B, S // dil, dil * GROUP_W), BF16) for _, dil in DSWA_GROUPS for _ in range(3)]
    dil_specs = [pl.BlockSpec((1, tm // dil, dil * GROUP_W), lambda b, i: (b, i, 0))
                 for _, dil in DSWA_GROUPS for _ in range(3)]
    out_shape = (dil_shape
                 + [jax.ShapeDtypeStruct((B, S, M_INNER), BF16)] * 6
                 + [jax.ShapeDtypeStruct((B, S, LANES), F32), jax.ShapeDtypeStruct((B, SUBLANES, S), F32)]
                 + [jax.ShapeDtypeStruct((B, win, 2 * GROUP_W), F32) for win, _ in DSWA_GROUPS]
                 + [jax.ShapeDtypeStruct((B, SUBLANES, 2 * M_INNER), F32)])
    out_specs = (dil_specs + [tok(M_INNER)] * 6
                 + [tok(LANES), pl.BlockSpec((1, SUBLANES, tm), lambda b, i: (b, 0, i))]
                 + [kv_spec(win) for win, _ in DSWA_GROUPS]
                 + [pl.BlockSpec((1, SUBLANES, 2 * M_INNER), lambda b, i: (b, 0, 0))])
    return pl.pallas_call(
        functools.partial(_inproj_seq_kernel, tm=tm, ns=ns),
        grid=(B, ns),
        in_specs=in_specs,
        out_specs=out_specs,
        out_shape=out_shape,
        scratch_shapes=[pltpu.VMEM((2 * M_INNER // LANES, SUBLANES + tm, LANES), F32),
                        pltpu.VMEM((3 * N_GROUPS, GROUP_W // LANES, tm, LANES), F32)],
        compiler_params=pltpu.CompilerParams(
            dimension_semantics=("arbitrary", "arbitrary"), vmem_limit_bytes=VMEM_LIMIT),
        name="inproj_seq",
    )(x, *wts, c0, ra, rm, rp)


def _inproj_step_kernel(x_ref, nmix_ref, waqkv_ref, wmqk_ref, wmv_ref, wmo_ref, wga_ref, wgb_ref,
                        wgc_ref, wgr_ref, bc_ref, br_ref, cw_ref, cb_ref, cs_ref, ra_ref, rm_ref, rp_ref,
                        qa_ref, kv0_ref, kv1_ref, kv2_ref, qm_ref, km_ref, vm_ref, og_ref, ga_ref, gb_ref,
                        gc_ref, cn_ref):
    del wgr_ref, br_ref
    xn = _rms(x_ref[...], nmix_ref[...]).astype(BF16)
    ra, rm, rp = ra_ref[...], rm_ref[...], rp_ref[...]
    kv_refs = (kv0_ref, kv1_ref, kv2_ref)
    for g in range(N_GROUPS):
        cs = slice(g * GROUP_W, (g + 1) * GROUP_W)
        zq = _wdot(xn, waqkv_ref, OFF_AQ + g * GROUP_W, OFF_AQ + (g + 1) * GROUP_W)
        qa_ref[:, cs] = _rope2(zq, ra, rm, rp) * (HEAD_DIM ** -0.5)
        zk = _wdot(xn, waqkv_ref, OFF_AK + g * GROUP_W, OFF_AK + (g + 1) * GROUP_W)
        kv_refs[g][:, :GROUP_W] = _rope2(zk, ra, rm, rp)
        kv_refs[g][:, GROUP_W:] = _wdot(xn, waqkv_ref, OFF_AV + g * GROUP_W, OFF_AV + (g + 1) * GROUP_W)

    for c in range(2 * M_INNER // 256):
        cs = slice(c * 256, (c + 1) * 256)
        pre = _wdot(xn, wmqk_ref, cs.start, cs.stop)
        y = cb_ref[:, cs] + cw_ref[CONV_W - 1:CONV_W, cs] * pre
        for j in range(CONV_W - 1):
            y = y + cw_ref[j:j + 1, cs] * cs_ref[j, :, cs]
        for j in range(CONV_W - 2):
            cn_ref[j, :, cs] = cs_ref[j + 1, :, cs]
        cn_ref[CONV_W - 2, :, cs] = pre
        s = y * _sigmoid(y)
        if c < M_INNER // 256:
            qm_ref[:, cs] = s
        else:
            km_ref[:, c * 256 - M_INNER:(c + 1) * 256 - M_INNER] = s * (M_DK ** -0.5)

    for c in range(M_INNER // 256):
        cs = slice(c * 256, (c + 1) * 256)
        vm_ref[:, cs] = _wdot(xn, wmv_ref, cs.start, cs.stop)
        og_ref[:, cs] = _sigmoid(_wdot(xn, wmo_ref, cs.start, cs.stop))
        ga_ref[:, cs] = _sigmoid(_wdot(xn, wga_ref, cs.start, cs.stop))
        gb_ref[:, cs] = _sigmoid(_wdot(xn, wgb_ref, cs.start, cs.stop))
    gc_ref[...] = _gate_cols(_wdot(xn, wgc_ref, 0, LANES) + bc_ref[...])


def _inproj_step(x, conv_state, tables, wts):
    nb = x.shape[0]
    sds = lambda w: jax.ShapeDtypeStruct((nb, w), F32)
    out_shape = ([sds(A_QKV)] + [sds(2 * GROUP_W)] * N_GROUPS + [sds(M_INNER)] * 6 + [sds(LANES)]
                 + [jax.ShapeDtypeStruct((CONV_W - 1, nb, 2 * M_INNER), F32)])
    return pl.pallas_call(
        _inproj_step_kernel,
        out_shape=out_shape,
        compiler_params=pltpu.CompilerParams(vmem_limit_bytes=VMEM_LIMIT),
        name="inproj_step",
    )(x, *wts, conv_state, *tables)


def _attn_seq_kernel(q_ref, k_ref, v_ref, kp_ref, vp_ref, o_ref, lse_ref, *, tq):
    i = pl.program_id(2)
    lane_head = lax.broadcasted_iota(jnp.int32, (1, GROUP_W), 1) // HEAD_DIM
    head_mask = [jnp.where(lane_head == h, 1.0, 0.0).astype(BF16) for h in range(HPG)]
    lane_head_full = lax.broadcasted_iota(jnp.int32, (BAND, GROUP_W), 1) // HEAD_DIM
    qi = lax.broadcasted_iota(jnp.int32, (BAND, 2 * BAND), 0)
    kj = lax.broadcasted_iota(jnp.int32, (BAND, 2 * BAND), 1)
    band = (kj >= qi) & (kj <= qi + BAND)
    first_lo = jnp.where(i > 0, 0, BAND)
    bias = jnp.where(band, 0.0, NEG)
    bias_first = jnp.where(band & (kj >= first_lo), 0.0, NEG)

    for j in range(tq // BAND):
        b1 = bias_first if j == 0 else bias
        q = q_ref[0, j * BAND:(j + 1) * BAND, :]
        if j == 0:
            kk = jnp.concatenate([kp_ref[0], k_ref[0, 0:BAND, :]], axis=0)
            vv = jnp.concatenate([vp_ref[0], v_ref[0, 0:BAND, :]], axis=0)
        else:
            kk = k_ref[0, (j - 1) * BAND:(j + 1) * BAND, :]
            vv = v_ref[0, (j - 1) * BAND:(j + 1) * BAND, :]
        qblk = jnp.concatenate([q * head_mask[h] for h in range(HPG)], axis=0)
        s = lax.dot_general(qblk, kk, _NT, preferred_element_type=F32) + jnp.concatenate([b1] * HPG, axis=0)
        mx = jnp.max(s, axis=-1, keepdims=True)
        e = jnp.exp(s - mx)
        den = jnp.sum(e, axis=-1, keepdims=True)
        pv = _dot(e.astype(BF16), vv) * (1.0 / den)
        lse_rows = mx + jnp.log(den)
        o = pv[0:BAND]
        lse = jnp.broadcast_to(lse_rows[0:BAND], (BAND, GROUP_W))
        for h in range(1, HPG):
            own = lane_head_full == h
            o = jnp.where(own, pv[h * BAND:(h + 1) * BAND], o)
            lse = jnp.where(own, lse_rows[h * BAND:(h + 1) * BAND], lse)
        o_ref[0, j * BAND:(j + 1) * BAND, :] = o
        lse_ref[0, j * BAND:(j + 1) * BAND, :] = lse


def _attn_seq(q, k, v, dil):
    B, nsub, _ = q.shape
    tq = min(ATT_TQ, nsub)
    in_spec = pl.BlockSpec((1, tq, GROUP_W), lambda b, r, i: (b, i, r))
    prev_spec = pl.BlockSpec((1, BAND, GROUP_W), lambda b, r, i: (b, jnp.maximum(i * (tq // BAND) - 1, 0), r))
    out_spec = pl.BlockSpec((1, tq, GROUP_W), lambda b, r, i: (b, i, r))
    o, lse = pl.pallas_call(
        functools.partial(_attn_seq_kernel, tq=tq),
        grid=(B, dil, nsub // tq),
        in_specs=[in_spec] * 3 + [prev_spec] * 2,
        out_specs=[out_spec] * 2,
        out_shape=[jax.ShapeDtypeStruct((B, nsub, dil * GROUP_W), F32)] * 2,
        compiler_params=pltpu.CompilerParams(
            dimension_semantics=("arbitrary", "arbitrary", "arbitrary"), vmem_limit_bytes=VMEM_LIMIT),
        name=f"attn_seq_d{dil}",
    )(q, k, v, k, v)
    return o, lse


def _mlstm_seq_kernel(q_ref, k_ref, v_ref, og_ref, gc_ref, gr_ref, mhn_ref,
                      ob_ref, c_out, n_out, m_out, c_s, n_s, m_s, *, L):
    c = pl.program_id(1)

    @pl.when(c == 0)
    def _():
        c_s[...] = jnp.zeros(c_s.shape, F32)
        n_s[...] = jnp.zeros(n_s.shape, F32)
        m_s[...] = jnp.zeros(m_s.shape, F32)

    gc = gc_ref[0]
    gr = gr_ref[0]
    ti = lax.broadcasted_iota(jnp.int32, (L, L), 0)
    si = lax.broadcasted_iota(jnp.int32, (L, L), 1)
    tril = ti >= si
    lane = lax.broadcasted_iota(jnp.int32, (1, LANES), 1)
    m_vec = jnp.zeros((1, LANES), F32)

    for h in range(M_HEADS):
        hs = slice(h * M_DK, (h + 1) * M_DK)
        b_col = gc[:, M_HEADS + h:M_HEADS + h + 1]
        i_col = gc[:, h:h + 1]
        b_row = gr[M_HEADS + h:M_HEADS + h + 1, :]
        i_row = gr[h:h + 1, :]
        m_prev = m_s[h, 0:1, 0:1]
        qh = q_ref[0, :, hs]
        kh = k_ref[0, :, hs]
        vh = v_ref[0, :, hs]
        c_prev = c_s[h]
        n_prev = n_s[h]

        dmat = jnp.where(tril, b_col - b_row + i_row, -jnp.inf)
        m_inter = b_col + m_prev
        m_t = jnp.maximum(m_inter, jnp.max(dmat, axis=-1, keepdims=True))
        w_intra = lax.dot_general(qh, kh, _NT, preferred_element_type=F32) * jnp.exp(dmat - m_t)
        w_inter = jnp.exp(m_inter - m_t)
        num = _dot(w_intra.astype(BF16), vh) + w_inter * _dot(qh, c_prev.astype(BF16))
        den = (jnp.sum(w_intra, axis=-1, keepdims=True)
               + w_inter * jnp.sum(qh.astype(F32) * n_prev, axis=-1, keepdims=True))
        hh = num / jnp.maximum(jnp.abs(den), jnp.exp(-m_t))
        hn = hh * lax.rsqrt(jnp.mean(hh * hh, axis=-1, keepdims=True) + EPS) * mhn_ref[h:h + 1, :]
        ob_ref[0, :, hs] = (hn * og_ref[0, :, hs].astype(F32)).astype(BF16)

        b_last = b_row[:, L - 1:L]
        m_new = jnp.maximum(b_last + m_prev, jnp.max(b_last - b_row + i_row, axis=-1, keepdims=True))
        wk = jnp.exp(b_last - b_col + i_col - m_new)
        decay = jnp.exp(b_last + m_prev - m_new)
        ks = kh.astype(F32) * wk
        c_new = decay * c_prev + lax.dot_general(ks.astype(BF16), vh, _TN, preferred_element_type=F32)
        n_new = decay * n_prev + jnp.sum(ks, axis=0, keepdims=True)
        c_s[h] = c_new
        n_s[h] = n_new
        m_s[h] = jnp.broadcast_to(m_new, (SUBLANES, LANES))
        c_out[0, h] = c_new
        n_out[0, h:h + 1, :] = n_new
        m_vec = jnp.where(lane == h, m_new, m_vec)
    m_out[0] = m_vec


def _mlstm_seq(qm, km, vm, og, gc, gr, mhn):
    B, S, _ = qm.shape
    L = M_CHUNK
    tok = pl.BlockSpec((1, L, M_INNER), lambda b, c: (b, c, 0))
    return pl.pallas_call(
        functools.partial(_mlstm_seq_kernel, L=L),
        grid=(B, S // L),
        in_specs=[tok, tok, tok, tok,
                  pl.BlockSpec((1, L, LANES), lambda b, c: (b, c, 0)),
                  pl.BlockSpec((1, SUBLANES, L), lambda b, c: (b, 0, c)),
                  pl.BlockSpec((M_HEADS, M_DK), lambda b, c: (0, 0))],
        out_specs=[tok,
                   pl.BlockSpec((1, M_HEADS, M_DK, M_DK), lambda b, c: (b, 0, 0, 0)),
                   pl.BlockSpec((1, M_HEADS, M_DK), lambda b, c: (b, 0, 0)),
                   pl.BlockSpec((1, 1, LANES), lambda b, c: (b, 0, 0))],
        out_shape=[jax.ShapeDtypeStruct((B, S, M_INNER), BF16),
                   jax.ShapeDtypeStruct((B, M_HEADS, M_DK, M_DK), F32),
                   jax.ShapeDtypeStruct((B, M_HEADS, M_DK), F32),
                   jax.ShapeDtypeStruct((B, 1, LANES), F32)],
        scratch_shapes=[pltpu.VMEM((M_HEADS, M_DK, M_DK), F32),
                        pltpu.VMEM((M_HEADS, 1, M_DK), F32),
                        pltpu.VMEM((M_HEADS, SUBLANES, LANES), F32)],
        compiler_params=pltpu.CompilerParams(
            dimension_semantics=("arbitrary", "arbitrary"), vmem_limit_bytes=VMEM_LIMIT),
        name="mlstm_seq",
    )(qm, km, vm, og, gc, gr, mhn)


def _post_kernel(x_ref, o0_ref, o1_ref, o2_ref, l0_ref, l1_ref, l2_ref, ob_ref, ga_ref, gb_ref, p_ref,
                 wpa_ref, wpb_ref, wout_ref, nffn_ref, wg_ref, wu_ref, wd_ref, nple_ref, wpg_ref, wpp_ref,
                 nfin_ref, y_ref, *ubuf, dils):
    lead = (0,) * (len(x_ref.shape) - 2)
    rd = lambda r: r[lead] if lead else r[...]
    tm = x_ref.shape[-2]

    def undilate(ref, dil, slot):
        if dil == 1:
            return rd(ref)
        halves = range(GROUP_W // LANES)
        for r in range(dil):
            for c in halves:
                lo = r * GROUP_W + c * LANES
                ubuf[0][slot, c, pl.ds(r, tm // dil, stride=dil), :] = ref[0, :, lo:lo + LANES]
        return jnp.concatenate([ubuf[0][slot, c] for c in halves], axis=1)

    l0, l1, l2 = (undilate(r, d, 2 * g) for g, (r, d) in enumerate(zip((l0_ref, l1_ref, l2_ref), dils)))
    o0, o1, o2 = (undilate(r, d, 2 * g + 1) for g, (r, d) in enumerate(zip((o0_ref, o1_ref, o2_ref), dils)))
    lmax = jnp.maximum(jnp.maximum(l0, l1), l2)
    e0, e1, e2 = jnp.exp(l0 - lmax), jnp.exp(l1 - lmax), jnp.exp(l2 - lmax)
    o_a = (e0 * o0 + e1 * o1 + e2 * o2) / (e0 + e1 + e2)

    mix = (rd(ga_ref).astype(F32) * _dot(o_a.astype(BF16), wpa_ref[...])
           + rd(gb_ref).astype(F32) * _dot(rd(ob_ref).astype(BF16), wpb_ref[...]))
    h = rd(x_ref) + _dot(mix.astype(BF16), wout_ref[...])

    xf = _rms(h, nffn_ref[...]).astype(BF16)
    half = D_FF // 2
    for c in range(2):
        cs = slice(c * half, (c + 1) * half)
        gate = _dot(xf, wg_ref[:, cs])
        act = (gate * _sigmoid(gate)) * _dot(xf, wu_ref[:, cs])
        h = h + _dot(act.astype(BF16), wd_ref[cs, :])

    xp = _rms(h, nple_ref[...]).astype(BF16)
    h = h + _sigmoid(_dot(xp, wpg_ref[...])) * _dot(rd(p_ref).astype(BF16), wpp_ref[...])
    y = _rms(h, nfin_ref[...])
    if lead:
        y_ref[lead] = y
    else:
        y_ref[...] = y


def _post_step(x, os_, ls_, ob, ga, gb, p, wts):
    return pl.pallas_call(
        functools.partial(_post_kernel, dils=(1,) * N_GROUPS),
        out_shape=jax.ShapeDtypeStruct(x.shape, F32),
        compiler_params=pltpu.CompilerParams(vmem_limit_bytes=VMEM_LIMIT),
        name="post_step",
    )(x, *os_, *ls_, ob, ga, gb, p, *wts)


SM_ROWS = 16
SM_LANES = 512
T_QA, T_KN, T_VN, T_QM, T_KM, T_VM, T_OG, T_N, T_GATE = range(9)
SO_ROWS = 8
SO_OB, SO_N, SO_M, SO_LSE, SO_O = range(5)
HALF_HEADS = HPG // 2
HALF_W = HALF_HEADS * HEAD_DIM


def _eye(n):
    return lax.broadcasted_iota(jnp.int32, (n, n), 0) == lax.broadcasted_iota(jnp.int32, (n, n), 1)


def _col_of(row, eye):
    return jnp.sum(jnp.where(eye, jnp.broadcast_to(row, eye.shape), 0.0), axis=1, keepdims=True)


def _row_of(col, eye):
    return jnp.sum(jnp.where(eye, jnp.broadcast_to(col, eye.shape), 0.0), axis=0, keepdims=True)


def _round_bf16(x):
    return x.astype(BF16).astype(F32)


def _sample_step(tile, kv_refs, cst_ref, mhn_ref, so_ref, kvn_refs, cso_ref):
    so_ref[0] = jnp.zeros((SO_ROWS, SM_LANES), F32)
    eye_a = _eye(HALF_W)
    eye_m = _eye(M_DK)

    for g, (w, dil) in enumerate(DSWA_GROUPS):
        ls = slice(g * HALF_W, (g + 1) * HALF_W)
        k_t = kv_refs[g][0, 0, 0]
        v_t = kv_refs[g][0, 1, 0]
        k_row = tile[T_KN:T_KN + 1, ls]
        v_row = tile[T_VN:T_VN + 1, ls]
        qb = _round_bf16(tile[T_QA:T_QA + 1, ls])
        prod = k_t * _col_of(qb, eye_a)
        qk_new = qb * _round_bf16(k_row)
        pos = lax.broadcasted_iota(jnp.int32, (1, w), 1)
        pe, pn, lse = [], [], []
        for hl in range(HALF_HEADS):
            hd = slice(hl * HEAD_DIM, (hl + 1) * HEAD_DIM)
            s = jnp.where(pos % dil == 0, jnp.sum(prod[hd, :], axis=0, keepdims=True), NEG)
            s_new = jnp.sum(qk_new[:, hd], axis=-1, keepdims=True)
            mx = jnp.maximum(jnp.max(s, axis=-1, keepdims=True), s_new)
            e = jnp.exp(s - mx)
            e_new = jnp.exp(s_new - mx)
            den = jnp.sum(e, axis=-1, keepdims=True) + e_new
            pe.append(jnp.broadcast_to(e / den, (HEAD_DIM, w)))
            pn.append(jnp.broadcast_to(e_new / den, (1, HEAD_DIM)))
            lse.append(jnp.broadcast_to(mx + jnp.log(den), (1, HEAD_DIM)))
        o_col = jnp.sum(v_t * jnp.concatenate(pe, axis=0), axis=-1, keepdims=True)
        so_ref[0, SO_O:SO_O + 1, ls] = _row_of(o_col, eye_a) + jnp.concatenate(pn, axis=1) * _round_bf16(v_row)
        so_ref[0, SO_LSE:SO_LSE + 1, ls] = jnp.concatenate(lse, axis=1)
        last = lax.broadcasted_iota(jnp.int32, (HALF_W, w), 1) == w - 1
        kvn_refs[g][0, 0, 0] = jnp.where(last, _col_of(k_row, eye_a), pltpu.roll(k_t, w - 1, 1))
        kvn_refs[g][0, 1, 0] = jnp.where(last, _col_of(v_row, eye_a), pltpu.roll(v_t, w - 1, 1))

    eye = eye_m
    for j in range(M_HEADS // 2):
        hs = slice(j * M_DK, (j + 1) * M_DK)
        ig = tile[T_GATE:T_GATE + 1, j:j + 1]
        lf = tile[T_GATE:T_GATE + 1, 2 + j:3 + j]
        m_prev = tile[T_GATE:T_GATE + 1, 4 + j:5 + j]
        qb = _round_bf16(tile[T_QM:T_QM + 1, hs])
        kb = _round_bf16(tile[T_KM:T_KM + 1, hs])
        vb = _round_bf16(tile[T_VM:T_VM + 1, hs])
        n_prev = tile[T_N:T_N + 1, hs]
        c_prev = cst_ref[0, j]

        m_inter = lf + m_prev
        m_t = jnp.maximum(m_inter, ig)
        w_intra = jnp.sum(qb * kb, axis=-1, keepdims=True) * jnp.exp(ig - m_t)
        w_inter = jnp.exp(m_inter - m_t)
        q_c = jnp.sum(c_prev * _col_of(qb, eye), axis=0, keepdims=True)
        num = w_intra * vb + w_inter * q_c
        den = w_intra + w_inter * jnp.sum(qb * n_prev, axis=-1, keepdims=True)
        hh = num / jnp.maximum(jnp.abs(den), jnp.exp(-m_t))
        hn = hh * lax.rsqrt(jnp.mean(hh * hh, axis=-1, keepdims=True) + EPS) * mhn_ref[0, j:j + 1, :]
        ks = kb * jnp.exp(ig - m_t)
        so_ref[0, SO_OB:SO_OB + 1, hs] = hn * tile[T_OG:T_OG + 1, hs]
        so_ref[0, SO_N:SO_N + 1, hs] = w_inter * n_prev + ks
        so_ref[0, SO_M:SO_M + 1, hs] = jnp.broadcast_to(m_t, (1, M_DK))
        cso_ref[0, j] = w_inter * c_prev + _col_of(_round_bf16(ks), eye) * vb


N_POST_IN = 22


def _post_fused_kernel(*refs, dils, ns):
    post_in = refs[:N_POST_IN]
    sm_ref, c0_ref, c1_ref, c2_ref, cst_ref, mhn_ref = refs[N_POST_IN:N_POST_IN + 6]
    y_ref, so_ref, cn0_ref, cn1_ref, cn2_ref, cso_ref = refs[N_POST_IN + 6:N_POST_IN + 12]
    (ubuf,) = refs[N_POST_IN + 12:]
    _post_kernel(*post_in, y_ref, ubuf, dils=dils)
    _sample_step(sm_ref[0], (c0_ref, c1_ref, c2_ref), cst_ref, mhn_ref, so_ref,
                 (cn0_ref, cn1_ref, cn2_ref), cso_ref)


def _post_fused(x, os_, ls_, ob, ga, gb, p, wts, sm_in, caches_t, c_state, mhn):
    B, S, _ = x.shape
    NB = c_state.shape[0]
    tm = TM_POST
    ns = S // tm
    assert B * ns == 2 * NB, "one half of a sample batch element per post-mixer grid step"
    tok = lambda w: pl.BlockSpec((1, tm, w), lambda b, i: (b, i, 0))
    dils = tuple(dil for _, dil in DSWA_GROUPS)
    dil_specs = [pl.BlockSpec((1, tm // dil, dil * GROUP_W), lambda b, i: (b, i, 0)) for dil in dils]
    step = lambda b, i: b * ns + i
    cache_specs = [pl.BlockSpec((1, 2, 1, HALF_W, c.shape[-1]), lambda b, i: (step(b, i) // 2, 0, step(b, i) % 2, 0, 0))
                   for c in caches_t]
    cst_spec = pl.BlockSpec((1, M_HEADS // 2, M_DK, M_DK), lambda b, i: (step(b, i) // 2, step(b, i) % 2, 0, 0))
    in_specs = ([tok(D_MODEL)] + dil_specs * 2 + [tok(M_INNER)] * 3 + [tok(PLE_DIM)]
                + [_const_spec(w.shape) for w in wts]
                + [pl.BlockSpec((1, SM_ROWS, SM_LANES), lambda b, i: (step(b, i), 0, 0))]
                + cache_specs
                + [cst_spec, pl.BlockSpec((1, M_HEADS // 2, M_DK), lambda b, i: (step(b, i) % 2, 0, 0))])
    out_specs = ([tok(D_MODEL), pl.BlockSpec((1, SO_ROWS, SM_LANES), lambda b, i: (step(b, i), 0, 0))]
                 + cache_specs + [cst_spec])
    out_shape = ([jax.ShapeDtypeStruct((B, S, D_MODEL), F32),
                  jax.ShapeDtypeStruct((2 * NB, SO_ROWS, SM_LANES), F32)]
                 + [jax.ShapeDtypeStruct(c.shape, F32) for c in caches_t]
                 + [jax.ShapeDtypeStruct(c_state.shape, F32)])
    return pl.pallas_call(
        functools.partial(_post_fused_kernel, dils=dils, ns=ns),
        grid=(B, ns),
        in_specs=in_specs,
        out_specs=out_specs,
        out_shape=out_shape,
        scratch_shapes=[pltpu.VMEM((2 * N_GROUPS, GROUP_W // LANES, tm, LANES), F32)],
        compiler_params=pltpu.CompilerParams(
            dimension_semantics=("arbitrary", "arbitrary"), vmem_limit_bytes=VMEM_LIMIT),
        name="post_seq_sample_stream",
    )(x, *os_, *ls_, ob, ga, gb, p, *wts, sm_in, *caches_t, c_state, mhn)


def _rope_tables(pos):
    pos = np.asarray(pos, np.float64)
    t = pos.shape[0]
    inv = np.power(ROPE_THETA, -np.arange(ROPE_HALF, dtype=np.float64) / ROPE_HALF)
    ang = pos[:, None] * inv[None, :]
    cos, sin = np.cos(ang), np.sin(ang)
    one = np.ones((t, HEAD_DIM - ROPE_DIM))
    zero = np.zeros((t, HEAD_DIM - ROPE_DIM))
    z8 = np.zeros((t, ROPE_HALF))
    a = np.concatenate([cos, cos, one], axis=1)
    m = np.concatenate([-sin, z8, zero], axis=1)
    p = np.concatenate([z8, sin, zero], axis=1)
    rep = LANES // HEAD_DIM
    return tuple(jnp.asarray(np.tile(v, (1, rep)).astype(np.float32)) for v in (a, m, p))


def kernel(x_prompt, x_sample, cache_kv_w128, cache_kv_w512, cache_kv_w2048, state_conv, state_C, state_n,
           state_m, p_prompt, p_sample, norm_mix, w_in, conv_w, conv_b, b_igate, b_fgate, mh_norm, w_proj_a,
           w_proj_b, w_out, norm_ffn, w_gate, w_up, w_down, norm_ple, w_ple_gate, w_ple_proj, norm_final):
    depth = w_in.shape[0]
    assert depth == 1, "single trunk layer"
    B, S, _ = x_prompt.shape
    NB = x_sample.shape[0]
    assert x_sample.shape[1] == 1
    caches = (cache_kv_w128, cache_kv_w512, cache_kv_w2048)
    assert S >= DSWA_GROUPS[-1][0] and S % (BAND * DSWA_GROUPS[-1][1]) == 0
    past = 0
    l = 0

    wt = w_in[l].T.astype(BF16)
    gate_wt = wt[OFF_MI:OFF_GA]
    gate_b = jnp.concatenate([b_igate[l], b_fgate[l]])
    in_wts = (
        norm_mix[l][None, :],
        wt[OFF_AQ:OFF_MQ],
        wt[OFF_MQ:OFF_MV],
        wt[OFF_MV:OFF_MO],
        wt[OFF_MO:OFF_MI],
        wt[OFF_GA:OFF_GB],
        wt[OFF_GB:IN_COLS],
        jnp.pad(gate_wt, ((0, LANES - 2 * M_HEADS), (0, 0))),
        gate_wt,
        jnp.pad(gate_b, (0, LANES - 2 * M_HEADS))[None, :],
        gate_b[:, None],
        conv_w[l],
        conv_b[l][None, :],
    )
    post_wts = (
        w_proj_a[l].astype(BF16), w_proj_b[l].astype(BF16), w_out[l].astype(BF16), norm_ffn[l][None, :],
        w_gate[l].astype(BF16), w_up[l].astype(BF16), w_down[l].astype(BF16), norm_ple[l][None, :],
        w_ple_gate[l].astype(BF16), w_ple_proj[l].astype(BF16), norm_final[None, :],
    )
    mhn = mh_norm[l]

    c0 = jnp.zeros((B, SUBLANES, 2 * M_INNER), F32)
    tabs_p = _rope_tables(past + np.arange(S))
    (*qkv, qm, km, vm, og, ga, gb, gc, gr, kv0, kv1, kv2, cn) = _inproj_seq(x_prompt, c0, tabs_p, in_wts)
    os_p, ls_p = [], []
    for g, (_, dil) in enumerate(DSWA_GROUPS):
        o, lse = _attn_seq(*qkv[3 * g:3 * g + 3], dil)
        os_p.append(o)
        ls_p.append(lse)
    ob, c_p, n_p, m_p = _mlstm_seq(qm, km, vm, og, gc, gr, mhn)

    tabs_s = _rope_tables(np.full((NB,), PAST_LEN))
    conv_s = jnp.swapaxes(state_conv[l], 0, 1)
    (qa_s, kvn0, kvn1, kvn2, qm_s, km_s, vm_s, og_s, ga_s, gb_s, gc_s, cn_s) = _inproj_step(
        x_sample[:, 0, :], conv_s, tabs_s, in_wts)
    pad_row = lambda t: jnp.pad(t, ((0, 0), (0, SM_LANES - t.shape[1])))
    att_half = lambda t: pad_row(t.reshape(NB, N_GROUPS, 2, HALF_W).transpose(0, 2, 1, 3).reshape(2 * NB, -1))
    mem_half = lambda t: t.reshape(2 * NB, SM_LANES)
    k_new = jnp.concatenate([kv[:, :GROUP_W] for kv in (kvn0, kvn1, kvn2)], axis=1)
    v_new = jnp.concatenate([kv[:, GROUP_W:] for kv in (kvn0, kvn1, kvn2)], axis=1)
    gates = jnp.concatenate([t.reshape(2 * NB, M_HEADS // 2) for t in
                             (gc_s[:, :M_HEADS], gc_s[:, M_HEADS:2 * M_HEADS], state_m[l])], axis=1)
    sm_rows = [att_half(qa_s), att_half(k_new), att_half(v_new), mem_half(qm_s), mem_half(km_s), mem_half(vm_s),
               mem_half(og_s), mem_half(state_n[l].reshape(NB, M_INNER)), pad_row(gates)]
    sm_rows += [jnp.zeros((2 * NB, SM_LANES), F32)] * (SM_ROWS - len(sm_rows))
    sm_in = jnp.stack(sm_rows, axis=1)
    caches_t = [jnp.transpose(c[l], (0, 2, 3, 4, 1)).reshape(NB, 2, 2, HALF_W, c.shape[2]) for c in caches]
    for c, (w, dil) in zip(caches_t, DSWA_GROUPS):
        assert c.shape[-1] == w == BAND * dil, "each cache must hold exactly one full window"
    y_prompt, so, cn0, cn1, cn2, c_s = _post_fused(x_prompt, os_p, ls_p, ob, ga, gb, p_prompt[l], post_wts,
                                                   sm_in, caches_t, state_C[l], mhn.reshape(2, M_HEADS // 2, M_DK))
    kv_s = [jnp.transpose(c.reshape(NB, 2, HPG, HEAD_DIM, -1), (0, 4, 1, 2, 3)) for c in (cn0, cn1, cn2)]
    att_full = lambda t: t[:, :N_GROUPS * HALF_W].reshape(NB, 2, N_GROUPS, HALF_W).transpose(0, 2, 1, 3).reshape(
        NB, N_GROUPS, GROUP_W)
    o_s, l_s = att_full(so[:, SO_O, :]), att_full(so[:, SO_LSE, :])
    os_s = [o_s[:, g, :] for g in range(N_GROUPS)]
    ls_s = [l_s[:, g, :] for g in range(N_GROUPS)]
    ob_s = so[:, SO_OB, :].reshape(NB, M_INNER)
    n_s = so[:, SO_N, :].reshape(NB, M_HEADS, M_DK)
    m_s = so[:, SO_M, ::M_DK].reshape(NB, M_HEADS)
    y_sample = _post_step(x_sample[:, 0, :], os_s, ls_s, ob_s, ga_s, gb_s, p_sample[l][:, 0, :], post_wts)

    kv_shape = lambda n, w: (1, n, w, 2, HPG, HEAD_DIM)
    return (
        y_prompt, y_sample[:, None, :],
        kv0.reshape(kv_shape(B, DSWA_GROUPS[0][0])), kv_s[0].reshape(kv_shape(NB, DSWA_GROUPS[0][0])),
        kv1.reshape(kv_shape(B, DSWA_GROUPS[1][0])), kv_s[1].reshape(kv_shape(NB, DSWA_GROUPS[1][0])),
        kv2.reshape(kv_shape(B, DSWA_GROUPS[2][0])), kv_s[2].reshape(kv_shape(NB, DSWA_GROUPS[2][0])),
        cn[:, SUBLANES - (CONV_W - 1):, :][None], jnp.swapaxes(cn_s, 0, 1)[None],
        c_p[None], c_s[None],
        n_p[None], n_s[None],
        m_p[:, 0, :M_HEADS][None], m_s[None],
    )
```

```python
import functools

import jax
import jax.numpy as jnp
import numpy as np
from jax import lax
from jax.experimental import pallas as pl
from jax.experimental.pallas import tpu as pltpu

F32 = jnp.float32
BF16 = jnp.bfloat16

D_MODEL = 1024
PAST_LEN = 16384
HEAD_DIM = 64
ROPE_DIM = HEAD_DIM // 4
ROPE_HALF = ROPE_DIM // 2
ROPE_THETA = 500000.0
DSWA_GROUPS = ((128, 1), (512, 4), (2048, 16))
HPG = 4
N_GROUPS = len(DSWA_GROUPS)
GROUP_W = HPG * HEAD_DIM
A_QKV = N_GROUPS * GROUP_W
BAND = 128
M_HEADS = 4
M_INNER = D_MODEL
M_DK = M_INNER // M_HEADS
CONV_W = 4
D_FF = 2816
PLE_DIM = 256
EPS = 1e-6
NEG = -1e30

OFF_AQ = 0
OFF_AK = OFF_AQ + A_QKV
OFF_AV = OFF_AK + A_QKV
OFF_MQ = OFF_AV + A_QKV
OFF_MK = OFF_MQ + M_INNER
OFF_MV = OFF_MK + M_INNER
OFF_MO = OFF_MV + M_INNER
OFF_MI = OFF_MO + M_INNER
OFF_MF = OFF_MI + M_HEADS
OFF_GA = OFF_MF + M_HEADS
OFF_GB = OFF_GA + D_MODEL
IN_COLS = OFF_GB + D_MODEL

LANES = 128
SUBLANES = 8
VMEM_LIMIT = 56 * 1024 * 1024

TM_IN = 512
TM_POST = 256
M_CHUNK = 256
ATT_TQ = 1024

_NT = (((1,), (1,)), ((), ()))
_TN = (((0,), (0,)), ((), ()))


def _dot(a, b):
    return jnp.dot(a, b, preferred_element_type=F32)


def _wdot(a, wt_ref, lo, hi):
    return lax.dot_general(a, wt_ref[lo:hi, :], _NT, preferred_element_type=F32)


def _rms(x, g):
    return x * lax.rsqrt(jnp.mean(x * x, axis=-1, keepdims=True) + EPS) * g


def _bf16_pieces(x):
    hi = x.astype(BF16)
    r1 = x - hi.astype(F32)
    mid = r1.astype(BF16)
    lo = (r1 - mid.astype(F32)).astype(BF16)
    return hi, mid, lo


def _sigmoid(x):
    return 0.5 * jnp.tanh(0.5 * x) + 0.5


def _log_sigmoid(x):
    return jnp.minimum(x, 0.0) - jnp.log1p(jnp.exp(-jnp.abs(x)))


def _rope(t, ra, rm, rp):
    return t * ra + pltpu.roll(t, LANES - ROPE_HALF, 1) * rm + pltpu.roll(t, ROPE_HALF, 1) * rp


def _rope2(t, ra, rm, rp):
    return jnp.concatenate([_rope(t[:, :LANES], ra, rm, rp), _rope(t[:, LANES:], ra, rm, rp)], axis=1)


def _gate_cols(z):
    lane = lax.broadcasted_iota(jnp.int32, z.shape, 1)
    return jnp.where(lane < M_HEADS, z, _log_sigmoid(z))


def _inproj_seq_kernel(x_ref, nmix_ref, waqkv_ref, wmqk_ref, wmv_ref, wmo_ref, wga_ref, wgb_ref,
                       wgc_ref, wgr_ref, bc_ref, br_ref, cw_ref, cb_ref, c0_ref, ra_ref, rm_ref, rp_ref,
                       q0_ref, k0_ref, v0_ref, q1_ref, k1_ref, v1_ref, q2_ref, k2_ref, v2_ref,
                       qm_ref, km_ref, vm_ref, og_ref, ga_ref, gb_ref,
                       gc_ref, gr_ref, kv0_ref, kv1_ref, kv2_ref, cn_ref, cbuf, dbuf, *, tm, ns):
    i = pl.program_id(1)
    xn = _rms(x_ref[0], nmix_ref[...]).astype(BF16)
    ra, rm, rp = ra_ref[...], rm_ref[...], rp_ref[...]
    kv_refs = (kv0_ref, kv1_ref, kv2_ref)
    qkv_refs = ((q0_ref, k0_ref, v0_ref), (q1_ref, k1_ref, v1_ref), (q2_ref, k2_ref, v2_ref))

    def put_dilated(ref, val, dil, slot):
        if dil == 1:
            ref[0] = val.astype(BF16)
            return
        for c in range(GROUP_W // LANES):
            dbuf[slot, c] = val[:, c * LANES:(c + 1) * LANES]
        for r in range(dil):
            for c in range(GROUP_W // LANES):
                lo = r * GROUP_W + c * LANES
                ref[0, :, lo:lo + LANES] = dbuf[slot, c, pl.ds(r, tm // dil, stride=dil), :].astype(BF16)

    for g, (win, dil) in enumerate(DSWA_GROUPS):
        zq = _wdot(xn, waqkv_ref, OFF_AQ + g * GROUP_W, OFF_AQ + (g + 1) * GROUP_W)
        put_dilated(qkv_refs[g][0], _rope2(zq, ra, rm, rp) * (HEAD_DIM ** -0.5), dil, 3 * g)
        zk = _wdot(xn, waqkv_ref, OFF_AK + g * GROUP_W, OFF_AK + (g + 1) * GROUP_W)
        kr = _rope2(zk, ra, rm, rp)
        put_dilated(qkv_refs[g][1], kr, dil, 3 * g + 1)
        zv = _wdot(xn, waqkv_ref, OFF_AV + g * GROUP_W, OFF_AV + (g + 1) * GROUP_W)
        put_dilated(qkv_refs[g][2], zv, dil, 3 * g + 2)
        rows = min(win, tm)
        nblk = win // rows

        @pl.when(i >= ns - nblk)
        def _():
            kv_refs[g][0, :, :GROUP_W] = kr[tm - rows:, :]
            kv_refs[g][0, :, GROUP_W:] = zv[tm - rows:, :]

    @pl.when(i == 0)
    def _():
        for blk in range(2 * M_INNER // LANES):
            cbuf[blk, 0:SUBLANES, :] = c0_ref[0, :, blk * LANES:(blk + 1) * LANES]

    for c in range(2 * M_INNER // 256):
        pre = _wdot(xn, wmqk_ref, c * 256, (c + 1) * 256)
        halves = []
        for hf in range(256 // LANES):
            blk = c * (256 // LANES) + hf
            ls = slice(blk * LANES, (blk + 1) * LANES)
            pre_h = pre[:, hf * LANES:(hf + 1) * LANES]
            cbuf[blk, SUBLANES:SUBLANES + tm, :] = pre_h
            y = cb_ref[:, ls] + cw_ref[CONV_W - 1:CONV_W, ls] * pre_h
            for d in range(1, CONV_W):
                y = y + cw_ref[CONV_W - 1 - d:CONV_W - d, ls] * cbuf[blk, pl.ds(SUBLANES - d, tm), :]
            tail = pre_h[tm - SUBLANES:, :]
            cbuf[blk, 0:SUBLANES, :] = tail
            cn_ref[0, :, ls] = tail
            halves.append(y * _sigmoid(y))
        s = jnp.concatenate(halves, axis=1)
        if c < M_INNER // 256:
            qm_ref[0, :, c * 256:(c + 1) * 256] = s.astype(BF16)
        else:
            km_ref[0, :, c * 256 - M_INNER:(c + 1) * 256 - M_INNER] = (s * (M_DK ** -0.5)).astype(BF16)
        cs = slice((c // 2) * 256, (c // 2 + 1) * 256)
        if c % 2 == 0:
            vm_ref[0, :, cs] = _wdot(xn, wmv_ref, cs.start, cs.stop).astype(BF16)
            og_ref[0, :, cs] = _sigmoid(_wdot(xn, wmo_ref, cs.start, cs.stop)).astype(BF16)
        else:
            ga_ref[0, :, cs] = _sigmoid(_wdot(xn, wga_ref, cs.start, cs.stop)).astype(BF16)
            gb_ref[0, :, cs] = _sigmoid(_wdot(xn, wgb_ref, cs.start, cs.stop)).astype(BF16)

    gcv = _gate_cols(_wdot(xn, wgc_ref, 0, LANES) + bc_ref[...])
    zr = lax.dot_general(wgr_ref[...], xn, _NT, preferred_element_type=F32) + br_ref[...]
    sub = lax.broadcasted_iota(jnp.int32, zr.shape, 0)
    grv = jnp.where(sub < M_HEADS, zr, _log_sigmoid(zr))
    L = M_CHUNK
    ti = lax.broadcasted_iota(jnp.int32, (L, L), 0)
    si = lax.broadcasted_iota(jnp.int32, (L, L), 1)
    lower = jnp.where(ti >= si, 1.0, 0.0).astype(BF16)
    upper = jnp.where(ti <= si, 1.0, 0.0).astype(BF16)
    lane = lax.broadcasted_iota(jnp.int32, (L, LANES), 1)
    sub_l = lax.broadcasted_iota(jnp.int32, (SUBLANES, L), 0)
    for ch in range(tm // L):
        rs = slice(ch * L, (ch + 1) * L)
        cum_c = sum(_dot(lower, p) for p in _bf16_pieces(gcv[rs, :]))
        cum_r = sum(_dot(p, upper) for p in _bf16_pieces(grv[:, rs]))
        gc_ref[0, rs, :] = jnp.where(lane < M_HEADS, gcv[rs, :], cum_c)
        gr_ref[0, :, rs] = jnp.where(sub_l < M_HEADS, grv[:, rs], cum_r)


def _const_spec(shape):
    nd = len(shape)
    return pl.BlockSpec(shape, lambda *_: (0,) * nd, pipeline_mode=pl.Buffered(1))


def _inproj_seq(x, c0, tables, wts):
    B, S, _ = x.shape
    tm = TM_IN
    ns = S // tm
    ra, rm, rp = tables

    def kv_spec(win):
        rows = min(win, tm)
        nblk = win // rows
        return pl.BlockSpec((1, rows, 2 * GROUP_W), lambda b, i: (b, jnp.maximum(i - (ns - nblk), 0), 0))

    tok = lambda w: pl.BlockSpec((1, tm, w), lambda b, i: (b, i, 0))
    tab = pl.BlockSpec((tm, LANES), lambda b, i: (i, 0))
    in_specs = ([tok(D_MODEL)] + [_const_spec(w.shape) for w in wts]
                + [pl.BlockSpec((1, SUBLANES, 2 * M_INNER), lambda b, i: (b, 0, 0)), tab, tab, tab])
    dil_shape = [jax.ShapeDtypeStruct((B, S // dil, dil * GROUP_W), BF16) for _, dil in DSWA_GROUPS for _ in range(3)]
    dil_specs = [pl.BlockSpec((1, tm // dil, dil * GROUP_W), lambda b, i: (b, i, 0))
                 for _, dil in DSWA_GROUPS for _ in range(3)]
    out_shape = (dil_shape
                 + [jax.ShapeDtypeStruct((B, S, M_INNER), BF16)] * 6
                 + [jax.ShapeDtypeStruct((B, S, LANES), F32), jax.ShapeDtypeStruct((B, SUBLANES, S), F32)]
                 + [jax.ShapeDtypeStruct((B, win, 2 * GROUP_W), F32) for win, _ in DSWA_GROUPS]
                 + [jax.ShapeDtypeStruct((B, SUBLANES, 2 * M_INNER), F32)])
    out_specs = (dil_specs + [tok(M_INNER)] * 6
                 + [tok(LANES), pl.BlockSpec((1, SUBLANES, tm), lambda b, i: (b, 0, i))]
                 + [kv_spec(win) for win, _ in DSWA_GROUPS]
                 + [pl.BlockSpec((1, SUBLANES, 2 * M_INNER), lambda b, i: (b, 0, 0))])
    return pl.pallas_call(
        functools.partial(_inproj_seq_kernel, tm=tm, ns=ns),
        grid=(B, ns),
        in_specs=in_specs,
        out_specs=out_specs,
        out_shape=out_shape,
        scratch_shapes=[pltpu.VMEM((2 * M_INNER // LANES, SUBLANES + tm, LANES), F32),
                        pltpu.VMEM((3 * N_GROUPS, GROUP_W // LANES, tm, LANES), F32)],
        compiler_params=pltpu.CompilerParams(
            dimension_semantics=("arbitrary", "arbitrary"), vmem_limit_bytes=VMEM_LIMIT),
        name="inproj_seq",
    )(x, *wts, c0, ra, rm, rp)


def _inproj_step_kernel(x_ref, nmix_ref, waqkv_ref, wmqk_ref, wmv_ref, wmo_ref, wga_ref, wgb_ref,
                        wgc_ref, wgr_ref, bc_ref, br_ref, cw_ref, cb_ref, cs_ref, ra_ref, rm_ref, rp_ref,
                        qa_ref, kv0_ref, kv1_ref, kv2_ref, qm_ref, km_ref, vm_ref, og_ref, ga_ref, gb_ref,
                        gc_ref, cn_ref):
    del wgr_ref, br_ref
    xn = _rms(x_ref[...], nmix_ref[...]).astype(BF16)
    ra, rm, rp = ra_ref[...], rm_ref[...], rp_ref[...]
    kv_refs = (kv0_ref, kv1_ref, kv2_ref)
    for g in range(N_GROUPS):
        cs = slice(g * GROUP_W, (g + 1) * GROUP_W)
        zq = _wdot(xn, waqkv_ref, OFF_AQ + g * GROUP_W, OFF_AQ + (g + 1) * GROUP_W)
        qa_ref[:, cs] = _rope2(zq, ra, rm, rp) * (HEAD_DIM ** -0.5)
        zk = _wdot(xn, waqkv_ref, OFF_AK + g * GROUP_W, OFF_AK + (g + 1) * GROUP_W)
        kv_refs[g][:, :GROUP_W] = _rope2(zk, ra, rm, rp)
        kv_refs[g][:, GROUP_W:] = _wdot(xn, waqkv_ref, OFF_AV + g * GROUP_W, OFF_AV + (g + 1) * GROUP_W)

    for c in range(2 * M_INNER // 256):
        cs = slice(c * 256, (c + 1) * 256)
        pre = _wdot(xn, wmqk_ref, cs.start, cs.stop)
        y = cb_ref[:, cs] + cw_ref[CONV_W - 1:CONV_W, cs] * pre
        for j in range(CONV_W - 1):
            y = y + cw_ref[j:j + 1, cs] * cs_ref[j, :, cs]
        for j in range(CONV_W - 2):
            cn_ref[j, :, cs] = cs_ref[j + 1, :, cs]
        cn_ref[CONV_W - 2, :, cs] = pre
        s = y * _sigmoid(y)
        if c < M_INNER // 256:
            qm_ref[:, cs] = s
        else:
            km_ref[:, c * 256 - M_INNER:(c + 1) * 256 - M_INNER] = s * (M_DK ** -0.5)

    for c in range(M_INNER // 256):
        cs = slice(c * 256, (c + 1) * 256)
        vm_ref[:, cs] = _wdot(xn, wmv_ref, cs.start, cs.stop)
        og_ref[:, cs] = _sigmoid(_wdot(xn, wmo_ref, cs.start, cs.stop))
        ga_ref[:, cs] = _sigmoid(_wdot(xn, wga_ref, cs.start, cs.stop))
        gb_ref[:, cs] = _sigmoid(_wdot(xn, wgb_ref, cs.start, cs.stop))
    gc_ref[...] = _gate_cols(_wdot(xn, wgc_ref, 0, LANES) + bc_ref[...])


def _inproj_step(x, conv_state, tables, wts):
    nb = x.shape[0]
    sds = lambda w: jax.ShapeDtypeStruct((nb, w), F32)
    out_shape = ([sds(A_QKV)] + [sds(2 * GROUP_W)] * N_GROUPS + [sds(M_INNER)] * 6 + [sds(LANES)]
                 + [jax.ShapeDtypeStruct((CONV_W - 1, nb, 2 * M_INNER), F32)])
    return pl.pallas_call(
        _inproj_step_kernel,
        out_shape=out_shape,
        compiler_params=pltpu.CompilerParams(vmem_limit_bytes=VMEM_LIMIT),
        name="inproj_step",
    )(x, *wts, conv_state, *tables)


def _attn_seq_kernel(q_ref, k_ref, v_ref, kp_ref, vp_ref, o_ref, lse_ref, *, tq, rpb):
    i = pl.program_id(2)
    lane_head = lax.broadcasted_iota(jnp.int32, (1, GROUP_W), 1) // HEAD_DIM
    head_mask = [jnp.where(lane_head == h, 1.0, 0.0).astype(BF16) for h in range(HPG)]
    lane_head_full = lax.broadcasted_iota(jnp.int32, (BAND, GROUP_W), 1) // HEAD_DIM
    qi = lax.broadcasted_iota(jnp.int32, (BAND, 2 * BAND), 0)
    kj = lax.broadcasted_iota(jnp.int32, (BAND, 2 * BAND), 1)
    band = (kj >= qi) & (kj <= qi + BAND)
    first_lo = jnp.where(i > 0, 0, BAND)
    bias = jnp.where(band, 0.0, NEG)
    bias_first = jnp.where(band & (kj >= first_lo), 0.0, NEG)

    for rr, j in [(rr, j) for rr in range(rpb) for j in range(tq // BAND)]:
        ls = slice(rr * GROUP_W, (rr + 1) * GROUP_W)
        b1 = bias_first if j == 0 else bias
        q = q_ref[0, j * BAND:(j + 1) * BAND, ls]
        if j == 0:
            kk = jnp.concatenate([kp_ref[0, :, ls], k_ref[0, 0:BAND, ls]], axis=0)
            vv = jnp.concatenate([vp_ref[0, :, ls], v_ref[0, 0:BAND, ls]], axis=0)
        else:
            kk = k_ref[0, (j - 1) * BAND:(j + 1) * BAND, ls]
            vv = v_ref[0, (j - 1) * BAND:(j + 1) * BAND, ls]
        qblk = jnp.concatenate([q * head_mask[h] for h in range(HPG)], axis=0)
        s = lax.dot_general(qblk, kk, _NT, preferred_element_type=F32) + jnp.concatenate([b1] * HPG, axis=0)
        mx = jnp.max(s, axis=-1, keepdims=True)
        e = jnp.exp(s - mx)
        den = jnp.sum(e, axis=-1, keepdims=True)
        pv = _dot(e.astype(BF16), vv) * (1.0 / den)
        lse_rows = mx + jnp.log(den)
        o = pv[0:BAND]
        lse = jnp.broadcast_to(lse_rows[0:BAND], (BAND, GROUP_W))
        for h in range(1, HPG):
            own = lane_head_full == h
            o = jnp.where(own, pv[h * BAND:(h + 1) * BAND], o)
            lse = jnp.where(own, lse_rows[h * BAND:(h + 1) * BAND], lse)
        o_ref[0, j * BAND:(j + 1) * BAND, ls] = o
        lse_ref[0, j * BAND:(j + 1) * BAND, ls] = lse


def _attn_seq(q, k, v, dil):
    B, nsub, _ = q.shape
    tq = min(ATT_TQ, nsub)
    rpb = min(dil, ATT_TQ // tq)
    in_spec = pl.BlockSpec((1, tq, rpb * GROUP_W), lambda b, r, i: (b, i, r))
    prev_spec = pl.BlockSpec((1, BAND, rpb * GROUP_W), lambda b, r, i: (b, jnp.maximum(i * (tq // BAND) - 1, 0), r))
    out_spec = pl.BlockSpec((1, tq, rpb * GROUP_W), lambda b, r, i: (b, i, r))
    o, lse = pl.pallas_call(
        functools.partial(_attn_seq_kernel, tq=tq, rpb=rpb),
        grid=(B, dil // rpb, nsub // tq),
        in_specs=[in_spec] * 3 + [prev_spec] * 2,
        out_specs=[out_spec] * 2,
        out_shape=[jax.ShapeDtypeStruct((B, nsub, dil * GROUP_W), F32)] * 2,
        compiler_params=pltpu.CompilerParams(
            dimension_semantics=("arbitrary", "arbitrary", "arbitrary"), vmem_limit_bytes=VMEM_LIMIT),
        name=f"attn_seq_d{dil}",
    )(q, k, v, k, v)
    return o, lse


def _mlstm_seq_kernel(q_ref, k_ref, v_ref, og_ref, gc_ref, gr_ref, mhn_ref,
                      ob_ref, c_out, n_out, m_out, c_s, n_s, m_s, *, L):
    c = pl.program_id(1)

    @pl.when(c == 0)
    def _():
        c_s[...] = jnp.zeros(c_s.shape, F32)
        n_s[...] = jnp.zeros(n_s.shape, F32)
        m_s[...] = jnp.zeros(m_s.shape, F32)

    gc = gc_ref[0]
    gr = gr_ref[0]
    ti = lax.broadcasted_iota(jnp.int32, (L, L), 0)
    si = lax.broadcasted_iota(jnp.int32, (L, L), 1)
    tril = ti >= si
    lane = lax.broadcasted_iota(jnp.int32, (1, LANES), 1)
    m_vec = jnp.zeros((1, LANES), F32)

    for h in range(M_HEADS):
        hs = slice(h * M_DK, (h + 1) * M_DK)
        b_col = gc[:, M_HEADS + h:M_HEADS + h + 1]
        i_col = gc[:, h:h + 1]
        b_row = gr[M_HEADS + h:M_HEADS + h + 1, :]
        i_row = gr[h:h + 1, :]
        m_prev = m_s[h, 0:1, 0:1]
        qh = q_ref[0, :, hs]
        kh = k_ref[0, :, hs]
        vh = v_ref[0, :, hs]
        c_prev = c_s[h]
        n_prev = n_s[h]

        dmat = jnp.where(tril, b_col - b_row + i_row, -jnp.inf)
        m_inter = b_col + m_prev
        m_t = jnp.maximum(m_inter, jnp.max(dmat, axis=-1, keepdims=True))
        w_intra = lax.dot_general(qh, kh, _NT, preferred_element_type=F32) * jnp.exp(dmat - m_t)
        w_inter = jnp.exp(m_inter - m_t)
        num = _dot(w_intra.astype(BF16), vh) + w_inter * _dot(qh, c_prev.astype(BF16))
        den = (jnp.sum(w_intra, axis=-1, keepdims=True)
               + w_inter * jnp.sum(qh.astype(F32) * n_prev, axis=-1, keepdims=True))
        hh = num / jnp.maximum(jnp.abs(den), jnp.exp(-m_t))
        hn = hh * lax.rsqrt(jnp.mean(hh * hh, axis=-1, keepdims=True) + EPS) * mhn_ref[h:h + 1, :]
        ob_ref[0, :, hs] = (hn * og_ref[0, :, hs].astype(F32)).astype(BF16)

        b_last = b_row[:, L - 1:L]
        m_new = jnp.maximum(b_last + m_prev, jnp.max(b_last - b_row + i_row, axis=-1, keepdims=True))
        wk = jnp.exp(b_last - b_col + i_col - m_new)
        decay = jnp.exp(b_last + m_prev - m_new)
        ks = kh.astype(F32) * wk
        c_new = decay * c_prev + lax.dot_general(ks.astype(BF16), vh, _TN, preferred_element_type=F32)
        n_new = decay * n_prev + jnp.sum(ks, axis=0, keepdims=True)
        c_s[h] = c_new
        n_s[h] = n_new
        m_s[h] = jnp.broadcast_to(m_new, (SUBLANES, LANES))
        c_out[0, h] = c_new
        n_out[0, h:h + 1, :] = n_new
        m_vec = jnp.where(lane == h, m_new, m_vec)
    m_out[0] = m_vec


def _mlstm_seq(qm, km, vm, og, gc, gr, mhn):
    B, S, _ = qm.shape
    L = M_CHUNK
    tok = pl.BlockSpec((1, L, M_INNER), lambda b, c: (b, c, 0))
    return pl.pallas_call(
        functools.partial(_mlstm_seq_kernel, L=L),
        grid=(B, S // L),
        in_specs=[tok, tok, tok, tok,
                  pl.BlockSpec((1, L, LANES), lambda b, c: (b, c, 0)),
                  pl.BlockSpec((1, SUBLANES, L), lambda b, c: (b, 0, c)),
                  pl.BlockSpec((M_HEADS, M_DK), lambda b, c: (0, 0))],
        out_specs=[tok,
                   pl.BlockSpec((1, M_HEADS, M_DK, M_DK), lambda b, c: (b, 0, 0, 0)),
                   pl.BlockSpec((1, M_HEADS, M_DK), lambda b, c: (b, 0, 0)),
                   pl.BlockSpec((1, 1, LANES), lambda b, c: (b, 0, 0))],
        out_shape=[jax.ShapeDtypeStruct((B, S, M_INNER), BF16),
                   jax.ShapeDtypeStruct((B, M_HEADS, M_DK, M_DK), F32),
                   jax.ShapeDtypeStruct((B, M_HEADS, M_DK), F32),
                   jax.ShapeDtypeStruct((B, 1, LANES), F32)],
        scratch_shapes=[pltpu.VMEM((M_HEADS, M_DK, M_DK), F32),
                        pltpu.VMEM((M_HEADS, 1, M_DK), F32),
                        pltpu.VMEM((M_HEADS, SUBLANES, LANES), F32)],
        compiler_params=pltpu.CompilerParams(
            dimension_semantics=("arbitrary", "arbitrary"), vmem_limit_bytes=VMEM_LIMIT),
        name="mlstm_seq",
    )(qm, km, vm, og, gc, gr, mhn)


def _post_kernel(x_ref, o0_ref, o1_ref, o2_ref, l0_ref, l1_ref, l2_ref, ob_ref, ga_ref, gb_ref, p_ref,
                 wpa_ref, wpb_ref, wout_ref, nffn_ref, wg_ref, wu_ref, wd_ref, nple_ref, wpg_ref, wpp_ref,
                 nfin_ref, y_ref, *ubuf, dils):
    lead = (0,) * (len(x_ref.shape) - 2)
    rd = lambda r: r[lead] if lead else r[...]
    tm = x_ref.shape[-2]

    def undilate(ref, dil, slot):
        if dil == 1:
            return rd(ref)
        halves = range(GROUP_W // LANES)
        for r in range(dil):
            for c in halves:
                lo = r * GROUP_W + c * LANES
                ubuf[0][slot, c, pl.ds(r, tm // dil, stride=dil), :] = ref[0, :, lo:lo + LANES]
        return jnp.concatenate([ubuf[0][slot, c] for c in halves], axis=1)

    l0, l1, l2 = (undilate(r, d, 2 * g) for g, (r, d) in enumerate(zip((l0_ref, l1_ref, l2_ref), dils)))
    o0, o1, o2 = (undilate(r, d, 2 * g + 1) for g, (r, d) in enumerate(zip((o0_ref, o1_ref, o2_ref), dils)))
    lmax = jnp.maximum(jnp.maximum(l0, l1), l2)
    e0, e1, e2 = jnp.exp(l0 - lmax), jnp.exp(l1 - lmax), jnp.exp(l2 - lmax)
    o_a = (e0 * o0 + e1 * o1 + e2 * o2) / (e0 + e1 + e2)

    mix = (rd(ga_ref).astype(F32) * _dot(o_a.astype(BF16), wpa_ref[...])
           + rd(gb_ref).astype(F32) * _dot(rd(ob_ref).astype(BF16), wpb_ref[...]))
    h = rd(x_ref) + _dot(mix.astype(BF16), wout_ref[...])

    xf = _rms(h, nffn_ref[...]).astype(BF16)
    half = D_FF // 2
    for c in range(2):
        cs = slice(c * half, (c + 1) * half)
        gate = _dot(xf, wg_ref[:, cs])
        act = (gate * _sigmoid(gate)) * _dot(xf, wu_ref[:, cs])
        h = h + _dot(act.astype(BF16), wd_ref[cs, :])

    xp = _rms(h, nple_ref[...]).astype(BF16)
    h = h + _sigmoid(_dot(xp, wpg_ref[...])) * _dot(rd(p_ref).astype(BF16), wpp_ref[...])
    y = _rms(h, nfin_ref[...])
    if lead:
        y_ref[lead] = y
    else:
        y_ref[...] = y


def _post_step(x, os_, ls_, ob, ga, gb, p, wts):
    return pl.pallas_call(
        functools.partial(_post_kernel, dils=(1,) * N_GROUPS),
        out_shape=jax.ShapeDtypeStruct(x.shape, F32),
        compiler_params=pltpu.CompilerParams(vmem_limit_bytes=VMEM_LIMIT),
        name="post_step",
    )(x, *os_, *ls_, ob, ga, gb, p, *wts)


SM_ROWS = 16
SM_LANES = 512
T_QA, T_KN, T_VN, T_QM, T_KM, T_VM, T_OG, T_N, T_GATE = range(9)
SO_ROWS = 8
SO_OB, SO_N, SO_M, SO_LSE, SO_O = range(5)
HALF_HEADS = HPG // 2
HALF_W = HALF_HEADS * HEAD_DIM


def _eye(n):
    return lax.broadcasted_iota(jnp.int32, (n, n), 0) == lax.broadcasted_iota(jnp.int32, (n, n), 1)


def _col_of(row, eye):
    return jnp.sum(jnp.where(eye, jnp.broadcast_to(row, eye.shape), 0.0), axis=1, keepdims=True)


def _row_of(col, eye):
    return jnp.sum(jnp.where(eye, jnp.broadcast_to(col, eye.shape), 0.0), axis=0, keepdims=True)


def _round_bf16(x):
    return x.astype(BF16).astype(F32)


def _sample_step(tile, kv_refs, cst_ref, mhn_ref, so_ref, kvn_refs, cso_ref):
    so_ref[0] = jnp.zeros((SO_ROWS, SM_LANES), F32)
    eye_a = _eye(HALF_W)
    eye_m = _eye(M_DK)

    for g, (w, dil) in enumerate(DSWA_GROUPS):
        ls = slice(g * HALF_W, (g + 1) * HALF_W)
        k_t = kv_refs[g][0, 0, 0]
        v_t = kv_refs[g][0, 1, 0]
        k_row = tile[T_KN:T_KN + 1, ls]
        v_row = tile[T_VN:T_VN + 1, ls]
        qb = _round_bf16(tile[T_QA:T_QA + 1, ls])
        prod = k_t * _col_of(qb, eye_a)
        qk_new = qb * _round_bf16(k_row)
        pos = lax.broadcasted_iota(jnp.int32, (1, w), 1)
        pe, pn, lse = [], [], []
        for hl in range(HALF_HEADS):
            hd = slice(hl * HEAD_DIM, (hl + 1) * HEAD_DIM)
            s = jnp.where(pos % dil == 0, jnp.sum(prod[hd, :], axis=0, keepdims=True), NEG)
            s_new = jnp.sum(qk_new[:, hd], axis=-1, keepdims=True)
            mx = jnp.maximum(jnp.max(s, axis=-1, keepdims=True), s_new)
            e = jnp.exp(s - mx)
            e_new = jnp.exp(s_new - mx)
            den = jnp.sum(e, axis=-1, keepdims=True) + e_new
            pe.append(jnp.broadcast_to(e / den, (HEAD_DIM, w)))
            pn.append(jnp.broadcast_to(e_new / den, (1, HEAD_DIM)))
            lse.append(jnp.broadcast_to(mx + jnp.log(den), (1, HEAD_DIM)))
        o_col = jnp.sum(v_t * jnp.concatenate(pe, axis=0), axis=-1, keepdims=True)
        so_ref[0, SO_O:SO_O + 1, ls] = _row_of(o_col, eye_a) + jnp.concatenate(pn, axis=1) * _round_bf16(v_row)
        so_ref[0, SO_LSE:SO_LSE + 1, ls] = jnp.concatenate(lse, axis=1)
        last = lax.broadcasted_iota(jnp.int32, (HALF_W, w), 1) == w - 1
        kvn_refs[g][0, 0, 0] = jnp.where(last, _col_of(k_row, eye_a), pltpu.roll(k_t, w - 1, 1))
        kvn_refs[g][0, 1, 0] = jnp.where(last, _col_of(v_row, eye_a), pltpu.roll(v_t, w - 1, 1))

    eye = eye_m
    for j in range(M_HEADS // 2):
        hs = slice(j * M_DK, (j + 1) * M_DK)
        ig = tile[T_GATE:T_GATE + 1, j:j + 1]
        lf = tile[T_GATE:T_GATE + 1, 2 + j:3 + j]
        m_prev = tile[T_GATE:T_GATE + 1, 4 + j:5 + j]
        qb = _round_bf16(tile[T_QM:T_QM + 1, hs])
        kb = _round_bf16(tile[T_KM:T_KM + 1, hs])
        vb = _round_bf16(tile[T_VM:T_VM + 1, hs])
        n_prev = tile[T_N:T_N + 1, hs]
        c_prev = cst_ref[0, j]

        m_inter = lf + m_prev
        m_t = jnp.maximum(m_inter, ig)
        w_intra = jnp.sum(qb * kb, axis=-1, keepdims=True) * jnp.exp(ig - m_t)
        w_inter = jnp.exp(m_inter - m_t)
        q_c = jnp.sum(c_prev * _col_of(qb, eye), axis=0, keepdims=True)
        num = w_intra * vb + w_inter * q_c
        den = w_intra + w_inter * jnp.sum(qb * n_prev, axis=-1, keepdims=True)
        hh = num / jnp.maximum(jnp.abs(den), jnp.exp(-m_t))
        hn = hh * lax.rsqrt(jnp.mean(hh * hh, axis=-1, keepdims=True) + EPS) * mhn_ref[0, j:j + 1, :]
        ks = kb * jnp.exp(ig - m_t)
        so_ref[0, SO_OB:SO_OB + 1, hs] = hn * tile[T_OG:T_OG + 1, hs]
        so_ref[0, SO_N:SO_N + 1, hs] = w_inter * n_prev + ks
        so_ref[0, SO_M:SO_M + 1, hs] = jnp.broadcast_to(m_t, (1, M_DK))
        cso_ref[0, j] = w_inter * c_prev + _col_of(_round_bf16(ks), eye) * vb


N_POST_IN = 22


def _post_fused_kernel(*refs, dils, ns):
    post_in = refs[:N_POST_IN]
    sm_ref, c0_ref, c1_ref, c2_ref, cst_ref, mhn_ref = refs[N_POST_IN:N_POST_IN + 6]
    y_ref, so_ref, cn0_ref, cn1_ref, cn2_ref, cso_ref = refs[N_POST_IN + 6:N_POST_IN + 12]
    (ubuf,) = refs[N_POST_IN + 12:]
    _post_kernel(*post_in, y_ref, ubuf, dils=dils)
    _sample_step(sm_ref[0], (c0_ref, c1_ref, c2_ref), cst_ref, mhn_ref, so_ref,
                 (cn0_ref, cn1_ref, cn2_ref), cso_ref)


def _post_fused(x, os_, ls_, ob, ga, gb, p, wts, sm_in, caches_t, c_state, mhn):
    B, S, _ = x.shape
    NB = c_state.shape[0]
    tm = TM_POST
    ns = S // tm
    assert B * ns == 2 * NB, "one half of a sample batch element per post-mixer grid step"
    tok = lambda w: pl.BlockSpec((1, tm, w), lambda b, i: (b, i, 0))
    dils = tuple(dil for _, dil in DSWA_GROUPS)
    dil_specs = [pl.BlockSpec((1, tm // dil, dil * GROUP_W), lambda b, i: (b, i, 0)) for dil in dils]
    step = lambda b, i: b * ns + i
    cache_specs = [pl.BlockSpec((1, 2, 1, HALF_W, c.shape[-1]), lambda b, i: (step(b, i) // 2, 0, step(b, i) % 2, 0, 0))
                   for c in caches_t]
    cst_spec = pl.BlockSpec((1, M_HEADS // 2, M_DK, M_DK), lambda b, i: (step(b, i) // 2, step(b, i) % 2, 0, 0))
    in_specs = ([tok(D_MODEL)] + dil_specs * 2 + [tok(M_INNER)] * 3 + [tok(PLE_DIM)]
                + [_const_spec(w.shape) for w in wts]
                + [pl.BlockSpec((1, SM_ROWS, SM_LANES), lambda b, i: (step(b, i), 0, 0))]
                + cache_specs
                + [cst_spec, pl.BlockSpec((1, M_HEADS // 2, M_DK), lambda b, i: (step(b, i) % 2, 0, 0))])
    out_specs = ([tok(D_MODEL), pl.BlockSpec((1, SO_ROWS, SM_LANES), lambda b, i: (step(b, i), 0, 0))]
                 + cache_specs + [cst_spec])
    out_shape = ([jax.ShapeDtypeStruct((B, S, D_MODEL), F32),
                  jax.ShapeDtypeStruct((2 * NB, SO_ROWS, SM_LANES), F32)]
                 + [jax.ShapeDtypeStruct(c.shape, F32) for c in caches_t]
                 + [jax.ShapeDtypeStruct(c_state.shape, F32)])
    return pl.pallas_call(
        functools.partial(_post_fused_kernel, dils=dils, ns=ns),
        grid=(B, ns),
        in_specs=in_specs,
        out_specs=out_specs,
        out_shape=out_shape,
        scratch_shapes=[pltpu.VMEM((2 * N_GROUPS, GROUP_W // LANES, tm, LANES), F32)],
        compiler_params=pltpu.CompilerParams(
            dimension_semantics=("arbitrary", "arbitrary"), vmem_limit_bytes=VMEM_LIMIT),
        name="post_seq_sample_stream",
    )(x, *os_, *ls_, ob, ga, gb, p, *wts, sm_in, *caches_t, c_state, mhn)


def _rope_tables(pos):
    pos = np.asarray(pos, np.float64)
    t = pos.shape[0]
    inv = np.power(ROPE_THETA, -np.arange(ROPE_HALF, dtype=np.float64) / ROPE_HALF)
    ang = pos[:, None] * inv[None, :]
    cos, sin = np.cos(ang), np.sin(ang)
    one = np.ones((t, HEAD_DIM - ROPE_DIM))
    zero = np.zeros((t, HEAD_DIM - ROPE_DIM))
    z8 = np.zeros((t, ROPE_HALF))
    a = np.concatenate([cos, cos, one], axis=1)
    m = np.concatenate([-sin, z8, zero], axis=1)
    p = np.concatenate([z8, sin, zero], axis=1)
    rep = LANES // HEAD_DIM
    return tuple(jnp.asarray(np.tile(v, (1, rep)).astype(np.float32)) for v in (a, m, p))


def kernel(x_prompt, x_sample, cache_kv_w128, cache_kv_w512, cache_kv_w2048, state_conv, state_C, state_n,
           state_m, p_prompt, p_sample, norm_mix, w_in, conv_w, conv_b, b_igate, b_fgate, mh_norm, w_proj_a,
           w_proj_b, w_out, norm_ffn, w_gate, w_up, w_down, norm_ple, w_ple_gate, w_ple_proj, norm_final):
    depth = w_in.shape[0]
    assert depth == 1, "single trunk layer"
    B, S, _ = x_prompt.shape
    NB = x_sample.shape[0]
    assert x_sample.shape[1] == 1
    caches = (cache_kv_w128, cache_kv_w512, cache_kv_w2048)
    assert S >= DSWA_GROUPS[-1][0] and S % (BAND * DSWA_GROUPS[-1][1]) == 0
    past = 0
    l = 0

    wt = w_in[l].T.astype(BF16)
    gate_wt = wt[OFF_MI:OFF_GA]
    gate_b = jnp.concatenate([b_igate[l], b_fgate[l]])
    in_wts = (
        norm_mix[l][None, :],
        wt[OFF_AQ:OFF_MQ],
        wt[OFF_MQ:OFF_MV],
        wt[OFF_MV:OFF_MO],
        wt[OFF_MO:OFF_MI],
        wt[OFF_GA:OFF_GB],
        wt[OFF_GB:IN_COLS],
        jnp.pad(gate_wt, ((0, LANES - 2 * M_HEADS), (0, 0))),
        gate_wt,
        jnp.pad(gate_b, (0, LANES - 2 * M_HEADS))[None, :],
        gate_b[:, None],
        conv_w[l],
        conv_b[l][None, :],
    )
    post_wts = (
        w_proj_a[l].astype(BF16), w_proj_b[l].astype(BF16), w_out[l].astype(BF16), norm_ffn[l][None, :],
        w_gate[l].astype(BF16), w_up[l].astype(BF16), w_down[l].astype(BF16), norm_ple[l][None, :],
        w_ple_gate[l].astype(BF16), w_ple_proj[l].astype(BF16), norm_final[None, :],
    )
    mhn = mh_norm[l]

    c0 = jnp.zeros((B, SUBLANES, 2 * M_INNER), F32)
    tabs_p = _rope_tables(past + np.arange(S))
    (*qkv, qm, km, vm, og, ga, gb, gc, gr, kv0, kv1, kv2, cn) = _inproj_seq(x_prompt, c0, tabs_p, in_wts)
    os_p, ls_p = [], []
    for g, (_, dil) in enumerate(DSWA_GROUPS):
        o, lse = _attn_seq(*qkv[3 * g:3 * g + 3], dil)
        os_p.append(o)
        ls_p.append(lse)
    ob, c_p, n_p, m_p = _mlstm_seq(qm, km, vm, og, gc, gr, mhn)

    tabs_s = _rope_tables(np.full((NB,), PAST_LEN))
    conv_s = jnp.swapaxes(state_conv[l], 0, 1)
    (qa_s, kvn0, kvn1, kvn2, qm_s, km_s, vm_s, og_s, ga_s, gb_s, gc_s, cn_s) = _inproj_step(
        x_sample[:, 0, :], conv_s, tabs_s, in_wts)
    pad_row = lambda t: jnp.pad(t, ((0, 0), (0, SM_LANES - t.shape[1])))
    att_half = lambda t: pad_row(t.reshape(NB, N_GROUPS, 2, HALF_W).transpose(0, 2, 1, 3).reshape(2 * NB, -1))
    mem_half = lambda t: t.reshape(2 * NB, SM_LANES)
    k_new = jnp.concatenate([kv[:, :GROUP_W] for kv in (kvn0, kvn1, kvn2)], axis=1)
    v_new = jnp.concatenate([kv[:, GROUP_W:] for kv in (kvn0, kvn1, kvn2)], axis=1)
    gates = jnp.concatenate([t.reshape(2 * NB, M_HEADS // 2) for t in
                             (gc_s[:, :M_HEADS], gc_s[:, M_HEADS:2 * M_HEADS], state_m[l])], axis=1)
    sm_rows = [att_half(qa_s), att_half(k_new), att_half(v_new), mem_half(qm_s), mem_half(km_s), mem_half(vm_s),
               mem_half(og_s), mem_half(state_n[l].reshape(NB, M_INNER)), pad_row(gates)]
    sm_rows += [jnp.zeros((2 * NB, SM_LANES), F32)] * (SM_ROWS - len(sm_rows))
    sm_in = jnp.stack(sm_rows, axis=1)
    caches_t = [jnp.transpose(c[l], (0, 2, 3, 4, 1)).reshape(NB, 2, 2, HALF_W, c.shape[2]) for c in caches]
    for c, (w, dil) in zip(caches_t, DSWA_GROUPS):
        assert c.shape[-1] == w == BAND * dil, "each cache must hold exactly one full window"
    y_prompt, so, cn0, cn1, cn2, c_s = _post_fused(x_prompt, os_p, ls_p, ob, ga, gb, p_prompt[l], post_wts,
                                                   sm_in, caches_t, state_C[l], mhn.reshape(2, M_HEADS // 2, M_DK))
    kv_s = [jnp.transpose(c.reshape(NB, 2, HPG, HEAD_DIM, -1), (0, 4, 1, 2, 3)) for c in (cn0, cn1, cn2)]
    att_full = lambda t: t[:, :N_GROUPS * HALF_W].reshape(NB, 2, N_GROUPS, HALF_W).transpose(0, 2, 1, 3).reshape(
        NB, N_GROUPS, GROUP_W)
    o_s, l_s = att_full(so[:, SO_O, :]), att_full(so[:, SO_LSE, :])
    os_s = [o_s[:, g, :] for g in range(N_GROUPS)]
    ls_s = [l_s[:, g, :] for g in range(N_GROUPS)]
    ob_s = so[:, SO_OB, :].reshape(NB, M_INNER)
    n_s = so[:, SO_N, :].reshape(NB, M_HEADS, M_DK)
    m_s = so[:, SO_M, ::M_DK].reshape(NB, M_HEADS)
    y_sample = _post_step(x_sample[:, 0, :], os_s, ls_s, ob_s, ga_s, gb_s, p_sample[l][:, 0, :], post_wts)

    kv_shape = lambda n, w: (1, n, w, 2, HPG, HEAD_DIM)
    return (
        y_prompt, y_sample[:, None, :],
        kv0.reshape(kv_shape(B, DSWA_GROUPS[0][0])), kv_s[0].reshape(kv_shape(NB, DSWA_GROUPS[0][0])),
        kv1.reshape(kv_shape(B, DSWA_GROUPS[1][0])), kv_s[1].reshape(kv_shape(NB, DSWA_GROUPS[1][0])),
        kv2.reshape(kv_shape(B, DSWA_GROUPS[2][0])), kv_s[2].reshape(kv_shape(NB, DSWA_GROUPS[2][0])),
        cn[:, SUBLANES - (CONV_W - 1):, :][None], jnp.swapaxes(cn_s, 0, 1)[None],
        c_p[None], c_s[None],
        n_p[None], n_s[None],
        m_p[:, 0, :M_HEADS][None], m_s[None],
    )
```

```python
import functools

import jax
import jax.numpy as jnp
import numpy as np
from jax import lax
from jax.experimental import pallas as pl
from jax.experimental.pallas import tpu as pltpu

F32 = jnp.float32
BF16 = jnp.bfloat16

D_MODEL = 1024
PAST_LEN = 16384
HEAD_DIM = 64
ROPE_DIM = HEAD_DIM // 4
ROPE_HALF = ROPE_DIM // 2
ROPE_THETA = 500000.0
DSWA_GROUPS = ((128, 1), (512, 4), (2048, 16))
HPG = 4
N_GROUPS = len(DSWA_GROUPS)
GROUP_W = HPG * HEAD_DIM
A_QKV = N_GROUPS * GROUP_W
BAND = 128
M_HEADS = 4
M_INNER = D_MODEL
M_DK = M_INNER // M_HEADS
CONV_W = 4
D_FF = 2816
PLE_DIM = 256
EPS = 1e-6
NEG = -1e30

OFF_AQ = 0
OFF_AK = OFF_AQ + A_QKV
OFF_AV = OFF_AK + A_QKV
OFF_MQ = OFF_AV + A_QKV
OFF_MK = OFF_MQ + M_INNER
OFF_MV = OFF_MK + M_INNER
OFF_MO = OFF_MV + M_INNER
OFF_MI = OFF_MO + M_INNER
OFF_MF = OFF_MI + M_HEADS
OFF_GA = OFF_MF + M_HEADS
OFF_GB = OFF_GA + D_MODEL
IN_COLS = OFF_GB + D_MODEL

LANES = 128
SUBLANES = 8
VMEM_LIMIT = 56 * 1024 * 1024

TM_IN = 512
TM_POST = 256
M_CHUNK = 256
MLSTM_G = 2
ATT_TQ = 1024

_NT = (((1,), (1,)), ((), ()))
_TN = (((0,), (0,)), ((), ()))


def _dot(a, b):
    return jnp.dot(a, b, preferred_element_type=F32)


def _wdot(a, wt_ref, lo, hi):
    return lax.dot_general(a, wt_ref[lo:hi, :], _NT, preferred_element_type=F32)


def _rms(x, g):
    return x * lax.rsqrt(jnp.mean(x * x, axis=-1, keepdims=True) + EPS) * g


def _bf16_pieces(x):
    hi = x.astype(BF16)
    r1 = x - hi.astype(F32)
    mid = r1.astype(BF16)
    lo = (r1 - mid.astype(F32)).astype(BF16)
    return hi, mid, lo


def _sigmoid(x):
    return 0.5 * jnp.tanh(0.5 * x) + 0.5


def _log_sigmoid(x):
    return jnp.minimum(x, 0.0) - jnp.log1p(jnp.exp(-jnp.abs(x)))


def _rope(t, ra, rm, rp):
    return t * ra + pltpu.roll(t, LANES - ROPE_HALF, 1) * rm + pltpu.roll(t, ROPE_HALF, 1) * rp


def _rope2(t, ra, rm, rp):
    return jnp.concatenate([_rope(t[:, :LANES], ra, rm, rp), _rope(t[:, LANES:], ra, rm, rp)], axis=1)


def _gate_cols(z):
    lane = lax.broadcasted_iota(jnp.int32, z.shape, 1)
    return jnp.where(lane < M_HEADS, z, _log_sigmoid(z))


def _inproj_seq_kernel(x_ref, nmix_ref, waqkv_ref, wmqk_ref, wmv_ref, wmo_ref, wga_ref, wgb_ref,
                       wgc_ref, wgr_ref, bc_ref, br_ref, cw_ref, cb_ref, c0_ref, ra_ref, rm_ref, rp_ref,
                       q0_ref, k0_ref, v0_ref, q1_ref, k1_ref, v1_ref, q2_ref, k2_ref, v2_ref,
                       qm_ref, km_ref, vm_ref, og_ref, ga_ref, gb_ref,
                       gc_ref, gr_ref, kv0_ref, kv1_ref, kv2_ref, cn_ref, cbuf, dbuf, *, tm, ns):
    i = pl.program_id(1)
    xn = _rms(x_ref[0], nmix_ref[...]).astype(BF16)
    ra, rm, rp = ra_ref[...], rm_ref[...], rp_ref[...]
    kv_refs = (kv0_ref, kv1_ref, kv2_ref)
    qkv_refs = ((q0_ref, k0_ref, v0_ref), (q1_ref, k1_ref, v1_ref), (q2_ref, k2_ref, v2_ref))

    def put_dilated(ref, val, dil, slot):
        if dil == 1:
            ref[0] = val.astype(BF16)
            return
        for c in range(GROUP_W // LANES):
            dbuf[slot, c] = val[:, c * LANES:(c + 1) * LANES]
        for r in range(dil):
            for c in range(GROUP_W // LANES):
                lo = r * GROUP_W + c * LANES
                ref[0, :, lo:lo + LANES] = dbuf[slot, c, pl.ds(r, tm // dil, stride=dil), :].astype(BF16)

    for g, (win, dil) in enumerate(DSWA_GROUPS):
        zq = _wdot(xn, waqkv_ref, OFF_AQ + g * GROUP_W, OFF_AQ + (g + 1) * GROUP_W)
        put_dilated(qkv_refs[g][0], _rope2(zq, ra, rm, rp) * (HEAD_DIM ** -0.5), dil, 3 * g)
        zk = _wdot(xn, waqkv_ref, OFF_AK + g * GROUP_W, OFF_AK + (g + 1) * GROUP_W)
        kr = _rope2(zk, ra, rm, rp)
        put_dilated(qkv_refs[g][1], kr, dil, 3 * g + 1)
        zv = _wdot(xn, waqkv_ref, OFF_AV + g * GROUP_W, OFF_AV + (g + 1) * GROUP_W)
        put_dilated(qkv_refs[g][2], zv, dil, 3 * g + 2)
        rows = min(win, tm)
        nblk = win // rows

        @pl.when(i >= ns - nblk)
        def _():
            kv_refs[g][0, :, :GROUP_W] = kr[tm - rows:, :]
            kv_refs[g][0, :, GROUP_W:] = zv[tm - rows:, :]

    @pl.when(i == 0)
    def _():
        for blk in range(2 * M_INNER // LANES):
            cbuf[blk, 0:SUBLANES, :] = c0_ref[0, :, blk * LANES:(blk + 1) * LANES]

    for c in range(2 * M_INNER // 256):
        pre = _wdot(xn, wmqk_ref, c * 256, (c + 1) * 256)
        halves = []
        for hf in range(256 // LANES):
            blk = c * (256 // LANES) + hf
            ls = slice(blk * LANES, (blk + 1) * LANES)
            pre_h = pre[:, hf * LANES:(hf + 1) * LANES]
            cbuf[blk, SUBLANES:SUBLANES + tm, :] = pre_h
            y = cb_ref[:, ls] + cw_ref[CONV_W - 1:CONV_W, ls] * pre_h
            for d in range(1, CONV_W):
                y = y + cw_ref[CONV_W - 1 - d:CONV_W - d, ls] * cbuf[blk, pl.ds(SUBLANES - d, tm), :]
            tail = pre_h[tm - SUBLANES:, :]
            cbuf[blk, 0:SUBLANES, :] = tail
            cn_ref[0, :, ls] = tail
            halves.append(y * _sigmoid(y))
        s = jnp.concatenate(halves, axis=1)
        if c < M_INNER // 256:
            qm_ref[0, :, c * 256:(c + 1) * 256] = s.astype(BF16)
        else:
            km_ref[0, :, c * 256 - M_INNER:(c + 1) * 256 - M_INNER] = (s * (M_DK ** -0.5)).astype(BF16)
        cs = slice((c % 2) * 512, (c % 2 + 1) * 512)
        w_ref, o_ref = ((wmv_ref, vm_ref), (wmo_ref, og_ref), (wga_ref, ga_ref), (wgb_ref, gb_ref))[c // 2]
        z = _wdot(xn, w_ref, cs.start, cs.stop)
        o_ref[0, :, cs] = (z if o_ref is vm_ref else _sigmoid(z)).astype(BF16)

    gcv = _gate_cols(_wdot(xn, wgc_ref, 0, LANES) + bc_ref[...])
    zr = lax.dot_general(wgr_ref[...], xn, _NT, preferred_element_type=F32) + br_ref[...]
    sub = lax.broadcasted_iota(jnp.int32, zr.shape, 0)
    grv = jnp.where(sub < M_HEADS, zr, _log_sigmoid(zr))
    L = M_CHUNK
    ti = lax.broadcasted_iota(jnp.int32, (L, L), 0)
    si = lax.broadcasted_iota(jnp.int32, (L, L), 1)
    lower = jnp.where(ti >= si, 1.0, 0.0).astype(BF16)
    upper = jnp.where(ti <= si, 1.0, 0.0).astype(BF16)
    lane = lax.broadcasted_iota(jnp.int32, (L, LANES), 1)
    sub_l = lax.broadcasted_iota(jnp.int32, (SUBLANES, L), 0)
    for ch in range(tm // L):
        rs = slice(ch * L, (ch + 1) * L)
        cum_c = sum(_dot(lower, p) for p in _bf16_pieces(gcv[rs, :]))
        cum_r = sum(_dot(p, upper) for p in _bf16_pieces(grv[:, rs]))
        gc_ref[0, rs, :] = jnp.where(lane < M_HEADS, gcv[rs, :], cum_c)
        gr_ref[0, :, rs] = jnp.where(sub_l < M_HEADS, grv[:, rs], cum_r)


def _const_spec(shape):
    nd = len(shape)
    return pl.BlockSpec(shape, lambda *_: (0,) * nd, pipeline_mode=pl.Buffered(1))


def _inproj_seq(x, c0, tables, wts):
    B, S, _ = x.shape
    tm = TM_IN
    ns = S // tm
    ra, rm, rp = tables

    def kv_spec(win):
        rows = min(win, tm)
        nblk = win // rows
        return pl.BlockSpec((1, rows, 2 * GROUP_W), lambda b, i: (b, jnp.maximum(i - (ns - nblk), 0), 0))

    tok = lambda w: pl.BlockSpec((1, tm, w), lambda b, i: (b, i, 0))
    tab = pl.BlockSpec((tm, LANES), lambda b, i: (i, 0))
    in_specs = ([tok(D_MODEL)] + [_const_spec(w.shape) for w in wts]
                + [pl.BlockSpec((1, SUBLANES, 2 * M_INNER), lambda b, i: (b, 0, 0)), tab, tab, tab])
    dil_shape = [jax.ShapeDtypeStruct((B, S // dil, dil * GROUP_W), BF16) for _, dil in DSWA_GROUPS for _ in range(3)]
    dil_specs = [pl.BlockSpec((1, tm // dil, dil * GROUP_W), lambda b, i: (b, i, 0))
                 for _, dil in DSWA_GROUPS for _ in range(3)]
    out_shape = (dil_shape
                 + [jax.ShapeDtypeStruct((B, S, M_INNER), BF16)] * 6
                 + [jax.ShapeDtypeStruct((B, S, LANES), F32), jax.ShapeDtypeStruct((B, SUBLANES, S), F32)]
                 + [jax.ShapeDtypeStruct((B, win, 2 * GROUP_W), F32) for win, _ in DSWA_GROUPS]
                 + [jax.ShapeDtypeStruct((B, SUBLANES, 2 * M_INNER), F32)])
    out_specs = (dil_specs + [tok(M_INNER)] * 6
                 + [tok(LANES), pl.BlockSpec((1, SUBLANES, tm), lambda b, i: (b, 0, i))]
                 + [kv_spec(win) for win, _ in DSWA_GROUPS]
                 + [pl.BlockSpec((1, SUBLANES, 2 * M_INNER), lambda b, i: (b, 0, 0))])
    return pl.pallas_call(
        functools.partial(_inproj_seq_kernel, tm=tm, ns=ns),
        grid=(B, ns),
        in_specs=in_specs,
        out_specs=out_specs,
        out_shape=out_shape,
        scratch_shapes=[pltpu.VMEM((2 * M_INNER // LANES, SUBLANES + tm, LANES), F32),
                        pltpu.VMEM((3 * N_GROUPS, GROUP_W // LANES, tm, LANES), F32)],
        compiler_params=pltpu.CompilerParams(
            dimension_semantics=("arbitrary", "arbitrary"), vmem_limit_bytes=VMEM_LIMIT),
        name="inproj_seq",
    )(x, *wts, c0, ra, rm, rp)


def _inproj_step_kernel(x_ref, nmix_ref, waqkv_ref, wmqk_ref, wmv_ref, wmo_ref, wga_ref, wgb_ref,
                        wgc_ref, wgr_ref, bc_ref, br_ref, cw_ref, cb_ref, cs_ref, ra_ref, rm_ref, rp_ref,
                        qa_ref, kv0_ref, kv1_ref, kv2_ref, qm_ref, km_ref, vm_ref, og_ref, ga_ref, gb_ref,
                        gc_ref, cn_ref):
    del wgr_ref, br_ref
    xn = _rms(x_ref[...], nmix_ref[...]).astype(BF16)
    ra, rm, rp = ra_ref[...], rm_ref[...], rp_ref[...]
    kv_refs = (kv0_ref, kv1_ref, kv2_ref)
    for g in range(N_GROUPS):
        cs = slice(g * GROUP_W, (g + 1) * GROUP_W)
        zq = _wdot(xn, waqkv_ref, OFF_AQ + g * GROUP_W, OFF_AQ + (g + 1) * GROUP_W)
        qa_ref[:, cs] = _rope2(zq, ra, rm, rp) * (HEAD_DIM ** -0.5)
        zk = _wdot(xn, waqkv_ref, OFF_AK + g * GROUP_W, OFF_AK + (g + 1) * GROUP_W)
        kv_refs[g][:, :GROUP_W] = _rope2(zk, ra, rm, rp)
        kv_refs[g][:, GROUP_W:] = _wdot(xn, waqkv_ref, OFF_AV + g * GROUP_W, OFF_AV + (g + 1) * GROUP_W)

    for c in range(2 * M_INNER // 256):
        cs = slice(c * 256, (c + 1) * 256)
        pre = _wdot(xn, wmqk_ref, cs.start, cs.stop)
        y = cb_ref[:, cs] + cw_ref[CONV_W - 1:CONV_W, cs] * pre
        for j in range(CONV_W - 1):
            y = y + cw_ref[j:j + 1, cs] * cs_ref[j, :, cs]
        for j in range(CONV_W - 2):
            cn_ref[j, :, cs] = cs_ref[j + 1, :, cs]
        cn_ref[CONV_W - 2, :, cs] = pre
        s = y * _sigmoid(y)
        if c < M_INNER // 256:
            qm_ref[:, cs] = s
        else:
            km_ref[:, c * 256 - M_INNER:(c + 1) * 256 - M_INNER] = s * (M_DK ** -0.5)

    for c in range(M_INNER // 256):
        cs = slice(c * 256, (c + 1) * 256)
        vm_ref[:, cs] = _wdot(xn, wmv_ref, cs.start, cs.stop)
        og_ref[:, cs] = _sigmoid(_wdot(xn, wmo_ref, cs.start, cs.stop))
        ga_ref[:, cs] = _sigmoid(_wdot(xn, wga_ref, cs.start, cs.stop))
        gb_ref[:, cs] = _sigmoid(_wdot(xn, wgb_ref, cs.start, cs.stop))
    gc_ref[...] = _gate_cols(_wdot(xn, wgc_ref, 0, LANES) + bc_ref[...])


def _inproj_step(x, conv_state, tables, wts):
    nb = x.shape[0]
    sds = lambda w: jax.ShapeDtypeStruct((nb, w), F32)
    out_shape = ([sds(A_QKV)] + [sds(2 * GROUP_W)] * N_GROUPS + [sds(M_INNER)] * 6 + [sds(LANES)]
                 + [jax.ShapeDtypeStruct((CONV_W - 1, nb, 2 * M_INNER), F32)])
    return pl.pallas_call(
        _inproj_step_kernel,
        out_shape=out_shape,
        compiler_params=pltpu.CompilerParams(vmem_limit_bytes=VMEM_LIMIT),
        name="inproj_step",
    )(x, *wts, conv_state, *tables)


def _attn_seq_kernel(q_ref, k_ref, v_ref, kp_ref, vp_ref, w_ref, o_ref, lse_ref, wb_ref, *, tq, rpb):
    wb_ref[...] = w_ref[...].astype(BF16)
    i = pl.program_id(2)
    lane_head = lax.broadcasted_iota(jnp.int32, (1, GROUP_W), 1) // HEAD_DIM
    head_mask = [jnp.where(lane_head == h, 1.0, 0.0).astype(BF16) for h in range(HPG)]
    lane_head_full = lax.broadcasted_iota(jnp.int32, (BAND, GROUP_W), 1) // HEAD_DIM
    qi = lax.broadcasted_iota(jnp.int32, (BAND, 2 * BAND), 0)
    kj = lax.broadcasted_iota(jnp.int32, (BAND, 2 * BAND), 1)
    band = (kj >= qi) & (kj <= qi + BAND)
    first_lo = jnp.where(i > 0, 0, BAND)
    bias = jnp.where(band, 0.0, NEG)
    bias_first = jnp.where(band & (kj >= first_lo), 0.0, NEG)

    for rr, j in [(rr, j) for rr in range(rpb) for j in range(tq // BAND)]:
        ls = slice(rr * GROUP_W, (rr + 1) * GROUP_W)
        b1 = bias_first if j == 0 else bias
        q = q_ref[0, j * BAND:(j + 1) * BAND, ls]
        if j == 0:
            kk = jnp.concatenate([kp_ref[0, :, ls], k_ref[0, 0:BAND, ls]], axis=0)
            vv = jnp.concatenate([vp_ref[0, :, ls], v_ref[0, 0:BAND, ls]], axis=0)
        else:
            kk = k_ref[0, (j - 1) * BAND:(j + 1) * BAND, ls]
            vv = v_ref[0, (j - 1) * BAND:(j + 1) * BAND, ls]
        qblk = jnp.concatenate([q * head_mask[h] for h in range(HPG)], axis=0)
        s = lax.dot_general(qblk, kk, _NT, preferred_element_type=F32) + jnp.concatenate([b1] * HPG, axis=0)
        mx = jnp.max(s, axis=-1, keepdims=True)
        e = jnp.exp(s - mx)
        den = jnp.sum(e, axis=-1, keepdims=True)
        pv = _dot(e.astype(BF16), vv) * (1.0 / den)
        lse_rows = mx + jnp.log(den)
        o = pv[0:BAND]
        lse = jnp.broadcast_to(lse_rows[0:BAND], (BAND, GROUP_W))
        for h in range(1, HPG):
            own = lane_head_full == h
            o = jnp.where(own, pv[h * BAND:(h + 1) * BAND], o)
            lse = jnp.where(own, lse_rows[h * BAND:(h + 1) * BAND], lse)
        o_ref[0, j * BAND:(j + 1) * BAND, ls] = o
        lse_ref[0, j * BAND:(j + 1) * BAND, ls] = lse


def _attn_seq(q, k, v, dil, w):
    B, nsub, _ = q.shape
    tq = min(ATT_TQ, nsub)
    rpb = min(dil, ATT_TQ // tq)
    g1, g2 = dil // rpb, nsub // tq
    w_rows = w.shape[0] // (B * g1 * g2)
    assert w_rows * B * g1 * g2 == w.shape[0] and w_rows % 16 == 0
    w_spec = pl.BlockSpec((w_rows, w.shape[1]), lambda b, r, i: ((b * g1 + r) * g2 + i, 0))
    in_spec = pl.BlockSpec((1, tq, rpb * GROUP_W), lambda b, r, i: (b, i, r))
    prev_spec = pl.BlockSpec((1, BAND, rpb * GROUP_W), lambda b, r, i: (b, jnp.maximum(i * (tq // BAND) - 1, 0), r))
    out_spec = pl.BlockSpec((1, tq, rpb * GROUP_W), lambda b, r, i: (b, i, r))
    return pl.pallas_call(
        functools.partial(_attn_seq_kernel, tq=tq, rpb=rpb),
        grid=(B, g1, g2),
        in_specs=[in_spec] * 3 + [prev_spec] * 2 + [w_spec],
        out_specs=[out_spec] * 2 + [w_spec],
        out_shape=[jax.ShapeDtypeStruct((B, nsub, dil * GROUP_W), F32)] * 2 + [jax.ShapeDtypeStruct(w.shape, BF16)],
        compiler_params=pltpu.CompilerParams(
            dimension_semantics=("arbitrary", "arbitrary", "arbitrary"), vmem_limit_bytes=VMEM_LIMIT),
        name=f"attn_seq_d{dil}",
    )(q, k, v, k, v, w)


def _mlstm_seq_kernel(q_ref, k_ref, v_ref, og_ref, gc_ref, gr_ref, mhn_ref,
                      ob_ref, c_out, n_out, m_out, c_s, n_s, m_s, *, L, G):
    c = pl.program_id(1)

    @pl.when(c == 0)
    def _():
        c_s[...] = jnp.zeros(c_s.shape, F32)
        n_s[...] = jnp.zeros(n_s.shape, F32)
        m_s[...] = jnp.zeros(m_s.shape, F32)

    ti = lax.broadcasted_iota(jnp.int32, (L, L), 0)
    si = lax.broadcasted_iota(jnp.int32, (L, L), 1)
    tril = ti >= si
    lane = lax.broadcasted_iota(jnp.int32, (1, LANES), 1)
    m_vec = [jnp.zeros((1, LANES), F32)] * G

    for g, h in [(g, h) for g in range(G) for h in range(M_HEADS)]:
        gc = gc_ref[g]
        gr = gr_ref[g]
        hs = slice(h * M_DK, (h + 1) * M_DK)
        b_col = gc[:, M_HEADS + h:M_HEADS + h + 1]
        i_col = gc[:, h:h + 1]
        b_row = gr[M_HEADS + h:M_HEADS + h + 1, :]
        i_row = gr[h:h + 1, :]
        m_prev = m_s[g, h, 0:1, 0:1]
        qh = q_ref[g, :, hs]
        kh = k_ref[g, :, hs]
        vh = v_ref[g, :, hs]
        c_prev = c_s[g, h]
        n_prev = n_s[g, h]

        dmat = jnp.where(tril, b_col - b_row + i_row, -jnp.inf)
        m_inter = b_col + m_prev
        m_t = jnp.maximum(m_inter, jnp.max(dmat, axis=-1, keepdims=True))
        w_intra = lax.dot_general(qh, kh, _NT, preferred_element_type=F32) * jnp.exp(dmat - m_t)
        w_inter = jnp.exp(m_inter - m_t)
        num = _dot(w_intra.astype(BF16), vh) + w_inter * _dot(qh, c_prev.astype(BF16))
        den = (jnp.sum(w_intra, axis=-1, keepdims=True)
               + w_inter * jnp.sum(qh.astype(F32) * n_prev, axis=-1, keepdims=True))
        hh = num / jnp.maximum(jnp.abs(den), jnp.exp(-m_t))
        hn = hh * lax.rsqrt(jnp.mean(hh * hh, axis=-1, keepdims=True) + EPS) * mhn_ref[h:h + 1, :]
        ob_ref[g, :, hs] = (hn * og_ref[g, :, hs].astype(F32)).astype(BF16)

        b_last = b_row[:, L - 1:L]
        m_new = jnp.maximum(b_last + m_prev, jnp.max(b_last - b_row + i_row, axis=-1, keepdims=True))
        wk = jnp.exp(b_last - b_col + i_col - m_new)
        decay = jnp.exp(b_last + m_prev - m_new)
        ks = kh.astype(F32) * wk
        c_new = decay * c_prev + lax.dot_general(ks.astype(BF16), vh, _TN, preferred_element_type=F32)
        n_new = decay * n_prev + jnp.sum(ks, axis=0, keepdims=True)
        c_s[g, h] = c_new
        n_s[g, h] = n_new
        m_s[g, h] = jnp.broadcast_to(m_new, (SUBLANES, LANES))
        c_out[g, h] = c_new
        n_out[g, h:h + 1, :] = n_new
        m_vec[g] = jnp.where(lane == h, m_new, m_vec[g])
    for g in range(G):
        m_out[g] = m_vec[g]


def _mlstm_seq(qm, km, vm, og, gc, gr, mhn):
    B, S, _ = qm.shape
    L = M_CHUNK
    G = MLSTM_G
    tok = pl.BlockSpec((G, L, M_INNER), lambda b, c: (b, c, 0))
    return pl.pallas_call(
        functools.partial(_mlstm_seq_kernel, L=L, G=G),
        grid=(B // G, S // L),
        in_specs=[tok, tok, tok, tok,
                  pl.BlockSpec((G, L, LANES), lambda b, c: (b, c, 0)),
                  pl.BlockSpec((G, SUBLANES, L), lambda b, c: (b, 0, c)),
                  pl.BlockSpec((M_HEADS, M_DK), lambda b, c: (0, 0))],
        out_specs=[tok,
                   pl.BlockSpec((G, M_HEADS, M_DK, M_DK), lambda b, c: (b, 0, 0, 0)),
                   pl.BlockSpec((G, M_HEADS, M_DK), lambda b, c: (b, 0, 0)),
                   pl.BlockSpec((G, 1, LANES), lambda b, c: (b, 0, 0))],
        out_shape=[jax.ShapeDtypeStruct((B, S, M_INNER), BF16),
                   jax.ShapeDtypeStruct((B, M_HEADS, M_DK, M_DK), F32),
                   jax.ShapeDtypeStruct((B, M_HEADS, M_DK), F32),
                   jax.ShapeDtypeStruct((B, 1, LANES), F32)],
        scratch_shapes=[pltpu.VMEM((G, M_HEADS, M_DK, M_DK), F32),
                        pltpu.VMEM((G, M_HEADS, 1, M_DK), F32),
                        pltpu.VMEM((G, M_HEADS, SUBLANES, LANES), F32)],
        compiler_params=pltpu.CompilerParams(
            dimension_semantics=("arbitrary", "arbitrary"), vmem_limit_bytes=VMEM_LIMIT),
        name="mlstm_seq",
    )(qm, km, vm, og, gc, gr, mhn)


def _post_kernel(x_ref, o0_ref, o1_ref, o2_ref, l0_ref, l1_ref, l2_ref, ob_ref, ga_ref, gb_ref, p_ref,
                 wpa_ref, wpb_ref, wout_ref, nffn_ref, wg_ref, wu_ref, wd_ref, nple_ref, wpg_ref, wpp_ref,
                 nfin_ref, y_ref, *ubuf, dils):
    lead = (0,) * (len(x_ref.shape) - 2)
    rd = lambda r: r[lead] if lead else r[...]
    tm = x_ref.shape[-2]

    def undilate(ref, dil, slot):
        if dil == 1:
            return rd(ref)
        halves = range(GROUP_W // LANES)
        for r in range(dil):
            for c in halves:
                lo = r * GROUP_W + c * LANES
                ubuf[0][slot, c, pl.ds(r, tm // dil, stride=dil), :] = ref[0, :, lo:lo + LANES]
        return jnp.concatenate([ubuf[0][slot, c] for c in halves], axis=1)

    l0, l1, l2 = (undilate(r, d, 2 * g) for g, (r, d) in enumerate(zip((l0_ref, l1_ref, l2_ref), dils)))
    o0, o1, o2 = (undilate(r, d, 2 * g + 1) for g, (r, d) in enumerate(zip((o0_ref, o1_ref, o2_ref), dils)))
    lmax = jnp.maximum(jnp.maximum(l0, l1), l2)
    e0, e1, e2 = jnp.exp(l0 - lmax), jnp.exp(l1 - lmax), jnp.exp(l2 - lmax)
    o_a = (e0 * o0 + e1 * o1 + e2 * o2) / (e0 + e1 + e2)

    mix = (rd(ga_ref).astype(F32) * _dot(o_a.astype(BF16), wpa_ref[...])
           + rd(gb_ref).astype(F32) * _dot(rd(ob_ref).astype(BF16), wpb_ref[...]))
    h = rd(x_ref) + _dot(mix.astype(BF16), wout_ref[...])

    xf = _rms(h, nffn_ref[...]).astype(BF16)
    half = D_FF // 2
    for c in range(2):
        cs = slice(c * half, (c + 1) * half)
        gate = _dot(xf, wg_ref[:, cs])
        act = (gate * _sigmoid(gate)) * _dot(xf, wu_ref[:, cs])
        h = h + _dot(act.astype(BF16), wd_ref[cs, :])

    xp = _rms(h, nple_ref[...]).astype(BF16)
    h = h + _sigmoid(_dot(xp, wpg_ref[...])) * _dot(rd(p_ref).astype(BF16), wpp_ref[...])
    y = _rms(h, nfin_ref[...])
    if lead:
        y_ref[lead] = y
    else:
        y_ref[...] = y


def _post_step(x, os_, ls_, ob, ga, gb, p, wts):
    return pl.pallas_call(
        functools.partial(_post_kernel, dils=(1,) * N_GROUPS),
        out_shape=jax.ShapeDtypeStruct(x.shape, F32),
        compiler_params=pltpu.CompilerParams(vmem_limit_bytes=VMEM_LIMIT),
        name="post_step",
    )(x, *os_, *ls_, ob, ga, gb, p, *wts)


SM_ROWS = 16
SM_LANES = 512
T_QA, T_KN, T_VN, T_QM, T_KM, T_VM, T_OG, T_N, T_GATE = range(9)
SO_ROWS = 8
SO_OB, SO_N, SO_M, SO_LSE, SO_O = range(5)
HALF_HEADS = HPG // 2
HALF_W = HALF_HEADS * HEAD_DIM


def _eye(n):
    return lax.broadcasted_iota(jnp.int32, (n, n), 0) == lax.broadcasted_iota(jnp.int32, (n, n), 1)


def _col_of(row, eye):
    return jnp.sum(jnp.where(eye, jnp.broadcast_to(row, eye.shape), 0.0), axis=1, keepdims=True)


def _row_of(col, eye):
    return jnp.sum(jnp.where(eye, jnp.broadcast_to(col, eye.shape), 0.0), axis=0, keepdims=True)


def _round_bf16(x):
    return x.astype(BF16).astype(F32)


def _sample_step(tile, kv_refs, cst_ref, mhn_ref, so_ref, kvn_refs, cso_ref):
    so_ref[0] = jnp.zeros((SO_ROWS, SM_LANES), F32)
    eye_a = _eye(HALF_W)
    eye_m = _eye(M_DK)

    for g, (w, dil) in enumerate(DSWA_GROUPS):
        ls = slice(g * HALF_W, (g + 1) * HALF_W)
        k_t = kv_refs[g][0, 0, 0]
        v_t = kv_refs[g][0, 1, 0]
        k_row = tile[T_KN:T_KN + 1, ls]
        v_row = tile[T_VN:T_VN + 1, ls]
        qb = _round_bf16(tile[T_QA:T_QA + 1, ls])
        prod = k_t * _col_of(qb, eye_a)
        qk_new = qb * _round_bf16(k_row)
        pos = lax.broadcasted_iota(jnp.int32, (1, w), 1)
        pe, pn, lse = [], [], []
        for hl in range(HALF_HEADS):
            hd = slice(hl * HEAD_DIM, (hl + 1) * HEAD_DIM)
            s = jnp.where(pos % dil == 0, jnp.sum(prod[hd, :], axis=0, keepdims=True), NEG)
            s_new = jnp.sum(qk_new[:, hd], axis=-1, keepdims=True)
            mx = jnp.maximum(jnp.max(s, axis=-1, keepdims=True), s_new)
            e = jnp.exp(s - mx)
            e_new = jnp.exp(s_new - mx)
            den = jnp.sum(e, axis=-1, keepdims=True) + e_new
            pe.append(jnp.broadcast_to(e / den, (HEAD_DIM, w)))
            pn.append(jnp.broadcast_to(e_new / den, (1, HEAD_DIM)))
            lse.append(jnp.broadcast_to(mx + jnp.log(den), (1, HEAD_DIM)))
        o_col = jnp.sum(v_t * jnp.concatenate(pe, axis=0), axis=-1, keepdims=True)
        so_ref[0, SO_O:SO_O + 1, ls] = _row_of(o_col, eye_a) + jnp.concatenate(pn, axis=1) * _round_bf16(v_row)
        so_ref[0, SO_LSE:SO_LSE + 1, ls] = jnp.concatenate(lse, axis=1)
        last = lax.broadcasted_iota(jnp.int32, (HALF_W, w), 1) == w - 1
        kvn_refs[g][0, 0, 0] = jnp.where(last, _col_of(k_row, eye_a), pltpu.roll(k_t, w - 1, 1))
        kvn_refs[g][0, 1, 0] = jnp.where(last, _col_of(v_row, eye_a), pltpu.roll(v_t, w - 1, 1))

    eye = eye_m
    for j in range(M_HEADS // 2):
        hs = slice(j * M_DK, (j + 1) * M_DK)
        ig = tile[T_GATE:T_GATE + 1, j:j + 1]
        lf = tile[T_GATE:T_GATE + 1, 2 + j:3 + j]
        m_prev = tile[T_GATE:T_GATE + 1, 4 + j:5 + j]
        qb = _round_bf16(tile[T_QM:T_QM + 1, hs])
        kb = _round_bf16(tile[T_KM:T_KM + 1, hs])
        vb = _round_bf16(tile[T_VM:T_VM + 1, hs])
        n_prev = tile[T_N:T_N + 1, hs]
        c_prev = cst_ref[0, j]

        m_inter = lf + m_prev
        m_t = jnp.maximum(m_inter, ig)
        w_intra = jnp.sum(qb * kb, axis=-1, keepdims=True) * jnp.exp(ig - m_t)
        w_inter = jnp.exp(m_inter - m_t)
        q_c = jnp.sum(c_prev * _col_of(qb, eye), axis=0, keepdims=True)
        num = w_intra * vb + w_inter * q_c
        den = w_intra + w_inter * jnp.sum(qb * n_prev, axis=-1, keepdims=True)
        hh = num / jnp.maximum(jnp.abs(den), jnp.exp(-m_t))
        hn = hh * lax.rsqrt(jnp.mean(hh * hh, axis=-1, keepdims=True) + EPS) * mhn_ref[0, j:j + 1, :]
        ks = kb * jnp.exp(ig - m_t)
        so_ref[0, SO_OB:SO_OB + 1, hs] = hn * tile[T_OG:T_OG + 1, hs]
        so_ref[0, SO_N:SO_N + 1, hs] = w_inter * n_prev + ks
        so_ref[0, SO_M:SO_M + 1, hs] = jnp.broadcast_to(m_t, (1, M_DK))
        cso_ref[0, j] = w_inter * c_prev + _col_of(_round_bf16(ks), eye) * vb


N_POST_IN = 22


def _post_fused_kernel(*refs, dils, ns):
    post_in = refs[:N_POST_IN]
    sm_ref, c0_ref, c1_ref, c2_ref, cst_ref, mhn_ref = refs[N_POST_IN:N_POST_IN + 6]
    y_ref, so_ref, cn0_ref, cn1_ref, cn2_ref, cso_ref = refs[N_POST_IN + 6:N_POST_IN + 12]
    (ubuf,) = refs[N_POST_IN + 12:]
    _post_kernel(*post_in, y_ref, ubuf, dils=dils)
    _sample_step(sm_ref[0], (c0_ref, c1_ref, c2_ref), cst_ref, mhn_ref, so_ref,
                 (cn0_ref, cn1_ref, cn2_ref), cso_ref)


def _post_fused(x, os_, ls_, ob, ga, gb, p, wts, sm_in, caches_t, c_state, mhn):
    B, S, _ = x.shape
    NB = c_state.shape[0]
    tm = TM_POST
    ns = S // tm
    assert B * ns == 2 * NB, "one half of a sample batch element per post-mixer grid step"
    tok = lambda w: pl.BlockSpec((1, tm, w), lambda b, i: (b, i, 0))
    dils = tuple(dil for _, dil in DSWA_GROUPS)
    dil_specs = [pl.BlockSpec((1, tm // dil, dil * GROUP_W), lambda b, i: (b, i, 0)) for dil in dils]
    step = lambda b, i: b * ns + i
    cache_specs = [pl.BlockSpec((1, 2, 1, HALF_W, c.shape[-1]), lambda b, i: (step(b, i) // 2, 0, step(b, i) % 2, 0, 0))
                   for c in caches_t]
    cst_spec = pl.BlockSpec((1, M_HEADS // 2, M_DK, M_DK), lambda b, i: (step(b, i) // 2, step(b, i) % 2, 0, 0))
    in_specs = ([tok(D_MODEL)] + dil_specs * 2 + [tok(M_INNER)] * 3 + [tok(PLE_DIM)]
                + [_const_spec(w.shape) for w in wts]
                + [pl.BlockSpec((1, SM_ROWS, SM_LANES), lambda b, i: (step(b, i), 0, 0))]
                + cache_specs
                + [cst_spec, pl.BlockSpec((1, M_HEADS // 2, M_DK), lambda b, i: (step(b, i) % 2, 0, 0))])
    out_specs = ([tok(D_MODEL), pl.BlockSpec((1, SO_ROWS, SM_LANES), lambda b, i: (step(b, i), 0, 0))]
                 + cache_specs + [cst_spec])
    out_shape = ([jax.ShapeDtypeStruct((B, S, D_MODEL), F32),
                  jax.ShapeDtypeStruct((2 * NB, SO_ROWS, SM_LANES), F32)]
                 + [jax.ShapeDtypeStruct(c.shape, F32) for c in caches_t]
                 + [jax.ShapeDtypeStruct(c_state.shape, F32)])
    return pl.pallas_call(
        functools.partial(_post_fused_kernel, dils=dils, ns=ns),
        grid=(B, ns),
        in_specs=in_specs,
        out_specs=out_specs,
        out_shape=out_shape,
        scratch_shapes=[pltpu.VMEM((2 * N_GROUPS, GROUP_W // LANES, tm, LANES), F32)],
        compiler_params=pltpu.CompilerParams(
            dimension_semantics=("arbitrary", "arbitrary"), vmem_limit_bytes=VMEM_LIMIT),
        name="post_seq_sample_stream",
    )(x, *os_, *ls_, ob, ga, gb, p, *wts, sm_in, *caches_t, c_state, mhn)


def _rope_tables(pos):
    pos = np.asarray(pos, np.float64)
    t = pos.shape[0]
    inv = np.power(ROPE_THETA, -np.arange(ROPE_HALF, dtype=np.float64) / ROPE_HALF)
    ang = pos[:, None] * inv[None, :]
    cos, sin = np.cos(ang), np.sin(ang)
    one = np.ones((t, HEAD_DIM - ROPE_DIM))
    zero = np.zeros((t, HEAD_DIM - ROPE_DIM))
    z8 = np.zeros((t, ROPE_HALF))
    a = np.concatenate([cos, cos, one], axis=1)
    m = np.concatenate([-sin, z8, zero], axis=1)
    p = np.concatenate([z8, sin, zero], axis=1)
    rep = LANES // HEAD_DIM
    return tuple(jnp.asarray(np.tile(v, (1, rep)).astype(np.float32)) for v in (a, m, p))


def kernel(x_prompt, x_sample, cache_kv_w128, cache_kv_w512, cache_kv_w2048, state_conv, state_C, state_n,
           state_m, p_prompt, p_sample, norm_mix, w_in, conv_w, conv_b, b_igate, b_fgate, mh_norm, w_proj_a,
           w_proj_b, w_out, norm_ffn, w_gate, w_up, w_down, norm_ple, w_ple_gate, w_ple_proj, norm_final):
    depth = w_in.shape[0]
    assert depth == 1, "single trunk layer"
    B, S, _ = x_prompt.shape
    NB = x_sample.shape[0]
    assert x_sample.shape[1] == 1
    caches = (cache_kv_w128, cache_kv_w512, cache_kv_w2048)
    assert S >= DSWA_GROUPS[-1][0] and S % (BAND * DSWA_GROUPS[-1][1]) == 0
    past = 0
    l = 0

    wt = w_in[l].T.astype(BF16)
    gate_wt = wt[OFF_MI:OFF_GA]
    gate_b = jnp.concatenate([b_igate[l], b_fgate[l]])
    in_wts = (
        norm_mix[l][None, :],
        wt[OFF_AQ:OFF_MQ],
        wt[OFF_MQ:OFF_MV],
        wt[OFF_MV:OFF_MO],
        wt[OFF_MO:OFF_MI],
        wt[OFF_GA:OFF_GB],
        wt[OFF_GB:IN_COLS],
        jnp.pad(gate_wt, ((0, LANES - 2 * M_HEADS), (0, 0))),
        gate_wt,
        jnp.pad(gate_b, (0, LANES - 2 * M_HEADS))[None, :],
        gate_b[:, None],
        conv_w[l],
        conv_b[l][None, :],
    )
    mhn = mh_norm[l]

    c0 = jnp.zeros((B, SUBLANES, 2 * M_INNER), F32)
    tabs_p = _rope_tables(past + np.arange(S))
    (*qkv, qm, km, vm, og, ga, gb, gc, gr, kv0, kv1, kv2, cn) = _inproj_seq(x_prompt, c0, tabs_p, in_wts)
    os_p, ls_p, ffn_bf16 = [], [], []
    for g, ((_, dil), w_ffn) in enumerate(zip(DSWA_GROUPS, (w_gate[l], w_up[l], w_down[l]))):
        o, lse, wb = _attn_seq(*qkv[3 * g:3 * g + 3], dil, w_ffn)
        os_p.append(o)
        ls_p.append(lse)
        ffn_bf16.append(wb)
    post_wts = (
        w_proj_a[l].astype(BF16), w_proj_b[l].astype(BF16), w_out[l].astype(BF16), norm_ffn[l][None, :],
        *ffn_bf16, norm_ple[l][None, :],
        w_ple_gate[l].astype(BF16), w_ple_proj[l].astype(BF16), norm_final[None, :],
    )
    ob, c_p, n_p, m_p = _mlstm_seq(qm, km, vm, og, gc, gr, mhn)

    tabs_s = _rope_tables(np.full((NB,), PAST_LEN))
    conv_s = jnp.swapaxes(state_conv[l], 0, 1)
    (qa_s, kvn0, kvn1, kvn2, qm_s, km_s, vm_s, og_s, ga_s, gb_s, gc_s, cn_s) = _inproj_step(
        x_sample[:, 0, :], conv_s, tabs_s, in_wts)
    pad_row = lambda t: jnp.pad(t, ((0, 0), (0, SM_LANES - t.shape[1])))
    att_half = lambda t: pad_row(t.reshape(NB, N_GROUPS, 2, HALF_W).transpose(0, 2, 1, 3).reshape(2 * NB, -1))
    mem_half = lambda t: t.reshape(2 * NB, SM_LANES)
    k_new = jnp.concatenate([kv[:, :GROUP_W] for kv in (kvn0, kvn1, kvn2)], axis=1)
    v_new = jnp.concatenate([kv[:, GROUP_W:] for kv in (kvn0, kvn1, kvn2)], axis=1)
    gates = jnp.concatenate([t.reshape(2 * NB, M_HEADS // 2) for t in
                             (gc_s[:, :M_HEADS], gc_s[:, M_HEADS:2 * M_HEADS], state_m[l])], axis=1)
    sm_rows = [att_half(qa_s), att_half(k_new), att_half(v_new), mem_half(qm_s), mem_half(km_s), mem_half(vm_s),
               mem_half(og_s), mem_half(state_n[l].reshape(NB, M_INNER)), pad_row(gates)]
    sm_rows += [jnp.zeros((2 * NB, SM_LANES), F32)] * (SM_ROWS - len(sm_rows))
    sm_in = jnp.stack(sm_rows, axis=1)
    caches_t = [jnp.transpose(c[l], (0, 2, 3, 4, 1)).reshape(NB, 2, 2, HALF_W, c.shape[2]) for c in caches]
    for c, (w, dil) in zip(caches_t, DSWA_GROUPS):
        assert c.shape[-1] == w == BAND * dil, "each cache must hold exactly one full window"
    y_prompt, so, cn0, cn1, cn2, c_s = _post_fused(x_prompt, os_p, ls_p, ob, ga, gb, p_prompt[l], post_wts,
                                                   sm_in, caches_t, state_C[l], mhn.reshape(2, M_HEADS // 2, M_DK))
    kv_s = [jnp.transpose(c.reshape(NB, 2, HPG, HEAD_DIM, -1), (0, 4, 1, 2, 3)) for c in (cn0, cn1, cn2)]
    att_full = lambda t: t[:, :N_GROUPS * HALF_W].reshape(NB, 2, N_GROUPS, HALF_W).transpose(0, 2, 1, 3).reshape(
        NB, N_GROUPS, GROUP_W)
    o_s, l_s = att_full(so[:, SO_O, :]), att_full(so[:, SO_LSE, :])
    os_s = [o_s[:, g, :] for g in range(N_GROUPS)]
    ls_s = [l_s[:, g, :] for g in range(N_GROUPS)]
    ob_s = so[:, SO_OB, :].reshape(NB, M_INNER)
    n_s = so[:, SO_N, :].reshape(NB, M_HEADS, M_DK)
    m_s = so[:, SO_M, ::M_DK].reshape(NB, M_HEADS)
    y_sample = _post_step(x_sample[:, 0, :], os_s, ls_s, ob_s, ga_s, gb_s, p_sample[l][:, 0, :], post_wts)

    kv_shape = lambda n, w: (1, n, w, 2, HPG, HEAD_DIM)
    return (
        y_prompt, y_sample[:, None, :],
        kv0.reshape(kv_shape(B, DSWA_GROUPS[0][0])), kv_s[0].reshape(kv_shape(NB, DSWA_GROUPS[0][0])),
        kv1.reshape(kv_shape(B, DSWA_GROUPS[1][0])), kv_s[1].reshape(kv_shape(NB, DSWA_GROUPS[1][0])),
        kv2.reshape(kv_shape(B, DSWA_GROUPS[2][0])), kv_s[2].reshape(kv_shape(NB, DSWA_GROUPS[2][0])),
        cn[:, SUBLANES - (CONV_W - 1):, :][None], jnp.swapaxes(cn_s, 0, 1)[None],
        c_p[None], c_s[None],
        n_p[None], n_s[None],
        m_p[:, 0, :M_HEADS][None], m_s[None],
    )
```

```python
import functools

import jax
import jax.numpy as jnp
import numpy as np
from jax import lax
from jax.experimental import pallas as pl
from jax.experimental.pallas import tpu as pltpu

F32 = jnp.float32
BF16 = jnp.bfloat16

D_MODEL = 1024
PAST_LEN = 16384
HEAD_DIM = 64
ROPE_DIM = HEAD_DIM // 4
ROPE_HALF = ROPE_DIM // 2
ROPE_THETA = 500000.0
DSWA_GROUPS = ((128, 1), (512, 4), (2048, 16))
HPG = 4
N_GROUPS = len(DSWA_GROUPS)
GROUP_W = HPG * HEAD_DIM
A_QKV = N_GROUPS * GROUP_W
BAND = 128
M_HEADS = 4
M_INNER = D_MODEL
M_DK = M_INNER // M_HEADS
CONV_W = 4
D_FF = 2816
PLE_DIM = 256
EPS = 1e-6
NEG = -1e30

OFF_AQ = 0
OFF_AK = OFF_AQ + A_QKV
OFF_AV = OFF_AK + A_QKV
OFF_MQ = OFF_AV + A_QKV
OFF_MK = OFF_MQ + M_INNER
OFF_MV = OFF_MK + M_INNER
OFF_MO = OFF_MV + M_INNER
OFF_MI = OFF_MO + M_INNER
OFF_MF = OFF_MI + M_HEADS
OFF_GA = OFF_MF + M_HEADS
OFF_GB = OFF_GA + D_MODEL
IN_COLS = OFF_GB + D_MODEL

LANES = 128
SUBLANES = 8
VMEM_LIMIT = 56 * 1024 * 1024

TM_IN = 512
TM_POST = 256
M_CHUNK = 512
CUM_BLOCK = 256
MLSTM_G = 2
ATT_TQ = 1024

_NT = (((1,), (1,)), ((), ()))
_TN = (((0,), (0,)), ((), ()))


def _dot(a, b):
    return jnp.dot(a, b, preferred_element_type=F32)


def _wdot(a, wt_ref, lo, hi):
    return lax.dot_general(a, wt_ref[lo:hi, :], _NT, preferred_element_type=F32)


def _rms(x, g):
    return x * lax.rsqrt(jnp.mean(x * x, axis=-1, keepdims=True) + EPS) * g


def _bf16_pieces(x):
    hi = x.astype(BF16)
    r1 = x - hi.astype(F32)
    mid = r1.astype(BF16)
    lo = (r1 - mid.astype(F32)).astype(BF16)
    return hi, mid, lo


def _sigmoid(x):
    return 0.5 * jnp.tanh(0.5 * x) + 0.5


def _log_sigmoid(x):
    return jnp.minimum(x, 0.0) - jnp.log1p(jnp.exp(-jnp.abs(x)))


def _rope(t, ra, rm, rp):
    return t * ra + pltpu.roll(t, LANES - ROPE_HALF, 1) * rm + pltpu.roll(t, ROPE_HALF, 1) * rp


def _rope2(t, ra, rm, rp):
    return jnp.concatenate([_rope(t[:, :LANES], ra, rm, rp), _rope(t[:, LANES:], ra, rm, rp)], axis=1)


def _gate_cols(z):
    lane = lax.broadcasted_iota(jnp.int32, z.shape, 1)
    return jnp.where(lane < M_HEADS, z, _log_sigmoid(z))


def _inproj_seq_kernel(x_ref, nmix_ref, waqkv_ref, wmqk_ref, wmv_ref, wmo_ref, wga_ref, wgb_ref,
                       wgc_ref, wgr_ref, bc_ref, br_ref, cw_ref, cb_ref, c0_ref, ra_ref, rm_ref, rp_ref,
                       q0_ref, k0_ref, v0_ref, q1_ref, k1_ref, v1_ref, q2_ref, k2_ref, v2_ref,
                       qm_ref, km_ref, vm_ref, og_ref, ga_ref, gb_ref,
                       gc_ref, gr_ref, kv0_ref, kv1_ref, kv2_ref, cn_ref, cbuf, dbuf, *, tm, ns):
    i = pl.program_id(1)
    xn = _rms(x_ref[0], nmix_ref[...]).astype(BF16)
    ra, rm, rp = ra_ref[...], rm_ref[...], rp_ref[...]
    kv_refs = (kv0_ref, kv1_ref, kv2_ref)
    qkv_refs = ((q0_ref, k0_ref, v0_ref), (q1_ref, k1_ref, v1_ref), (q2_ref, k2_ref, v2_ref))

    def put_dilated(ref, val, dil, slot):
        if dil == 1:
            ref[0] = val.astype(BF16)
            return
        for c in range(GROUP_W // LANES):
            dbuf[slot, c] = val[:, c * LANES:(c + 1) * LANES]
        for r in range(dil):
            for c in range(GROUP_W // LANES):
                lo = r * GROUP_W + c * LANES
                ref[0, :, lo:lo + LANES] = dbuf[slot, c, pl.ds(r, tm // dil, stride=dil), :].astype(BF16)

    for g, (win, dil) in enumerate(DSWA_GROUPS):
        zq = _wdot(xn, waqkv_ref, OFF_AQ + g * GROUP_W, OFF_AQ + (g + 1) * GROUP_W)
        put_dilated(qkv_refs[g][0], _rope2(zq, ra, rm, rp) * (HEAD_DIM ** -0.5), dil, 3 * g)
        zk = _wdot(xn, waqkv_ref, OFF_AK + g * GROUP_W, OFF_AK + (g + 1) * GROUP_W)
        kr = _rope2(zk, ra, rm, rp)
        put_dilated(qkv_refs[g][1], kr, dil, 3 * g + 1)
        zv = _wdot(xn, waqkv_ref, OFF_AV + g * GROUP_W, OFF_AV + (g + 1) * GROUP_W)
        put_dilated(qkv_refs[g][2], zv, dil, 3 * g + 2)
        rows = min(win, tm)
        nblk = win // rows

        @pl.when(i >= ns - nblk)
        def _():
            kv_refs[g][0, :, :GROUP_W] = kr[tm - rows:, :]
            kv_refs[g][0, :, GROUP_W:] = zv[tm - rows:, :]

    @pl.when(i == 0)
    def _():
        for blk in range(2 * M_INNER // LANES):
            cbuf[blk, 0:SUBLANES, :] = c0_ref[0, :, blk * LANES:(blk + 1) * LANES]

    for c in range(2 * M_INNER // 256):
        pre = _wdot(xn, wmqk_ref, c * 256, (c + 1) * 256)
        halves = []
        for hf in range(256 // LANES):
            blk = c * (256 // LANES) + hf
            ls = slice(blk * LANES, (blk + 1) * LANES)
            pre_h = pre[:, hf * LANES:(hf + 1) * LANES]
            cbuf[blk, SUBLANES:SUBLANES + tm, :] = pre_h
            y = cb_ref[:, ls] + cw_ref[CONV_W - 1:CONV_W, ls] * pre_h
            for d in range(1, CONV_W):
                y = y + cw_ref[CONV_W - 1 - d:CONV_W - d, ls] * cbuf[blk, pl.ds(SUBLANES - d, tm), :]
            tail = pre_h[tm - SUBLANES:, :]
            cbuf[blk, 0:SUBLANES, :] = tail
            cn_ref[0, :, ls] = tail
            halves.append(y * _sigmoid(y))
        s = jnp.concatenate(halves, axis=1)
        if c < M_INNER // 256:
            qm_ref[0, :, c * 256:(c + 1) * 256] = s.astype(BF16)
        else:
            km_ref[0, :, c * 256 - M_INNER:(c + 1) * 256 - M_INNER] = (s * (M_DK ** -0.5)).astype(BF16)
        cs = slice((c % 2) * 512, (c % 2 + 1) * 512)
        w_ref, o_ref = ((wmv_ref, vm_ref), (wmo_ref, og_ref), (wga_ref, ga_ref), (wgb_ref, gb_ref))[c // 2]
        z = _wdot(xn, w_ref, cs.start, cs.stop)
        o_ref[0, :, cs] = (z if o_ref is vm_ref else _sigmoid(z)).astype(BF16)

    gcv = _gate_cols(_wdot(xn, wgc_ref, 0, LANES) + bc_ref[...])
    zr = lax.dot_general(wgr_ref[...], xn, _NT, preferred_element_type=F32) + br_ref[...]
    sub = lax.broadcasted_iota(jnp.int32, zr.shape, 0)
    grv = jnp.where(sub < M_HEADS, zr, _log_sigmoid(zr))
    L = CUM_BLOCK
    ti = lax.broadcasted_iota(jnp.int32, (L, L), 0)
    si = lax.broadcasted_iota(jnp.int32, (L, L), 1)
    lower = jnp.where(ti >= si, 1.0, 0.0).astype(BF16)
    upper = jnp.where(ti <= si, 1.0, 0.0).astype(BF16)
    lane = lax.broadcasted_iota(jnp.int32, (L, LANES), 1)
    sub_l = lax.broadcasted_iota(jnp.int32, (SUBLANES, L), 0)
    for blk in range(tm // L):
        rs = slice(blk * L, (blk + 1) * L)
        cum_c = sum(_dot(lower, p) for p in _bf16_pieces(gcv[rs, :]))
        cum_r = sum(_dot(p, upper) for p in _bf16_pieces(grv[:, rs]))
        if (blk * L) % M_CHUNK:
            cum_c = cum_c + carry_c
            cum_r = cum_r + carry_r
        carry_c, carry_r = cum_c[L - 1:L, :], cum_r[:, L - 1:L]
        gc_ref[0, rs, :] = jnp.where(lane < M_HEADS, gcv[rs, :], cum_c)
        gr_ref[0, :, rs] = jnp.where(sub_l < M_HEADS, grv[:, rs], cum_r)


def _const_spec(shape):
    nd = len(shape)
    return pl.BlockSpec(shape, lambda *_: (0,) * nd, pipeline_mode=pl.Buffered(1))


def _inproj_seq(x, c0, tables, wts):
    B, S, _ = x.shape
    tm = TM_IN
    ns = S // tm
    ra, rm, rp = tables

    def kv_spec(win):
        rows = min(win, tm)
        nblk = win // rows
        return pl.BlockSpec((1, rows, 2 * GROUP_W), lambda b, i: (b, jnp.maximum(i - (ns - nblk), 0), 0))

    tok = lambda w: pl.BlockSpec((1, tm, w), lambda b, i: (b, i, 0))
    tab = pl.BlockSpec((tm, LANES), lambda b, i: (i, 0))
    in_specs = ([tok(D_MODEL)] + [_const_spec(w.shape) for w in wts]
                + [pl.BlockSpec((1, SUBLANES, 2 * M_INNER), lambda b, i: (b, 0, 0)), tab, tab, tab])
    dil_shape = [jax.ShapeDtypeStruct((B, S // dil, dil * GROUP_W), BF16) for _, dil in DSWA_GROUPS for _ in range(3)]
    dil_specs = [pl.BlockSpec((1, tm // dil, dil * GROUP_W), lambda b, i: (b, i, 0))
                 for _, dil in DSWA_GROUPS for _ in range(3)]
    out_shape = (dil_shape
                 + [jax.ShapeDtypeStruct((B, S, M_INNER), BF16)] * 6
                 + [jax.ShapeDtypeStruct((B, S, LANES), F32), jax.ShapeDtypeStruct((B, SUBLANES, S), F32)]
                 + [jax.ShapeDtypeStruct((B, win, 2 * GROUP_W), F32) for win, _ in DSWA_GROUPS]
                 + [jax.ShapeDtypeStruct((B, SUBLANES, 2 * M_INNER), F32)])
    out_specs = (dil_specs + [tok(M_INNER)] * 6
                 + [tok(LANES), pl.BlockSpec((1, SUBLANES, tm), lambda b, i: (b, 0, i))]
                 + [kv_spec(win) for win, _ in DSWA_GROUPS]
                 + [pl.BlockSpec((1, SUBLANES, 2 * M_INNER), lambda b, i: (b, 0, 0))])
    return pl.pallas_call(
        functools.partial(_inproj_seq_kernel, tm=tm, ns=ns),
        grid=(B, ns),
        in_specs=in_specs,
        out_specs=out_specs,
        out_shape=out_shape,
        scratch_shapes=[pltpu.VMEM((2 * M_INNER // LANES, SUBLANES + tm, LANES), F32),
                        pltpu.VMEM((3 * N_GROUPS, GROUP_W // LANES, tm, LANES), F32)],
        compiler_params=pltpu.CompilerParams(
            dimension_semantics=("arbitrary", "arbitrary"), vmem_limit_bytes=VMEM_LIMIT),
        name="inproj_seq",
    )(x, *wts, c0, ra, rm, rp)


def _inproj_step_kernel(x_ref, nmix_ref, waqkv_ref, wmqk_ref, wmv_ref, wmo_ref, wga_ref, wgb_ref,
                        wgc_ref, wgr_ref, bc_ref, br_ref, cw_ref, cb_ref, cs_ref, ra_ref, rm_ref, rp_ref,
                        qa_ref, kv0_ref, kv1_ref, kv2_ref, qm_ref, km_ref, vm_ref, og_ref, ga_ref, gb_ref,
                        gc_ref, cn_ref):
    del wgr_ref, br_ref
    xn = _rms(x_ref[...], nmix_ref[...]).astype(BF16)
    ra, rm, rp = ra_ref[...], rm_ref[...], rp_ref[...]
    kv_refs = (kv0_ref, kv1_ref, kv2_ref)
    for g in range(N_GROUPS):
        cs = slice(g * GROUP_W, (g + 1) * GROUP_W)
        zq = _wdot(xn, waqkv_ref, OFF_AQ + g * GROUP_W, OFF_AQ + (g + 1) * GROUP_W)
        qa_ref[:, cs] = _rope2(zq, ra, rm, rp) * (HEAD_DIM ** -0.5)
        zk = _wdot(xn, waqkv_ref, OFF_AK + g * GROUP_W, OFF_AK + (g + 1) * GROUP_W)
        kv_refs[g][:, :GROUP_W] = _rope2(zk, ra, rm, rp)
        kv_refs[g][:, GROUP_W:] = _wdot(xn, waqkv_ref, OFF_AV + g * GROUP_W, OFF_AV + (g + 1) * GROUP_W)

    for c in range(2 * M_INNER // 256):
        cs = slice(c * 256, (c + 1) * 256)
        pre = _wdot(xn, wmqk_ref, cs.start, cs.stop)
        y = cb_ref[:, cs] + cw_ref[CONV_W - 1:CONV_W, cs] * pre
        for j in range(CONV_W - 1):
            y = y + cw_ref[j:j + 1, cs] * cs_ref[j, :, cs]
        for j in range(CONV_W - 2):
            cn_ref[j, :, cs] = cs_ref[j + 1, :, cs]
        cn_ref[CONV_W - 2, :, cs] = pre
        s = y * _sigmoid(y)
        if c < M_INNER // 256:
            qm_ref[:, cs] = s
        else:
            km_ref[:, c * 256 - M_INNER:(c + 1) * 256 - M_INNER] = s * (M_DK ** -0.5)

    for c in range(M_INNER // 256):
        cs = slice(c * 256, (c + 1) * 256)
        vm_ref[:, cs] = _wdot(xn, wmv_ref, cs.start, cs.stop)
        og_ref[:, cs] = _sigmoid(_wdot(xn, wmo_ref, cs.start, cs.stop))
        ga_ref[:, cs] = _sigmoid(_wdot(xn, wga_ref, cs.start, cs.stop))
        gb_ref[:, cs] = _sigmoid(_wdot(xn, wgb_ref, cs.start, cs.stop))
    gc_ref[...] = _gate_cols(_wdot(xn, wgc_ref, 0, LANES) + bc_ref[...])


def _inproj_step(x, conv_state, tables, wts):
    nb = x.shape[0]
    sds = lambda w: jax.ShapeDtypeStruct((nb, w), F32)
    out_shape = ([sds(A_QKV)] + [sds(2 * GROUP_W)] * N_GROUPS + [sds(M_INNER)] * 6 + [sds(LANES)]
                 + [jax.ShapeDtypeStruct((CONV_W - 1, nb, 2 * M_INNER), F32)])
    return pl.pallas_call(
        _inproj_step_kernel,
        out_shape=out_shape,
        compiler_params=pltpu.CompilerParams(vmem_limit_bytes=VMEM_LIMIT),
        name="inproj_step",
    )(x, *wts, conv_state, *tables)


def _attn_seq_kernel(q_ref, k_ref, v_ref, kp_ref, vp_ref, w_ref, o_ref, lse_ref, wb_ref, *, tq, rpb):
    wb_ref[...] = w_ref[...].astype(BF16)
    i = pl.program_id(2)
    lane_head = lax.broadcasted_iota(jnp.int32, (1, GROUP_W), 1) // HEAD_DIM
    head_mask = [jnp.where(lane_head == h, 1.0, 0.0).astype(BF16) for h in range(HPG)]
    lane_head_full = lax.broadcasted_iota(jnp.int32, (BAND, GROUP_W), 1) // HEAD_DIM
    qi = lax.broadcasted_iota(jnp.int32, (BAND, 2 * BAND), 0)
    kj = lax.broadcasted_iota(jnp.int32, (BAND, 2 * BAND), 1)
    band = (kj >= qi) & (kj <= qi + BAND)
    first_lo = jnp.where(i > 0, 0, BAND)
    bias = jnp.where(band, 0.0, NEG)
    bias_first = jnp.where(band & (kj >= first_lo), 0.0, NEG)

    for rr, j in [(rr, j) for rr in range(rpb) for j in range(tq // BAND)]:
        ls = slice(rr * GROUP_W, (rr + 1) * GROUP_W)
        b1 = bias_first if j == 0 else bias
        q = q_ref[0, j * BAND:(j + 1) * BAND, ls]
        if j == 0:
            kk = jnp.concatenate([kp_ref[0, :, ls], k_ref[0, 0:BAND, ls]], axis=0)
            vv = jnp.concatenate([vp_ref[0, :, ls], v_ref[0, 0:BAND, ls]], axis=0)
        else:
            kk = k_ref[0, (j - 1) * BAND:(j + 1) * BAND, ls]
            vv = v_ref[0, (j - 1) * BAND:(j + 1) * BAND, ls]
        qblk = jnp.concatenate([q * head_mask[h] for h in range(HPG)], axis=0)
        s = lax.dot_general(qblk, kk, _NT, preferred_element_type=F32) + jnp.concatenate([b1] * HPG, axis=0)
        mx = jnp.max(s, axis=-1, keepdims=True)
        e = jnp.exp(s - mx)
        den = jnp.sum(e, axis=-1, keepdims=True)
        pv = _dot(e.astype(BF16), vv) * (1.0 / den)
        lse_rows = mx + jnp.log(den)
        o = pv[0:BAND]
        lse = jnp.broadcast_to(lse_rows[0:BAND], (BAND, GROUP_W))
        for h in range(1, HPG):
            own = lane_head_full == h
            o = jnp.where(own, pv[h * BAND:(h + 1) * BAND], o)
            lse = jnp.where(own, lse_rows[h * BAND:(h + 1) * BAND], lse)
        o_ref[0, j * BAND:(j + 1) * BAND, ls] = o
        lse_ref[0, j * BAND:(j + 1) * BAND, ls] = lse


def _attn_seq(q, k, v, dil, w):
    B, nsub, _ = q.shape
    tq = min(ATT_TQ, nsub)
    rpb = min(dil, ATT_TQ // tq)
    g1, g2 = dil // rpb, nsub // tq
    w_rows = w.shape[0] // (B * g1 * g2)
    assert w_rows * B * g1 * g2 == w.shape[0] and w_rows % 16 == 0
    w_spec = pl.BlockSpec((w_rows, w.shape[1]), lambda b, r, i: ((b * g1 + r) * g2 + i, 0))
    in_spec = pl.BlockSpec((1, tq, rpb * GROUP_W), lambda b, r, i: (b, i, r))
    prev_spec = pl.BlockSpec((1, BAND, rpb * GROUP_W), lambda b, r, i: (b, jnp.maximum(i * (tq // BAND) - 1, 0), r))
    out_spec = pl.BlockSpec((1, tq, rpb * GROUP_W), lambda b, r, i: (b, i, r))
    return pl.pallas_call(
        functools.partial(_attn_seq_kernel, tq=tq, rpb=rpb),
        grid=(B, g1, g2),
        in_specs=[in_spec] * 3 + [prev_spec] * 2 + [w_spec],
        out_specs=[out_spec] * 2 + [w_spec],
        out_shape=[jax.ShapeDtypeStruct((B, nsub, dil * GROUP_W), F32)] * 2 + [jax.ShapeDtypeStruct(w.shape, BF16)],
        compiler_params=pltpu.CompilerParams(
            dimension_semantics=("arbitrary", "arbitrary", "arbitrary"), vmem_limit_bytes=VMEM_LIMIT),
        name=f"attn_seq_d{dil}",
    )(q, k, v, k, v, w)


def _mlstm_seq_kernel(q_ref, k_ref, v_ref, og_ref, gc_ref, gr_ref, mhn_ref,
                      ob_ref, c_out, n_out, m_out, c_s, n_s, m_s, *, L, G):
    c = pl.program_id(1)

    @pl.when(c == 0)
    def _():
        c_s[...] = jnp.zeros(c_s.shape, F32)
        n_s[...] = jnp.zeros(n_s.shape, F32)
        m_s[...] = jnp.zeros(m_s.shape, F32)

    ti = lax.broadcasted_iota(jnp.int32, (L, L), 0)
    si = lax.broadcasted_iota(jnp.int32, (L, L), 1)
    tril = ti >= si
    lane = lax.broadcasted_iota(jnp.int32, (1, LANES), 1)
    m_vec = [jnp.zeros((1, LANES), F32)] * G

    for g, h in [(g, h) for g in range(G) for h in range(M_HEADS)]:
        gc = gc_ref[g]
        gr = gr_ref[g]
        hs = slice(h * M_DK, (h + 1) * M_DK)
        b_col = gc[:, M_HEADS + h:M_HEADS + h + 1]
        i_col = gc[:, h:h + 1]
        b_row = gr[M_HEADS + h:M_HEADS + h + 1, :]
        i_row = gr[h:h + 1, :]
        m_prev = m_s[g, h, 0:1, 0:1]
        qh = q_ref[g, :, hs]
        kh = k_ref[g, :, hs]
        vh = v_ref[g, :, hs]
        c_prev = c_s[g, h]
        n_prev = n_s[g, h]

        dmat = jnp.where(tril, b_col - b_row + i_row, -jnp.inf)
        m_inter = b_col + m_prev
        m_t = jnp.maximum(m_inter, jnp.max(dmat, axis=-1, keepdims=True))
        w_intra = lax.dot_general(qh, kh, _NT, preferred_element_type=F32) * jnp.exp(dmat - m_t)
        w_inter = jnp.exp(m_inter - m_t)
        num = _dot(w_intra.astype(BF16), vh) + w_inter * _dot(qh, c_prev.astype(BF16))
        den = (jnp.sum(w_intra, axis=-1, keepdims=True)
               + w_inter * jnp.sum(qh.astype(F32) * n_prev, axis=-1, keepdims=True))
        hh = num / jnp.maximum(jnp.abs(den), jnp.exp(-m_t))
        hn = hh * lax.rsqrt(jnp.mean(hh * hh, axis=-1, keepdims=True) + EPS) * mhn_ref[h:h + 1, :]
        ob_ref[g, :, hs] = (hn * og_ref[g, :, hs].astype(F32)).astype(BF16)

        b_last = b_row[:, L - 1:L]
        m_new = jnp.maximum(b_last + m_prev, jnp.max(b_last - b_row + i_row, axis=-1, keepdims=True))
        wk = jnp.exp(b_last - b_col + i_col - m_new)
        decay = jnp.exp(b_last + m_prev - m_new)
        ks = kh.astype(F32) * wk
        c_new = decay * c_prev + lax.dot_general(ks.astype(BF16), vh, _TN, preferred_element_type=F32)
        n_new = decay * n_prev + jnp.sum(ks, axis=0, keepdims=True)
        c_s[g, h] = c_new
        n_s[g, h] = n_new
        m_s[g, h] = jnp.broadcast_to(m_new, (SUBLANES, LANES))
        c_out[g, h] = c_new
        n_out[g, h:h + 1, :] = n_new
        m_vec[g] = jnp.where(lane == h, m_new, m_vec[g])
    for g in range(G):
        m_out[g] = m_vec[g]


def _mlstm_seq(qm, km, vm, og, gc, gr, mhn):
    B, S, _ = qm.shape
    L = M_CHUNK
    G = MLSTM_G
    tok = pl.BlockSpec((G, L, M_INNER), lambda b, c: (b, c, 0))
    return pl.pallas_call(
        functools.partial(_mlstm_seq_kernel, L=L, G=G),
        grid=(B // G, S // L),
        in_specs=[tok, tok, tok, tok,
                  pl.BlockSpec((G, L, LANES), lambda b, c: (b, c, 0)),
                  pl.BlockSpec((G, SUBLANES, L), lambda b, c: (b, 0, c)),
                  pl.BlockSpec((M_HEADS, M_DK), lambda b, c: (0, 0))],
        out_specs=[tok,
                   pl.BlockSpec((G, M_HEADS, M_DK, M_DK), lambda b, c: (b, 0, 0, 0)),
                   pl.BlockSpec((G, M_HEADS, M_DK), lambda b, c: (b, 0, 0)),
                   pl.BlockSpec((G, 1, LANES), lambda b, c: (b, 0, 0))],
        out_shape=[jax.ShapeDtypeStruct((B, S, M_INNER), BF16),
                   jax.ShapeDtypeStruct((B, M_HEADS, M_DK, M_DK), F32),
                   jax.ShapeDtypeStruct((B, M_HEADS, M_DK), F32),
                   jax.ShapeDtypeStruct((B, 1, LANES), F32)],
        scratch_shapes=[pltpu.VMEM((G, M_HEADS, M_DK, M_DK), F32),
                        pltpu.VMEM((G, M_HEADS, 1, M_DK), F32),
                        pltpu.VMEM((G, M_HEADS, SUBLANES, LANES), F32)],
        compiler_params=pltpu.CompilerParams(
            dimension_semantics=("arbitrary", "arbitrary"), vmem_limit_bytes=VMEM_LIMIT),
        name="mlstm_seq",
    )(qm, km, vm, og, gc, gr, mhn)


def _post_kernel(x_ref, o0_ref, o1_ref, o2_ref, l0_ref, l1_ref, l2_ref, ob_ref, ga_ref, gb_ref, p_ref,
                 wpa_ref, wpb_ref, wout_ref, nffn_ref, wg_ref, wu_ref, wd_ref, nple_ref, wpg_ref, wpp_ref,
                 nfin_ref, y_ref, *ubuf, dils):
    lead = (0,) * (len(x_ref.shape) - 2)
    rd = lambda r: r[lead] if lead else r[...]
    tm = x_ref.shape[-2]

    def undilate(ref, dil, slot):
        if dil == 1:
            return rd(ref)
        halves = range(GROUP_W // LANES)
        for r in range(dil):
            for c in halves:
                lo = r * GROUP_W + c * LANES
                ubuf[0][slot, c, pl.ds(r, tm // dil, stride=dil), :] = ref[0, :, lo:lo + LANES]
        return jnp.concatenate([ubuf[0][slot, c] for c in halves], axis=1)

    l0, l1, l2 = (undilate(r, d, 2 * g) for g, (r, d) in enumerate(zip((l0_ref, l1_ref, l2_ref), dils)))
    o0, o1, o2 = (undilate(r, d, 2 * g + 1) for g, (r, d) in enumerate(zip((o0_ref, o1_ref, o2_ref), dils)))
    lmax = jnp.maximum(jnp.maximum(l0, l1), l2)
    e0, e1, e2 = jnp.exp(l0 - lmax), jnp.exp(l1 - lmax), jnp.exp(l2 - lmax)
    o_a = (e0 * o0 + e1 * o1 + e2 * o2) / (e0 + e1 + e2)

    mix = (rd(ga_ref).astype(F32) * _dot(o_a.astype(BF16), wpa_ref[...])
           + rd(gb_ref).astype(F32) * _dot(rd(ob_ref).astype(BF16), wpb_ref[...]))
    h = rd(x_ref) + _dot(mix.astype(BF16), wout_ref[...])

    xf = _rms(h, nffn_ref[...]).astype(BF16)
    half = D_FF // 2
    for c in range(2):
        cs = slice(c * half, (c + 1) * half)
        gate = _dot(xf, wg_ref[:, cs])
        act = (gate * _sigmoid(gate)) * _dot(xf, wu_ref[:, cs])
        h = h + _dot(act.astype(BF16), wd_ref[cs, :])

    xp = _rms(h, nple_ref[...]).astype(BF16)
    h = h + _sigmoid(_dot(xp, wpg_ref[...])) * _dot(rd(p_ref).astype(BF16), wpp_ref[...])
    y = _rms(h, nfin_ref[...])
    if lead:
        y_ref[lead] = y
    else:
        y_ref[...] = y


def _post_step(x, os_, ls_, ob, ga, gb, p, wts):
    return pl.pallas_call(
        functools.partial(_post_kernel, dils=(1,) * N_GROUPS),
        out_shape=jax.ShapeDtypeStruct(x.shape, F32),
        compiler_params=pltpu.CompilerParams(vmem_limit_bytes=VMEM_LIMIT),
        name="post_step",
    )(x, *os_, *ls_, ob, ga, gb, p, *wts)


SM_ROWS = 16
SM_LANES = 512
T_QA, T_KN, T_VN, T_QM, T_KM, T_VM, T_OG, T_N, T_GATE = range(9)
SO_ROWS = 8
SO_OB, SO_N, SO_M, SO_LSE, SO_O = range(5)
HALF_HEADS = HPG // 2
HALF_W = HALF_HEADS * HEAD_DIM


def _eye(n):
    return lax.broadcasted_iota(jnp.int32, (n, n), 0) == lax.broadcasted_iota(jnp.int32, (n, n), 1)


def _col_of(row, eye):
    return jnp.sum(jnp.where(eye, jnp.broadcast_to(row, eye.shape), 0.0), axis=1, keepdims=True)


def _row_of(col, eye):
    return jnp.sum(jnp.where(eye, jnp.broadcast_to(col, eye.shape), 0.0), axis=0, keepdims=True)


def _round_bf16(x):
    return x.astype(BF16).astype(F32)


def _sample_step(tile, kv_refs, cst_ref, mhn_ref, so_ref, kvn_refs, cso_ref):
    so_ref[0] = jnp.zeros((SO_ROWS, SM_LANES), F32)
    eye_a = _eye(HALF_W)
    eye_m = _eye(M_DK)

    for g, (w, dil) in enumerate(DSWA_GROUPS):
        ls = slice(g * HALF_W, (g + 1) * HALF_W)
        k_t = kv_refs[g][0, 0, 0]
        v_t = kv_refs[g][0, 1, 0]
        k_row = tile[T_KN:T_KN + 1, ls]
        v_row = tile[T_VN:T_VN + 1, ls]
        qb = _round_bf16(tile[T_QA:T_QA + 1, ls])
        prod = k_t * _col_of(qb, eye_a)
        qk_new = qb * _round_bf16(k_row)
        pos = lax.broadcasted_iota(jnp.int32, (1, w), 1)
        pe, pn, lse = [], [], []
        for hl in range(HALF_HEADS):
            hd = slice(hl * HEAD_DIM, (hl + 1) * HEAD_DIM)
            s = jnp.where(pos % dil == 0, jnp.sum(prod[hd, :], axis=0, keepdims=True), NEG)
            s_new = jnp.sum(qk_new[:, hd], axis=-1, keepdims=True)
            mx = jnp.maximum(jnp.max(s, axis=-1, keepdims=True), s_new)
            e = jnp.exp(s - mx)
            e_new = jnp.exp(s_new - mx)
            den = jnp.sum(e, axis=-1, keepdims=True) + e_new
            pe.append(jnp.broadcast_to(e / den, (HEAD_DIM, w)))
            pn.append(jnp.broadcast_to(e_new / den, (1, HEAD_DIM)))
            lse.append(jnp.broadcast_to(mx + jnp.log(den), (1, HEAD_DIM)))
        o_col = jnp.sum(v_t * jnp.concatenate(pe, axis=0), axis=-1, keepdims=True)
        so_ref[0, SO_O:SO_O + 1, ls] = _row_of(o_col, eye_a) + jnp.concatenate(pn, axis=1) * _round_bf16(v_row)
        so_ref[0, SO_LSE:SO_LSE + 1, ls] = jnp.concatenate(lse, axis=1)
        last = lax.broadcasted_iota(jnp.int32, (HALF_W, w), 1) == w - 1
        kvn_refs[g][0, 0, 0] = jnp.where(last, _col_of(k_row, eye_a), pltpu.roll(k_t, w - 1, 1))
        kvn_refs[g][0, 1, 0] = jnp.where(last, _col_of(v_row, eye_a), pltpu.roll(v_t, w - 1, 1))

    eye = eye_m
    for j in range(M_HEADS // 2):
        hs = slice(j * M_DK, (j + 1) * M_DK)
        ig = tile[T_GATE:T_GATE + 1, j:j + 1]
        lf = tile[T_GATE:T_GATE + 1, 2 + j:3 + j]
        m_prev = tile[T_GATE:T_GATE + 1, 4 + j:5 + j]
        qb = _round_bf16(tile[T_QM:T_QM + 1, hs])
        kb = _round_bf16(tile[T_KM:T_KM + 1, hs])
        vb = _round_bf16(tile[T_VM:T_VM + 1, hs])
        n_prev = tile[T_N:T_N + 1, hs]
        c_prev = cst_ref[0, j]

        m_inter = lf + m_prev
        m_t = jnp.maximum(m_inter, ig)
        w_intra = jnp.sum(qb * kb, axis=-1, keepdims=True) * jnp.exp(ig - m_t)
        w_inter = jnp.exp(m_inter - m_t)
        q_c = jnp.sum(c_prev * _col_of(qb, eye), axis=0, keepdims=True)
        num = w_intra * vb + w_inter * q_c
        den = w_intra + w_inter * jnp.sum(qb * n_prev, axis=-1, keepdims=True)
        hh = num / jnp.maximum(jnp.abs(den), jnp.exp(-m_t))
        hn = hh * lax.rsqrt(jnp.mean(hh * hh, axis=-1, keepdims=True) + EPS) * mhn_ref[0, j:j + 1, :]
        ks = kb * jnp.exp(ig - m_t)
        so_ref[0, SO_OB:SO_OB + 1, hs] = hn * tile[T_OG:T_OG + 1, hs]
        so_ref[0, SO_N:SO_N + 1, hs] = w_inter * n_prev + ks
        so_ref[0, SO_M:SO_M + 1, hs] = jnp.broadcast_to(m_t, (1, M_DK))
        cso_ref[0, j] = w_inter * c_prev + _col_of(_round_bf16(ks), eye) * vb


N_POST_IN = 22


def _post_fused_kernel(*refs, dils, ns):
    post_in = refs[:N_POST_IN]
    sm_ref, c0_ref, c1_ref, c2_ref, cst_ref, mhn_ref = refs[N_POST_IN:N_POST_IN + 6]
    y_ref, so_ref, cn0_ref, cn1_ref, cn2_ref, cso_ref = refs[N_POST_IN + 6:N_POST_IN + 12]
    (ubuf,) = refs[N_POST_IN + 12:]
    _post_kernel(*post_in, y_ref, ubuf, dils=dils)
    _sample_step(sm_ref[0], (c0_ref, c1_ref, c2_ref), cst_ref, mhn_ref, so_ref,
                 (cn0_ref, cn1_ref, cn2_ref), cso_ref)


def _post_fused(x, os_, ls_, ob, ga, gb, p, wts, sm_in, caches_t, c_state, mhn):
    B, S, _ = x.shape
    NB = c_state.shape[0]
    tm = TM_POST
    ns = S // tm
    assert B * ns == 2 * NB, "one half of a sample batch element per post-mixer grid step"
    tok = lambda w: pl.BlockSpec((1, tm, w), lambda b, i: (b, i, 0))
    dils = tuple(dil for _, dil in DSWA_GROUPS)
    dil_specs = [pl.BlockSpec((1, tm // dil, dil * GROUP_W), lambda b, i: (b, i, 0)) for dil in dils]
    step = lambda b, i: b * ns + i
    cache_specs = [pl.BlockSpec((1, 2, 1, HALF_W, c.shape[-1]), lambda b, i: (step(b, i) // 2, 0, step(b, i) % 2, 0, 0))
                   for c in caches_t]
    cst_spec = pl.BlockSpec((1, M_HEADS // 2, M_DK, M_DK), lambda b, i: (step(b, i) // 2, step(b, i) % 2, 0, 0))
    in_specs = ([tok(D_MODEL)] + dil_specs * 2 + [tok(M_INNER)] * 3 + [tok(PLE_DIM)]
                + [_const_spec(w.shape) for w in wts]
                + [pl.BlockSpec((1, SM_ROWS, SM_LANES), lambda b, i: (step(b, i), 0, 0))]
                + cache_specs
                + [cst_spec, pl.BlockSpec((1, M_HEADS // 2, M_DK), lambda b, i: (step(b, i) % 2, 0, 0))])
    out_specs = ([tok(D_MODEL), pl.BlockSpec((1, SO_ROWS, SM_LANES), lambda b, i: (step(b, i), 0, 0))]
                 + cache_specs + [cst_spec])
    out_shape = ([jax.ShapeDtypeStruct((B, S, D_MODEL), F32),
                  jax.ShapeDtypeStruct((2 * NB, SO_ROWS, SM_LANES), F32)]
                 + [jax.ShapeDtypeStruct(c.shape, F32) for c in caches_t]
                 + [jax.ShapeDtypeStruct(c_state.shape, F32)])
    return pl.pallas_call(
        functools.partial(_post_fused_kernel, dils=dils, ns=ns),
        grid=(B, ns),
        in_specs=in_specs,
        out_specs=out_specs,
        out_shape=out_shape,
        scratch_shapes=[pltpu.VMEM((2 * N_GROUPS, GROUP_W // LANES, tm, LANES), F32)],
        compiler_params=pltpu.CompilerParams(
            dimension_semantics=("arbitrary", "arbitrary"), vmem_limit_bytes=VMEM_LIMIT),
        name="post_seq_sample_stream",
    )(x, *os_, *ls_, ob, ga, gb, p, *wts, sm_in, *caches_t, c_state, mhn)


def _rope_tables(pos):
    pos = np.asarray(pos, np.float64)
    t = pos.shape[0]
    inv = np.power(ROPE_THETA, -np.arange(ROPE_HALF, dtype=np.float64) / ROPE_HALF)
    ang = pos[:, None] * inv[None, :]
    cos, sin = np.cos(ang), np.sin(ang)
    one = np.ones((t, HEAD_DIM - ROPE_DIM))
    zero = np.zeros((t, HEAD_DIM - ROPE_DIM))
    z8 = np.zeros((t, ROPE_HALF))
    a = np.concatenate([cos, cos, one], axis=1)
    m = np.concatenate([-sin, z8, zero], axis=1)
    p = np.concatenate([z8, sin, zero], axis=1)
    rep = LANES // HEAD_DIM
    return tuple(jnp.asarray(np.tile(v, (1, rep)).astype(np.float32)) for v in (a, m, p))


def kernel(x_prompt, x_sample, cache_kv_w128, cache_kv_w512, cache_kv_w2048, state_conv, state_C, state_n,
           state_m, p_prompt, p_sample, norm_mix, w_in, conv_w, conv_b, b_igate, b_fgate, mh_norm, w_proj_a,
           w_proj_b, w_out, norm_ffn, w_gate, w_up, w_down, norm_ple, w_ple_gate, w_ple_proj, norm_final):
    depth = w_in.shape[0]
    assert depth == 1, "single trunk layer"
    B, S, _ = x_prompt.shape
    NB = x_sample.shape[0]
    assert x_sample.shape[1] == 1
    caches = (cache_kv_w128, cache_kv_w512, cache_kv_w2048)
    assert S >= DSWA_GROUPS[-1][0] and S % (BAND * DSWA_GROUPS[-1][1]) == 0
    past = 0
    l = 0

    wt = w_in[l].T.astype(BF16)
    gate_wt = wt[OFF_MI:OFF_GA]
    gate_b = jnp.concatenate([b_igate[l], b_fgate[l]])
    in_wts = (
        norm_mix[l][None, :],
        wt[OFF_AQ:OFF_MQ],
        wt[OFF_MQ:OFF_MV],
        wt[OFF_MV:OFF_MO],
        wt[OFF_MO:OFF_MI],
        wt[OFF_GA:OFF_GB],
        wt[OFF_GB:IN_COLS],
        jnp.pad(gate_wt, ((0, LANES - 2 * M_HEADS), (0, 0))),
        gate_wt,
        jnp.pad(gate_b, (0, LANES - 2 * M_HEADS))[None, :],
        gate_b[:, None],
        conv_w[l],
        conv_b[l][None, :],
    )
    mhn = mh_norm[l]

    c0 = jnp.zeros((B, SUBLANES, 2 * M_INNER), F32)
    tabs_p = _rope_tables(past + np.arange(S))
    (*qkv, qm, km, vm, og, ga, gb, gc, gr, kv0, kv1, kv2, cn) = _inproj_seq(x_prompt, c0, tabs_p, in_wts)
    os_p, ls_p, ffn_bf16 = [], [], []
    for g, ((_, dil), w_ffn) in enumerate(zip(DSWA_GROUPS, (w_gate[l], w_up[l], w_down[l]))):
        o, lse, wb = _attn_seq(*qkv[3 * g:3 * g + 3], dil, w_ffn)
        os_p.append(o)
        ls_p.append(lse)
        ffn_bf16.append(wb)
    post_wts = (
        w_proj_a[l].astype(BF16), w_proj_b[l].astype(BF16), w_out[l].astype(BF16), norm_ffn[l][None, :],
        *ffn_bf16, norm_ple[l][None, :],
        w_ple_gate[l].astype(BF16), w_ple_proj[l].astype(BF16), norm_final[None, :],
    )
    ob, c_p, n_p, m_p = _mlstm_seq(qm, km, vm, og, gc, gr, mhn)

    tabs_s = _rope_tables(np.full((NB,), PAST_LEN))
    conv_s = jnp.swapaxes(state_conv[l], 0, 1)
    (qa_s, kvn0, kvn1, kvn2, qm_s, km_s, vm_s, og_s, ga_s, gb_s, gc_s, cn_s) = _inproj_step(
        x_sample[:, 0, :], conv_s, tabs_s, in_wts)
    pad_row = lambda t: jnp.pad(t, ((0, 0), (0, SM_LANES - t.shape[1])))
    att_half = lambda t: pad_row(t.reshape(NB, N_GROUPS, 2, HALF_W).transpose(0, 2, 1, 3).reshape(2 * NB, -1))
    mem_half = lambda t: t.reshape(2 * NB, SM_LANES)
    k_new = jnp.concatenate([kv[:, :GROUP_W] for kv in (kvn0, kvn1, kvn2)], axis=1)
    v_new = jnp.concatenate([kv[:, GROUP_W:] for kv in (kvn0, kvn1, kvn2)], axis=1)
    gates = jnp.concatenate([t.reshape(2 * NB, M_HEADS // 2) for t in
                             (gc_s[:, :M_HEADS], gc_s[:, M_HEADS:2 * M_HEADS], state_m[l])], axis=1)
    sm_rows = [att_half(qa_s), att_half(k_new), att_half(v_new), mem_half(qm_s), mem_half(km_s), mem_half(vm_s),
               mem_half(og_s), mem_half(state_n[l].reshape(NB, M_INNER)), pad_row(gates)]
    sm_rows += [jnp.zeros((2 * NB, SM_LANES), F32)] * (SM_ROWS - len(sm_rows))
    sm_in = jnp.stack(sm_rows, axis=1)
    caches_t = [jnp.transpose(c[l], (0, 2, 3, 4, 1)).reshape(NB, 2, 2, HALF_W, c.shape[2]) for c in caches]
    for c, (w, dil) in zip(caches_t, DSWA_GROUPS):
        assert c.shape[-1] == w == BAND * dil, "each cache must hold exactly one full window"
    y_prompt, so, cn0, cn1, cn2, c_s = _post_fused(x_prompt, os_p, ls_p, ob, ga, gb, p_prompt[l], post_wts,
                                                   sm_in, caches_t, state_C[l], mhn.reshape(2, M_HEADS // 2, M_DK))
    kv_s = [jnp.transpose(c.reshape(NB, 2, HPG, HEAD_DIM, -1), (0, 4, 1, 2, 3)) for c in (cn0, cn1, cn2)]
    att_full = lambda t: t[:, :N_GROUPS * HALF_W].reshape(NB, 2, N_GROUPS, HALF_W).transpose(0, 2, 1, 3).reshape(
        NB, N_GROUPS, GROUP_W)
    o_s, l_s = att_full(so[:, SO_O, :]), att_full(so[:, SO_LSE, :])
    os_s = [o_s[:, g, :] for g in range(N_GROUPS)]
    ls_s = [l_s[:, g, :] for g in range(N_GROUPS)]
    ob_s = so[:, SO_OB, :].reshape(NB, M_INNER)
    n_s = so[:, SO_N, :].reshape(NB, M_HEADS, M_DK)
    m_s = so[:, SO_M, ::M_DK].reshape(NB, M_HEADS)
    y_sample = _post_step(x_sample[:, 0, :], os_s, ls_s, ob_s, ga_s, gb_s, p_sample[l][:, 0, :], post_wts)

    kv_shape = lambda n, w: (1, n, w, 2, HPG, HEAD_DIM)
    return (
        y_prompt, y_sample[:, None, :],
        kv0.reshape(kv_shape(B, DSWA_GROUPS[0][0])), kv_s[0].reshape(kv_shape(NB, DSWA_GROUPS[0][0])),
        kv1.reshape(kv_shape(B, DSWA_GROUPS[1][0])), kv_s[1].reshape(kv_shape(NB, DSWA_GROUPS[1][0])),
        kv2.reshape(kv_shape(B, DSWA_GROUPS[2][0])), kv_s[2].reshape(kv_shape(NB, DSWA_GROUPS[2][0])),
        cn[:, SUBLANES - (CONV_W - 1):, :][None], jnp.swapaxes(cn_s, 0, 1)[None],
        c_p[None], c_s[None],
        n_p[None], n_s[None],
        m_p[:, 0, :M_HEADS][None], m_s[None],
    )
```

```python
import functools

import jax
import jax.numpy as jnp
import numpy as np
from jax import lax
from jax.experimental import pallas as pl
from jax.experimental.pallas import tpu as pltpu

F32 = jnp.float32
BF16 = jnp.bfloat16

D_MODEL = 1024
PAST_LEN = 16384
HEAD_DIM = 64
ROPE_DIM = HEAD_DIM // 4
ROPE_HALF = ROPE_DIM // 2
ROPE_THETA = 500000.0
DSWA_GROUPS = ((128, 1), (512, 4), (2048, 16))
HPG = 4
N_GROUPS = len(DSWA_GROUPS)
GROUP_W = HPG * HEAD_DIM
A_QKV = N_GROUPS * GROUP_W
BAND = 128
M_HEADS = 4
M_INNER = D_MODEL
M_DK = M_INNER // M_HEADS
CONV_W = 4
D_FF = 2816
PLE_DIM = 256
EPS = 1e-6
NEG = -1e30

OFF_AQ = 0
OFF_AK = OFF_AQ + A_QKV
OFF_AV = OFF_AK + A_QKV
OFF_MQ = OFF_AV + A_QKV
OFF_MK = OFF_MQ + M_INNER
OFF_MV = OFF_MK + M_INNER
OFF_MO = OFF_MV + M_INNER
OFF_MI = OFF_MO + M_INNER
OFF_MF = OFF_MI + M_HEADS
OFF_GA = OFF_MF + M_HEADS
OFF_GB = OFF_GA + D_MODEL
IN_COLS = OFF_GB + D_MODEL

LANES = 128
SUBLANES = 8
VMEM_LIMIT = 56 * 1024 * 1024

TM_IN = 512
TM_POST = 256
M_CHUNK = 512
CUM_BLOCK = 256
MLSTM_G = 2
ATT_TQ = 1024

_NT = (((1,), (1,)), ((), ()))
_TN = (((0,), (0,)), ((), ()))


def _dot(a, b):
    return jnp.dot(a, b, preferred_element_type=F32)


def _wdot(a, wt_ref, lo, hi):
    return lax.dot_general(a, wt_ref[lo:hi, :], _NT, preferred_element_type=F32)


def _rms(x, g):
    return x * lax.rsqrt(jnp.mean(x * x, axis=-1, keepdims=True) + EPS) * g


def _bf16_pieces(x):
    hi = x.astype(BF16)
    r1 = x - hi.astype(F32)
    mid = r1.astype(BF16)
    lo = (r1 - mid.astype(F32)).astype(BF16)
    return hi, mid, lo


def _sigmoid(x):
    return 0.5 * jnp.tanh(0.5 * x) + 0.5


def _log_sigmoid(x):
    return jnp.minimum(x, 0.0) - jnp.log1p(jnp.exp(-jnp.abs(x)))


def _rope(t, ra, rm, rp):
    return t * ra + pltpu.roll(t, LANES - ROPE_HALF, 1) * rm + pltpu.roll(t, ROPE_HALF, 1) * rp


def _rope2(t, ra, rm, rp):
    return jnp.concatenate([_rope(t[:, :LANES], ra, rm, rp), _rope(t[:, LANES:], ra, rm, rp)], axis=1)


def _gate_cols(z):
    lane = lax.broadcasted_iota(jnp.int32, z.shape, 1)
    return jnp.where(lane < M_HEADS, z, _log_sigmoid(z))


def _inproj_seq_kernel(x_ref, nmix_ref, waqkv_ref, wmqk_ref, wmv_ref, wmo_ref, wga_ref, wgb_ref,
                       wgc_ref, wgr_ref, bc_ref, br_ref, cw_ref, cb_ref, c0_ref, ra_ref, rm_ref, rp_ref,
                       q0_ref, k0_ref, v0_ref, q1_ref, k1_ref, v1_ref, q2_ref, k2_ref, v2_ref,
                       qm_ref, km_ref, vm_ref, og_ref, ga_ref, gb_ref,
                       gc_ref, gr_ref, kv0_ref, kv1_ref, kv2_ref, cn_ref, cbuf, dbuf, *, tm, ns):
    i = pl.program_id(1)
    xn = _rms(x_ref[0], nmix_ref[...]).astype(BF16)
    ra, rm, rp = ra_ref[...], rm_ref[...], rp_ref[...]
    kv_refs = (kv0_ref, kv1_ref, kv2_ref)
    qkv_refs = ((q0_ref, k0_ref, v0_ref), (q1_ref, k1_ref, v1_ref), (q2_ref, k2_ref, v2_ref))

    def put_dilated(ref, val, dil, slot):
        if dil == 1:
            ref[0] = val.astype(BF16)
            return
        for c in range(GROUP_W // LANES):
            dbuf[slot, c] = val[:, c * LANES:(c + 1) * LANES]
        for r in range(dil):
            for c in range(GROUP_W // LANES):
                lo = r * GROUP_W + c * LANES
                ref[0, :, lo:lo + LANES] = dbuf[slot, c, pl.ds(r, tm // dil, stride=dil), :].astype(BF16)

    for g, (win, dil) in enumerate(DSWA_GROUPS):
        zq = _wdot(xn, waqkv_ref, OFF_AQ + g * GROUP_W, OFF_AQ + (g + 1) * GROUP_W)
        put_dilated(qkv_refs[g][0], _rope2(zq, ra, rm, rp) * (HEAD_DIM ** -0.5), dil, 3 * g)
        zk = _wdot(xn, waqkv_ref, OFF_AK + g * GROUP_W, OFF_AK + (g + 1) * GROUP_W)
        kr = _rope2(zk, ra, rm, rp)
        put_dilated(qkv_refs[g][1], kr, dil, 3 * g + 1)
        zv = _wdot(xn, waqkv_ref, OFF_AV + g * GROUP_W, OFF_AV + (g + 1) * GROUP_W)
        put_dilated(qkv_refs[g][2], zv, dil, 3 * g + 2)
        rows = min(win, tm)
        nblk = win // rows

        @pl.when(i >= ns - nblk)
        def _():
            kv_refs[g][0, :, :GROUP_W] = kr[tm - rows:, :]
            kv_refs[g][0, :, GROUP_W:] = zv[tm - rows:, :]

    @pl.when(i == 0)
    def _():
        for blk in range(2 * M_INNER // LANES):
            cbuf[blk, 0:SUBLANES, :] = c0_ref[0, :, blk * LANES:(blk + 1) * LANES]

    for c in range(2 * M_INNER // 256):
        pre = _wdot(xn, wmqk_ref, c * 256, (c + 1) * 256)
        halves = []
        for hf in range(256 // LANES):
            blk = c * (256 // LANES) + hf
            ls = slice(blk * LANES, (blk + 1) * LANES)
            pre_h = pre[:, hf * LANES:(hf + 1) * LANES]
            cbuf[blk, SUBLANES:SUBLANES + tm, :] = pre_h
            y = cb_ref[:, ls] + cw_ref[CONV_W - 1:CONV_W, ls] * pre_h
            for d in range(1, CONV_W):
                y = y + cw_ref[CONV_W - 1 - d:CONV_W - d, ls] * cbuf[blk, pl.ds(SUBLANES - d, tm), :]
            tail = pre_h[tm - SUBLANES:, :]
            cbuf[blk, 0:SUBLANES, :] = tail
            cn_ref[0, :, ls] = tail
            halves.append(y * _sigmoid(y))
        s = jnp.concatenate(halves, axis=1)
        if c < M_INNER // 256:
            qm_ref[0, :, c * 256:(c + 1) * 256] = s.astype(BF16)
        else:
            km_ref[0, :, c * 256 - M_INNER:(c + 1) * 256 - M_INNER] = (s * (M_DK ** -0.5)).astype(BF16)
        cs = slice((c % 2) * 512, (c % 2 + 1) * 512)
        w_ref, o_ref = ((wmv_ref, vm_ref), (wmo_ref, og_ref), (wga_ref, ga_ref), (wgb_ref, gb_ref))[c // 2]
        z = _wdot(xn, w_ref, cs.start, cs.stop)
        o_ref[0, :, cs] = (z if o_ref is vm_ref else _sigmoid(z)).astype(BF16)

    gcv = _gate_cols(_wdot(xn, wgc_ref, 0, LANES) + bc_ref[...])
    zr = lax.dot_general(wgr_ref[...], xn, _NT, preferred_element_type=F32) + br_ref[...]
    sub = lax.broadcasted_iota(jnp.int32, zr.shape, 0)
    grv = jnp.where(sub < M_HEADS, zr, _log_sigmoid(zr))
    L = CUM_BLOCK
    ti = lax.broadcasted_iota(jnp.int32, (L, L), 0)
    si = lax.broadcasted_iota(jnp.int32, (L, L), 1)
    lower = jnp.where(ti >= si, 1.0, 0.0).astype(BF16)
    upper = jnp.where(ti <= si, 1.0, 0.0).astype(BF16)
    lane = lax.broadcasted_iota(jnp.int32, (L, LANES), 1)
    sub_l = lax.broadcasted_iota(jnp.int32, (SUBLANES, L), 0)
    for blk in range(tm // L):
        rs = slice(blk * L, (blk + 1) * L)
        cum_c = sum(_dot(lower, p) for p in _bf16_pieces(gcv[rs, :]))
        cum_r = sum(_dot(p, upper) for p in _bf16_pieces(grv[:, rs]))
        if (blk * L) % M_CHUNK:
            cum_c = cum_c + carry_c
            cum_r = cum_r + carry_r
        carry_c, carry_r = cum_c[L - 1:L, :], cum_r[:, L - 1:L]
        gc_ref[0, rs, :] = jnp.where(lane < M_HEADS, gcv[rs, :], cum_c)
        gr_ref[0, :, rs] = jnp.where(sub_l < M_HEADS, grv[:, rs], cum_r)


def _const_spec(shape):
    nd = len(shape)
    return pl.BlockSpec(shape, lambda *_: (0,) * nd, pipeline_mode=pl.Buffered(1))


def _inproj_seq(x, c0, tables, wts):
    B, S, _ = x.shape
    tm = TM_IN
    ns = S // tm
    ra, rm, rp = tables

    def kv_spec(win):
        rows = min(win, tm)
        nblk = win // rows
        return pl.BlockSpec((1, rows, 2 * GROUP_W), lambda b, i: (b, jnp.maximum(i - (ns - nblk), 0), 0))

    tok = lambda w: pl.BlockSpec((1, tm, w), lambda b, i: (b, i, 0))
    tab = pl.BlockSpec((tm, LANES), lambda b, i: (i, 0))
    in_specs = ([tok(D_MODEL)] + [_const_spec(w.shape) for w in wts]
                + [pl.BlockSpec((1, SUBLANES, 2 * M_INNER), lambda b, i: (b, 0, 0)), tab, tab, tab])
    dil_shape = [jax.ShapeDtypeStruct((B, S // dil, dil * GROUP_W), BF16) for _, dil in DSWA_GROUPS for _ in range(3)]
    dil_specs = [pl.BlockSpec((1, tm // dil, dil * GROUP_W), lambda b, i: (b, i, 0))
                 for _, dil in DSWA_GROUPS for _ in range(3)]
    out_shape = (dil_shape
                 + [jax.ShapeDtypeStruct((B, S, M_INNER), BF16)] * 6
                 + [jax.ShapeDtypeStruct((B, S, LANES), F32), jax.ShapeDtypeStruct((B, SUBLANES, S), F32)]
                 + [jax.ShapeDtypeStruct((B, win, 2 * GROUP_W), F32) for win, _ in DSWA_GROUPS]
                 + [jax.ShapeDtypeStruct((B, SUBLANES, 2 * M_INNER), F32)])
    out_specs = (dil_specs + [tok(M_INNER)] * 6
                 + [tok(LANES), pl.BlockSpec((1, SUBLANES, tm), lambda b, i: (b, 0, i))]
                 + [kv_spec(win) for win, _ in DSWA_GROUPS]
                 + [pl.BlockSpec((1, SUBLANES, 2 * M_INNER), lambda b, i: (b, 0, 0))])
    return pl.pallas_call(
        functools.partial(_inproj_seq_kernel, tm=tm, ns=ns),
        grid=(B, ns),
        in_specs=in_specs,
        out_specs=out_specs,
        out_shape=out_shape,
        scratch_shapes=[pltpu.VMEM((2 * M_INNER // LANES, SUBLANES + tm, LANES), F32),
                        pltpu.VMEM((3 * N_GROUPS, GROUP_W // LANES, tm, LANES), F32)],
        compiler_params=pltpu.CompilerParams(
            dimension_semantics=("arbitrary", "arbitrary"), vmem_limit_bytes=VMEM_LIMIT),
        name="inproj_seq",
    )(x, *wts, c0, ra, rm, rp)


def _inproj_step_kernel(x_ref, nmix_ref, waqkv_ref, wmqk_ref, wmv_ref, wmo_ref, wga_ref, wgb_ref,
                        wgc_ref, wgr_ref, bc_ref, br_ref, cw_ref, cb_ref, cs_ref, ra_ref, rm_ref, rp_ref,
                        qa_ref, kv0_ref, kv1_ref, kv2_ref, qm_ref, km_ref, vm_ref, og_ref, ga_ref, gb_ref,
                        gc_ref, cn_ref):
    del wgr_ref, br_ref
    xn = _rms(x_ref[...], nmix_ref[...]).astype(BF16)
    ra, rm, rp = ra_ref[...], rm_ref[...], rp_ref[...]
    kv_refs = (kv0_ref, kv1_ref, kv2_ref)
    for g in range(N_GROUPS):
        cs = slice(g * GROUP_W, (g + 1) * GROUP_W)
        zq = _wdot(xn, waqkv_ref, OFF_AQ + g * GROUP_W, OFF_AQ + (g + 1) * GROUP_W)
        qa_ref[:, cs] = _rope2(zq, ra, rm, rp) * (HEAD_DIM ** -0.5)
        zk = _wdot(xn, waqkv_ref, OFF_AK + g * GROUP_W, OFF_AK + (g + 1) * GROUP_W)
        kv_refs[g][:, :GROUP_W] = _rope2(zk, ra, rm, rp)
        kv_refs[g][:, GROUP_W:] = _wdot(xn, waqkv_ref, OFF_AV + g * GROUP_W, OFF_AV + (g + 1) * GROUP_W)

    for c in range(2 * M_INNER // 256):
        cs = slice(c * 256, (c + 1) * 256)
        pre = _wdot(xn, wmqk_ref, cs.start, cs.stop)
        y = cb_ref[:, cs] + cw_ref[CONV_W - 1:CONV_W, cs] * pre
        for j in range(CONV_W - 1):
            y = y + cw_ref[j:j + 1, cs] * cs_ref[j, :, cs]
        for j in range(CONV_W - 2):
            cn_ref[j, :, cs] = cs_ref[j + 1, :, cs]
        cn_ref[CONV_W - 2, :, cs] = pre
        s = y * _sigmoid(y)
        if c < M_INNER // 256:
            qm_ref[:, cs] = s
        else:
            km_ref[:, c * 256 - M_INNER:(c + 1) * 256 - M_INNER] = s * (M_DK ** -0.5)

    for c in range(M_INNER // 256):
        cs = slice(c * 256, (c + 1) * 256)
        vm_ref[:, cs] = _wdot(xn, wmv_ref, cs.start, cs.stop)
        og_ref[:, cs] = _sigmoid(_wdot(xn, wmo_ref, cs.start, cs.stop))
        ga_ref[:, cs] = _sigmoid(_wdot(xn, wga_ref, cs.start, cs.stop))
        gb_ref[:, cs] = _sigmoid(_wdot(xn, wgb_ref, cs.start, cs.stop))
    gc_ref[...] = _gate_cols(_wdot(xn, wgc_ref, 0, LANES) + bc_ref[...])


def _inproj_step(x, conv_state, tables, wts):
    nb = x.shape[0]
    sds = lambda w: jax.ShapeDtypeStruct((nb, w), F32)
    out_shape = ([sds(A_QKV)] + [sds(2 * GROUP_W)] * N_GROUPS + [sds(M_INNER)] * 6 + [sds(LANES)]
                 + [jax.ShapeDtypeStruct((CONV_W - 1, nb, 2 * M_INNER), F32)])
    return pl.pallas_call(
        _inproj_step_kernel,
        out_shape=out_shape,
        compiler_params=pltpu.CompilerParams(vmem_limit_bytes=VMEM_LIMIT),
        name="inproj_step",
    )(x, *wts, conv_state, *tables)


def _attn_seq_kernel(q_ref, k_ref, v_ref, kp_ref, vp_ref, w_ref, o_ref, lse_ref, wb_ref, *, tq, rpb):
    wb_ref[...] = w_ref[...].astype(BF16)
    i = pl.program_id(2)
    lane_head = lax.broadcasted_iota(jnp.int32, (1, GROUP_W), 1) // HEAD_DIM
    head_mask = [jnp.where(lane_head == h, 1.0, 0.0).astype(BF16) for h in range(HPG)]
    lane_head_full = lax.broadcasted_iota(jnp.int32, (BAND, GROUP_W), 1) // HEAD_DIM
    qi = lax.broadcasted_iota(jnp.int32, (BAND, 2 * BAND), 0)
    kj = lax.broadcasted_iota(jnp.int32, (BAND, 2 * BAND), 1)
    band = (kj >= qi) & (kj <= qi + BAND)
    first_lo = jnp.where(i > 0, 0, BAND)
    bias = jnp.where(band, 0.0, NEG)
    bias_first = jnp.where(band & (kj >= first_lo), 0.0, NEG)

    for rr, j in [(rr, j) for rr in range(rpb) for j in range(tq // BAND)]:
        ls = slice(rr * GROUP_W, (rr + 1) * GROUP_W)
        b1 = bias_first if j == 0 else bias
        q = q_ref[0, j * BAND:(j + 1) * BAND, ls]
        if j == 0:
            kk = jnp.concatenate([kp_ref[0, :, ls], k_ref[0, 0:BAND, ls]], axis=0)
            vv = jnp.concatenate([vp_ref[0, :, ls], v_ref[0, 0:BAND, ls]], axis=0)
        else:
            kk = k_ref[0, (j - 1) * BAND:(j + 1) * BAND, ls]
            vv = v_ref[0, (j - 1) * BAND:(j + 1) * BAND, ls]
        qblk = jnp.concatenate([q * head_mask[h] for h in range(HPG)], axis=0)
        s = lax.dot_general(qblk, kk, _NT, preferred_element_type=F32) + jnp.concatenate([b1] * HPG, axis=0)
        mx = jnp.max(s, axis=-1, keepdims=True)
        e = jnp.exp(s - mx)
        den = jnp.sum(e, axis=-1, keepdims=True)
        pv = _dot(e.astype(BF16), vv) * (1.0 / den)
        lse_rows = mx + jnp.log(den)
        o = pv[0:BAND]
        lse = jnp.broadcast_to(lse_rows[0:BAND], (BAND, GROUP_W))
        for h in range(1, HPG):
            own = lane_head_full == h
            o = jnp.where(own, pv[h * BAND:(h + 1) * BAND], o)
            lse = jnp.where(own, lse_rows[h * BAND:(h + 1) * BAND], lse)
        o_ref[0, j * BAND:(j + 1) * BAND, ls] = o
        lse_ref[0, j * BAND:(j + 1) * BAND, ls] = lse


def _attn_seq(q, k, v, dil, w):
    B, nsub, _ = q.shape
    tq = min(ATT_TQ, nsub)
    rpb = min(dil, ATT_TQ // tq)
    g1, g2 = dil // rpb, nsub // tq
    w_rows = w.shape[0] // (B * g1 * g2)
    assert w_rows * B * g1 * g2 == w.shape[0] and w_rows % 16 == 0
    w_spec = pl.BlockSpec((w_rows, w.shape[1]), lambda b, r, i: ((b * g1 + r) * g2 + i, 0))
    in_spec = pl.BlockSpec((1, tq, rpb * GROUP_W), lambda b, r, i: (b, i, r))
    prev_spec = pl.BlockSpec((1, BAND, rpb * GROUP_W), lambda b, r, i: (b, jnp.maximum(i * (tq // BAND) - 1, 0), r))
    out_spec = pl.BlockSpec((1, tq, rpb * GROUP_W), lambda b, r, i: (b, i, r))
    return pl.pallas_call(
        functools.partial(_attn_seq_kernel, tq=tq, rpb=rpb),
        grid=(B, g1, g2),
        in_specs=[in_spec] * 3 + [prev_spec] * 2 + [w_spec],
        out_specs=[out_spec] * 2 + [w_spec],
        out_shape=[jax.ShapeDtypeStruct((B, nsub, dil * GROUP_W), F32)] * 2 + [jax.ShapeDtypeStruct(w.shape, BF16)],
        compiler_params=pltpu.CompilerParams(
            dimension_semantics=("arbitrary", "arbitrary", "arbitrary"), vmem_limit_bytes=VMEM_LIMIT),
        name=f"attn_seq_d{dil}",
    )(q, k, v, k, v, w)


N_SIDE_CASTS = 5


def _mlstm_seq_kernel(q_ref, k_ref, v_ref, og_ref, gc_ref, gr_ref, mhn_ref, *rest, L, G):
    w_refs = rest[:N_SIDE_CASTS]
    ob_ref, c_out, n_out, m_out = rest[N_SIDE_CASTS:N_SIDE_CASTS + 4]
    wb_refs = rest[N_SIDE_CASTS + 4:2 * N_SIDE_CASTS + 4]
    c_s, n_s, m_s = rest[2 * N_SIDE_CASTS + 4:]
    for w_ref, wb_ref in zip(w_refs, wb_refs):
        wb_ref[...] = w_ref[...].astype(BF16)
    c = pl.program_id(1)

    @pl.when(c == 0)
    def _():
        c_s[...] = jnp.zeros(c_s.shape, F32)
        n_s[...] = jnp.zeros(n_s.shape, F32)
        m_s[...] = jnp.zeros(m_s.shape, F32)

    ti = lax.broadcasted_iota(jnp.int32, (L, L), 0)
    si = lax.broadcasted_iota(jnp.int32, (L, L), 1)
    tril = ti >= si
    lane = lax.broadcasted_iota(jnp.int32, (1, LANES), 1)
    m_vec = [jnp.zeros((1, LANES), F32)] * G

    for g, h in [(g, h) for g in range(G) for h in range(M_HEADS)]:
        gc = gc_ref[g]
        gr = gr_ref[g]
        hs = slice(h * M_DK, (h + 1) * M_DK)
        b_col = gc[:, M_HEADS + h:M_HEADS + h + 1]
        i_col = gc[:, h:h + 1]
        b_row = gr[M_HEADS + h:M_HEADS + h + 1, :]
        i_row = gr[h:h + 1, :]
        m_prev = m_s[g, h, 0:1, 0:1]
        qh = q_ref[g, :, hs]
        kh = k_ref[g, :, hs]
        vh = v_ref[g, :, hs]
        c_prev = c_s[g, h]
        n_prev = n_s[g, h]

        dmat = jnp.where(tril, b_col - b_row + i_row, -jnp.inf)
        m_inter = b_col + m_prev
        m_t = jnp.maximum(m_inter, jnp.max(dmat, axis=-1, keepdims=True))
        w_intra = lax.dot_general(qh, kh, _NT, preferred_element_type=F32) * jnp.exp(dmat - m_t)
        w_inter = jnp.exp(m_inter - m_t)
        num = _dot(w_intra.astype(BF16), vh) + w_inter * _dot(qh, c_prev.astype(BF16))
        den = (jnp.sum(w_intra, axis=-1, keepdims=True)
               + w_inter * jnp.sum(qh.astype(F32) * n_prev, axis=-1, keepdims=True))
        hh = num / jnp.maximum(jnp.abs(den), jnp.exp(-m_t))
        hn = hh * lax.rsqrt(jnp.mean(hh * hh, axis=-1, keepdims=True) + EPS) * mhn_ref[h:h + 1, :]
        ob_ref[g, :, hs] = (hn * og_ref[g, :, hs].astype(F32)).astype(BF16)

        b_last = b_row[:, L - 1:L]
        m_new = jnp.maximum(b_last + m_prev, jnp.max(b_last - b_row + i_row, axis=-1, keepdims=True))
        wk = jnp.exp(b_last - b_col + i_col - m_new)
        decay = jnp.exp(b_last + m_prev - m_new)
        ks = kh.astype(F32) * wk
        c_new = decay * c_prev + lax.dot_general(ks.astype(BF16), vh, _TN, preferred_element_type=F32)
        n_new = decay * n_prev + jnp.sum(ks, axis=0, keepdims=True)
        c_s[g, h] = c_new
        n_s[g, h] = n_new
        m_s[g, h] = jnp.broadcast_to(m_new, (SUBLANES, LANES))
        c_out[g, h] = c_new
        n_out[g, h:h + 1, :] = n_new
        m_vec[g] = jnp.where(lane == h, m_new, m_vec[g])
    for g in range(G):
        m_out[g] = m_vec[g]


def _mlstm_seq(qm, km, vm, og, gc, gr, mhn, side_w):
    B, S, _ = qm.shape
    L = M_CHUNK
    G = MLSTM_G
    nc = S // L
    steps = (B // G) * nc
    assert len(side_w) == N_SIDE_CASTS and all(w.shape[0] % (16 * steps) == 0 for w in side_w)
    w_specs = [pl.BlockSpec((w.shape[0] // steps, w.shape[1]), lambda b, c: (b * nc + c, 0)) for w in side_w]
    tok = pl.BlockSpec((G, L, M_INNER), lambda b, c: (b, c, 0))
    return pl.pallas_call(
        functools.partial(_mlstm_seq_kernel, L=L, G=G),
        grid=(B // G, nc),
        in_specs=[tok, tok, tok, tok,
                  pl.BlockSpec((G, L, LANES), lambda b, c: (b, c, 0)),
                  pl.BlockSpec((G, SUBLANES, L), lambda b, c: (b, 0, c)),
                  pl.BlockSpec((M_HEADS, M_DK), lambda b, c: (0, 0))] + w_specs,
        out_specs=[tok,
                   pl.BlockSpec((G, M_HEADS, M_DK, M_DK), lambda b, c: (b, 0, 0, 0)),
                   pl.BlockSpec((G, M_HEADS, M_DK), lambda b, c: (b, 0, 0)),
                   pl.BlockSpec((G, 1, LANES), lambda b, c: (b, 0, 0))] + w_specs,
        out_shape=[jax.ShapeDtypeStruct((B, S, M_INNER), BF16),
                   jax.ShapeDtypeStruct((B, M_HEADS, M_DK, M_DK), F32),
                   jax.ShapeDtypeStruct((B, M_HEADS, M_DK), F32),
                   jax.ShapeDtypeStruct((B, 1, LANES), F32)]
                  + [jax.ShapeDtypeStruct(w.shape, BF16) for w in side_w],
        scratch_shapes=[pltpu.VMEM((G, M_HEADS, M_DK, M_DK), F32),
                        pltpu.VMEM((G, M_HEADS, 1, M_DK), F32),
                        pltpu.VMEM((G, M_HEADS, SUBLANES, LANES), F32)],
        compiler_params=pltpu.CompilerParams(
            dimension_semantics=("arbitrary", "arbitrary"), vmem_limit_bytes=VMEM_LIMIT),
        name="mlstm_seq",
    )(qm, km, vm, og, gc, gr, mhn, *side_w)


def _post_kernel(x_ref, o0_ref, o1_ref, o2_ref, l0_ref, l1_ref, l2_ref, ob_ref, ga_ref, gb_ref, p_ref,
                 wpa_ref, wpb_ref, wout_ref, nffn_ref, wg_ref, wu_ref, wd_ref, nple_ref, wpg_ref, wpp_ref,
                 nfin_ref, y_ref, *ubuf, dils):
    lead = (0,) * (len(x_ref.shape) - 2)
    rd = lambda r: r[lead] if lead else r[...]
    tm = x_ref.shape[-2]

    def undilate(ref, dil, slot):
        if dil == 1:
            return rd(ref)
        halves = range(GROUP_W // LANES)
        for r in range(dil):
            for c in halves:
                lo = r * GROUP_W + c * LANES
                ubuf[0][slot, c, pl.ds(r, tm // dil, stride=dil), :] = ref[0, :, lo:lo + LANES]
        return jnp.concatenate([ubuf[0][slot, c] for c in halves], axis=1)

    l0, l1, l2 = (undilate(r, d, 2 * g) for g, (r, d) in enumerate(zip((l0_ref, l1_ref, l2_ref), dils)))
    o0, o1, o2 = (undilate(r, d, 2 * g + 1) for g, (r, d) in enumerate(zip((o0_ref, o1_ref, o2_ref), dils)))
    lmax = jnp.maximum(jnp.maximum(l0, l1), l2)
    e0, e1, e2 = jnp.exp(l0 - lmax), jnp.exp(l1 - lmax), jnp.exp(l2 - lmax)
    o_a = (e0 * o0 + e1 * o1 + e2 * o2) / (e0 + e1 + e2)

    mix = (rd(ga_ref).astype(F32) * _dot(o_a.astype(BF16), wpa_ref[...])
           + rd(gb_ref).astype(F32) * _dot(rd(ob_ref).astype(BF16), wpb_ref[...]))
    h = rd(x_ref) + _dot(mix.astype(BF16), wout_ref[...])

    xf = _rms(h, nffn_ref[...]).astype(BF16)
    half = D_FF // 2
    for c in range(2):
        cs = slice(c * half, (c + 1) * half)
        gate = _dot(xf, wg_ref[:, cs])
        act = (gate * _sigmoid(gate)) * _dot(xf, wu_ref[:, cs])
        h = h + _dot(act.astype(BF16), wd_ref[cs, :])

    xp = _rms(h, nple_ref[...]).astype(BF16)
    h = h + _sigmoid(_dot(xp, wpg_ref[...])) * _dot(rd(p_ref).astype(BF16), wpp_ref[...])
    y = _rms(h, nfin_ref[...])
    if lead:
        y_ref[lead] = y
    else:
        y_ref[...] = y


def _post_step(x, os_, ls_, ob, ga, gb, p, wts):
    return pl.pallas_call(
        functools.partial(_post_kernel, dils=(1,) * N_GROUPS),
        out_shape=jax.ShapeDtypeStruct(x.shape, F32),
        compiler_params=pltpu.CompilerParams(vmem_limit_bytes=VMEM_LIMIT),
        name="post_step",
    )(x, *os_, *ls_, ob, ga, gb, p, *wts)


SM_ROWS = 16
SM_LANES = 512
T_QA, T_KN, T_VN, T_QM, T_KM, T_VM, T_OG, T_N, T_GATE = range(9)
SO_ROWS = 8
SO_OB, SO_N, SO_M, SO_LSE, SO_O = range(5)
HALF_HEADS = HPG // 2
HALF_W = HALF_HEADS * HEAD_DIM


def _eye(n):
    return lax.broadcasted_iota(jnp.int32, (n, n), 0) == lax.broadcasted_iota(jnp.int32, (n, n), 1)


def _col_of(row, eye):
    return jnp.sum(jnp.where(eye, jnp.broadcast_to(row, eye.shape), 0.0), axis=1, keepdims=True)


def _row_of(col, eye):
    return jnp.sum(jnp.where(eye, jnp.broadcast_to(col, eye.shape), 0.0), axis=0, keepdims=True)


def _round_bf16(x):
    return x.astype(BF16).astype(F32)


def _sample_step(tile, kv_refs, cst_ref, mhn_ref, so_ref, kvn_refs, cso_ref):
    so_ref[0] = jnp.zeros((SO_ROWS, SM_LANES), F32)
    eye_a = _eye(HALF_W)
    eye_m = _eye(M_DK)

    for g, (w, dil) in enumerate(DSWA_GROUPS):
        ls = slice(g * HALF_W, (g + 1) * HALF_W)
        k_t = kv_refs[g][0, 0, 0]
        v_t = kv_refs[g][0, 1, 0]
        k_row = tile[T_KN:T_KN + 1, ls]
        v_row = tile[T_VN:T_VN + 1, ls]
        qb = _round_bf16(tile[T_QA:T_QA + 1, ls])
        prod = k_t * _col_of(qb, eye_a)
        qk_new = qb * _round_bf16(k_row)
        pos = lax.broadcasted_iota(jnp.int32, (1, w), 1)
        pe, pn, lse = [], [], []
        for hl in range(HALF_HEADS):
            hd = slice(hl * HEAD_DIM, (hl + 1) * HEAD_DIM)
            s = jnp.where(pos % dil == 0, jnp.sum(prod[hd, :], axis=0, keepdims=True), NEG)
            s_new = jnp.sum(qk_new[:, hd], axis=-1, keepdims=True)
            mx = jnp.maximum(jnp.max(s, axis=-1, keepdims=True), s_new)
            e = jnp.exp(s - mx)
            e_new = jnp.exp(s_new - mx)
            den = jnp.sum(e, axis=-1, keepdims=True) + e_new
            pe.append(jnp.broadcast_to(e / den, (HEAD_DIM, w)))
            pn.append(jnp.broadcast_to(e_new / den, (1, HEAD_DIM)))
            lse.append(jnp.broadcast_to(mx + jnp.log(den), (1, HEAD_DIM)))
        o_col = jnp.sum(v_t * jnp.concatenate(pe, axis=0), axis=-1, keepdims=True)
        so_ref[0, SO_O:SO_O + 1, ls] = _row_of(o_col, eye_a) + jnp.concatenate(pn, axis=1) * _round_bf16(v_row)
        so_ref[0, SO_LSE:SO_LSE + 1, ls] = jnp.concatenate(lse, axis=1)
        last = lax.broadcasted_iota(jnp.int32, (HALF_W, w), 1) == w - 1
        kvn_refs[g][0, 0, 0] = jnp.where(last, _col_of(k_row, eye_a), pltpu.roll(k_t, w - 1, 1))
        kvn_refs[g][0, 1, 0] = jnp.where(last, _col_of(v_row, eye_a), pltpu.roll(v_t, w - 1, 1))

    eye = eye_m
    for j in range(M_HEADS // 2):
        hs = slice(j * M_DK, (j + 1) * M_DK)
        ig = tile[T_GATE:T_GATE + 1, j:j + 1]
        lf = tile[T_GATE:T_GATE + 1, 2 + j:3 + j]
        m_prev = tile[T_GATE:T_GATE + 1, 4 + j:5 + j]
        qb = _round_bf16(tile[T_QM:T_QM + 1, hs])
        kb = _round_bf16(tile[T_KM:T_KM + 1, hs])
        vb = _round_bf16(tile[T_VM:T_VM + 1, hs])
        n_prev = tile[T_N:T_N + 1, hs]
        c_prev = cst_ref[0, j]

        m_inter = lf + m_prev
        m_t = jnp.maximum(m_inter, ig)
        w_intra = jnp.sum(qb * kb, axis=-1, keepdims=True) * jnp.exp(ig - m_t)
        w_inter = jnp.exp(m_inter - m_t)
        q_c = jnp.sum(c_prev * _col_of(qb, eye), axis=0, keepdims=True)
        num = w_intra * vb + w_inter * q_c
        den = w_intra + w_inter * jnp.sum(qb * n_prev, axis=-1, keepdims=True)
        hh = num / jnp.maximum(jnp.abs(den), jnp.exp(-m_t))
        hn = hh * lax.rsqrt(jnp.mean(hh * hh, axis=-1, keepdims=True) + EPS) * mhn_ref[0, j:j + 1, :]
        ks = kb * jnp.exp(ig - m_t)
        so_ref[0, SO_OB:SO_OB + 1, hs] = hn * tile[T_OG:T_OG + 1, hs]
        so_ref[0, SO_N:SO_N + 1, hs] = w_inter * n_prev + ks
        so_ref[0, SO_M:SO_M + 1, hs] = jnp.broadcast_to(m_t, (1, M_DK))
        cso_ref[0, j] = w_inter * c_prev + _col_of(_round_bf16(ks), eye) * vb


N_POST_IN = 22


def _post_fused_kernel(*refs, dils, ns):
    post_in = refs[:N_POST_IN]
    sm_ref, c0_ref, c1_ref, c2_ref, cst_ref, mhn_ref = refs[N_POST_IN:N_POST_IN + 6]
    y_ref, so_ref, cn0_ref, cn1_ref, cn2_ref, cso_ref = refs[N_POST_IN + 6:N_POST_IN + 12]
    (ubuf,) = refs[N_POST_IN + 12:]
    _post_kernel(*post_in, y_ref, ubuf, dils=dils)
    _sample_step(sm_ref[0], (c0_ref, c1_ref, c2_ref), cst_ref, mhn_ref, so_ref,
                 (cn0_ref, cn1_ref, cn2_ref), cso_ref)


def _post_fused(x, os_, ls_, ob, ga, gb, p, wts, sm_in, caches_t, c_state, mhn):
    B, S, _ = x.shape
    NB = c_state.shape[0]
    tm = TM_POST
    ns = S // tm
    assert B * ns == 2 * NB, "one half of a sample batch element per post-mixer grid step"
    tok = lambda w: pl.BlockSpec((1, tm, w), lambda b, i: (b, i, 0))
    dils = tuple(dil for _, dil in DSWA_GROUPS)
    dil_specs = [pl.BlockSpec((1, tm // dil, dil * GROUP_W), lambda b, i: (b, i, 0)) for dil in dils]
    step = lambda b, i: b * ns + i
    cache_specs = [pl.BlockSpec((1, 2, 1, HALF_W, c.shape[-1]), lambda b, i: (step(b, i) // 2, 0, step(b, i) % 2, 0, 0))
                   for c in caches_t]
    cst_spec = pl.BlockSpec((1, M_HEADS // 2, M_DK, M_DK), lambda b, i: (step(b, i) // 2, step(b, i) % 2, 0, 0))
    in_specs = ([tok(D_MODEL)] + dil_specs * 2 + [tok(M_INNER)] * 3 + [tok(PLE_DIM)]
                + [_const_spec(w.shape) for w in wts]
                + [pl.BlockSpec((1, SM_ROWS, SM_LANES), lambda b, i: (step(b, i), 0, 0))]
                + cache_specs
                + [cst_spec, pl.BlockSpec((1, M_HEADS // 2, M_DK), lambda b, i: (step(b, i) % 2, 0, 0))])
    out_specs = ([tok(D_MODEL), pl.BlockSpec((1, SO_ROWS, SM_LANES), lambda b, i: (step(b, i), 0, 0))]
                 + cache_specs + [cst_spec])
    out_shape = ([jax.ShapeDtypeStruct((B, S, D_MODEL), F32),
                  jax.ShapeDtypeStruct((2 * NB, SO_ROWS, SM_LANES), F32)]
                 + [jax.ShapeDtypeStruct(c.shape, F32) for c in caches_t]
                 + [jax.ShapeDtypeStruct(c_state.shape, F32)])
    return pl.pallas_call(
        functools.partial(_post_fused_kernel, dils=dils, ns=ns),
        grid=(B, ns),
        in_specs=in_specs,
        out_specs=out_specs,
        out_shape=out_shape,
        scratch_shapes=[pltpu.VMEM((2 * N_GROUPS, GROUP_W // LANES, tm, LANES), F32)],
        compiler_params=pltpu.CompilerParams(
            dimension_semantics=("arbitrary", "arbitrary"), vmem_limit_bytes=VMEM_LIMIT),
        name="post_seq_sample_stream",
    )(x, *os_, *ls_, ob, ga, gb, p, *wts, sm_in, *caches_t, c_state, mhn)


def _rope_tables(pos):
    pos = np.asarray(pos, np.float64)
    t = pos.shape[0]
    inv = np.power(ROPE_THETA, -np.arange(ROPE_HALF, dtype=np.float64) / ROPE_HALF)
    ang = pos[:, None] * inv[None, :]
    cos, sin = np.cos(ang), np.sin(ang)
    one = np.ones((t, HEAD_DIM - ROPE_DIM))
    zero = np.zeros((t, HEAD_DIM - ROPE_DIM))
    z8 = np.zeros((t, ROPE_HALF))
    a = np.concatenate([cos, cos, one], axis=1)
    m = np.concatenate([-sin, z8, zero], axis=1)
    p = np.concatenate([z8, sin, zero], axis=1)
    rep = LANES // HEAD_DIM
    return tuple(jnp.asarray(np.tile(v, (1, rep)).astype(np.float32)) for v in (a, m, p))


def kernel(x_prompt, x_sample, cache_kv_w128, cache_kv_w512, cache_kv_w2048, state_conv, state_C, state_n,
           state_m, p_prompt, p_sample, norm_mix, w_in, conv_w, conv_b, b_igate, b_fgate, mh_norm, w_proj_a,
           w_proj_b, w_out, norm_ffn, w_gate, w_up, w_down, norm_ple, w_ple_gate, w_ple_proj, norm_final):
    depth = w_in.shape[0]
    assert depth == 1, "single trunk layer"
    B, S, _ = x_prompt.shape
    NB = x_sample.shape[0]
    assert x_sample.shape[1] == 1
    caches = (cache_kv_w128, cache_kv_w512, cache_kv_w2048)
    assert S >= DSWA_GROUPS[-1][0] and S % (BAND * DSWA_GROUPS[-1][1]) == 0
    past = 0
    l = 0

    wt = w_in[l].T.astype(BF16)
    gate_wt = wt[OFF_MI:OFF_GA]
    gate_b = jnp.concatenate([b_igate[l], b_fgate[l]])
    in_wts = (
        norm_mix[l][None, :],
        wt[OFF_AQ:OFF_MQ],
        wt[OFF_MQ:OFF_MV],
        wt[OFF_MV:OFF_MO],
        wt[OFF_MO:OFF_MI],
        wt[OFF_GA:OFF_GB],
        wt[OFF_GB:IN_COLS],
        jnp.pad(gate_wt, ((0, LANES - 2 * M_HEADS), (0, 0))),
        gate_wt,
        jnp.pad(gate_b, (0, LANES - 2 * M_HEADS))[None, :],
        gate_b[:, None],
        conv_w[l],
        conv_b[l][None, :],
    )
    mhn = mh_norm[l]

    c0 = jnp.zeros((B, SUBLANES, 2 * M_INNER), F32)
    tabs_p = _rope_tables(past + np.arange(S))
    (*qkv, qm, km, vm, og, ga, gb, gc, gr, kv0, kv1, kv2, cn) = _inproj_seq(x_prompt, c0, tabs_p, in_wts)
    os_p, ls_p, ffn_bf16 = [], [], []
    for g, ((_, dil), w_ffn) in enumerate(zip(DSWA_GROUPS, (w_gate[l], w_up[l], w_down[l]))):
        o, lse, wb = _attn_seq(*qkv[3 * g:3 * g + 3], dil, w_ffn)
        os_p.append(o)
        ls_p.append(lse)
        ffn_bf16.append(wb)
    ob, c_p, n_p, m_p, wpa, wpb, wo, wpg, wpp = _mlstm_seq(
        qm, km, vm, og, gc, gr, mhn, (w_proj_a[l], w_proj_b[l], w_out[l], w_ple_gate[l], w_ple_proj[l]))
    post_wts = (wpa, wpb, wo, norm_ffn[l][None, :], *ffn_bf16, norm_ple[l][None, :], wpg, wpp, norm_final[None, :])

    tabs_s = _rope_tables(np.full((NB,), PAST_LEN))
    conv_s = jnp.swapaxes(state_conv[l], 0, 1)
    (qa_s, kvn0, kvn1, kvn2, qm_s, km_s, vm_s, og_s, ga_s, gb_s, gc_s, cn_s) = _inproj_step(
        x_sample[:, 0, :], conv_s, tabs_s, in_wts)
    pad_row = lambda t: jnp.pad(t, ((0, 0), (0, SM_LANES - t.shape[1])))
    att_half = lambda t: pad_row(t.reshape(NB, N_GROUPS, 2, HALF_W).transpose(0, 2, 1, 3).reshape(2 * NB, -1))
    mem_half = lambda t: t.reshape(2 * NB, SM_LANES)
    k_new = jnp.concatenate([kv[:, :GROUP_W] for kv in (kvn0, kvn1, kvn2)], axis=1)
    v_new = jnp.concatenate([kv[:, GROUP_W:] for kv in (kvn0, kvn1, kvn2)], axis=1)
    gates = jnp.concatenate([t.reshape(2 * NB, M_HEADS // 2) for t in
                             (gc_s[:, :M_HEADS], gc_s[:, M_HEADS:2 * M_HEADS], state_m[l])], axis=1)
    sm_rows = [att_half(qa_s), att_half(k_new), att_half(v_new), mem_half(qm_s), mem_half(km_s), mem_half(vm_s),
               mem_half(og_s), mem_half(state_n[l].reshape(NB, M_INNER)), pad_row(gates)]
    sm_rows += [jnp.zeros((2 * NB, SM_LANES), F32)] * (SM_ROWS - len(sm_rows))
    sm_in = jnp.stack(sm_rows, axis=1)
    caches_t = [jnp.transpose(c[l], (0, 2, 3, 4, 1)).reshape(NB, 2, 2, HALF_W, c.shape[2]) for c in caches]
    for c, (w, dil) in zip(caches_t, DSWA_GROUPS):
        assert c.shape[-1] == w == BAND * dil, "each cache must hold exactly one full window"
    y_prompt, so, cn0, cn1, cn2, c_s = _post_fused(x_prompt, os_p, ls_p, ob, ga, gb, p_prompt[l], post_wts,
                                                   sm_in, caches_t, state_C[l], mhn.reshape(2, M_HEADS // 2, M_DK))
    kv_s = [jnp.transpose(c.reshape(NB, 2, HPG, HEAD_DIM, -1), (0, 4, 1, 2, 3)) for c in (cn0, cn1, cn2)]
    att_full = lambda t: t[:, :N_GROUPS * HALF_W].reshape(NB, 2, N_GROUPS, HALF_W).transpose(0, 2, 1, 3).reshape(
        NB, N_GROUPS, GROUP_W)
    o_s, l_s = att_full(so[:, SO_O, :]), att_full(so[:, SO_LSE, :])
    os_s = [o_s[:, g, :] for g in range(N_GROUPS)]
    ls_s = [l_s[:, g, :] for g in range(N_GROUPS)]
    ob_s = so[:, SO_OB, :].reshape(NB, M_INNER)
    n_s = so[:, SO_N, :].reshape(NB, M_HEADS, M_DK)
    m_s = so[:, SO_M, ::M_DK].reshape(NB, M_HEADS)
    y_sample = _post_step(x_sample[:, 0, :], os_s, ls_s, ob_s, ga_s, gb_s, p_sample[l][:, 0, :], post_wts)

    kv_shape = lambda n, w: (1, n, w, 2, HPG, HEAD_DIM)
    return (
        y_prompt, y_sample[:, None, :],
        kv0.reshape(kv_shape(B, DSWA_GROUPS[0][0])), kv_s[0].reshape(kv_shape(NB, DSWA_GROUPS[0][0])),
        kv1.reshape(kv_shape(B, DSWA_GROUPS[1][0])), kv_s[1].reshape(kv_shape(NB, DSWA_GROUPS[1][0])),
        kv2.reshape(kv_shape(B, DSWA_GROUPS[2][0])), kv_s[2].reshape(kv_shape(NB, DSWA_GROUPS[2][0])),
        cn[:, SUBLANES - (CONV_W - 1):, :][None], jnp.swapaxes(cn_s, 0, 1)[None],
        c_p[None], c_s[None],
        n_p[None], n_s[None],
        m_p[:, 0, :M_HEADS][None], m_s[None],
    )
```

```python
import functools

import jax
import jax.numpy as jnp
import numpy as np
from jax import lax
from jax.experimental import pallas as pl
from jax.experimental.pallas import tpu as pltpu

F32 = jnp.float32
BF16 = jnp.bfloat16

D_MODEL = 1024
PAST_LEN = 16384
HEAD_DIM = 64
ROPE_DIM = HEAD_DIM // 4
ROPE_HALF = ROPE_DIM // 2
ROPE_THETA = 500000.0
DSWA_GROUPS = ((128, 1), (512, 4), (2048, 16))
HPG = 4
N_GROUPS = len(DSWA_GROUPS)
GROUP_W = HPG * HEAD_DIM
A_QKV = N_GROUPS * GROUP_W
BAND = 128
M_HEADS = 4
M_INNER = D_MODEL
M_DK = M_INNER // M_HEADS
CONV_W = 4
D_FF = 2816
PLE_DIM = 256
EPS = 1e-6
NEG = -1e30

OFF_AQ = 0
OFF_AK = OFF_AQ + A_QKV
OFF_AV = OFF_AK + A_QKV
OFF_MQ = OFF_AV + A_QKV
OFF_MK = OFF_MQ + M_INNER
OFF_MV = OFF_MK + M_INNER
OFF_MO = OFF_MV + M_INNER
OFF_MI = OFF_MO + M_INNER
OFF_MF = OFF_MI + M_HEADS
OFF_GA = OFF_MF + M_HEADS
OFF_GB = OFF_GA + D_MODEL
IN_COLS = OFF_GB + D_MODEL

LANES = 128
SUBLANES = 8
VMEM_LIMIT = 56 * 1024 * 1024

TM_IN = 512
TM_POST = 256
M_CHUNK = 512
CUM_BLOCK = 256
MLSTM_G = 2
ATT_TQ = 2048

_NT = (((1,), (1,)), ((), ()))
_TN = (((0,), (0,)), ((), ()))


def _dot(a, b):
    return jnp.dot(a, b, preferred_element_type=F32)


def _wdot(a, wt_ref, lo, hi):
    return lax.dot_general(a, wt_ref[lo:hi, :], _NT, preferred_element_type=F32)


def _rms(x, g):
    return x * lax.rsqrt(jnp.mean(x * x, axis=-1, keepdims=True) + EPS) * g


def _bf16_pieces(x):
    hi = x.astype(BF16)
    r1 = x - hi.astype(F32)
    mid = r1.astype(BF16)
    lo = (r1 - mid.astype(F32)).astype(BF16)
    return hi, mid, lo


def _sigmoid(x):
    return 0.5 * jnp.tanh(0.5 * x) + 0.5


def _log_sigmoid(x):
    return jnp.minimum(x, 0.0) - jnp.log1p(jnp.exp(-jnp.abs(x)))


def _rope(t, ra, rm, rp):
    return t * ra + pltpu.roll(t, LANES - ROPE_HALF, 1) * rm + pltpu.roll(t, ROPE_HALF, 1) * rp


def _rope2(t, ra, rm, rp):
    return jnp.concatenate([_rope(t[:, :LANES], ra, rm, rp), _rope(t[:, LANES:], ra, rm, rp)], axis=1)


def _gate_cols(z):
    lane = lax.broadcasted_iota(jnp.int32, z.shape, 1)
    return jnp.where(lane < M_HEADS, z, _log_sigmoid(z))


def _inproj_seq_kernel(x_ref, nmix_ref, waqkv_ref, wmqk_ref, wmv_ref, wmo_ref, wga_ref, wgb_ref,
                       wgc_ref, wgr_ref, bc_ref, br_ref, cw_ref, cb_ref, c0_ref, ra_ref, rm_ref, rp_ref,
                       q0_ref, k0_ref, v0_ref, q1_ref, k1_ref, v1_ref, q2_ref, k2_ref, v2_ref,
                       qm_ref, km_ref, vm_ref, og_ref, ga_ref, gb_ref,
                       gc_ref, gr_ref, kv0_ref, kv1_ref, kv2_ref, cn_ref, cbuf, dbuf, *, tm, ns):
    i = pl.program_id(1)
    xn = _rms(x_ref[0], nmix_ref[...]).astype(BF16)
    ra, rm, rp = ra_ref[...], rm_ref[...], rp_ref[...]
    kv_refs = (kv0_ref, kv1_ref, kv2_ref)
    qkv_refs = ((q0_ref, k0_ref, v0_ref), (q1_ref, k1_ref, v1_ref), (q2_ref, k2_ref, v2_ref))

    def put_dilated(ref, val, dil, slot):
        if dil == 1:
            ref[0] = val.astype(BF16)
            return
        for c in range(GROUP_W // LANES):
            dbuf[slot, c] = val[:, c * LANES:(c + 1) * LANES]
        for r in range(dil):
            for c in range(GROUP_W // LANES):
                lo = r * GROUP_W + c * LANES
                ref[0, :, lo:lo + LANES] = dbuf[slot, c, pl.ds(r, tm // dil, stride=dil), :].astype(BF16)

    for g, (win, dil) in enumerate(DSWA_GROUPS):
        zq = _wdot(xn, waqkv_ref, OFF_AQ + g * GROUP_W, OFF_AQ + (g + 1) * GROUP_W)
        put_dilated(qkv_refs[g][0], _rope2(zq, ra, rm, rp) * (HEAD_DIM ** -0.5), dil, 3 * g)
        zk = _wdot(xn, waqkv_ref, OFF_AK + g * GROUP_W, OFF_AK + (g + 1) * GROUP_W)
        kr = _rope2(zk, ra, rm, rp)
        put_dilated(qkv_refs[g][1], kr, dil, 3 * g + 1)
        zv = _wdot(xn, waqkv_ref, OFF_AV + g * GROUP_W, OFF_AV + (g + 1) * GROUP_W)
        put_dilated(qkv_refs[g][2], zv, dil, 3 * g + 2)
        rows = min(win, tm)
        nblk = win // rows

        @pl.when(i >= ns - nblk)
        def _():
            kv_refs[g][0, :, :GROUP_W] = kr[tm - rows:, :]
            kv_refs[g][0, :, GROUP_W:] = zv[tm - rows:, :]

    @pl.when(i == 0)
    def _():
        for blk in range(2 * M_INNER // LANES):
            cbuf[blk, 0:SUBLANES, :] = c0_ref[0, :, blk * LANES:(blk + 1) * LANES]

    for c in range(2 * M_INNER // 256):
        pre = _wdot(xn, wmqk_ref, c * 256, (c + 1) * 256)
        halves = []
        for hf in range(256 // LANES):
            blk = c * (256 // LANES) + hf
            ls = slice(blk * LANES, (blk + 1) * LANES)
            pre_h = pre[:, hf * LANES:(hf + 1) * LANES]
            cbuf[blk, SUBLANES:SUBLANES + tm, :] = pre_h
            y = cb_ref[:, ls] + cw_ref[CONV_W - 1:CONV_W, ls] * pre_h
            for d in range(1, CONV_W):
                y = y + cw_ref[CONV_W - 1 - d:CONV_W - d, ls] * cbuf[blk, pl.ds(SUBLANES - d, tm), :]
            tail = pre_h[tm - SUBLANES:, :]
            cbuf[blk, 0:SUBLANES, :] = tail
            cn_ref[0, :, ls] = tail
            halves.append(y * _sigmoid(y))
        s = jnp.concatenate(halves, axis=1)
        if c < M_INNER // 256:
            qm_ref[0, :, c * 256:(c + 1) * 256] = s.astype(BF16)
        else:
            km_ref[0, :, c * 256 - M_INNER:(c + 1) * 256 - M_INNER] = (s * (M_DK ** -0.5)).astype(BF16)
        cs = slice((c % 2) * 512, (c % 2 + 1) * 512)
        w_ref, o_ref = ((wmv_ref, vm_ref), (wmo_ref, og_ref), (wga_ref, ga_ref), (wgb_ref, gb_ref))[c // 2]
        z = _wdot(xn, w_ref, cs.start, cs.stop)
        o_ref[0, :, cs] = (z if o_ref is vm_ref else _sigmoid(z)).astype(BF16)

    gcv = _gate_cols(_wdot(xn, wgc_ref, 0, LANES) + bc_ref[...])
    zr = lax.dot_general(wgr_ref[...], xn, _NT, preferred_element_type=F32) + br_ref[...]
    sub = lax.broadcasted_iota(jnp.int32, zr.shape, 0)
    grv = jnp.where(sub < M_HEADS, zr, _log_sigmoid(zr))
    L = CUM_BLOCK
    ti = lax.broadcasted_iota(jnp.int32, (L, L), 0)
    si = lax.broadcasted_iota(jnp.int32, (L, L), 1)
    lower = jnp.where(ti >= si, 1.0, 0.0).astype(BF16)
    upper = jnp.where(ti <= si, 1.0, 0.0).astype(BF16)
    lane = lax.broadcasted_iota(jnp.int32, (L, LANES), 1)
    sub_l = lax.broadcasted_iota(jnp.int32, (SUBLANES, L), 0)
    for blk in range(tm // L):
        rs = slice(blk * L, (blk + 1) * L)
        cum_c = sum(_dot(lower, p) for p in _bf16_pieces(gcv[rs, :]))
        cum_r = sum(_dot(p, upper) for p in _bf16_pieces(grv[:, rs]))
        if (blk * L) % M_CHUNK:
            cum_c = cum_c + carry_c
            cum_r = cum_r + carry_r
        carry_c, carry_r = cum_c[L - 1:L, :], cum_r[:, L - 1:L]
        gc_ref[0, rs, :] = jnp.where(lane < M_HEADS, gcv[rs, :], cum_c)
        gr_ref[0, :, rs] = jnp.where(sub_l < M_HEADS, grv[:, rs], cum_r)


def _const_spec(shape):
    nd = len(shape)
    return pl.BlockSpec(shape, lambda *_: (0,) * nd, pipeline_mode=pl.Buffered(1))


def _inproj_seq(x, c0, tables, wts):
    B, S, _ = x.shape
    tm = TM_IN
    ns = S // tm
    ra, rm, rp = tables

    def kv_spec(win):
        rows = min(win, tm)
        nblk = win // rows
        return pl.BlockSpec((1, rows, 2 * GROUP_W), lambda b, i: (b, jnp.maximum(i - (ns - nblk), 0), 0))

    tok = lambda w: pl.BlockSpec((1, tm, w), lambda b, i: (b, i, 0))
    tab = pl.BlockSpec((tm, LANES), lambda b, i: (i, 0))
    in_specs = ([tok(D_MODEL)] + [_const_spec(w.shape) for w in wts]
                + [pl.BlockSpec((1, SUBLANES, 2 * M_INNER), lambda b, i: (b, 0, 0)), tab, tab, tab])
    dil_shape = [jax.ShapeDtypeStruct((B, S // dil, dil * GROUP_W), BF16) for _, dil in DSWA_GROUPS for _ in range(3)]
    dil_specs = [pl.BlockSpec((1, tm // dil, dil * GROUP_W), lambda b, i: (b, i, 0))
                 for _, dil in DSWA_GROUPS for _ in range(3)]
    out_shape = (dil_shape
                 + [jax.ShapeDtypeStruct((B, S, M_INNER), BF16)] * 6
                 + [jax.ShapeDtypeStruct((B, S, LANES), F32), jax.ShapeDtypeStruct((B, SUBLANES, S), F32)]
                 + [jax.ShapeDtypeStruct((B, win, 2 * GROUP_W), F32) for win, _ in DSWA_GROUPS]
                 + [jax.ShapeDtypeStruct((B, SUBLANES, 2 * M_INNER), F32)])
    out_specs = (dil_specs + [tok(M_INNER)] * 6
                 + [tok(LANES), pl.BlockSpec((1, SUBLANES, tm), lambda b, i: (b, 0, i))]
                 + [kv_spec(win) for win, _ in DSWA_GROUPS]
                 + [pl.BlockSpec((1, SUBLANES, 2 * M_INNER), lambda b, i: (b, 0, 0))])
    return pl.pallas_call(
        functools.partial(_inproj_seq_kernel, tm=tm, ns=ns),
        grid=(B, ns),
        in_specs=in_specs,
        out_specs=out_specs,
        out_shape=out_shape,
        scratch_shapes=[pltpu.VMEM((2 * M_INNER // LANES, SUBLANES + tm, LANES), F32),
                        pltpu.VMEM((3 * N_GROUPS, GROUP_W // LANES, tm, LANES), F32)],
        compiler_params=pltpu.CompilerParams(
            dimension_semantics=("arbitrary", "arbitrary"), vmem_limit_bytes=VMEM_LIMIT),
        name="inproj_seq",
    )(x, *wts, c0, ra, rm, rp)


def _inproj_step_kernel(x_ref, nmix_ref, waqkv_ref, wmqk_ref, wmv_ref, wmo_ref, wga_ref, wgb_ref,
                        wgc_ref, wgr_ref, bc_ref, br_ref, cw_ref, cb_ref, cs_ref, ra_ref, rm_ref, rp_ref,
                        qa_ref, kv0_ref, kv1_ref, kv2_ref, qm_ref, km_ref, vm_ref, og_ref, ga_ref, gb_ref,
                        gc_ref, cn_ref):
    del wgr_ref, br_ref
    xn = _rms(x_ref[...], nmix_ref[...]).astype(BF16)
    ra, rm, rp = ra_ref[...], rm_ref[...], rp_ref[...]
    kv_refs = (kv0_ref, kv1_ref, kv2_ref)
    for g in range(N_GROUPS):
        cs = slice(g * GROUP_W, (g + 1) * GROUP_W)
        zq = _wdot(xn, waqkv_ref, OFF_AQ + g * GROUP_W, OFF_AQ + (g + 1) * GROUP_W)
        qa_ref[:, cs] = _rope2(zq, ra, rm, rp) * (HEAD_DIM ** -0.5)
        zk = _wdot(xn, waqkv_ref, OFF_AK + g * GROUP_W, OFF_AK + (g + 1) * GROUP_W)
        kv_refs[g][:, :GROUP_W] = _rope2(zk, ra, rm, rp)
        kv_refs[g][:, GROUP_W:] = _wdot(xn, waqkv_ref, OFF_AV + g * GROUP_W, OFF_AV + (g + 1) * GROUP_W)

    for c in range(2 * M_INNER // 256):
        cs = slice(c * 256, (c + 1) * 256)
        pre = _wdot(xn, wmqk_ref, cs.start, cs.stop)
        y = cb_ref[:, cs] + cw_ref[CONV_W - 1:CONV_W, cs] * pre
        for j in range(CONV_W - 1):
            y = y + cw_ref[j:j + 1, cs] * cs_ref[j, :, cs]
        for j in range(CONV_W - 2):
            cn_ref[j, :, cs] = cs_ref[j + 1, :, cs]
        cn_ref[CONV_W - 2, :, cs] = pre
        s = y * _sigmoid(y)
        if c < M_INNER // 256:
            qm_ref[:, cs] = s
        else:
            km_ref[:, c * 256 - M_INNER:(c + 1) * 256 - M_INNER] = s * (M_DK ** -0.5)

    for c in range(M_INNER // 256):
        cs = slice(c * 256, (c + 1) * 256)
        vm_ref[:, cs] = _wdot(xn, wmv_ref, cs.start, cs.stop)
        og_ref[:, cs] = _sigmoid(_wdot(xn, wmo_ref, cs.start, cs.stop))
        ga_ref[:, cs] = _sigmoid(_wdot(xn, wga_ref, cs.start, cs.stop))
        gb_ref[:, cs] = _sigmoid(_wdot(xn, wgb_ref, cs.start, cs.stop))
    gc_ref[...] = _gate_cols(_wdot(xn, wgc_ref, 0, LANES) + bc_ref[...])


def _inproj_step(x, conv_state, tables, wts):
    nb = x.shape[0]
    sds = lambda w: jax.ShapeDtypeStruct((nb, w), F32)
    out_shape = ([sds(A_QKV)] + [sds(2 * GROUP_W)] * N_GROUPS + [sds(M_INNER)] * 6 + [sds(LANES)]
                 + [jax.ShapeDtypeStruct((CONV_W - 1, nb, 2 * M_INNER), F32)])
    return pl.pallas_call(
        _inproj_step_kernel,
        out_shape=out_shape,
        compiler_params=pltpu.CompilerParams(vmem_limit_bytes=VMEM_LIMIT),
        name="inproj_step",
    )(x, *wts, conv_state, *tables)


def _attn_seq_kernel(q_ref, k_ref, v_ref, kp_ref, vp_ref, w_ref, o_ref, lse_ref, wb_ref, *, tq, rpb):
    wb_ref[...] = w_ref[...].astype(BF16)
    i = pl.program_id(2)
    lane_head = lax.broadcasted_iota(jnp.int32, (1, GROUP_W), 1) // HEAD_DIM
    head_mask = [jnp.where(lane_head == h, 1.0, 0.0).astype(BF16) for h in range(HPG)]
    lane_head_full = lax.broadcasted_iota(jnp.int32, (BAND, GROUP_W), 1) // HEAD_DIM
    qi = lax.broadcasted_iota(jnp.int32, (BAND, 2 * BAND), 0)
    kj = lax.broadcasted_iota(jnp.int32, (BAND, 2 * BAND), 1)
    band = (kj >= qi) & (kj <= qi + BAND)
    first_lo = jnp.where(i > 0, 0, BAND)
    bias = jnp.where(band, 0.0, NEG)
    bias_first = jnp.where(band & (kj >= first_lo), 0.0, NEG)

    for rr, j in [(rr, j) for rr in range(rpb) for j in range(tq // BAND)]:
        ls = slice(rr * GROUP_W, (rr + 1) * GROUP_W)
        b1 = bias_first if j == 0 else bias
        q = q_ref[0, j * BAND:(j + 1) * BAND, ls]
        if j == 0:
            kk = jnp.concatenate([kp_ref[0, :, ls], k_ref[0, 0:BAND, ls]], axis=0)
            vv = jnp.concatenate([vp_ref[0, :, ls], v_ref[0, 0:BAND, ls]], axis=0)
        else:
            kk = k_ref[0, (j - 1) * BAND:(j + 1) * BAND, ls]
            vv = v_ref[0, (j - 1) * BAND:(j + 1) * BAND, ls]
        qblk = jnp.concatenate([q * head_mask[h] for h in range(HPG)], axis=0)
        s = lax.dot_general(qblk, kk, _NT, preferred_element_type=F32) + jnp.concatenate([b1] * HPG, axis=0)
        mx = jnp.max(s, axis=-1, keepdims=True)
        e = jnp.exp(s - mx)
        den = jnp.sum(e, axis=-1, keepdims=True)
        pv = _dot(e.astype(BF16), vv) * (1.0 / den)
        lse_rows = mx + jnp.log(den)
        o = pv[0:BAND]
        lse = jnp.broadcast_to(lse_rows[0:BAND], (BAND, GROUP_W))
        for h in range(1, HPG):
            own = lane_head_full == h
            o = jnp.where(own, pv[h * BAND:(h + 1) * BAND], o)
            lse = jnp.where(own, lse_rows[h * BAND:(h + 1) * BAND], lse)
        o_ref[0, j * BAND:(j + 1) * BAND, ls] = o
        lse_ref[0, j * BAND:(j + 1) * BAND, ls] = lse


def _attn_seq(q, k, v, dil, w):
    B, nsub, _ = q.shape
    tq = min(ATT_TQ, nsub)
    rpb = min(dil, ATT_TQ // tq)
    g1, g2 = dil // rpb, nsub // tq
    w_rows = w.shape[0] // (B * g1 * g2)
    assert w_rows * B * g1 * g2 == w.shape[0] and w_rows % 16 == 0
    w_spec = pl.BlockSpec((w_rows, w.shape[1]), lambda b, r, i: ((b * g1 + r) * g2 + i, 0))
    in_spec = pl.BlockSpec((1, tq, rpb * GROUP_W), lambda b, r, i: (b, i, r))
    prev_spec = pl.BlockSpec((1, BAND, rpb * GROUP_W), lambda b, r, i: (b, jnp.maximum(i * (tq // BAND) - 1, 0), r))
    out_spec = pl.BlockSpec((1, tq, rpb * GROUP_W), lambda b, r, i: (b, i, r))
    return pl.pallas_call(
        functools.partial(_attn_seq_kernel, tq=tq, rpb=rpb),
        grid=(B, g1, g2),
        in_specs=[in_spec] * 3 + [prev_spec] * 2 + [w_spec],
        out_specs=[out_spec] * 2 + [w_spec],
        out_shape=[jax.ShapeDtypeStruct((B, nsub, dil * GROUP_W), F32)] * 2 + [jax.ShapeDtypeStruct(w.shape, BF16)],
        compiler_params=pltpu.CompilerParams(
            dimension_semantics=("arbitrary", "arbitrary", "arbitrary"), vmem_limit_bytes=VMEM_LIMIT),
        name=f"attn_seq_d{dil}",
    )(q, k, v, k, v, w)


N_SIDE_CASTS = 5


def _mlstm_seq_kernel(q_ref, k_ref, v_ref, og_ref, gc_ref, gr_ref, mhn_ref, *rest, L, G):
    w_refs = rest[:N_SIDE_CASTS]
    ob_ref, c_out, n_out, m_out = rest[N_SIDE_CASTS:N_SIDE_CASTS + 4]
    wb_refs = rest[N_SIDE_CASTS + 4:2 * N_SIDE_CASTS + 4]
    c_s, n_s, m_s = rest[2 * N_SIDE_CASTS + 4:]
    for w_ref, wb_ref in zip(w_refs, wb_refs):
        wb_ref[...] = w_ref[...].astype(BF16)
    c = pl.program_id(1)

    @pl.when(c == 0)
    def _():
        c_s[...] = jnp.zeros(c_s.shape, F32)
        n_s[...] = jnp.zeros(n_s.shape, F32)
        m_s[...] = jnp.zeros(m_s.shape, F32)

    ti = lax.broadcasted_iota(jnp.int32, (L, L), 0)
    si = lax.broadcasted_iota(jnp.int32, (L, L), 1)
    tril = ti >= si
    lane = lax.broadcasted_iota(jnp.int32, (1, LANES), 1)
    m_vec = [jnp.zeros((1, LANES), F32)] * G

    for g, h in [(g, h) for g in range(G) for h in range(M_HEADS)]:
        gc = gc_ref[g]
        gr = gr_ref[g]
        hs = slice(h * M_DK, (h + 1) * M_DK)
        b_col = gc[:, M_HEADS + h:M_HEADS + h + 1]
        i_col = gc[:, h:h + 1]
        b_row = gr[M_HEADS + h:M_HEADS + h + 1, :]
        i_row = gr[h:h + 1, :]
        m_prev = m_s[g, h, 0:1, 0:1]
        qh = q_ref[g, :, hs]
        kh = k_ref[g, :, hs]
        vh = v_ref[g, :, hs]
        c_prev = c_s[g, h]
        n_prev = n_s[g, h]

        dmat = jnp.where(tril, b_col - b_row + i_row, -jnp.inf)
        m_inter = b_col + m_prev
        m_t = jnp.maximum(m_inter, jnp.max(dmat, axis=-1, keepdims=True))
        w_intra = lax.dot_general(qh, kh, _NT, preferred_element_type=F32) * jnp.exp(dmat - m_t)
        w_inter = jnp.exp(m_inter - m_t)
        num = _dot(w_intra.astype(BF16), vh) + w_inter * _dot(qh, c_prev.astype(BF16))
        den = (jnp.sum(w_intra, axis=-1, keepdims=True)
               + w_inter * jnp.sum(qh.astype(F32) * n_prev, axis=-1, keepdims=True))
        hh = num / jnp.maximum(jnp.abs(den), jnp.exp(-m_t))
        hn = hh * lax.rsqrt(jnp.mean(hh * hh, axis=-1, keepdims=True) + EPS) * mhn_ref[h:h + 1, :]
        ob_ref[g, :, hs] = (hn * og_ref[g, :, hs].astype(F32)).astype(BF16)

        b_last = b_row[:, L - 1:L]
        m_new = jnp.maximum(b_last + m_prev, jnp.max(b_last - b_row + i_row, axis=-1, keepdims=True))
        wk = jnp.exp(b_last - b_col + i_col - m_new)
        decay = jnp.exp(b_last + m_prev - m_new)
        ks = kh.astype(F32) * wk
        c_new = decay * c_prev + lax.dot_general(ks.astype(BF16), vh, _TN, preferred_element_type=F32)
        n_new = decay * n_prev + jnp.sum(ks, axis=0, keepdims=True)
        c_s[g, h] = c_new
        n_s[g, h] = n_new
        m_s[g, h] = jnp.broadcast_to(m_new, (SUBLANES, LANES))
        c_out[g, h] = c_new
        n_out[g, h:h + 1, :] = n_new
        m_vec[g] = jnp.where(lane == h, m_new, m_vec[g])
    for g in range(G):
        m_out[g] = m_vec[g]


def _mlstm_seq(qm, km, vm, og, gc, gr, mhn, side_w):
    B, S, _ = qm.shape
    L = M_CHUNK
    G = MLSTM_G
    nc = S // L
    steps = (B // G) * nc
    assert len(side_w) == N_SIDE_CASTS and all(w.shape[0] % (16 * steps) == 0 for w in side_w)
    w_specs = [pl.BlockSpec((w.shape[0] // steps, w.shape[1]), lambda b, c: (b * nc + c, 0)) for w in side_w]
    tok = pl.BlockSpec((G, L, M_INNER), lambda b, c: (b, c, 0))
    return pl.pallas_call(
        functools.partial(_mlstm_seq_kernel, L=L, G=G),
        grid=(B // G, nc),
        in_specs=[tok, tok, tok, tok,
                  pl.BlockSpec((G, L, LANES), lambda b, c: (b, c, 0)),
                  pl.BlockSpec((G, SUBLANES, L), lambda b, c: (b, 0, c)),
                  pl.BlockSpec((M_HEADS, M_DK), lambda b, c: (0, 0))] + w_specs,
        out_specs=[tok,
                   pl.BlockSpec((G, M_HEADS, M_DK, M_DK), lambda b, c: (b, 0, 0, 0)),
                   pl.BlockSpec((G, M_HEADS, M_DK), lambda b, c: (b, 0, 0)),
                   pl.BlockSpec((G, 1, LANES), lambda b, c: (b, 0, 0))] + w_specs,
        out_shape=[jax.ShapeDtypeStruct((B, S, M_INNER), BF16),
                   jax.ShapeDtypeStruct((B, M_HEADS, M_DK, M_DK), F32),
                   jax.ShapeDtypeStruct((B, M_HEADS, M_DK), F32),
                   jax.ShapeDtypeStruct((B, 1, LANES), F32)]
                  + [jax.ShapeDtypeStruct(w.shape, BF16) for w in side_w],
        scratch_shapes=[pltpu.VMEM((G, M_HEADS, M_DK, M_DK), F32),
                        pltpu.VMEM((G, M_HEADS, 1, M_DK), F32),
                        pltpu.VMEM((G, M_HEADS, SUBLANES, LANES), F32)],
        compiler_params=pltpu.CompilerParams(
            dimension_semantics=("arbitrary", "arbitrary"), vmem_limit_bytes=VMEM_LIMIT),
        name="mlstm_seq",
    )(qm, km, vm, og, gc, gr, mhn, *side_w)


def _post_kernel(x_ref, o0_ref, o1_ref, o2_ref, l0_ref, l1_ref, l2_ref, ob_ref, ga_ref, gb_ref, p_ref,
                 wpa_ref, wpb_ref, wout_ref, nffn_ref, wg_ref, wu_ref, wd_ref, nple_ref, wpg_ref, wpp_ref,
                 nfin_ref, y_ref, *ubuf, dils):
    lead = (0,) * (len(x_ref.shape) - 2)
    rd = lambda r: r[lead] if lead else r[...]
    tm = x_ref.shape[-2]

    def undilate(ref, dil, slot):
        if dil == 1:
            return rd(ref)
        halves = range(GROUP_W // LANES)
        for r in range(dil):
            for c in halves:
                lo = r * GROUP_W + c * LANES
                ubuf[0][slot, c, pl.ds(r, tm // dil, stride=dil), :] = ref[0, :, lo:lo + LANES]
        return jnp.concatenate([ubuf[0][slot, c] for c in halves], axis=1)

    l0, l1, l2 = (undilate(r, d, 2 * g) for g, (r, d) in enumerate(zip((l0_ref, l1_ref, l2_ref), dils)))
    o0, o1, o2 = (undilate(r, d, 2 * g + 1) for g, (r, d) in enumerate(zip((o0_ref, o1_ref, o2_ref), dils)))
    lmax = jnp.maximum(jnp.maximum(l0, l1), l2)
    e0, e1, e2 = jnp.exp(l0 - lmax), jnp.exp(l1 - lmax), jnp.exp(l2 - lmax)
    o_a = (e0 * o0 + e1 * o1 + e2 * o2) / (e0 + e1 + e2)

    mix = (rd(ga_ref).astype(F32) * _dot(o_a.astype(BF16), wpa_ref[...])
           + rd(gb_ref).astype(F32) * _dot(rd(ob_ref).astype(BF16), wpb_ref[...]))
    h = rd(x_ref) + _dot(mix.astype(BF16), wout_ref[...])

    xf = _rms(h, nffn_ref[...]).astype(BF16)
    half = D_FF // 2
    for c in range(2):
        cs = slice(c * half, (c + 1) * half)
        gate = _dot(xf, wg_ref[:, cs])
        act = (gate * _sigmoid(gate)) * _dot(xf, wu_ref[:, cs])
        h = h + _dot(act.astype(BF16), wd_ref[cs, :])

    xp = _rms(h, nple_ref[...]).astype(BF16)
    h = h + _sigmoid(_dot(xp, wpg_ref[...])) * _dot(rd(p_ref).astype(BF16), wpp_ref[...])
    y = _rms(h, nfin_ref[...])
    if lead:
        y_ref[lead] = y
    else:
        y_ref[...] = y


def _post_step(x, os_, ls_, ob, ga, gb, p, wts):
    return pl.pallas_call(
        functools.partial(_post_kernel, dils=(1,) * N_GROUPS),
        out_shape=jax.ShapeDtypeStruct(x.shape, F32),
        compiler_params=pltpu.CompilerParams(vmem_limit_bytes=VMEM_LIMIT),
        name="post_step",
    )(x, *os_, *ls_, ob, ga, gb, p, *wts)


SM_ROWS = 16
SM_LANES = 512
T_QA, T_KN, T_VN, T_QM, T_KM, T_VM, T_OG, T_N, T_GATE = range(9)
SO_ROWS = 8
SO_OB, SO_N, SO_M, SO_LSE, SO_O = range(5)
HALF_HEADS = HPG // 2
HALF_W = HALF_HEADS * HEAD_DIM


def _eye(n):
    return lax.broadcasted_iota(jnp.int32, (n, n), 0) == lax.broadcasted_iota(jnp.int32, (n, n), 1)


def _col_of(row, eye):
    return jnp.sum(jnp.where(eye, jnp.broadcast_to(row, eye.shape), 0.0), axis=1, keepdims=True)


def _row_of(col, eye):
    return jnp.sum(jnp.where(eye, jnp.broadcast_to(col, eye.shape), 0.0), axis=0, keepdims=True)


def _round_bf16(x):
    return x.astype(BF16).astype(F32)


def _sample_step(tile, kv_refs, cst_ref, mhn_ref, so_ref, kvn_refs, cso_ref):
    so_ref[0] = jnp.zeros((SO_ROWS, SM_LANES), F32)
    eye_a = _eye(HALF_W)
    eye_m = _eye(M_DK)

    for g, (w, dil) in enumerate(DSWA_GROUPS):
        ls = slice(g * HALF_W, (g + 1) * HALF_W)
        k_t = kv_refs[g][0, 0, 0]
        v_t = kv_refs[g][0, 1, 0]
        k_row = tile[T_KN:T_KN + 1, ls]
        v_row = tile[T_VN:T_VN + 1, ls]
        qb = _round_bf16(tile[T_QA:T_QA + 1, ls])
        prod = k_t * _col_of(qb, eye_a)
        qk_new = qb * _round_bf16(k_row)
        pos = lax.broadcasted_iota(jnp.int32, (1, w), 1)
        pe, pn, lse = [], [], []
        for hl in range(HALF_HEADS):
            hd = slice(hl * HEAD_DIM, (hl + 1) * HEAD_DIM)
            s = jnp.where(pos % dil == 0, jnp.sum(prod[hd, :], axis=0, keepdims=True), NEG)
            s_new = jnp.sum(qk_new[:, hd], axis=-1, keepdims=True)
            mx = jnp.maximum(jnp.max(s, axis=-1, keepdims=True), s_new)
            e = jnp.exp(s - mx)
            e_new = jnp.exp(s_new - mx)
            den = jnp.sum(e, axis=-1, keepdims=True) + e_new
            pe.append(jnp.broadcast_to(e / den, (HEAD_DIM, w)))
            pn.append(jnp.broadcast_to(e_new / den, (1, HEAD_DIM)))
            lse.append(jnp.broadcast_to(mx + jnp.log(den), (1, HEAD_DIM)))
        o_col = jnp.sum(v_t * jnp.concatenate(pe, axis=0), axis=-1, keepdims=True)
        so_ref[0, SO_O:SO_O + 1, ls] = _row_of(o_col, eye_a) + jnp.concatenate(pn, axis=1) * _round_bf16(v_row)
        so_ref[0, SO_LSE:SO_LSE + 1, ls] = jnp.concatenate(lse, axis=1)
        last = lax.broadcasted_iota(jnp.int32, (HALF_W, w), 1) == w - 1
        kvn_refs[g][0, 0, 0] = jnp.where(last, _col_of(k_row, eye_a), pltpu.roll(k_t, w - 1, 1))
        kvn_refs[g][0, 1, 0] = jnp.where(last, _col_of(v_row, eye_a), pltpu.roll(v_t, w - 1, 1))

    eye = eye_m
    for j in range(M_HEADS // 2):
        hs = slice(j * M_DK, (j + 1) * M_DK)
        ig = tile[T_GATE:T_GATE + 1, j:j + 1]
        lf = tile[T_GATE:T_GATE + 1, 2 + j:3 + j]
        m_prev = tile[T_GATE:T_GATE + 1, 4 + j:5 + j]
        qb = _round_bf16(tile[T_QM:T_QM + 1, hs])
        kb = _round_bf16(tile[T_KM:T_KM + 1, hs])
        vb = _round_bf16(tile[T_VM:T_VM + 1, hs])
        n_prev = tile[T_N:T_N + 1, hs]
        c_prev = cst_ref[0, j]

        m_inter = lf + m_prev
        m_t = jnp.maximum(m_inter, ig)
        w_intra = jnp.sum(qb * kb, axis=-1, keepdims=True) * jnp.exp(ig - m_t)
        w_inter = jnp.exp(m_inter - m_t)
        q_c = jnp.sum(c_prev * _col_of(qb, eye), axis=0, keepdims=True)
        num = w_intra * vb + w_inter * q_c
        den = w_intra + w_inter * jnp.sum(qb * n_prev, axis=-1, keepdims=True)
        hh = num / jnp.maximum(jnp.abs(den), jnp.exp(-m_t))
        hn = hh * lax.rsqrt(jnp.mean(hh * hh, axis=-1, keepdims=True) + EPS) * mhn_ref[0, j:j + 1, :]
        ks = kb * jnp.exp(ig - m_t)
        so_ref[0, SO_OB:SO_OB + 1, hs] = hn * tile[T_OG:T_OG + 1, hs]
        so_ref[0, SO_N:SO_N + 1, hs] = w_inter * n_prev + ks
        so_ref[0, SO_M:SO_M + 1, hs] = jnp.broadcast_to(m_t, (1, M_DK))
        cso_ref[0, j] = w_inter * c_prev + _col_of(_round_bf16(ks), eye) * vb


N_POST_IN = 22


def _post_fused_kernel(*refs, dils, ns):
    post_in = refs[:N_POST_IN]
    sm_ref, c0_ref, c1_ref, c2_ref, cst_ref, mhn_ref = refs[N_POST_IN:N_POST_IN + 6]
    y_ref, so_ref, cn0_ref, cn1_ref, cn2_ref, cso_ref = refs[N_POST_IN + 6:N_POST_IN + 12]
    (ubuf,) = refs[N_POST_IN + 12:]
    _post_kernel(*post_in, y_ref, ubuf, dils=dils)
    _sample_step(sm_ref[0], (c0_ref, c1_ref, c2_ref), cst_ref, mhn_ref, so_ref,
                 (cn0_ref, cn1_ref, cn2_ref), cso_ref)


def _post_fused(x, os_, ls_, ob, ga, gb, p, wts, sm_in, caches_t, c_state, mhn):
    B, S, _ = x.shape
    NB = c_state.shape[0]
    tm = TM_POST
    ns = S // tm
    assert B * ns == 2 * NB, "one half of a sample batch element per post-mixer grid step"
    tok = lambda w: pl.BlockSpec((1, tm, w), lambda b, i: (b, i, 0))
    dils = tuple(dil for _, dil in DSWA_GROUPS)
    dil_specs = [pl.BlockSpec((1, tm // dil, dil * GROUP_W), lambda b, i: (b, i, 0)) for dil in dils]
    step = lambda b, i: b * ns + i
    cache_specs = [pl.BlockSpec((1, 2, 1, HALF_W, c.shape[-1]), lambda b, i: (step(b, i) // 2, 0, step(b, i) % 2, 0, 0))
                   for c in caches_t]
    cst_spec = pl.BlockSpec((1, M_HEADS // 2, M_DK, M_DK), lambda b, i: (step(b, i) // 2, step(b, i) % 2, 0, 0))
    in_specs = ([tok(D_MODEL)] + dil_specs * 2 + [tok(M_INNER)] * 3 + [tok(PLE_DIM)]
                + [_const_spec(w.shape) for w in wts]
                + [pl.BlockSpec((1, SM_ROWS, SM_LANES), lambda b, i: (step(b, i), 0, 0))]
                + cache_specs
                + [cst_spec, pl.BlockSpec((1, M_HEADS // 2, M_DK), lambda b, i: (step(b, i) % 2, 0, 0))])
    out_specs = ([tok(D_MODEL), pl.BlockSpec((1, SO_ROWS, SM_LANES), lambda b, i: (step(b, i), 0, 0))]
                 + cache_specs + [cst_spec])
    out_shape = ([jax.ShapeDtypeStruct((B, S, D_MODEL), F32),
                  jax.ShapeDtypeStruct((2 * NB, SO_ROWS, SM_LANES), F32)]
                 + [jax.ShapeDtypeStruct(c.shape, F32) for c in caches_t]
                 + [jax.ShapeDtypeStruct(c_state.shape, F32)])
    return pl.pallas_call(
        functools.partial(_post_fused_kernel, dils=dils, ns=ns),
        grid=(B, ns),
        in_specs=in_specs,
        out_specs=out_specs,
        out_shape=out_shape,
        scratch_shapes=[pltpu.VMEM((2 * N_GROUPS, GROUP_W // LANES, tm, LANES), F32)],
        compiler_params=pltpu.CompilerParams(
            dimension_semantics=("arbitrary", "arbitrary"), vmem_limit_bytes=VMEM_LIMIT),
        name="post_seq_sample_stream",
    )(x, *os_, *ls_, ob, ga, gb, p, *wts, sm_in, *caches_t, c_state, mhn)


def _rope_tables(pos):
    pos = np.asarray(pos, np.float64)
    t = pos.shape[0]
    inv = np.power(ROPE_THETA, -np.arange(ROPE_HALF, dtype=np.float64) / ROPE_HALF)
    ang = pos[:, None] * inv[None, :]
    cos, sin = np.cos(ang), np.sin(ang)
    one = np.ones((t, HEAD_DIM - ROPE_DIM))
    zero = np.zeros((t, HEAD_DIM - ROPE_DIM))
    z8 = np.zeros((t, ROPE_HALF))
    a = np.concatenate([cos, cos, one], axis=1)
    m = np.concatenate([-sin, z8, zero], axis=1)
    p = np.concatenate([z8, sin, zero], axis=1)
    rep = LANES // HEAD_DIM
    return tuple(jnp.asarray(np.tile(v, (1, rep)).astype(np.float32)) for v in (a, m, p))


def kernel(x_prompt, x_sample, cache_kv_w128, cache_kv_w512, cache_kv_w2048, state_conv, state_C, state_n,
           state_m, p_prompt, p_sample, norm_mix, w_in, conv_w, conv_b, b_igate, b_fgate, mh_norm, w_proj_a,
           w_proj_b, w_out, norm_ffn, w_gate, w_up, w_down, norm_ple, w_ple_gate, w_ple_proj, norm_final):
    depth = w_in.shape[0]
    assert depth == 1, "single trunk layer"
    B, S, _ = x_prompt.shape
    NB = x_sample.shape[0]
    assert x_sample.shape[1] == 1
    caches = (cache_kv_w128, cache_kv_w512, cache_kv_w2048)
    assert S >= DSWA_GROUPS[-1][0] and S % (BAND * DSWA_GROUPS[-1][1]) == 0
    past = 0
    l = 0

    wt = w_in[l].T.astype(BF16)
    gate_wt = wt[OFF_MI:OFF_GA]
    gate_b = jnp.concatenate([b_igate[l], b_fgate[l]])
    in_wts = (
        norm_mix[l][None, :],
        wt[OFF_AQ:OFF_MQ],
        wt[OFF_MQ:OFF_MV],
        wt[OFF_MV:OFF_MO],
        wt[OFF_MO:OFF_MI],
        wt[OFF_GA:OFF_GB],
        wt[OFF_GB:IN_COLS],
        jnp.pad(gate_wt, ((0, LANES - 2 * M_HEADS), (0, 0))),
        gate_wt,
        jnp.pad(gate_b, (0, LANES - 2 * M_HEADS))[None, :],
        gate_b[:, None],
        conv_w[l],
        conv_b[l][None, :],
    )
    mhn = mh_norm[l]

    c0 = jnp.zeros((B, SUBLANES, 2 * M_INNER), F32)
    tabs_p = _rope_tables(past + np.arange(S))
    (*qkv, qm, km, vm, og, ga, gb, gc, gr, kv0, kv1, kv2, cn) = _inproj_seq(x_prompt, c0, tabs_p, in_wts)
    os_p, ls_p, ffn_bf16 = [], [], []
    for g, ((_, dil), w_ffn) in enumerate(zip(DSWA_GROUPS, (w_gate[l], w_up[l], w_down[l]))):
        o, lse, wb = _attn_seq(*qkv[3 * g:3 * g + 3], dil, w_ffn)
        os_p.append(o)
        ls_p.append(lse)
        ffn_bf16.append(wb)
    ob, c_p, n_p, m_p, wpa, wpb, wo, wpg, wpp = _mlstm_seq(
        qm, km, vm, og, gc, gr, mhn, (w_proj_a[l], w_proj_b[l], w_out[l], w_ple_gate[l], w_ple_proj[l]))
    post_wts = (wpa, wpb, wo, norm_ffn[l][None, :], *ffn_bf16, norm_ple[l][None, :], wpg, wpp, norm_final[None, :])

    tabs_s = _rope_tables(np.full((NB,), PAST_LEN))
    conv_s = jnp.swapaxes(state_conv[l], 0, 1)
    (qa_s, kvn0, kvn1, kvn2, qm_s, km_s, vm_s, og_s, ga_s, gb_s, gc_s, cn_s) = _inproj_step(
        x_sample[:, 0, :], conv_s, tabs_s, in_wts)
    pad_row = lambda t: jnp.pad(t, ((0, 0), (0, SM_LANES - t.shape[1])))
    att_half = lambda t: pad_row(t.reshape(NB, N_GROUPS, 2, HALF_W).transpose(0, 2, 1, 3).reshape(2 * NB, -1))
    mem_half = lambda t: t.reshape(2 * NB, SM_LANES)
    k_new = jnp.concatenate([kv[:, :GROUP_W] for kv in (kvn0, kvn1, kvn2)], axis=1)
    v_new = jnp.concatenate([kv[:, GROUP_W:] for kv in (kvn0, kvn1, kvn2)], axis=1)
    gates = jnp.concatenate([t.reshape(2 * NB, M_HEADS // 2) for t in
                             (gc_s[:, :M_HEADS], gc_s[:, M_HEADS:2 * M_HEADS], state_m[l])], axis=1)
    sm_rows = [att_half(qa_s), att_half(k_new), att_half(v_new), mem_half(qm_s), mem_half(km_s), mem_half(vm_s),
               mem_half(og_s), mem_half(state_n[l].reshape(NB, M_INNER)), pad_row(gates)]
    sm_rows += [jnp.zeros((2 * NB, SM_LANES), F32)] * (SM_ROWS - len(sm_rows))
    sm_in = jnp.stack(sm_rows, axis=1)
    caches_t = [jnp.transpose(c[l], (0, 2, 3, 4, 1)).reshape(NB, 2, 2, HALF_W, c.shape[2]) for c in caches]
    for c, (w, dil) in zip(caches_t, DSWA_GROUPS):
        assert c.shape[-1] == w == BAND * dil, "each cache must hold exactly one full window"
    y_prompt, so, cn0, cn1, cn2, c_s = _post_fused(x_prompt, os_p, ls_p, ob, ga, gb, p_prompt[l], post_wts,
                                                   sm_in, caches_t, state_C[l], mhn.reshape(2, M_HEADS // 2, M_DK))
    kv_s = [jnp.transpose(c.reshape(NB, 2, HPG, HEAD_DIM, -1), (0, 4, 1, 2, 3)) for c in (cn0, cn1, cn2)]
    att_full = lambda t: t[:, :N_GROUPS * HALF_W].reshape(NB, 2, N_GROUPS, HALF_W).transpose(0, 2, 1, 3).reshape(
        NB, N_GROUPS, GROUP_W)
    o_s, l_s = att_full(so[:, SO_O, :]), att_full(so[:, SO_LSE, :])
    os_s = [o_s[:, g, :] for g in range(N_GROUPS)]
    ls_s = [l_s[:, g, :] for g in range(N_GROUPS)]
    ob_s = so[:, SO_OB, :].reshape(NB, M_INNER)
    n_s = so[:, SO_N, :].reshape(NB, M_HEADS, M_DK)
    m_s = so[:, SO_M, ::M_DK].reshape(NB, M_HEADS)
    y_sample = _post_step(x_sample[:, 0, :], os_s, ls_s, ob_s, ga_s, gb_s, p_sample[l][:, 0, :], post_wts)

    kv_shape = lambda n, w: (1, n, w, 2, HPG, HEAD_DIM)
    return (
        y_prompt, y_sample[:, None, :],
        kv0.reshape(kv_shape(B, DSWA_GROUPS[0][0])), kv_s[0].reshape(kv_shape(NB, DSWA_GROUPS[0][0])),
        kv1.reshape(kv_shape(B, DSWA_GROUPS[1][0])), kv_s[1].reshape(kv_shape(NB, DSWA_GROUPS[1][0])),
        kv2.reshape(kv_shape(B, DSWA_GROUPS[2][0])), kv_s[2].reshape(kv_shape(NB, DSWA_GROUPS[2][0])),
        cn[:, SUBLANES - (CONV_W - 1):, :][None], jnp.swapaxes(cn_s, 0, 1)[None],
        c_p[None], c_s[None],
        n_p[None], n_s[None],
        m_p[:, 0, :M_HEADS][None], m_s[None],
    )
```

```python
import functools

import jax
import jax.numpy as jnp
import numpy as np
from jax import lax
from jax.experimental import pallas as pl
from jax.experimental.pallas import tpu as pltpu

F32 = jnp.float32
BF16 = jnp.bfloat16

D_MODEL = 1024
PAST_LEN = 16384
HEAD_DIM = 64
ROPE_DIM = HEAD_DIM // 4
ROPE_HALF = ROPE_DIM // 2
ROPE_THETA = 500000.0
DSWA_GROUPS = ((128, 1), (512, 4), (2048, 16))
HPG = 4
N_GROUPS = len(DSWA_GROUPS)
GROUP_W = HPG * HEAD_DIM
A_QKV = N_GROUPS * GROUP_W
BAND = 128
M_HEADS = 4
M_INNER = D_MODEL
M_DK = M_INNER // M_HEADS
CONV_W = 4
D_FF = 2816
PLE_DIM = 256
EPS = 1e-6
NEG = -1e30

OFF_AQ = 0
OFF_AK = OFF_AQ + A_QKV
OFF_AV = OFF_AK + A_QKV
OFF_MQ = OFF_AV + A_QKV
OFF_MK = OFF_MQ + M_INNER
OFF_MV = OFF_MK + M_INNER
OFF_MO = OFF_MV + M_INNER
OFF_MI = OFF_MO + M_INNER
OFF_MF = OFF_MI + M_HEADS
OFF_GA = OFF_MF + M_HEADS
OFF_GB = OFF_GA + D_MODEL
IN_COLS = OFF_GB + D_MODEL

LANES = 128
SUBLANES = 8
VMEM_LIMIT = 56 * 1024 * 1024

TM_IN = 512
TM_POST = 256
M_CHUNK = 512
CUM_BLOCK = 256
MLSTM_G = 2
ATT_TQ = 2048
MXU_DIM = 256
FFN_CHUNKS = ((0, 4 * MXU_DIM), (4 * MXU_DIM, 8 * MXU_DIM), (8 * MXU_DIM, D_FF))

_NT = (((1,), (1,)), ((), ()))
_TN = (((0,), (0,)), ((), ()))


def _dot(a, b):
    return jnp.dot(a, b, preferred_element_type=F32)


def _wdot(a, wt_ref, lo, hi):
    return lax.dot_general(a, wt_ref[lo:hi, :], _NT, preferred_element_type=F32)


def _rms(x, g):
    return x * lax.rsqrt(jnp.mean(x * x, axis=-1, keepdims=True) + EPS) * g


def _bf16_pieces(x):
    hi = x.astype(BF16)
    r1 = x - hi.astype(F32)
    mid = r1.astype(BF16)
    lo = (r1 - mid.astype(F32)).astype(BF16)
    return hi, mid, lo


def _sigmoid(x):
    return 0.5 * jnp.tanh(0.5 * x) + 0.5


def _log_sigmoid(x):
    return jnp.minimum(x, 0.0) - jnp.log1p(jnp.exp(-jnp.abs(x)))


def _rope(t, ra, rm, rp):
    return t * ra + pltpu.roll(t, LANES - ROPE_HALF, 1) * rm + pltpu.roll(t, ROPE_HALF, 1) * rp


def _rope2(t, ra, rm, rp):
    return jnp.concatenate([_rope(t[:, :LANES], ra, rm, rp), _rope(t[:, LANES:], ra, rm, rp)], axis=1)


def _gate_cols(z):
    lane = lax.broadcasted_iota(jnp.int32, z.shape, 1)
    return jnp.where(lane < M_HEADS, z, _log_sigmoid(z))


def _inproj_seq_kernel(x_ref, nmix_ref, waqkv_ref, wmqk_ref, wmv_ref, wmo_ref, wga_ref, wgb_ref,
                       wgc_ref, wgr_ref, bc_ref, br_ref, cw_ref, cb_ref, c0_ref, ra_ref, rm_ref, rp_ref,
                       q0_ref, k0_ref, v0_ref, q1_ref, k1_ref, v1_ref, q2_ref, k2_ref, v2_ref,
                       qm_ref, km_ref, vm_ref, og_ref, ga_ref, gb_ref,
                       gc_ref, gr_ref, kv0_ref, kv1_ref, kv2_ref, cn_ref, cbuf, dbuf, *, tm, ns):
    i = pl.program_id(1)
    xn = _rms(x_ref[0], nmix_ref[...]).astype(BF16)
    ra, rm, rp = ra_ref[...], rm_ref[...], rp_ref[...]
    kv_refs = (kv0_ref, kv1_ref, kv2_ref)
    qkv_refs = ((q0_ref, k0_ref, v0_ref), (q1_ref, k1_ref, v1_ref), (q2_ref, k2_ref, v2_ref))

    def put_dilated(ref, val, dil, slot):
        if dil == 1:
            ref[0] = val.astype(BF16)
            return
        for c in range(GROUP_W // LANES):
            dbuf[slot, c] = val[:, c * LANES:(c + 1) * LANES]
        for r in range(dil):
            for c in range(GROUP_W // LANES):
                lo = r * GROUP_W + c * LANES
                ref[0, :, lo:lo + LANES] = dbuf[slot, c, pl.ds(r, tm // dil, stride=dil), :].astype(BF16)

    for g, (win, dil) in enumerate(DSWA_GROUPS):
        zq = _wdot(xn, waqkv_ref, OFF_AQ + g * GROUP_W, OFF_AQ + (g + 1) * GROUP_W)
        put_dilated(qkv_refs[g][0], _rope2(zq, ra, rm, rp) * (HEAD_DIM ** -0.5), dil, 3 * g)
        zk = _wdot(xn, waqkv_ref, OFF_AK + g * GROUP_W, OFF_AK + (g + 1) * GROUP_W)
        kr = _rope2(zk, ra, rm, rp)
        put_dilated(qkv_refs[g][1], kr, dil, 3 * g + 1)
        zv = _wdot(xn, waqkv_ref, OFF_AV + g * GROUP_W, OFF_AV + (g + 1) * GROUP_W)
        put_dilated(qkv_refs[g][2], zv, dil, 3 * g + 2)
        rows = min(win, tm)
        nblk = win // rows

        @pl.when(i >= ns - nblk)
        def _():
            kv_refs[g][0, :, :GROUP_W] = kr[tm - rows:, :]
            kv_refs[g][0, :, GROUP_W:] = zv[tm - rows:, :]

    @pl.when(i == 0)
    def _():
        for blk in range(2 * M_INNER // LANES):
            cbuf[blk, 0:SUBLANES, :] = c0_ref[0, :, blk * LANES:(blk + 1) * LANES]

    for c in range(2 * M_INNER // 256):
        pre = _wdot(xn, wmqk_ref, c * 256, (c + 1) * 256)
        halves = []
        for hf in range(256 // LANES):
            blk = c * (256 // LANES) + hf
            ls = slice(blk * LANES, (blk + 1) * LANES)
            pre_h = pre[:, hf * LANES:(hf + 1) * LANES]
            cbuf[blk, SUBLANES:SUBLANES + tm, :] = pre_h
            y = cb_ref[:, ls] + cw_ref[CONV_W - 1:CONV_W, ls] * pre_h
            for d in range(1, CONV_W):
                y = y + cw_ref[CONV_W - 1 - d:CONV_W - d, ls] * cbuf[blk, pl.ds(SUBLANES - d, tm), :]
            tail = pre_h[tm - SUBLANES:, :]
            cbuf[blk, 0:SUBLANES, :] = tail
            cn_ref[0, :, ls] = tail
            halves.append(y * _sigmoid(y))
        s = jnp.concatenate(halves, axis=1)
        if c < M_INNER // 256:
            qm_ref[0, :, c * 256:(c + 1) * 256] = s.astype(BF16)
        else:
            km_ref[0, :, c * 256 - M_INNER:(c + 1) * 256 - M_INNER] = (s * (M_DK ** -0.5)).astype(BF16)
        cs = slice((c % 2) * 512, (c % 2 + 1) * 512)
        w_ref, o_ref = ((wmv_ref, vm_ref), (wmo_ref, og_ref), (wga_ref, ga_ref), (wgb_ref, gb_ref))[c // 2]
        z = _wdot(xn, w_ref, cs.start, cs.stop)
        o_ref[0, :, cs] = (z if o_ref is vm_ref else _sigmoid(z)).astype(BF16)

    gcv = _gate_cols(_wdot(xn, wgc_ref, 0, LANES) + bc_ref[...])
    zr = lax.dot_general(wgr_ref[...], xn, _NT, preferred_element_type=F32) + br_ref[...]
    sub = lax.broadcasted_iota(jnp.int32, zr.shape, 0)
    grv = jnp.where(sub < M_HEADS, zr, _log_sigmoid(zr))
    L = CUM_BLOCK
    ti = lax.broadcasted_iota(jnp.int32, (L, L), 0)
    si = lax.broadcasted_iota(jnp.int32, (L, L), 1)
    lower = jnp.where(ti >= si, 1.0, 0.0).astype(BF16)
    upper = jnp.where(ti <= si, 1.0, 0.0).astype(BF16)
    lane = lax.broadcasted_iota(jnp.int32, (L, LANES), 1)
    sub_l = lax.broadcasted_iota(jnp.int32, (SUBLANES, L), 0)
    for blk in range(tm // L):
        rs = slice(blk * L, (blk + 1) * L)
        cum_c = sum(_dot(lower, p) for p in _bf16_pieces(gcv[rs, :]))
        cum_r = sum(_dot(p, upper) for p in _bf16_pieces(grv[:, rs]))
        if (blk * L) % M_CHUNK:
            cum_c = cum_c + carry_c
            cum_r = cum_r + carry_r
        carry_c, carry_r = cum_c[L - 1:L, :], cum_r[:, L - 1:L]
        gc_ref[0, rs, :] = jnp.where(lane < M_HEADS, gcv[rs, :], cum_c)
        gr_ref[0, :, rs] = jnp.where(sub_l < M_HEADS, grv[:, rs], cum_r)


def _const_spec(shape):
    nd = len(shape)
    return pl.BlockSpec(shape, lambda *_: (0,) * nd, pipeline_mode=pl.Buffered(1))


def _inproj_seq(x, c0, tables, wts):
    B, S, _ = x.shape
    tm = TM_IN
    ns = S // tm
    ra, rm, rp = tables

    def kv_spec(win):
        rows = min(win, tm)
        nblk = win // rows
        return pl.BlockSpec((1, rows, 2 * GROUP_W), lambda b, i: (b, jnp.maximum(i - (ns - nblk), 0), 0))

    tok = lambda w: pl.BlockSpec((1, tm, w), lambda b, i: (b, i, 0))
    tab = pl.BlockSpec((tm, LANES), lambda b, i: (i, 0))
    in_specs = ([tok(D_MODEL)] + [_const_spec(w.shape) for w in wts]
                + [pl.BlockSpec((1, SUBLANES, 2 * M_INNER), lambda b, i: (b, 0, 0)), tab, tab, tab])
    dil_shape = [jax.ShapeDtypeStruct((B, S // dil, dil * GROUP_W), BF16) for _, dil in DSWA_GROUPS for _ in range(3)]
    dil_specs = [pl.BlockSpec((1, tm // dil, dil * GROUP_W), lambda b, i: (b, i, 0))
                 for _, dil in DSWA_GROUPS for _ in range(3)]
    out_shape = (dil_shape
                 + [jax.ShapeDtypeStruct((B, S, M_INNER), BF16)] * 6
                 + [jax.ShapeDtypeStruct((B, S, LANES), F32), jax.ShapeDtypeStruct((B, SUBLANES, S), F32)]
                 + [jax.ShapeDtypeStruct((B, win, 2 * GROUP_W), F32) for win, _ in DSWA_GROUPS]
                 + [jax.ShapeDtypeStruct((B, SUBLANES, 2 * M_INNER), F32)])
    out_specs = (dil_specs + [tok(M_INNER)] * 6
                 + [tok(LANES), pl.BlockSpec((1, SUBLANES, tm), lambda b, i: (b, 0, i))]
                 + [kv_spec(win) for win, _ in DSWA_GROUPS]
                 + [pl.BlockSpec((1, SUBLANES, 2 * M_INNER), lambda b, i: (b, 0, 0))])
    return pl.pallas_call(
        functools.partial(_inproj_seq_kernel, tm=tm, ns=ns),
        grid=(B, ns),
        in_specs=in_specs,
        out_specs=out_specs,
        out_shape=out_shape,
        scratch_shapes=[pltpu.VMEM((2 * M_INNER // LANES, SUBLANES + tm, LANES), F32),
                        pltpu.VMEM((3 * N_GROUPS, GROUP_W // LANES, tm, LANES), F32)],
        compiler_params=pltpu.CompilerParams(
            dimension_semantics=("arbitrary", "arbitrary"), vmem_limit_bytes=VMEM_LIMIT),
        name="inproj_seq",
    )(x, *wts, c0, ra, rm, rp)


def _inproj_step_kernel(x_ref, nmix_ref, waqkv_ref, wmqk_ref, wmv_ref, wmo_ref, wga_ref, wgb_ref,
                        wgc_ref, wgr_ref, bc_ref, br_ref, cw_ref, cb_ref, cs_ref, ra_ref, rm_ref, rp_ref,
                        qa_ref, kv0_ref, kv1_ref, kv2_ref, qm_ref, km_ref, vm_ref, og_ref, ga_ref, gb_ref,
                        gc_ref, cn_ref):
    del wgr_ref, br_ref
    xn = _rms(x_ref[...], nmix_ref[...]).astype(BF16)
    ra, rm, rp = ra_ref[...], rm_ref[...], rp_ref[...]
    kv_refs = (kv0_ref, kv1_ref, kv2_ref)
    for g in range(N_GROUPS):
        cs = slice(g * GROUP_W, (g + 1) * GROUP_W)
        zq = _wdot(xn, waqkv_ref, OFF_AQ + g * GROUP_W, OFF_AQ + (g + 1) * GROUP_W)
        qa_ref[:, cs] = _rope2(zq, ra, rm, rp) * (HEAD_DIM ** -0.5)
        zk = _wdot(xn, waqkv_ref, OFF_AK + g * GROUP_W, OFF_AK + (g + 1) * GROUP_W)
        kv_refs[g][:, :GROUP_W] = _rope2(zk, ra, rm, rp)
        kv_refs[g][:, GROUP_W:] = _wdot(xn, waqkv_ref, OFF_AV + g * GROUP_W, OFF_AV + (g + 1) * GROUP_W)

    for c in range(2 * M_INNER // 256):
        cs = slice(c * 256, (c + 1) * 256)
        pre = _wdot(xn, wmqk_ref, cs.start, cs.stop)
        y = cb_ref[:, cs] + cw_ref[CONV_W - 1:CONV_W, cs] * pre
        for j in range(CONV_W - 1):
            y = y + cw_ref[j:j + 1, cs] * cs_ref[j, :, cs]
        for j in range(CONV_W - 2):
            cn_ref[j, :, cs] = cs_ref[j + 1, :, cs]
        cn_ref[CONV_W - 2, :, cs] = pre
        s = y * _sigmoid(y)
        if c < M_INNER // 256:
            qm_ref[:, cs] = s
        else:
            km_ref[:, c * 256 - M_INNER:(c + 1) * 256 - M_INNER] = s * (M_DK ** -0.5)

    for c in range(M_INNER // 256):
        cs = slice(c * 256, (c + 1) * 256)
        vm_ref[:, cs] = _wdot(xn, wmv_ref, cs.start, cs.stop)
        og_ref[:, cs] = _sigmoid(_wdot(xn, wmo_ref, cs.start, cs.stop))
        ga_ref[:, cs] = _sigmoid(_wdot(xn, wga_ref, cs.start, cs.stop))
        gb_ref[:, cs] = _sigmoid(_wdot(xn, wgb_ref, cs.start, cs.stop))
    gc_ref[...] = _gate_cols(_wdot(xn, wgc_ref, 0, LANES) + bc_ref[...])


def _inproj_step(x, conv_state, tables, wts):
    nb = x.shape[0]
    sds = lambda w: jax.ShapeDtypeStruct((nb, w), F32)
    out_shape = ([sds(A_QKV)] + [sds(2 * GROUP_W)] * N_GROUPS + [sds(M_INNER)] * 6 + [sds(LANES)]
                 + [jax.ShapeDtypeStruct((CONV_W - 1, nb, 2 * M_INNER), F32)])
    return pl.pallas_call(
        _inproj_step_kernel,
        out_shape=out_shape,
        compiler_params=pltpu.CompilerParams(vmem_limit_bytes=VMEM_LIMIT),
        name="inproj_step",
    )(x, *wts, conv_state, *tables)


def _attn_seq_kernel(q_ref, k_ref, v_ref, kp_ref, vp_ref, w_ref, o_ref, lse_ref, wb_ref, *, tq, rpb):
    wb_ref[...] = w_ref[...].astype(BF16)
    i = pl.program_id(2)
    lane_head = lax.broadcasted_iota(jnp.int32, (1, GROUP_W), 1) // HEAD_DIM
    head_mask = [jnp.where(lane_head == h, 1.0, 0.0).astype(BF16) for h in range(HPG)]
    lane_head_full = lax.broadcasted_iota(jnp.int32, (BAND, GROUP_W), 1) // HEAD_DIM
    qi = lax.broadcasted_iota(jnp.int32, (BAND, 2 * BAND), 0)
    kj = lax.broadcasted_iota(jnp.int32, (BAND, 2 * BAND), 1)
    band = (kj >= qi) & (kj <= qi + BAND)
    first_lo = jnp.where(i > 0, 0, BAND)
    bias = jnp.where(band, 0.0, NEG)
    bias_first = jnp.where(band & (kj >= first_lo), 0.0, NEG)

    for rr, j in [(rr, j) for rr in range(rpb) for j in range(tq // BAND)]:
        ls = slice(rr * GROUP_W, (rr + 1) * GROUP_W)
        b1 = bias_first if j == 0 else bias
        q = q_ref[0, j * BAND:(j + 1) * BAND, ls]
        if j == 0:
            kk = jnp.concatenate([kp_ref[0, :, ls], k_ref[0, 0:BAND, ls]], axis=0)
            vv = jnp.concatenate([vp_ref[0, :, ls], v_ref[0, 0:BAND, ls]], axis=0)
        else:
            kk = k_ref[0, (j - 1) * BAND:(j + 1) * BAND, ls]
            vv = v_ref[0, (j - 1) * BAND:(j + 1) * BAND, ls]
        qblk = jnp.concatenate([q * head_mask[h] for h in range(HPG)], axis=0)
        s = lax.dot_general(qblk, kk, _NT, preferred_element_type=F32) + jnp.concatenate([b1] * HPG, axis=0)
        mx = jnp.max(s, axis=-1, keepdims=True)
        e = jnp.exp(s - mx)
        den = jnp.sum(e, axis=-1, keepdims=True)
        pv = _dot(e.astype(BF16), vv) * (1.0 / den)
        lse_rows = mx + jnp.log(den)
        o = pv[0:BAND]
        lse = jnp.broadcast_to(lse_rows[0:BAND], (BAND, GROUP_W))
        for h in range(1, HPG):
            own = lane_head_full == h
            o = jnp.where(own, pv[h * BAND:(h + 1) * BAND], o)
            lse = jnp.where(own, lse_rows[h * BAND:(h + 1) * BAND], lse)
        o_ref[0, j * BAND:(j + 1) * BAND, ls] = o
        lse_ref[0, j * BAND:(j + 1) * BAND, ls] = lse


def _attn_seq(q, k, v, dil, w):
    B, nsub, _ = q.shape
    tq = min(ATT_TQ, nsub)
    rpb = min(dil, ATT_TQ // tq)
    g1, g2 = dil // rpb, nsub // tq
    w_rows = w.shape[0] // (B * g1 * g2)
    assert w_rows * B * g1 * g2 == w.shape[0] and w_rows % 16 == 0
    w_spec = pl.BlockSpec((w_rows, w.shape[1]), lambda b, r, i: ((b * g1 + r) * g2 + i, 0))
    in_spec = pl.BlockSpec((1, tq, rpb * GROUP_W), lambda b, r, i: (b, i, r))
    prev_spec = pl.BlockSpec((1, BAND, rpb * GROUP_W), lambda b, r, i: (b, jnp.maximum(i * (tq // BAND) - 1, 0), r))
    out_spec = pl.BlockSpec((1, tq, rpb * GROUP_W), lambda b, r, i: (b, i, r))
    return pl.pallas_call(
        functools.partial(_attn_seq_kernel, tq=tq, rpb=rpb),
        grid=(B, g1, g2),
        in_specs=[in_spec] * 3 + [prev_spec] * 2 + [w_spec],
        out_specs=[out_spec] * 2 + [w_spec],
        out_shape=[jax.ShapeDtypeStruct((B, nsub, dil * GROUP_W), F32)] * 2 + [jax.ShapeDtypeStruct(w.shape, BF16)],
        compiler_params=pltpu.CompilerParams(
            dimension_semantics=("arbitrary", "arbitrary", "arbitrary"), vmem_limit_bytes=VMEM_LIMIT),
        name=f"attn_seq_d{dil}",
    )(q, k, v, k, v, w)


N_SIDE_CASTS = 5


def _mlstm_seq_kernel(q_ref, k_ref, v_ref, og_ref, gc_ref, gr_ref, mhn_ref, *rest, L, G):
    w_refs = rest[:N_SIDE_CASTS]
    ob_ref, c_out, n_out, m_out = rest[N_SIDE_CASTS:N_SIDE_CASTS + 4]
    wb_refs = rest[N_SIDE_CASTS + 4:2 * N_SIDE_CASTS + 4]
    c_s, n_s, m_s = rest[2 * N_SIDE_CASTS + 4:]
    for w_ref, wb_ref in zip(w_refs, wb_refs):
        wb_ref[...] = w_ref[...].astype(BF16)
    c = pl.program_id(1)

    @pl.when(c == 0)
    def _():
        c_s[...] = jnp.zeros(c_s.shape, F32)
        n_s[...] = jnp.zeros(n_s.shape, F32)
        m_s[...] = jnp.zeros(m_s.shape, F32)

    ti = lax.broadcasted_iota(jnp.int32, (L, L), 0)
    si = lax.broadcasted_iota(jnp.int32, (L, L), 1)
    tril = ti >= si
    lane = lax.broadcasted_iota(jnp.int32, (1, LANES), 1)
    m_vec = [jnp.zeros((1, LANES), F32)] * G

    for g, h in [(g, h) for g in range(G) for h in range(M_HEADS)]:
        gc = gc_ref[g]
        gr = gr_ref[g]
        hs = slice(h * M_DK, (h + 1) * M_DK)
        b_col = gc[:, M_HEADS + h:M_HEADS + h + 1]
        i_col = gc[:, h:h + 1]
        b_row = gr[M_HEADS + h:M_HEADS + h + 1, :]
        i_row = gr[h:h + 1, :]
        m_prev = m_s[g, h, 0:1, 0:1]
        qh = q_ref[g, :, hs]
        kh = k_ref[g, :, hs]
        vh = v_ref[g, :, hs]
        c_prev = c_s[g, h]
        n_prev = n_s[g, h]

        dmat = jnp.where(tril, b_col - b_row + i_row, -jnp.inf)
        m_inter = b_col + m_prev
        m_t = jnp.maximum(m_inter, jnp.max(dmat, axis=-1, keepdims=True))
        w_intra = lax.dot_general(qh, kh, _NT, preferred_element_type=F32) * jnp.exp(dmat - m_t)
        w_inter = jnp.exp(m_inter - m_t)
        num = _dot(w_intra.astype(BF16), vh) + w_inter * _dot(qh, c_prev.astype(BF16))
        den = (jnp.sum(w_intra, axis=-1, keepdims=True)
               + w_inter * jnp.sum(qh.astype(F32) * n_prev, axis=-1, keepdims=True))
        hh = num / jnp.maximum(jnp.abs(den), jnp.exp(-m_t))
        hn = hh * lax.rsqrt(jnp.mean(hh * hh, axis=-1, keepdims=True) + EPS) * mhn_ref[h:h + 1, :]
        ob_ref[g, :, hs] = (hn * og_ref[g, :, hs].astype(F32)).astype(BF16)

        b_last = b_row[:, L - 1:L]
        m_new = jnp.maximum(b_last + m_prev, jnp.max(b_last - b_row + i_row, axis=-1, keepdims=True))
        wk = jnp.exp(b_last - b_col + i_col - m_new)
        decay = jnp.exp(b_last + m_prev - m_new)
        ks = kh.astype(F32) * wk
        c_new = decay * c_prev + lax.dot_general(ks.astype(BF16), vh, _TN, preferred_element_type=F32)
        n_new = decay * n_prev + jnp.sum(ks, axis=0, keepdims=True)
        c_s[g, h] = c_new
        n_s[g, h] = n_new
        m_s[g, h] = jnp.broadcast_to(m_new, (SUBLANES, LANES))
        c_out[g, h] = c_new
        n_out[g, h:h + 1, :] = n_new
        m_vec[g] = jnp.where(lane == h, m_new, m_vec[g])
    for g in range(G):
        m_out[g] = m_vec[g]


def _mlstm_seq(qm, km, vm, og, gc, gr, mhn, side_w):
    B, S, _ = qm.shape
    L = M_CHUNK
    G = MLSTM_G
    nc = S // L
    steps = (B // G) * nc
    assert len(side_w) == N_SIDE_CASTS and all(w.shape[0] % (16 * steps) == 0 for w in side_w)
    w_specs = [pl.BlockSpec((w.shape[0] // steps, w.shape[1]), lambda b, c: (b * nc + c, 0)) for w in side_w]
    tok = pl.BlockSpec((G, L, M_INNER), lambda b, c: (b, c, 0))
    return pl.pallas_call(
        functools.partial(_mlstm_seq_kernel, L=L, G=G),
        grid=(B // G, nc),
        in_specs=[tok, tok, tok, tok,
                  pl.BlockSpec((G, L, LANES), lambda b, c: (b, c, 0)),
                  pl.BlockSpec((G, SUBLANES, L), lambda b, c: (b, 0, c)),
                  pl.BlockSpec((M_HEADS, M_DK), lambda b, c: (0, 0))] + w_specs,
        out_specs=[tok,
                   pl.BlockSpec((G, M_HEADS, M_DK, M_DK), lambda b, c: (b, 0, 0, 0)),
                   pl.BlockSpec((G, M_HEADS, M_DK), lambda b, c: (b, 0, 0)),
                   pl.BlockSpec((G, 1, LANES), lambda b, c: (b, 0, 0))] + w_specs,
        out_shape=[jax.ShapeDtypeStruct((B, S, M_INNER), BF16),
                   jax.ShapeDtypeStruct((B, M_HEADS, M_DK, M_DK), F32),
                   jax.ShapeDtypeStruct((B, M_HEADS, M_DK), F32),
                   jax.ShapeDtypeStruct((B, 1, LANES), F32)]
                  + [jax.ShapeDtypeStruct(w.shape, BF16) for w in side_w],
        scratch_shapes=[pltpu.VMEM((G, M_HEADS, M_DK, M_DK), F32),
                        pltpu.VMEM((G, M_HEADS, 1, M_DK), F32),
                        pltpu.VMEM((G, M_HEADS, SUBLANES, LANES), F32)],
        compiler_params=pltpu.CompilerParams(
            dimension_semantics=("arbitrary", "arbitrary"), vmem_limit_bytes=VMEM_LIMIT),
        name="mlstm_seq",
    )(qm, km, vm, og, gc, gr, mhn, *side_w)


def _post_kernel(x_ref, o0_ref, o1_ref, o2_ref, l0_ref, l1_ref, l2_ref, ob_ref, ga_ref, gb_ref, p_ref,
                 wpa_ref, wpb_ref, wout_ref, nffn_ref, wg_ref, wu_ref, wd_ref, nple_ref, wpg_ref, wpp_ref,
                 nfin_ref, y_ref, *ubuf, dils):
    lead = (0,) * (len(x_ref.shape) - 2)
    rd = lambda r: r[lead] if lead else r[...]
    tm = x_ref.shape[-2]

    def undilate(ref, dil, slot):
        if dil == 1:
            return rd(ref)
        halves = range(GROUP_W // LANES)
        for r in range(dil):
            for c in halves:
                lo = r * GROUP_W + c * LANES
                ubuf[0][slot, c, pl.ds(r, tm // dil, stride=dil), :] = ref[0, :, lo:lo + LANES]
        return jnp.concatenate([ubuf[0][slot, c] for c in halves], axis=1)

    l0, l1, l2 = (undilate(r, d, 2 * g) for g, (r, d) in enumerate(zip((l0_ref, l1_ref, l2_ref), dils)))
    o0, o1, o2 = (undilate(r, d, 2 * g + 1) for g, (r, d) in enumerate(zip((o0_ref, o1_ref, o2_ref), dils)))
    lmax = jnp.maximum(jnp.maximum(l0, l1), l2)
    e0, e1, e2 = jnp.exp(l0 - lmax), jnp.exp(l1 - lmax), jnp.exp(l2 - lmax)
    o_a = (e0 * o0 + e1 * o1 + e2 * o2) / (e0 + e1 + e2)

    mix = (rd(ga_ref).astype(F32) * _dot(o_a.astype(BF16), wpa_ref[...])
           + rd(gb_ref).astype(F32) * _dot(rd(ob_ref).astype(BF16), wpb_ref[...]))
    h = rd(x_ref) + _dot(mix.astype(BF16), wout_ref[...])

    xf = _rms(h, nffn_ref[...]).astype(BF16)
    for lo, hi in FFN_CHUNKS:
        cs = slice(lo, hi)
        gate = _dot(xf, wg_ref[:, cs])
        act = (gate * _sigmoid(gate)) * _dot(xf, wu_ref[:, cs])
        h = h + _dot(act.astype(BF16), wd_ref[cs, :])

    xp = _rms(h, nple_ref[...]).astype(BF16)
    h = h + _sigmoid(_dot(xp, wpg_ref[...])) * _dot(rd(p_ref).astype(BF16), wpp_ref[...])
    y = _rms(h, nfin_ref[...])
    if lead:
        y_ref[lead] = y
    else:
        y_ref[...] = y


def _post_step(x, os_, ls_, ob, ga, gb, p, wts):
    return pl.pallas_call(
        functools.partial(_post_kernel, dils=(1,) * N_GROUPS),
        out_shape=jax.ShapeDtypeStruct(x.shape, F32),
        compiler_params=pltpu.CompilerParams(vmem_limit_bytes=VMEM_LIMIT),
        name="post_step",
    )(x, *os_, *ls_, ob, ga, gb, p, *wts)


SM_ROWS = 16
SM_LANES = 512
T_QA, T_KN, T_VN, T_QM, T_KM, T_VM, T_OG, T_N, T_GATE = range(9)
SO_ROWS = 8
SO_OB, SO_N, SO_M, SO_LSE, SO_O = range(5)
HALF_HEADS = HPG // 2
HALF_W = HALF_HEADS * HEAD_DIM


def _eye(n):
    return lax.broadcasted_iota(jnp.int32, (n, n), 0) == lax.broadcasted_iota(jnp.int32, (n, n), 1)


def _col_of(row, eye):
    return jnp.sum(jnp.where(eye, jnp.broadcast_to(row, eye.shape), 0.0), axis=1, keepdims=True)


def _row_of(col, eye):
    return jnp.sum(jnp.where(eye, jnp.broadcast_to(col, eye.shape), 0.0), axis=0, keepdims=True)


def _round_bf16(x):
    return x.astype(BF16).astype(F32)


def _sample_step(tile, kv_refs, cst_ref, mhn_ref, so_ref, kvn_refs, cso_ref):
    so_ref[0] = jnp.zeros((SO_ROWS, SM_LANES), F32)
    eye_a = _eye(HALF_W)
    eye_m = _eye(M_DK)

    for g, (w, dil) in enumerate(DSWA_GROUPS):
        ls = slice(g * HALF_W, (g + 1) * HALF_W)
        k_t = kv_refs[g][0, 0, 0]
        v_t = kv_refs[g][0, 1, 0]
        k_row = tile[T_KN:T_KN + 1, ls]
        v_row = tile[T_VN:T_VN + 1, ls]
        qb = _round_bf16(tile[T_QA:T_QA + 1, ls])
        prod = k_t * _col_of(qb, eye_a)
        qk_new = qb * _round_bf16(k_row)
        pos = lax.broadcasted_iota(jnp.int32, (1, w), 1)
        pe, pn, lse = [], [], []
        for hl in range(HALF_HEADS):
            hd = slice(hl * HEAD_DIM, (hl + 1) * HEAD_DIM)
            s = jnp.where(pos % dil == 0, jnp.sum(prod[hd, :], axis=0, keepdims=True), NEG)
            s_new = jnp.sum(qk_new[:, hd], axis=-1, keepdims=True)
            mx = jnp.maximum(jnp.max(s, axis=-1, keepdims=True), s_new)
            e = jnp.exp(s - mx)
            e_new = jnp.exp(s_new - mx)
            den = jnp.sum(e, axis=-1, keepdims=True) + e_new
            pe.append(jnp.broadcast_to(e / den, (HEAD_DIM, w)))
            pn.append(jnp.broadcast_to(e_new / den, (1, HEAD_DIM)))
            lse.append(jnp.broadcast_to(mx + jnp.log(den), (1, HEAD_DIM)))
        o_col = jnp.sum(v_t * jnp.concatenate(pe, axis=0), axis=-1, keepdims=True)
        so_ref[0, SO_O:SO_O + 1, ls] = _row_of(o_col, eye_a) + jnp.concatenate(pn, axis=1) * _round_bf16(v_row)
        so_ref[0, SO_LSE:SO_LSE + 1, ls] = jnp.concatenate(lse, axis=1)
        last = lax.broadcasted_iota(jnp.int32, (HALF_W, w), 1) == w - 1
        kvn_refs[g][0, 0, 0] = jnp.where(last, _col_of(k_row, eye_a), pltpu.roll(k_t, w - 1, 1))
        kvn_refs[g][0, 1, 0] = jnp.where(last, _col_of(v_row, eye_a), pltpu.roll(v_t, w - 1, 1))

    eye = eye_m
    for j in range(M_HEADS // 2):
        hs = slice(j * M_DK, (j + 1) * M_DK)
        ig = tile[T_GATE:T_GATE + 1, j:j + 1]
        lf = tile[T_GATE:T_GATE + 1, 2 + j:3 + j]
        m_prev = tile[T_GATE:T_GATE + 1, 4 + j:5 + j]
        qb = _round_bf16(tile[T_QM:T_QM + 1, hs])
        kb = _round_bf16(tile[T_KM:T_KM + 1, hs])
        vb = _round_bf16(tile[T_VM:T_VM + 1, hs])
        n_prev = tile[T_N:T_N + 1, hs]
        c_prev = cst_ref[0, j]

        m_inter = lf + m_prev
        m_t = jnp.maximum(m_inter, ig)
        w_intra = jnp.sum(qb * kb, axis=-1, keepdims=True) * jnp.exp(ig - m_t)
        w_inter = jnp.exp(m_inter - m_t)
        q_c = jnp.sum(c_prev * _col_of(qb, eye), axis=0, keepdims=True)
        num = w_intra * vb + w_inter * q_c
        den = w_intra + w_inter * jnp.sum(qb * n_prev, axis=-1, keepdims=True)
        hh = num / jnp.maximum(jnp.abs(den), jnp.exp(-m_t))
        hn = hh * lax.rsqrt(jnp.mean(hh * hh, axis=-1, keepdims=True) + EPS) * mhn_ref[0, j:j + 1, :]
        ks = kb * jnp.exp(ig - m_t)
        so_ref[0, SO_OB:SO_OB + 1, hs] = hn * tile[T_OG:T_OG + 1, hs]
        so_ref[0, SO_N:SO_N + 1, hs] = w_inter * n_prev + ks
        so_ref[0, SO_M:SO_M + 1, hs] = jnp.broadcast_to(m_t, (1, M_DK))
        cso_ref[0, j] = w_inter * c_prev + _col_of(_round_bf16(ks), eye) * vb


N_POST_IN = 22


def _post_fused_kernel(*refs, dils, ns):
    post_in = refs[:N_POST_IN]
    sm_ref, c0_ref, c1_ref, c2_ref, cst_ref, mhn_ref = refs[N_POST_IN:N_POST_IN + 6]
    y_ref, so_ref, cn0_ref, cn1_ref, cn2_ref, cso_ref = refs[N_POST_IN + 6:N_POST_IN + 12]
    (ubuf,) = refs[N_POST_IN + 12:]
    _post_kernel(*post_in, y_ref, ubuf, dils=dils)
    _sample_step(sm_ref[0], (c0_ref, c1_ref, c2_ref), cst_ref, mhn_ref, so_ref,
                 (cn0_ref, cn1_ref, cn2_ref), cso_ref)


def _post_fused(x, os_, ls_, ob, ga, gb, p, wts, sm_in, caches_t, c_state, mhn):
    B, S, _ = x.shape
    NB = c_state.shape[0]
    tm = TM_POST
    ns = S // tm
    assert B * ns == 2 * NB, "one half of a sample batch element per post-mixer grid step"
    tok = lambda w: pl.BlockSpec((1, tm, w), lambda b, i: (b, i, 0))
    dils = tuple(dil for _, dil in DSWA_GROUPS)
    dil_specs = [pl.BlockSpec((1, tm // dil, dil * GROUP_W), lambda b, i: (b, i, 0)) for dil in dils]
    step = lambda b, i: b * ns + i
    cache_specs = [pl.BlockSpec((1, 2, 1, HALF_W, c.shape[-1]), lambda b, i: (step(b, i) // 2, 0, step(b, i) % 2, 0, 0))
                   for c in caches_t]
    cst_spec = pl.BlockSpec((1, M_HEADS // 2, M_DK, M_DK), lambda b, i: (step(b, i) // 2, step(b, i) % 2, 0, 0))
    in_specs = ([tok(D_MODEL)] + dil_specs * 2 + [tok(M_INNER)] * 3 + [tok(PLE_DIM)]
                + [_const_spec(w.shape) for w in wts]
                + [pl.BlockSpec((1, SM_ROWS, SM_LANES), lambda b, i: (step(b, i), 0, 0))]
                + cache_specs
                + [cst_spec, pl.BlockSpec((1, M_HEADS // 2, M_DK), lambda b, i: (step(b, i) % 2, 0, 0))])
    out_specs = ([tok(D_MODEL), pl.BlockSpec((1, SO_ROWS, SM_LANES), lambda b, i: (step(b, i), 0, 0))]
                 + cache_specs + [cst_spec])
    out_shape = ([jax.ShapeDtypeStruct((B, S, D_MODEL), F32),
                  jax.ShapeDtypeStruct((2 * NB, SO_ROWS, SM_LANES), F32)]
                 + [jax.ShapeDtypeStruct(c.shape, F32) for c in caches_t]
                 + [jax.ShapeDtypeStruct(c_state.shape, F32)])
    return pl.pallas_call(
        functools.partial(_post_fused_kernel, dils=dils, ns=ns),
        grid=(B, ns),
        in_specs=in_specs,
        out_specs=out_specs,
        out_shape=out_shape,
        scratch_shapes=[pltpu.VMEM((2 * N_GROUPS, GROUP_W // LANES, tm, LANES), F32)],
        compiler_params=pltpu.CompilerParams(
            dimension_semantics=("arbitrary", "arbitrary"), vmem_limit_bytes=VMEM_LIMIT),
        name="post_seq_sample_stream",
    )(x, *os_, *ls_, ob, ga, gb, p, *wts, sm_in, *caches_t, c_state, mhn)


def _rope_tables(pos):
    pos = np.asarray(pos, np.float64)
    t = pos.shape[0]
    inv = np.power(ROPE_THETA, -np.arange(ROPE_HALF, dtype=np.float64) / ROPE_HALF)
    ang = pos[:, None] * inv[None, :]
    cos, sin = np.cos(ang), np.sin(ang)
    one = np.ones((t, HEAD_DIM - ROPE_DIM))
    zero = np.zeros((t, HEAD_DIM - ROPE_DIM))
    z8 = np.zeros((t, ROPE_HALF))
    a = np.concatenate([cos, cos, one], axis=1)
    m = np.concatenate([-sin, z8, zero], axis=1)
    p = np.concatenate([z8, sin, zero], axis=1)
    rep = LANES // HEAD_DIM
    return tuple(jnp.asarray(np.tile(v, (1, rep)).astype(np.float32)) for v in (a, m, p))


def kernel(x_prompt, x_sample, cache_kv_w128, cache_kv_w512, cache_kv_w2048, state_conv, state_C, state_n,
           state_m, p_prompt, p_sample, norm_mix, w_in, conv_w, conv_b, b_igate, b_fgate, mh_norm, w_proj_a,
           w_proj_b, w_out, norm_ffn, w_gate, w_up, w_down, norm_ple, w_ple_gate, w_ple_proj, norm_final):
    depth = w_in.shape[0]
    assert depth == 1, "single trunk layer"
    B, S, _ = x_prompt.shape
    NB = x_sample.shape[0]
    assert x_sample.shape[1] == 1
    caches = (cache_kv_w128, cache_kv_w512, cache_kv_w2048)
    assert S >= DSWA_GROUPS[-1][0] and S % (BAND * DSWA_GROUPS[-1][1]) == 0
    past = 0
    l = 0

    wt = w_in[l].T.astype(BF16)
    gate_wt = wt[OFF_MI:OFF_GA]
    gate_b = jnp.concatenate([b_igate[l], b_fgate[l]])
    in_wts = (
        norm_mix[l][None, :],
        wt[OFF_AQ:OFF_MQ],
        wt[OFF_MQ:OFF_MV],
        wt[OFF_MV:OFF_MO],
        wt[OFF_MO:OFF_MI],
        wt[OFF_GA:OFF_GB],
        wt[OFF_GB:IN_COLS],
        jnp.pad(gate_wt, ((0, LANES - 2 * M_HEADS), (0, 0))),
        gate_wt,
        jnp.pad(gate_b, (0, LANES - 2 * M_HEADS))[None, :],
        gate_b[:, None],
        conv_w[l],
        conv_b[l][None, :],
    )
    mhn = mh_norm[l]

    c0 = jnp.zeros((B, SUBLANES, 2 * M_INNER), F32)
    tabs_p = _rope_tables(past + np.arange(S))
    (*qkv, qm, km, vm, og, ga, gb, gc, gr, kv0, kv1, kv2, cn) = _inproj_seq(x_prompt, c0, tabs_p, in_wts)
    os_p, ls_p, ffn_bf16 = [], [], []
    for g, ((_, dil), w_ffn) in enumerate(zip(DSWA_GROUPS, (w_gate[l], w_up[l], w_down[l]))):
        o, lse, wb = _attn_seq(*qkv[3 * g:3 * g + 3], dil, w_ffn)
        os_p.append(o)
        ls_p.append(lse)
        ffn_bf16.append(wb)
    ob, c_p, n_p, m_p, wpa, wpb, wo, wpg, wpp = _mlstm_seq(
        qm, km, vm, og, gc, gr, mhn, (w_proj_a[l], w_proj_b[l], w_out[l], w_ple_gate[l], w_ple_proj[l]))
    post_wts = (wpa, wpb, wo, norm_ffn[l][None, :], *ffn_bf16, norm_ple[l][None, :], wpg, wpp, norm_final[None, :])

    tabs_s = _rope_tables(np.full((NB,), PAST_LEN))
    conv_s = jnp.swapaxes(state_conv[l], 0, 1)
    (qa_s, kvn0, kvn1, kvn2, qm_s, km_s, vm_s, og_s, ga_s, gb_s, gc_s, cn_s) = _inproj_step(
        x_sample[:, 0, :], conv_s, tabs_s, in_wts)
    pad_row = lambda t: jnp.pad(t, ((0, 0), (0, SM_LANES - t.shape[1])))
    att_half = lambda t: pad_row(t.reshape(NB, N_GROUPS, 2, HALF_W).transpose(0, 2, 1, 3).reshape(2 * NB, -1))
    mem_half = lambda t: t.reshape(2 * NB, SM_LANES)
    k_new = jnp.concatenate([kv[:, :GROUP_W] for kv in (kvn0, kvn1, kvn2)], axis=1)
    v_new = jnp.concatenate([kv[:, GROUP_W:] for kv in (kvn0, kvn1, kvn2)], axis=1)
    gates = jnp.concatenate([t.reshape(2 * NB, M_HEADS // 2) for t in
                             (gc_s[:, :M_HEADS], gc_s[:, M_HEADS:2 * M_HEADS], state_m[l])], axis=1)
    sm_rows = [att_half(qa_s), att_half(k_new), att_half(v_new), mem_half(qm_s), mem_half(km_s), mem_half(vm_s),
               mem_half(og_s), mem_half(state_n[l].reshape(NB, M_INNER)), pad_row(gates)]
    sm_rows += [jnp.zeros((2 * NB, SM_LANES), F32)] * (SM_ROWS - len(sm_rows))
    sm_in = jnp.stack(sm_rows, axis=1)
    caches_t = [jnp.transpose(c[l], (0, 2, 3, 4, 1)).reshape(NB, 2, 2, HALF_W, c.shape[2]) for c in caches]
    for c, (w, dil) in zip(caches_t, DSWA_GROUPS):
        assert c.shape[-1] == w == BAND * dil, "each cache must hold exactly one full window"
    y_prompt, so, cn0, cn1, cn2, c_s = _post_fused(x_prompt, os_p, ls_p, ob, ga, gb, p_prompt[l], post_wts,
                                                   sm_in, caches_t, state_C[l], mhn.reshape(2, M_HEADS // 2, M_DK))
    kv_s = [jnp.transpose(c.reshape(NB, 2, HPG, HEAD_DIM, -1), (0, 4, 1, 2, 3)) for c in (cn0, cn1, cn2)]
    att_full = lambda t: t[:, :N_GROUPS * HALF_W].reshape(NB, 2, N_GROUPS, HALF_W).transpose(0, 2, 1, 3).reshape(
        NB, N_GROUPS, GROUP_W)
    o_s, l_s = att_full(so[:, SO_O, :]), att_full(so[:, SO_LSE, :])
    os_s = [o_s[:, g, :] for g in range(N_GROUPS)]
    ls_s = [l_s[:, g, :] for g in range(N_GROUPS)]
    ob_s = so[:, SO_OB, :].reshape(NB, M_INNER)
    n_s = so[:, SO_N, :].reshape(NB, M_HEADS, M_DK)
    m_s = so[:, SO_M, ::M_DK].reshape(NB, M_HEADS)
    y_sample = _post_step(x_sample[:, 0, :], os_s, ls_s, ob_s, ga_s, gb_s, p_sample[l][:, 0, :], post_wts)

    kv_shape = lambda n, w: (1, n, w, 2, HPG, HEAD_DIM)
    return (
        y_prompt, y_sample[:, None, :],
        kv0.reshape(kv_shape(B, DSWA_GROUPS[0][0])), kv_s[0].reshape(kv_shape(NB, DSWA_GROUPS[0][0])),
        kv1.reshape(kv_shape(B, DSWA_GROUPS[1][0])), kv_s[1].reshape(kv_shape(NB, DSWA_GROUPS[1][0])),
        kv2.reshape(kv_shape(B, DSWA_GROUPS[2][0])), kv_s[2].reshape(kv_shape(NB, DSWA_GROUPS[2][0])),
        cn[:, SUBLANES - (CONV_W - 1):, :][None], jnp.swapaxes(cn_s, 0, 1)[None],
        c_p[None], c_s[None],
        n_p[None], n_s[None],
        m_p[:, 0, :M_HEADS][None], m_s[None],
    )
```

```python
import functools

import jax
import jax.numpy as jnp
import numpy as np
from jax import lax
from jax.experimental import pallas as pl
from jax.experimental.pallas import tpu as pltpu

F32 = jnp.float32
BF16 = jnp.bfloat16

D_MODEL = 1024
PAST_LEN = 16384
HEAD_DIM = 64
ROPE_DIM = HEAD_DIM // 4
ROPE_HALF = ROPE_DIM // 2
ROPE_THETA = 500000.0
DSWA_GROUPS = ((128, 1), (512, 4), (2048, 16))
HPG = 4
N_GROUPS = len(DSWA_GROUPS)
GROUP_W = HPG * HEAD_DIM
A_QKV = N_GROUPS * GROUP_W
BAND = 128
M_HEADS = 4
M_INNER = D_MODEL
M_DK = M_INNER // M_HEADS
CONV_W = 4
D_FF = 2816
PLE_DIM = 256
EPS = 1e-6
NEG = -1e30

OFF_AQ = 0
OFF_AK = OFF_AQ + A_QKV
OFF_AV = OFF_AK + A_QKV
OFF_MQ = OFF_AV + A_QKV
OFF_MK = OFF_MQ + M_INNER
OFF_MV = OFF_MK + M_INNER
OFF_MO = OFF_MV + M_INNER
OFF_MI = OFF_MO + M_INNER
OFF_MF = OFF_MI + M_HEADS
OFF_GA = OFF_MF + M_HEADS
OFF_GB = OFF_GA + D_MODEL
IN_COLS = OFF_GB + D_MODEL

LANES = 128
SUBLANES = 8
VMEM_LIMIT = 56 * 1024 * 1024

TM_IN = 512
TM_POST = 256
M_CHUNK = 512
CUM_BLOCK = 256
MLSTM_G = 2
ATT_TQ = 2048
MXU_DIM = 256
FFN_CHUNKS = ((0, 4 * MXU_DIM), (4 * MXU_DIM, 8 * MXU_DIM), (8 * MXU_DIM, D_FF))

_NT = (((1,), (1,)), ((), ()))
_TN = (((0,), (0,)), ((), ()))


def _dot(a, b):
    return jnp.dot(a, b, preferred_element_type=F32)


def _wdot(a, wt_ref, lo, hi):
    return lax.dot_general(a, wt_ref[lo:hi, :], _NT, preferred_element_type=F32)


def _rms(x, g):
    return x * lax.rsqrt(jnp.mean(x * x, axis=-1, keepdims=True) + EPS) * g


def _bf16_pieces(x):
    hi = x.astype(BF16)
    r1 = x - hi.astype(F32)
    mid = r1.astype(BF16)
    lo = (r1 - mid.astype(F32)).astype(BF16)
    return hi, mid, lo


def _sigmoid_h(xh):
    return 0.5 * jnp.tanh(xh) + 0.5


def _silu_h(xh):
    return xh * (jnp.tanh(xh) + 1.0)


def _log_sigmoid(x):
    return jnp.minimum(x, 0.0) - jnp.log1p(jnp.exp(-jnp.abs(x)))


def _rope(t, ra, rm, rp):
    return t * ra + pltpu.roll(t, LANES - ROPE_HALF, 1) * rm + pltpu.roll(t, ROPE_HALF, 1) * rp


def _rope2(t, ra, rm, rp):
    return jnp.concatenate([_rope(t[:, :LANES], ra, rm, rp), _rope(t[:, LANES:], ra, rm, rp)], axis=1)


def _gate_cols(z):
    lane = lax.broadcasted_iota(jnp.int32, z.shape, 1)
    return jnp.where(lane < M_HEADS, z, _log_sigmoid(z))


def _inproj_seq_kernel(x_ref, nmix_ref, waqkv_ref, wmqk_ref, wmv_ref, wmo_ref, wga_ref, wgb_ref,
                       wgc_ref, wgr_ref, bc_ref, br_ref, cw_ref, cb_ref, c0_ref, ra_ref, rm_ref, rp_ref,
                       q0_ref, k0_ref, v0_ref, q1_ref, k1_ref, v1_ref, q2_ref, k2_ref, v2_ref,
                       qm_ref, km_ref, vm_ref, og_ref, ga_ref, gb_ref,
                       gc_ref, gr_ref, kv0_ref, kv1_ref, kv2_ref, cn_ref, cbuf, dbuf, *, tm, ns):
    i = pl.program_id(1)
    xn = _rms(x_ref[0], nmix_ref[...]).astype(BF16)
    ra, rm, rp = ra_ref[...], rm_ref[...], rp_ref[...]
    kv_refs = (kv0_ref, kv1_ref, kv2_ref)
    qkv_refs = ((q0_ref, k0_ref, v0_ref), (q1_ref, k1_ref, v1_ref), (q2_ref, k2_ref, v2_ref))

    def put_dilated(ref, val, dil, slot):
        if dil == 1:
            ref[0] = val.astype(BF16)
            return
        for c in range(GROUP_W // LANES):
            dbuf[slot, c] = val[:, c * LANES:(c + 1) * LANES]
        for r in range(dil):
            for c in range(GROUP_W // LANES):
                lo = r * GROUP_W + c * LANES
                ref[0, :, lo:lo + LANES] = dbuf[slot, c, pl.ds(r, tm // dil, stride=dil), :].astype(BF16)

    for g, (win, dil) in enumerate(DSWA_GROUPS):
        zq = _wdot(xn, waqkv_ref, OFF_AQ + g * GROUP_W, OFF_AQ + (g + 1) * GROUP_W)
        put_dilated(qkv_refs[g][0], _rope2(zq, ra, rm, rp) * (HEAD_DIM ** -0.5), dil, 3 * g)
        zk = _wdot(xn, waqkv_ref, OFF_AK + g * GROUP_W, OFF_AK + (g + 1) * GROUP_W)
        kr = _rope2(zk, ra, rm, rp)
        put_dilated(qkv_refs[g][1], kr, dil, 3 * g + 1)
        zv = _wdot(xn, waqkv_ref, OFF_AV + g * GROUP_W, OFF_AV + (g + 1) * GROUP_W)
        put_dilated(qkv_refs[g][2], zv, dil, 3 * g + 2)
        rows = min(win, tm)
        nblk = win // rows

        @pl.when(i >= ns - nblk)
        def _():
            kv_refs[g][0, :, :GROUP_W] = kr[tm - rows:, :]
            kv_refs[g][0, :, GROUP_W:] = zv[tm - rows:, :]

    @pl.when(i == 0)
    def _():
        for blk in range(2 * M_INNER // LANES):
            cbuf[blk, 0:SUBLANES, :] = c0_ref[0, :, blk * LANES:(blk + 1) * LANES]

    for c in range(2 * M_INNER // 256):
        pre = _wdot(xn, wmqk_ref, c * 256, (c + 1) * 256)
        halves = []
        for hf in range(256 // LANES):
            blk = c * (256 // LANES) + hf
            ls = slice(blk * LANES, (blk + 1) * LANES)
            pre_h = pre[:, hf * LANES:(hf + 1) * LANES]
            cbuf[blk, SUBLANES:SUBLANES + tm, :] = pre_h
            y = cb_ref[:, ls] + cw_ref[CONV_W - 1:CONV_W, ls] * pre_h
            for d in range(1, CONV_W):
                y = y + cw_ref[CONV_W - 1 - d:CONV_W - d, ls] * cbuf[blk, pl.ds(SUBLANES - d, tm), :]
            tail = pre_h[tm - SUBLANES:, :]
            cbuf[blk, 0:SUBLANES, :] = tail
            cn_ref[0, :, ls] = tail
            halves.append(_silu_h(y))
        s = jnp.concatenate(halves, axis=1)
        if c < M_INNER // 256:
            qm_ref[0, :, c * 256:(c + 1) * 256] = s.astype(BF16)
        else:
            km_ref[0, :, c * 256 - M_INNER:(c + 1) * 256 - M_INNER] = (s * (M_DK ** -0.5)).astype(BF16)
        cs = slice((c % 2) * 512, (c % 2 + 1) * 512)
        w_ref, o_ref = ((wmv_ref, vm_ref), (wmo_ref, og_ref), (wga_ref, ga_ref), (wgb_ref, gb_ref))[c // 2]
        z = _wdot(xn, w_ref, cs.start, cs.stop)
        o_ref[0, :, cs] = (z if o_ref is vm_ref else _sigmoid_h(z)).astype(BF16)

    gcv = _gate_cols(_wdot(xn, wgc_ref, 0, LANES) + bc_ref[...])
    zr = lax.dot_general(wgr_ref[...], xn, _NT, preferred_element_type=F32) + br_ref[...]
    sub = lax.broadcasted_iota(jnp.int32, zr.shape, 0)
    grv = jnp.where(sub < M_HEADS, zr, _log_sigmoid(zr))
    L = CUM_BLOCK
    ti = lax.broadcasted_iota(jnp.int32, (L, L), 0)
    si = lax.broadcasted_iota(jnp.int32, (L, L), 1)
    lower = jnp.where(ti >= si, 1.0, 0.0).astype(BF16)
    upper = jnp.where(ti <= si, 1.0, 0.0).astype(BF16)
    lane = lax.broadcasted_iota(jnp.int32, (L, LANES), 1)
    sub_l = lax.broadcasted_iota(jnp.int32, (SUBLANES, L), 0)
    for blk in range(tm // L):
        rs = slice(blk * L, (blk + 1) * L)
        cum_c = sum(_dot(lower, p) for p in _bf16_pieces(gcv[rs, :]))
        cum_r = sum(_dot(p, upper) for p in _bf16_pieces(grv[:, rs]))
        if (blk * L) % M_CHUNK:
            cum_c = cum_c + carry_c
            cum_r = cum_r + carry_r
        carry_c, carry_r = cum_c[L - 1:L, :], cum_r[:, L - 1:L]
        gc_ref[0, rs, :] = jnp.where(lane < M_HEADS, gcv[rs, :], cum_c)
        gr_ref[0, :, rs] = jnp.where(sub_l < M_HEADS, grv[:, rs], cum_r)


def _const_spec(shape):
    nd = len(shape)
    return pl.BlockSpec(shape, lambda *_: (0,) * nd, pipeline_mode=pl.Buffered(1))


def _inproj_seq(x, c0, tables, wts):
    B, S, _ = x.shape
    tm = TM_IN
    ns = S // tm
    ra, rm, rp = tables

    def kv_spec(win):
        rows = min(win, tm)
        nblk = win // rows
        return pl.BlockSpec((1, rows, 2 * GROUP_W), lambda b, i: (b, jnp.maximum(i - (ns - nblk), 0), 0))

    tok = lambda w: pl.BlockSpec((1, tm, w), lambda b, i: (b, i, 0))
    tab = pl.BlockSpec((tm, LANES), lambda b, i: (i, 0))
    in_specs = ([tok(D_MODEL)] + [_const_spec(w.shape) for w in wts]
                + [pl.BlockSpec((1, SUBLANES, 2 * M_INNER), lambda b, i: (b, 0, 0)), tab, tab, tab])
    dil_shape = [jax.ShapeDtypeStruct((B, S // dil, dil * GROUP_W), BF16) for _, dil in DSWA_GROUPS for _ in range(3)]
    dil_specs = [pl.BlockSpec((1, tm // dil, dil * GROUP_W), lambda b, i: (b, i, 0))
                 for _, dil in DSWA_GROUPS for _ in range(3)]
    out_shape = (dil_shape
                 + [jax.ShapeDtypeStruct((B, S, M_INNER), BF16)] * 6
                 + [jax.ShapeDtypeStruct((B, S, LANES), F32), jax.ShapeDtypeStruct((B, SUBLANES, S), F32)]
                 + [jax.ShapeDtypeStruct((B, win, 2 * GROUP_W), F32) for win, _ in DSWA_GROUPS]
                 + [jax.ShapeDtypeStruct((B, SUBLANES, 2 * M_INNER), F32)])
    out_specs = (dil_specs + [tok(M_INNER)] * 6
                 + [tok(LANES), pl.BlockSpec((1, SUBLANES, tm), lambda b, i: (b, 0, i))]
                 + [kv_spec(win) for win, _ in DSWA_GROUPS]
                 + [pl.BlockSpec((1, SUBLANES, 2 * M_INNER), lambda b, i: (b, 0, 0))])
    return pl.pallas_call(
        functools.partial(_inproj_seq_kernel, tm=tm, ns=ns),
        grid=(B, ns),
        in_specs=in_specs,
        out_specs=out_specs,
        out_shape=out_shape,
        scratch_shapes=[pltpu.VMEM((2 * M_INNER // LANES, SUBLANES + tm, LANES), F32),
                        pltpu.VMEM((3 * N_GROUPS, GROUP_W // LANES, tm, LANES), F32)],
        compiler_params=pltpu.CompilerParams(
            dimension_semantics=("arbitrary", "arbitrary"), vmem_limit_bytes=VMEM_LIMIT),
        name="inproj_seq",
    )(x, *wts, c0, ra, rm, rp)


def _inproj_step_kernel(x_ref, nmix_ref, waqkv_ref, wmqk_ref, wmv_ref, wmo_ref, wga_ref, wgb_ref,
                        wgc_ref, wgr_ref, bc_ref, br_ref, cw_ref, cb_ref, cs_ref, ra_ref, rm_ref, rp_ref,
                        qa_ref, kv0_ref, kv1_ref, kv2_ref, qm_ref, km_ref, vm_ref, og_ref, ga_ref, gb_ref,
                        gc_ref, cn_ref):
    del wgr_ref, br_ref
    xn = _rms(x_ref[...], nmix_ref[...]).astype(BF16)
    ra, rm, rp = ra_ref[...], rm_ref[...], rp_ref[...]
    kv_refs = (kv0_ref, kv1_ref, kv2_ref)
    for g in range(N_GROUPS):
        cs = slice(g * GROUP_W, (g + 1) * GROUP_W)
        zq = _wdot(xn, waqkv_ref, OFF_AQ + g * GROUP_W, OFF_AQ + (g + 1) * GROUP_W)
        qa_ref[:, cs] = _rope2(zq, ra, rm, rp) * (HEAD_DIM ** -0.5)
        zk = _wdot(xn, waqkv_ref, OFF_AK + g * GROUP_W, OFF_AK + (g + 1) * GROUP_W)
        kv_refs[g][:, :GROUP_W] = _rope2(zk, ra, rm, rp)
        kv_refs[g][:, GROUP_W:] = _wdot(xn, waqkv_ref, OFF_AV + g * GROUP_W, OFF_AV + (g + 1) * GROUP_W)

    for c in range(2 * M_INNER // 256):
        cs = slice(c * 256, (c + 1) * 256)
        pre = _wdot(xn, wmqk_ref, cs.start, cs.stop)
        y = cb_ref[:, cs] + cw_ref[CONV_W - 1:CONV_W, cs] * pre
        for j in range(CONV_W - 1):
            y = y + cw_ref[j:j + 1, cs] * cs_ref[j, :, cs]
        for j in range(CONV_W - 2):
            cn_ref[j, :, cs] = cs_ref[j + 1, :, cs]
        cn_ref[CONV_W - 2, :, cs] = pre
        s = _silu_h(y)
        if c < M_INNER // 256:
            qm_ref[:, cs] = s
        else:
            km_ref[:, c * 256 - M_INNER:(c + 1) * 256 - M_INNER] = s * (M_DK ** -0.5)

    for c in range(M_INNER // 256):
        cs = slice(c * 256, (c + 1) * 256)
        vm_ref[:, cs] = _wdot(xn, wmv_ref, cs.start, cs.stop)
        og_ref[:, cs] = _sigmoid_h(_wdot(xn, wmo_ref, cs.start, cs.stop))
        ga_ref[:, cs] = _sigmoid_h(_wdot(xn, wga_ref, cs.start, cs.stop))
        gb_ref[:, cs] = _sigmoid_h(_wdot(xn, wgb_ref, cs.start, cs.stop))
    gc_ref[...] = _gate_cols(_wdot(xn, wgc_ref, 0, LANES) + bc_ref[...])


def _inproj_step(x, conv_state, tables, wts):
    nb = x.shape[0]
    sds = lambda w: jax.ShapeDtypeStruct((nb, w), F32)
    out_shape = ([sds(A_QKV)] + [sds(2 * GROUP_W)] * N_GROUPS + [sds(M_INNER)] * 6 + [sds(LANES)]
                 + [jax.ShapeDtypeStruct((CONV_W - 1, nb, 2 * M_INNER), F32)])
    return pl.pallas_call(
        _inproj_step_kernel,
        out_shape=out_shape,
        compiler_params=pltpu.CompilerParams(vmem_limit_bytes=VMEM_LIMIT),
        name="inproj_step",
    )(x, *wts, conv_state, *tables)


def _cast_scaled(w_ref, wb_ref, scale):
    w = w_ref[...]
    wb_ref[...] = (w if scale == 1.0 else w * scale).astype(BF16)


def _attn_seq_kernel(q_ref, k_ref, v_ref, kp_ref, vp_ref, w_ref, o_ref, lse_ref, wb_ref, *, tq, rpb, w_scale):
    _cast_scaled(w_ref, wb_ref, w_scale)
    i = pl.program_id(2)
    lane_head = lax.broadcasted_iota(jnp.int32, (1, GROUP_W), 1) // HEAD_DIM
    head_mask = [jnp.where(lane_head == h, 1.0, 0.0).astype(BF16) for h in range(HPG)]
    lane_head_full = lax.broadcasted_iota(jnp.int32, (BAND, GROUP_W), 1) // HEAD_DIM
    qi = lax.broadcasted_iota(jnp.int32, (BAND, 2 * BAND), 0)
    kj = lax.broadcasted_iota(jnp.int32, (BAND, 2 * BAND), 1)
    band = (kj >= qi) & (kj <= qi + BAND)
    first_lo = jnp.where(i > 0, 0, BAND)
    bias = jnp.where(band, 0.0, NEG)
    bias_first = jnp.where(band & (kj >= first_lo), 0.0, NEG)

    for rr, j in [(rr, j) for rr in range(rpb) for j in range(tq // BAND)]:
        ls = slice(rr * GROUP_W, (rr + 1) * GROUP_W)
        b1 = bias_first if j == 0 else bias
        q = q_ref[0, j * BAND:(j + 1) * BAND, ls]
        if j == 0:
            kk = jnp.concatenate([kp_ref[0, :, ls], k_ref[0, 0:BAND, ls]], axis=0)
            vv = jnp.concatenate([vp_ref[0, :, ls], v_ref[0, 0:BAND, ls]], axis=0)
        else:
            kk = k_ref[0, (j - 1) * BAND:(j + 1) * BAND, ls]
            vv = v_ref[0, (j - 1) * BAND:(j + 1) * BAND, ls]
        qblk = jnp.concatenate([q * head_mask[h] for h in range(HPG)], axis=0)
        s = lax.dot_general(qblk, kk, _NT, preferred_element_type=F32) + jnp.concatenate([b1] * HPG, axis=0)
        mx = jnp.max(s, axis=-1, keepdims=True)
        e = jnp.exp(s - mx)
        den = jnp.sum(e, axis=-1, keepdims=True)
        pv = _dot(e.astype(BF16), vv) * (1.0 / den)
        lse_rows = mx + jnp.log(den)
        o = pv[0:BAND]
        lse = jnp.broadcast_to(lse_rows[0:BAND], (BAND, GROUP_W))
        for h in range(1, HPG):
            own = lane_head_full == h
            o = jnp.where(own, pv[h * BAND:(h + 1) * BAND], o)
            lse = jnp.where(own, lse_rows[h * BAND:(h + 1) * BAND], lse)
        o_ref[0, j * BAND:(j + 1) * BAND, ls] = o
        lse_ref[0, j * BAND:(j + 1) * BAND, ls] = lse


def _attn_seq(q, k, v, dil, w, w_scale):
    B, nsub, _ = q.shape
    tq = min(ATT_TQ, nsub)
    rpb = min(dil, ATT_TQ // tq)
    g1, g2 = dil // rpb, nsub // tq
    w_rows = w.shape[0] // (B * g1 * g2)
    assert w_rows * B * g1 * g2 == w.shape[0] and w_rows % 16 == 0
    w_spec = pl.BlockSpec((w_rows, w.shape[1]), lambda b, r, i: ((b * g1 + r) * g2 + i, 0))
    in_spec = pl.BlockSpec((1, tq, rpb * GROUP_W), lambda b, r, i: (b, i, r))
    prev_spec = pl.BlockSpec((1, BAND, rpb * GROUP_W), lambda b, r, i: (b, jnp.maximum(i * (tq // BAND) - 1, 0), r))
    out_spec = pl.BlockSpec((1, tq, rpb * GROUP_W), lambda b, r, i: (b, i, r))
    return pl.pallas_call(
        functools.partial(_attn_seq_kernel, tq=tq, rpb=rpb, w_scale=w_scale),
        grid=(B, g1, g2),
        in_specs=[in_spec] * 3 + [prev_spec] * 2 + [w_spec],
        out_specs=[out_spec] * 2 + [w_spec],
        out_shape=[jax.ShapeDtypeStruct((B, nsub, dil * GROUP_W), F32)] * 2 + [jax.ShapeDtypeStruct(w.shape, BF16)],
        compiler_params=pltpu.CompilerParams(
            dimension_semantics=("arbitrary", "arbitrary", "arbitrary"), vmem_limit_bytes=VMEM_LIMIT),
        name=f"attn_seq_d{dil}",
    )(q, k, v, k, v, w)


N_SIDE_CASTS = 5


def _mlstm_seq_kernel(q_ref, k_ref, v_ref, og_ref, gc_ref, gr_ref, mhn_ref, *rest, L, G, side_scales):
    w_refs = rest[:N_SIDE_CASTS]
    ob_ref, c_out, n_out, m_out = rest[N_SIDE_CASTS:N_SIDE_CASTS + 4]
    wb_refs = rest[N_SIDE_CASTS + 4:2 * N_SIDE_CASTS + 4]
    c_s, n_s, m_s = rest[2 * N_SIDE_CASTS + 4:]
    for w_ref, wb_ref, scale in zip(w_refs, wb_refs, side_scales):
        _cast_scaled(w_ref, wb_ref, scale)
    c = pl.program_id(1)

    @pl.when(c == 0)
    def _():
        c_s[...] = jnp.zeros(c_s.shape, F32)
        n_s[...] = jnp.zeros(n_s.shape, F32)
        m_s[...] = jnp.zeros(m_s.shape, F32)

    ti = lax.broadcasted_iota(jnp.int32, (L, L), 0)
    si = lax.broadcasted_iota(jnp.int32, (L, L), 1)
    tril = ti >= si
    lane = lax.broadcasted_iota(jnp.int32, (1, LANES), 1)
    m_vec = [jnp.zeros((1, LANES), F32)] * G

    for g, h in [(g, h) for g in range(G) for h in range(M_HEADS)]:
        gc = gc_ref[g]
        gr = gr_ref[g]
        hs = slice(h * M_DK, (h + 1) * M_DK)
        b_col = gc[:, M_HEADS + h:M_HEADS + h + 1]
        i_col = gc[:, h:h + 1]
        b_row = gr[M_HEADS + h:M_HEADS + h + 1, :]
        i_row = gr[h:h + 1, :]
        m_prev = m_s[g, h, 0:1, 0:1]
        qh = q_ref[g, :, hs]
        kh = k_ref[g, :, hs]
        vh = v_ref[g, :, hs]
        c_prev = c_s[g, h]
        n_prev = n_s[g, h]

        dmat = jnp.where(tril, b_col - b_row + i_row, -jnp.inf)
        m_inter = b_col + m_prev
        m_t = jnp.maximum(m_inter, jnp.max(dmat, axis=-1, keepdims=True))
        w_intra = lax.dot_general(qh, kh, _NT, preferred_element_type=F32) * jnp.exp(dmat - m_t)
        w_inter = jnp.exp(m_inter - m_t)
        num = _dot(w_intra.astype(BF16), vh) + w_inter * _dot(qh, c_prev.astype(BF16))
        den = (jnp.sum(w_intra, axis=-1, keepdims=True)
               + w_inter * jnp.sum(qh.astype(F32) * n_prev, axis=-1, keepdims=True))
        hh = num / jnp.maximum(jnp.abs(den), jnp.exp(-m_t))
        hn = hh * lax.rsqrt(jnp.mean(hh * hh, axis=-1, keepdims=True) + EPS) * mhn_ref[h:h + 1, :]
        ob_ref[g, :, hs] = (hn * og_ref[g, :, hs].astype(F32)).astype(BF16)

        b_last = b_row[:, L - 1:L]
        m_new = jnp.maximum(b_last + m_prev, jnp.max(b_last - b_row + i_row, axis=-1, keepdims=True))
        wk = jnp.exp(b_last - b_col + i_col - m_new)
        decay = jnp.exp(b_last + m_prev - m_new)
        ks = kh.astype(F32) * wk
        c_new = decay * c_prev + lax.dot_general(ks.astype(BF16), vh, _TN, preferred_element_type=F32)
        n_new = decay * n_prev + jnp.sum(ks, axis=0, keepdims=True)
        c_s[g, h] = c_new
        n_s[g, h] = n_new
        m_s[g, h] = jnp.broadcast_to(m_new, (SUBLANES, LANES))
        c_out[g, h] = c_new
        n_out[g, h:h + 1, :] = n_new
        m_vec[g] = jnp.where(lane == h, m_new, m_vec[g])
    for g in range(G):
        m_out[g] = m_vec[g]


def _mlstm_seq(qm, km, vm, og, gc, gr, mhn, side_w, side_scales):
    B, S, _ = qm.shape
    L = M_CHUNK
    G = MLSTM_G
    nc = S // L
    steps = (B // G) * nc
    assert len(side_w) == N_SIDE_CASTS and all(w.shape[0] % (16 * steps) == 0 for w in side_w)
    w_specs = [pl.BlockSpec((w.shape[0] // steps, w.shape[1]), lambda b, c: (b * nc + c, 0)) for w in side_w]
    tok = pl.BlockSpec((G, L, M_INNER), lambda b, c: (b, c, 0))
    return pl.pallas_call(
        functools.partial(_mlstm_seq_kernel, L=L, G=G, side_scales=tuple(side_scales)),
        grid=(B // G, nc),
        in_specs=[tok, tok, tok, tok,
                  pl.BlockSpec((G, L, LANES), lambda b, c: (b, c, 0)),
                  pl.BlockSpec((G, SUBLANES, L), lambda b, c: (b, 0, c)),
                  pl.BlockSpec((M_HEADS, M_DK), lambda b, c: (0, 0))] + w_specs,
        out_specs=[tok,
                   pl.BlockSpec((G, M_HEADS, M_DK, M_DK), lambda b, c: (b, 0, 0, 0)),
                   pl.BlockSpec((G, M_HEADS, M_DK), lambda b, c: (b, 0, 0)),
                   pl.BlockSpec((G, 1, LANES), lambda b, c: (b, 0, 0))] + w_specs,
        out_shape=[jax.ShapeDtypeStruct((B, S, M_INNER), BF16),
                   jax.ShapeDtypeStruct((B, M_HEADS, M_DK, M_DK), F32),
                   jax.ShapeDtypeStruct((B, M_HEADS, M_DK), F32),
                   jax.ShapeDtypeStruct((B, 1, LANES), F32)]
                  + [jax.ShapeDtypeStruct(w.shape, BF16) for w in side_w],
        scratch_shapes=[pltpu.VMEM((G, M_HEADS, M_DK, M_DK), F32),
                        pltpu.VMEM((G, M_HEADS, 1, M_DK), F32),
                        pltpu.VMEM((G, M_HEADS, SUBLANES, LANES), F32)],
        compiler_params=pltpu.CompilerParams(
            dimension_semantics=("arbitrary", "arbitrary"), vmem_limit_bytes=VMEM_LIMIT),
        name="mlstm_seq",
    )(qm, km, vm, og, gc, gr, mhn, *side_w)


def _post_kernel(x_ref, o0_ref, o1_ref, o2_ref, l0_ref, l1_ref, l2_ref, ob_ref, ga_ref, gb_ref, p_ref,
                 wpa_ref, wpb_ref, wout_ref, nffn_ref, wg_ref, wu_ref, wd_ref, nple_ref, wpg_ref, wpp_ref,
                 nfin_ref, y_ref, *ubuf, dils):
    lead = (0,) * (len(x_ref.shape) - 2)
    rd = lambda r: r[lead] if lead else r[...]
    tm = x_ref.shape[-2]

    def undilate(ref, dil, slot):
        if dil == 1:
            return rd(ref)
        halves = range(GROUP_W // LANES)
        for r in range(dil):
            for c in halves:
                lo = r * GROUP_W + c * LANES
                ubuf[0][slot, c, pl.ds(r, tm // dil, stride=dil), :] = ref[0, :, lo:lo + LANES]
        return jnp.concatenate([ubuf[0][slot, c] for c in halves], axis=1)

    l0, l1, l2 = (undilate(r, d, 2 * g) for g, (r, d) in enumerate(zip((l0_ref, l1_ref, l2_ref), dils)))
    o0, o1, o2 = (undilate(r, d, 2 * g + 1) for g, (r, d) in enumerate(zip((o0_ref, o1_ref, o2_ref), dils)))
    lmax = jnp.maximum(jnp.maximum(l0, l1), l2)
    e0, e1, e2 = jnp.exp(l0 - lmax), jnp.exp(l1 - lmax), jnp.exp(l2 - lmax)
    o_a = (e0 * o0 + e1 * o1 + e2 * o2) / (e0 + e1 + e2)

    mix = (rd(ga_ref).astype(F32) * _dot(o_a.astype(BF16), wpa_ref[...])
           + rd(gb_ref).astype(F32) * _dot(rd(ob_ref).astype(BF16), wpb_ref[...]))
    h = rd(x_ref) + _dot(mix.astype(BF16), wout_ref[...])

    xf = _rms(h, nffn_ref[...]).astype(BF16)
    for lo, hi in FFN_CHUNKS:
        cs = slice(lo, hi)
        gate = _dot(xf, wg_ref[:, cs])
        act = _silu_h(gate) * _dot(xf, wu_ref[:, cs])
        h = h + _dot(act.astype(BF16), wd_ref[cs, :])

    xp = _rms(h, nple_ref[...]).astype(BF16)
    h = h + _sigmoid_h(_dot(xp, wpg_ref[...])) * _dot(rd(p_ref).astype(BF16), wpp_ref[...])
    y = _rms(h, nfin_ref[...])
    if lead:
        y_ref[lead] = y
    else:
        y_ref[...] = y


def _post_step(x, os_, ls_, ob, ga, gb, p, wts):
    return pl.pallas_call(
        functools.partial(_post_kernel, dils=(1,) * N_GROUPS),
        out_shape=jax.ShapeDtypeStruct(x.shape, F32),
        compiler_params=pltpu.CompilerParams(vmem_limit_bytes=VMEM_LIMIT),
        name="post_step",
    )(x, *os_, *ls_, ob, ga, gb, p, *wts)


SM_ROWS = 16
SM_LANES = 512
T_QA, T_KN, T_VN, T_QM, T_KM, T_VM, T_OG, T_N, T_GATE = range(9)
SO_ROWS = 8
SO_OB, SO_N, SO_M, SO_LSE, SO_O = range(5)
HALF_HEADS = HPG // 2
HALF_W = HALF_HEADS * HEAD_DIM


def _eye(n):
    return lax.broadcasted_iota(jnp.int32, (n, n), 0) == lax.broadcasted_iota(jnp.int32, (n, n), 1)


def _col_of(row, eye):
    return jnp.sum(jnp.where(eye, jnp.broadcast_to(row, eye.shape), 0.0), axis=1, keepdims=True)


def _row_of(col, eye):
    return jnp.sum(jnp.where(eye, jnp.broadcast_to(col, eye.shape), 0.0), axis=0, keepdims=True)


def _round_bf16(x):
    return x.astype(BF16).astype(F32)


def _sample_step(tile, kv_refs, cst_ref, mhn_ref, so_ref, kvn_refs, cso_ref):
    so_ref[0] = jnp.zeros((SO_ROWS, SM_LANES), F32)
    eye_a = _eye(HALF_W)
    eye_m = _eye(M_DK)

    for g, (w, dil) in enumerate(DSWA_GROUPS):
        ls = slice(g * HALF_W, (g + 1) * HALF_W)
        k_t = kv_refs[g][0, 0, 0]
        v_t = kv_refs[g][0, 1, 0]
        k_row = tile[T_KN:T_KN + 1, ls]
        v_row = tile[T_VN:T_VN + 1, ls]
        qb = _round_bf16(tile[T_QA:T_QA + 1, ls])
        prod = k_t * _col_of(qb, eye_a)
        qk_new = qb * _round_bf16(k_row)
        pos = lax.broadcasted_iota(jnp.int32, (1, w), 1)
        pe, pn, lse = [], [], []
        for hl in range(HALF_HEADS):
            hd = slice(hl * HEAD_DIM, (hl + 1) * HEAD_DIM)
            s = jnp.where(pos % dil == 0, jnp.sum(prod[hd, :], axis=0, keepdims=True), NEG)
            s_new = jnp.sum(qk_new[:, hd], axis=-1, keepdims=True)
            mx = jnp.maximum(jnp.max(s, axis=-1, keepdims=True), s_new)
            e = jnp.exp(s - mx)
            e_new = jnp.exp(s_new - mx)
            den = jnp.sum(e, axis=-1, keepdims=True) + e_new
            pe.append(jnp.broadcast_to(e / den, (HEAD_DIM, w)))
            pn.append(jnp.broadcast_to(e_new / den, (1, HEAD_DIM)))
            lse.append(jnp.broadcast_to(mx + jnp.log(den), (1, HEAD_DIM)))
        o_col = jnp.sum(v_t * jnp.concatenate(pe, axis=0), axis=-1, keepdims=True)
        so_ref[0, SO_O:SO_O + 1, ls] = _row_of(o_col, eye_a) + jnp.concatenate(pn, axis=1) * _round_bf16(v_row)
        so_ref[0, SO_LSE:SO_LSE + 1, ls] = jnp.concatenate(lse, axis=1)
        last = lax.broadcasted_iota(jnp.int32, (HALF_W, w), 1) == w - 1
        kvn_refs[g][0, 0, 0] = jnp.where(last, _col_of(k_row, eye_a), pltpu.roll(k_t, w - 1, 1))
        kvn_refs[g][0, 1, 0] = jnp.where(last, _col_of(v_row, eye_a), pltpu.roll(v_t, w - 1, 1))

    eye = eye_m
    for j in range(M_HEADS // 2):
        hs = slice(j * M_DK, (j + 1) * M_DK)
        ig = tile[T_GATE:T_GATE + 1, j:j + 1]
        lf = tile[T_GATE:T_GATE + 1, 2 + j:3 + j]
        m_prev = tile[T_GATE:T_GATE + 1, 4 + j:5 + j]
        qb = _round_bf16(tile[T_QM:T_QM + 1, hs])
        kb = _round_bf16(tile[T_KM:T_KM + 1, hs])
        vb = _round_bf16(tile[T_VM:T_VM + 1, hs])
        n_prev = tile[T_N:T_N + 1, hs]
        c_prev = cst_ref[0, j]

        m_inter = lf + m_prev
        m_t = jnp.maximum(m_inter, ig)
        w_intra = jnp.sum(qb * kb, axis=-1, keepdims=True) * jnp.exp(ig - m_t)
        w_inter = jnp.exp(m_inter - m_t)
        q_c = jnp.sum(c_prev * _col_of(qb, eye), axis=0, keepdims=True)
        num = w_intra * vb + w_inter * q_c
        den = w_intra + w_inter * jnp.sum(qb * n_prev, axis=-1, keepdims=True)
        hh = num / jnp.maximum(jnp.abs(den), jnp.exp(-m_t))
        hn = hh * lax.rsqrt(jnp.mean(hh * hh, axis=-1, keepdims=True) + EPS) * mhn_ref[0, j:j + 1, :]
        ks = kb * jnp.exp(ig - m_t)
        so_ref[0, SO_OB:SO_OB + 1, hs] = hn * tile[T_OG:T_OG + 1, hs]
        so_ref[0, SO_N:SO_N + 1, hs] = w_inter * n_prev + ks
        so_ref[0, SO_M:SO_M + 1, hs] = jnp.broadcast_to(m_t, (1, M_DK))
        cso_ref[0, j] = w_inter * c_prev + _col_of(_round_bf16(ks), eye) * vb


N_POST_IN = 22


def _post_fused_kernel(*refs, dils, ns):
    post_in = refs[:N_POST_IN]
    sm_ref, c0_ref, c1_ref, c2_ref, cst_ref, mhn_ref = refs[N_POST_IN:N_POST_IN + 6]
    y_ref, so_ref, cn0_ref, cn1_ref, cn2_ref, cso_ref = refs[N_POST_IN + 6:N_POST_IN + 12]
    (ubuf,) = refs[N_POST_IN + 12:]
    _post_kernel(*post_in, y_ref, ubuf, dils=dils)
    _sample_step(sm_ref[0], (c0_ref, c1_ref, c2_ref), cst_ref, mhn_ref, so_ref,
                 (cn0_ref, cn1_ref, cn2_ref), cso_ref)


def _post_fused(x, os_, ls_, ob, ga, gb, p, wts, sm_in, caches_t, c_state, mhn):
    B, S, _ = x.shape
    NB = c_state.shape[0]
    tm = TM_POST
    ns = S // tm
    assert B * ns == 2 * NB, "one half of a sample batch element per post-mixer grid step"
    tok = lambda w: pl.BlockSpec((1, tm, w), lambda b, i: (b, i, 0))
    dils = tuple(dil for _, dil in DSWA_GROUPS)
    dil_specs = [pl.BlockSpec((1, tm // dil, dil * GROUP_W), lambda b, i: (b, i, 0)) for dil in dils]
    step = lambda b, i: b * ns + i
    cache_specs = [pl.BlockSpec((1, 2, 1, HALF_W, c.shape[-1]), lambda b, i: (step(b, i) // 2, 0, step(b, i) % 2, 0, 0))
                   for c in caches_t]
    cst_spec = pl.BlockSpec((1, M_HEADS // 2, M_DK, M_DK), lambda b, i: (step(b, i) // 2, step(b, i) % 2, 0, 0))
    in_specs = ([tok(D_MODEL)] + dil_specs * 2 + [tok(M_INNER)] * 3 + [tok(PLE_DIM)]
                + [_const_spec(w.shape) for w in wts]
                + [pl.BlockSpec((1, SM_ROWS, SM_LANES), lambda b, i: (step(b, i), 0, 0))]
                + cache_specs
                + [cst_spec, pl.BlockSpec((1, M_HEADS // 2, M_DK), lambda b, i: (step(b, i) % 2, 0, 0))])
    out_specs = ([tok(D_MODEL), pl.BlockSpec((1, SO_ROWS, SM_LANES), lambda b, i: (step(b, i), 0, 0))]
                 + cache_specs + [cst_spec])
    out_shape = ([jax.ShapeDtypeStruct((B, S, D_MODEL), F32),
                  jax.ShapeDtypeStruct((2 * NB, SO_ROWS, SM_LANES), F32)]
                 + [jax.ShapeDtypeStruct(c.shape, F32) for c in caches_t]
                 + [jax.ShapeDtypeStruct(c_state.shape, F32)])
    return pl.pallas_call(
        functools.partial(_post_fused_kernel, dils=dils, ns=ns),
        grid=(B, ns),
        in_specs=in_specs,
        out_specs=out_specs,
        out_shape=out_shape,
        scratch_shapes=[pltpu.VMEM((2 * N_GROUPS, GROUP_W // LANES, tm, LANES), F32)],
        compiler_params=pltpu.CompilerParams(
            dimension_semantics=("arbitrary", "arbitrary"), vmem_limit_bytes=VMEM_LIMIT),
        name="post_seq_sample_stream",
    )(x, *os_, *ls_, ob, ga, gb, p, *wts, sm_in, *caches_t, c_state, mhn)


def _rope_tables(pos):
    pos = np.asarray(pos, np.float64)
    t = pos.shape[0]
    inv = np.power(ROPE_THETA, -np.arange(ROPE_HALF, dtype=np.float64) / ROPE_HALF)
    ang = pos[:, None] * inv[None, :]
    cos, sin = np.cos(ang), np.sin(ang)
    one = np.ones((t, HEAD_DIM - ROPE_DIM))
    zero = np.zeros((t, HEAD_DIM - ROPE_DIM))
    z8 = np.zeros((t, ROPE_HALF))
    a = np.concatenate([cos, cos, one], axis=1)
    m = np.concatenate([-sin, z8, zero], axis=1)
    p = np.concatenate([z8, sin, zero], axis=1)
    rep = LANES // HEAD_DIM
    return tuple(jnp.asarray(np.tile(v, (1, rep)).astype(np.float32)) for v in (a, m, p))


def kernel(x_prompt, x_sample, cache_kv_w128, cache_kv_w512, cache_kv_w2048, state_conv, state_C, state_n,
           state_m, p_prompt, p_sample, norm_mix, w_in, conv_w, conv_b, b_igate, b_fgate, mh_norm, w_proj_a,
           w_proj_b, w_out, norm_ffn, w_gate, w_up, w_down, norm_ple, w_ple_gate, w_ple_proj, norm_final):
    depth = w_in.shape[0]
    assert depth == 1, "single trunk layer"
    B, S, _ = x_prompt.shape
    NB = x_sample.shape[0]
    assert x_sample.shape[1] == 1
    caches = (cache_kv_w128, cache_kv_w512, cache_kv_w2048)
    assert S >= DSWA_GROUPS[-1][0] and S % (BAND * DSWA_GROUPS[-1][1]) == 0
    past = 0
    l = 0

    row_scale = np.ones((IN_COLS, 1), np.float32)
    row_scale[OFF_MO:OFF_MI] = 0.5
    row_scale[OFF_GA:IN_COLS] = 0.5
    wt = (w_in[l].T * row_scale).astype(BF16)
    gate_wt = wt[OFF_MI:OFF_GA]
    gate_b = jnp.concatenate([b_igate[l], b_fgate[l]])
    in_wts = (
        norm_mix[l][None, :],
        wt[OFF_AQ:OFF_MQ],
        wt[OFF_MQ:OFF_MV],
        wt[OFF_MV:OFF_MO],
        wt[OFF_MO:OFF_MI],
        wt[OFF_GA:OFF_GB],
        wt[OFF_GB:IN_COLS],
        jnp.pad(gate_wt, ((0, LANES - 2 * M_HEADS), (0, 0))),
        gate_wt,
        jnp.pad(gate_b, (0, LANES - 2 * M_HEADS))[None, :],
        gate_b[:, None],
        0.5 * conv_w[l],
        0.5 * conv_b[l][None, :],
    )
    mhn = mh_norm[l]

    c0 = jnp.zeros((B, SUBLANES, 2 * M_INNER), F32)
    tabs_p = _rope_tables(past + np.arange(S))
    (*qkv, qm, km, vm, og, ga, gb, gc, gr, kv0, kv1, kv2, cn) = _inproj_seq(x_prompt, c0, tabs_p, in_wts)
    os_p, ls_p, ffn_bf16 = [], [], []
    ffn_w = ((w_gate[l], 0.5), (w_up[l], 1.0), (w_down[l], 1.0))
    for g, ((_, dil), (w_ffn, w_scale)) in enumerate(zip(DSWA_GROUPS, ffn_w)):
        o, lse, wb = _attn_seq(*qkv[3 * g:3 * g + 3], dil, w_ffn, w_scale)
        os_p.append(o)
        ls_p.append(lse)
        ffn_bf16.append(wb)
    ob, c_p, n_p, m_p, wpa, wpb, wo, wpg, wpp = _mlstm_seq(
        qm, km, vm, og, gc, gr, mhn, (w_proj_a[l], w_proj_b[l], w_out[l], w_ple_gate[l], w_ple_proj[l]),
        (1.0, 1.0, 1.0, 0.5, 1.0))
    post_wts = (wpa, wpb, wo, norm_ffn[l][None, :], *ffn_bf16, norm_ple[l][None, :], wpg, wpp, norm_final[None, :])

    tabs_s = _rope_tables(np.full((NB,), PAST_LEN))
    conv_s = jnp.swapaxes(state_conv[l], 0, 1)
    (qa_s, kvn0, kvn1, kvn2, qm_s, km_s, vm_s, og_s, ga_s, gb_s, gc_s, cn_s) = _inproj_step(
        x_sample[:, 0, :], conv_s, tabs_s, in_wts)
    pad_row = lambda t: jnp.pad(t, ((0, 0), (0, SM_LANES - t.shape[1])))
    att_half = lambda t: pad_row(t.reshape(NB, N_GROUPS, 2, HALF_W).transpose(0, 2, 1, 3).reshape(2 * NB, -1))
    mem_half = lambda t: t.reshape(2 * NB, SM_LANES)
    k_new = jnp.concatenate([kv[:, :GROUP_W] for kv in (kvn0, kvn1, kvn2)], axis=1)
    v_new = jnp.concatenate([kv[:, GROUP_W:] for kv in (kvn0, kvn1, kvn2)], axis=1)
    gates = jnp.concatenate([t.reshape(2 * NB, M_HEADS // 2) for t in
                             (gc_s[:, :M_HEADS], gc_s[:, M_HEADS:2 * M_HEADS], state_m[l])], axis=1)
    sm_rows = [att_half(qa_s), att_half(k_new), att_half(v_new), mem_half(qm_s), mem_half(km_s), mem_half(vm_s),
               mem_half(og_s), mem_half(state_n[l].reshape(NB, M_INNER)), pad_row(gates)]
    sm_rows += [jnp.zeros((2 * NB, SM_LANES), F32)] * (SM_ROWS - len(sm_rows))
    sm_in = jnp.stack(sm_rows, axis=1)
    caches_t = [jnp.transpose(c[l], (0, 2, 3, 4, 1)).reshape(NB, 2, 2, HALF_W, c.shape[2]) for c in caches]
    for c, (w, dil) in zip(caches_t, DSWA_GROUPS):
        assert c.shape[-1] == w == BAND * dil, "each cache must hold exactly one full window"
    y_prompt, so, cn0, cn1, cn2, c_s = _post_fused(x_prompt, os_p, ls_p, ob, ga, gb, p_prompt[l], post_wts,
                                                   sm_in, caches_t, state_C[l], mhn.reshape(2, M_HEADS // 2, M_DK))
    kv_s = [jnp.transpose(c.reshape(NB, 2, HPG, HEAD_DIM, -1), (0, 4, 1, 2, 3)) for c in (cn0, cn1, cn2)]
    att_full = lambda t: t[:, :N_GROUPS * HALF_W].reshape(NB, 2, N_GROUPS, HALF_W).transpose(0, 2, 1, 3).reshape(
        NB, N_GROUPS, GROUP_W)
    o_s, l_s = att_full(so[:, SO_O, :]), att_full(so[:, SO_LSE, :])
    os_s = [o_s[:, g, :] for g in range(N_GROUPS)]
    ls_s = [l_s[:, g, :] for g in range(N_GROUPS)]
    ob_s = so[:, SO_OB, :].reshape(NB, M_INNER)
    n_s = so[:, SO_N, :].reshape(NB, M_HEADS, M_DK)
    m_s = so[:, SO_M, ::M_DK].reshape(NB, M_HEADS)
    y_sample = _post_step(x_sample[:, 0, :], os_s, ls_s, ob_s, ga_s, gb_s, p_sample[l][:, 0, :], post_wts)

    kv_shape = lambda n, w: (1, n, w, 2, HPG, HEAD_DIM)
    return (
        y_prompt, y_sample[:, None, :],
        kv0.reshape(kv_shape(B, DSWA_GROUPS[0][0])), kv_s[0].reshape(kv_shape(NB, DSWA_GROUPS[0][0])),
        kv1.reshape(kv_shape(B, DSWA_GROUPS[1][0])), kv_s[1].reshape(kv_shape(NB, DSWA_GROUPS[1][0])),
        kv2.reshape(kv_shape(B, DSWA_GROUPS[2][0])), kv_s[2].reshape(kv_shape(NB, DSWA_GROUPS[2][0])),
        cn[:, SUBLANES - (CONV_W - 1):, :][None], jnp.swapaxes(cn_s, 0, 1)[None],
        c_p[None], c_s[None],
        n_p[None], n_s[None],
        m_p[:, 0, :M_HEADS][None], m_s[None],
    )
```

```python
import functools

import jax
import jax.numpy as jnp
import numpy as np
from jax import lax
from jax.experimental import pallas as pl
from jax.experimental.pallas import tpu as pltpu

F32 = jnp.float32
BF16 = jnp.bfloat16

D_MODEL = 1024
PAST_LEN = 16384
HEAD_DIM = 64
ROPE_DIM = HEAD_DIM // 4
ROPE_HALF = ROPE_DIM // 2
ROPE_THETA = 500000.0
DSWA_GROUPS = ((128, 1), (512, 4), (2048, 16))
HPG = 4
N_GROUPS = len(DSWA_GROUPS)
GROUP_W = HPG * HEAD_DIM
A_QKV = N_GROUPS * GROUP_W
BAND = 128
M_HEADS = 4
M_INNER = D_MODEL
M_DK = M_INNER // M_HEADS
CONV_W = 4
D_FF = 2816
PLE_DIM = 256
EPS = 1e-6
NEG = -1e30

OFF_AQ = 0
OFF_AK = OFF_AQ + A_QKV
OFF_AV = OFF_AK + A_QKV
OFF_MQ = OFF_AV + A_QKV
OFF_MK = OFF_MQ + M_INNER
OFF_MV = OFF_MK + M_INNER
OFF_MO = OFF_MV + M_INNER
OFF_MI = OFF_MO + M_INNER
OFF_MF = OFF_MI + M_HEADS
OFF_GA = OFF_MF + M_HEADS
OFF_GB = OFF_GA + D_MODEL
IN_COLS = OFF_GB + D_MODEL

LANES = 128
SUBLANES = 8
VMEM_LIMIT = 56 * 1024 * 1024

TM_IN = 512
TM_POST = 256
M_CHUNK = 512
CUM_BLOCK = 256
MLSTM_G = 2
ATT_TQ = 2048
MXU_DIM = 256
FFN_CHUNKS = ((0, 4 * MXU_DIM), (4 * MXU_DIM, 8 * MXU_DIM), (8 * MXU_DIM, D_FF))

_NT = (((1,), (1,)), ((), ()))
_TN = (((0,), (0,)), ((), ()))


def _dot(a, b):
    return jnp.dot(a, b, preferred_element_type=F32)


def _wdot(a, wt_ref, lo, hi):
    return lax.dot_general(a, wt_ref[lo:hi, :], _NT, preferred_element_type=F32)


def _rms(x, g):
    return x * lax.rsqrt(jnp.mean(x * x, axis=-1, keepdims=True) + EPS) * g


def _bf16_pieces(x):
    hi = x.astype(BF16)
    r1 = x - hi.astype(F32)
    mid = r1.astype(BF16)
    lo = (r1 - mid.astype(F32)).astype(BF16)
    return hi, mid, lo


def _sigmoid_h(xh):
    return 0.5 * jnp.tanh(xh) + 0.5


def _silu_h(xh):
    return xh * (jnp.tanh(xh) + 1.0)


def _log_sigmoid(x):
    return jnp.minimum(x, 0.0) - jnp.log1p(jnp.exp(-jnp.abs(x)))


def _rope(t, ra, rm, rp):
    return t * ra + pltpu.roll(t, LANES - ROPE_HALF, 1) * rm + pltpu.roll(t, ROPE_HALF, 1) * rp


def _rope2(t, ra, rm, rp):
    return jnp.concatenate([_rope(t[:, :LANES], ra, rm, rp), _rope(t[:, LANES:], ra, rm, rp)], axis=1)


def _gate_cols(z):
    lane = lax.broadcasted_iota(jnp.int32, z.shape, 1)
    return jnp.where(lane < M_HEADS, z, _log_sigmoid(z))


def _inproj_seq_kernel(x_ref, nmix_ref, waqkv_ref, wmqk_ref, wmv_ref, wmo_ref, wga_ref, wgb_ref,
                       wgc_ref, wgr_ref, bc_ref, br_ref, cw_ref, cb_ref, c0_ref, ra_ref, rm_ref, rp_ref,
                       q0_ref, k0_ref, v0_ref, q1_ref, k1_ref, v1_ref, q2_ref, k2_ref, v2_ref,
                       qm_ref, km_ref, vm_ref, og_ref, ga_ref, gb_ref,
                       gc_ref, gr_ref, kv0_ref, kv1_ref, kv2_ref, cn_ref, cbuf, dbuf, *, tm, ns):
    i = pl.program_id(1)
    xn = _rms(x_ref[0], nmix_ref[...]).astype(BF16)
    ra, rm, rp = ra_ref[...], rm_ref[...], rp_ref[...]
    kv_refs = (kv0_ref, kv1_ref, kv2_ref)
    qkv_refs = ((q0_ref, k0_ref, v0_ref), (q1_ref, k1_ref, v1_ref), (q2_ref, k2_ref, v2_ref))

    def put_dilated(ref, val, dil, slot):
        if dil == 1:
            ref[0] = val.astype(BF16)
            return
        for c in range(GROUP_W // LANES):
            dbuf[slot, c] = val[:, c * LANES:(c + 1) * LANES]
        for r in range(dil):
            for c in range(GROUP_W // LANES):
                lo = r * GROUP_W + c * LANES
                ref[0, :, lo:lo + LANES] = dbuf[slot, c, pl.ds(r, tm // dil, stride=dil), :].astype(BF16)

    for g, (win, dil) in enumerate(DSWA_GROUPS):
        zq = _wdot(xn, waqkv_ref, OFF_AQ + g * GROUP_W, OFF_AQ + (g + 1) * GROUP_W)
        put_dilated(qkv_refs[g][0], _rope2(zq, ra, rm, rp) * (HEAD_DIM ** -0.5), dil, 3 * g)
        zk = _wdot(xn, waqkv_ref, OFF_AK + g * GROUP_W, OFF_AK + (g + 1) * GROUP_W)
        kr = _rope2(zk, ra, rm, rp)
        put_dilated(qkv_refs[g][1], kr, dil, 3 * g + 1)
        zv = _wdot(xn, waqkv_ref, OFF_AV + g * GROUP_W, OFF_AV + (g + 1) * GROUP_W)
        put_dilated(qkv_refs[g][2], zv, dil, 3 * g + 2)
        rows = min(win, tm)
        nblk = win // rows

        @pl.when(i >= ns - nblk)
        def _():
            kv_refs[g][0, :, :GROUP_W] = kr[tm - rows:, :]
            kv_refs[g][0, :, GROUP_W:] = zv[tm - rows:, :]

    @pl.when(i == 0)
    def _():
        for blk in range(2 * M_INNER // LANES):
            cbuf[blk, 0:SUBLANES, :] = c0_ref[0, :, blk * LANES:(blk + 1) * LANES]

    for c in range(2 * M_INNER // 256):
        pre = _wdot(xn, wmqk_ref, c * 256, (c + 1) * 256)
        halves = []
        for hf in range(256 // LANES):
            blk = c * (256 // LANES) + hf
            ls = slice(blk * LANES, (blk + 1) * LANES)
            pre_h = pre[:, hf * LANES:(hf + 1) * LANES]
            cbuf[blk, SUBLANES:SUBLANES + tm, :] = pre_h
            y = cb_ref[:, ls] + cw_ref[CONV_W - 1:CONV_W, ls] * pre_h
            for d in range(1, CONV_W):
                y = y + cw_ref[CONV_W - 1 - d:CONV_W - d, ls] * cbuf[blk, pl.ds(SUBLANES - d, tm), :]
            tail = pre_h[tm - SUBLANES:, :]
            cbuf[blk, 0:SUBLANES, :] = tail
            cn_ref[0, :, ls] = tail
            halves.append(_silu_h(y))
        s = jnp.concatenate(halves, axis=1)
        if c < M_INNER // 256:
            qm_ref[0, :, c * 256:(c + 1) * 256] = s.astype(BF16)
        else:
            km_ref[0, :, c * 256 - M_INNER:(c + 1) * 256 - M_INNER] = (s * (M_DK ** -0.5)).astype(BF16)
        cs = slice((c % 2) * 512, (c % 2 + 1) * 512)
        w_ref, o_ref = ((wmv_ref, vm_ref), (wmo_ref, og_ref), (wga_ref, ga_ref), (wgb_ref, gb_ref))[c // 2]
        z = _wdot(xn, w_ref, cs.start, cs.stop)
        o_ref[0, :, cs] = (z if o_ref is vm_ref else _sigmoid_h(z)).astype(BF16)

    gcv = _gate_cols(_wdot(xn, wgc_ref, 0, LANES) + bc_ref[...])
    zr = lax.dot_general(wgr_ref[...], xn, _NT, preferred_element_type=F32) + br_ref[...]
    sub = lax.broadcasted_iota(jnp.int32, zr.shape, 0)
    grv = jnp.where(sub < M_HEADS, zr, _log_sigmoid(zr))
    L = CUM_BLOCK
    ti = lax.broadcasted_iota(jnp.int32, (L, L), 0)
    si = lax.broadcasted_iota(jnp.int32, (L, L), 1)
    lower = jnp.where(ti >= si, 1.0, 0.0).astype(BF16)
    upper = jnp.where(ti <= si, 1.0, 0.0).astype(BF16)
    lane = lax.broadcasted_iota(jnp.int32, (L, LANES), 1)
    sub_l = lax.broadcasted_iota(jnp.int32, (SUBLANES, L), 0)
    for blk in range(tm // L):
        rs = slice(blk * L, (blk + 1) * L)
        cum_c = sum(_dot(lower, p) for p in _bf16_pieces(gcv[rs, :]))
        cum_r = sum(_dot(p, upper) for p in _bf16_pieces(grv[:, rs]))
        if (blk * L) % M_CHUNK:
            cum_c = cum_c + carry_c
            cum_r = cum_r + carry_r
        carry_c, carry_r = cum_c[L - 1:L, :], cum_r[:, L - 1:L]
        gc_ref[0, rs, :] = jnp.where(lane < M_HEADS, gcv[rs, :], cum_c)
        gr_ref[0, :, rs] = jnp.where(sub_l < M_HEADS, grv[:, rs], cum_r)


def _const_spec(shape):
    nd = len(shape)
    return pl.BlockSpec(shape, lambda *_: (0,) * nd, pipeline_mode=pl.Buffered(1))


def _inproj_seq(x, c0, tables, wts):
    B, S, _ = x.shape
    tm = TM_IN
    ns = S // tm
    ra, rm, rp = tables

    def kv_spec(win):
        rows = min(win, tm)
        nblk = win // rows
        return pl.BlockSpec((1, rows, 2 * GROUP_W), lambda b, i: (b, jnp.maximum(i - (ns - nblk), 0), 0))

    tok = lambda w: pl.BlockSpec((1, tm, w), lambda b, i: (b, i, 0))
    tab = pl.BlockSpec((tm, LANES), lambda b, i: (i, 0))
    in_specs = ([tok(D_MODEL)] + [_const_spec(w.shape) for w in wts]
                + [pl.BlockSpec((1, SUBLANES, 2 * M_INNER), lambda b, i: (b, 0, 0)), tab, tab, tab])
    dil_shape = [jax.ShapeDtypeStruct((B, S // dil, dil * GROUP_W), BF16) for _, dil in DSWA_GROUPS for _ in range(3)]
    dil_specs = [pl.BlockSpec((1, tm // dil, dil * GROUP_W), lambda b, i: (b, i, 0))
                 for _, dil in DSWA_GROUPS for _ in range(3)]
    out_shape = (dil_shape
                 + [jax.ShapeDtypeStruct((B, S, M_INNER), BF16)] * 6
                 + [jax.ShapeDtypeStruct((B, S, LANES), F32), jax.ShapeDtypeStruct((B, SUBLANES, S), F32)]
                 + [jax.ShapeDtypeStruct((B, win, 2 * GROUP_W), F32) for win, _ in DSWA_GROUPS]
                 + [jax.ShapeDtypeStruct((B, SUBLANES, 2 * M_INNER), F32)])
    out_specs = (dil_specs + [tok(M_INNER)] * 6
                 + [tok(LANES), pl.BlockSpec((1, SUBLANES, tm), lambda b, i: (b, 0, i))]
                 + [kv_spec(win) for win, _ in DSWA_GROUPS]
                 + [pl.BlockSpec((1, SUBLANES, 2 * M_INNER), lambda b, i: (b, 0, 0))])
    return pl.pallas_call(
        functools.partial(_inproj_seq_kernel, tm=tm, ns=ns),
        grid=(B, ns),
        in_specs=in_specs,
        out_specs=out_specs,
        out_shape=out_shape,
        scratch_shapes=[pltpu.VMEM((2 * M_INNER // LANES, SUBLANES + tm, LANES), F32),
                        pltpu.VMEM((3 * N_GROUPS, GROUP_W // LANES, tm, LANES), F32)],
        compiler_params=pltpu.CompilerParams(
            dimension_semantics=("arbitrary", "arbitrary"), vmem_limit_bytes=VMEM_LIMIT),
        name="inproj_seq",
    )(x, *wts, c0, ra, rm, rp)


def _inproj_step_kernel(x_ref, nmix_ref, waqkv_ref, wmqk_ref, wmv_ref, wmo_ref, wga_ref, wgb_ref,
                        wgc_ref, wgr_ref, bc_ref, br_ref, cw_ref, cb_ref, cs_ref, ra_ref, rm_ref, rp_ref,
                        qa_ref, kv0_ref, kv1_ref, kv2_ref, qm_ref, km_ref, vm_ref, og_ref, ga_ref, gb_ref,
                        gc_ref, cn_ref):
    del wgr_ref, br_ref
    xn = _rms(x_ref[...], nmix_ref[...]).astype(BF16)
    ra, rm, rp = ra_ref[...], rm_ref[...], rp_ref[...]
    kv_refs = (kv0_ref, kv1_ref, kv2_ref)
    for g in range(N_GROUPS):
        cs = slice(g * GROUP_W, (g + 1) * GROUP_W)
        zq = _wdot(xn, waqkv_ref, OFF_AQ + g * GROUP_W, OFF_AQ + (g + 1) * GROUP_W)
        qa_ref[:, cs] = _rope2(zq, ra, rm, rp) * (HEAD_DIM ** -0.5)
        zk = _wdot(xn, waqkv_ref, OFF_AK + g * GROUP_W, OFF_AK + (g + 1) * GROUP_W)
        kv_refs[g][:, :GROUP_W] = _rope2(zk, ra, rm, rp)
        kv_refs[g][:, GROUP_W:] = _wdot(xn, waqkv_ref, OFF_AV + g * GROUP_W, OFF_AV + (g + 1) * GROUP_W)

    for c in range(2 * M_INNER // 256):
        cs = slice(c * 256, (c + 1) * 256)
        pre = _wdot(xn, wmqk_ref, cs.start, cs.stop)
        y = cb_ref[:, cs] + cw_ref[CONV_W - 1:CONV_W, cs] * pre
        for j in range(CONV_W - 1):
            y = y + cw_ref[j:j + 1, cs] * cs_ref[j, :, cs]
        for j in range(CONV_W - 2):
            cn_ref[j, :, cs] = cs_ref[j + 1, :, cs]
        cn_ref[CONV_W - 2, :, cs] = pre
        s = _silu_h(y)
        if c < M_INNER // 256:
            qm_ref[:, cs] = s
        else:
            km_ref[:, c * 256 - M_INNER:(c + 1) * 256 - M_INNER] = s * (M_DK ** -0.5)

    for c in range(M_INNER // 256):
        cs = slice(c * 256, (c + 1) * 256)
        vm_ref[:, cs] = _wdot(xn, wmv_ref, cs.start, cs.stop)
        og_ref[:, cs] = _sigmoid_h(_wdot(xn, wmo_ref, cs.start, cs.stop))
        ga_ref[:, cs] = _sigmoid_h(_wdot(xn, wga_ref, cs.start, cs.stop))
        gb_ref[:, cs] = _sigmoid_h(_wdot(xn, wgb_ref, cs.start, cs.stop))
    gc_ref[...] = _gate_cols(_wdot(xn, wgc_ref, 0, LANES) + bc_ref[...])


def _inproj_step(x, conv_state, tables, wts):
    nb = x.shape[0]
    sds = lambda w: jax.ShapeDtypeStruct((nb, w), F32)
    out_shape = ([sds(A_QKV)] + [sds(2 * GROUP_W)] * N_GROUPS + [sds(M_INNER)] * 6 + [sds(LANES)]
                 + [jax.ShapeDtypeStruct((CONV_W - 1, nb, 2 * M_INNER), F32)])
    return pl.pallas_call(
        _inproj_step_kernel,
        out_shape=out_shape,
        compiler_params=pltpu.CompilerParams(vmem_limit_bytes=VMEM_LIMIT),
        name="inproj_step",
    )(x, *wts, conv_state, *tables)


def _cast_scaled(w_ref, wb_ref, scale):
    w = w_ref[...]
    wb_ref[...] = pltpu.bitcast((w if scale == 1.0 else w * scale).astype(BF16), jnp.uint32)


def _unpack(w):
    return pltpu.bitcast(w, BF16)


def _attn_seq_kernel(q_ref, k_ref, v_ref, kp_ref, vp_ref, w_ref, o_ref, lse_ref, wb_ref, *, tq, rpb, w_scale):
    _cast_scaled(w_ref, wb_ref, w_scale)
    i = pl.program_id(2)
    lane_head = lax.broadcasted_iota(jnp.int32, (1, GROUP_W), 1) // HEAD_DIM
    head_mask = [jnp.where(lane_head == h, 1.0, 0.0).astype(BF16) for h in range(HPG)]
    lane_head_full = lax.broadcasted_iota(jnp.int32, (BAND, GROUP_W), 1) // HEAD_DIM
    qi = lax.broadcasted_iota(jnp.int32, (BAND, 2 * BAND), 0)
    kj = lax.broadcasted_iota(jnp.int32, (BAND, 2 * BAND), 1)
    band = (kj >= qi) & (kj <= qi + BAND)
    first_lo = jnp.where(i > 0, 0, BAND)
    bias = jnp.where(band, 0.0, NEG)
    bias_first = jnp.where(band & (kj >= first_lo), 0.0, NEG)

    for rr, j in [(rr, j) for rr in range(rpb) for j in range(tq // BAND)]:
        ls = slice(rr * GROUP_W, (rr + 1) * GROUP_W)
        b1 = bias_first if j == 0 else bias
        q = q_ref[0, j * BAND:(j + 1) * BAND, ls]
        if j == 0:
            kk = jnp.concatenate([kp_ref[0, :, ls], k_ref[0, 0:BAND, ls]], axis=0)
            vv = jnp.concatenate([vp_ref[0, :, ls], v_ref[0, 0:BAND, ls]], axis=0)
        else:
            kk = k_ref[0, (j - 1) * BAND:(j + 1) * BAND, ls]
            vv = v_ref[0, (j - 1) * BAND:(j + 1) * BAND, ls]
        qblk = jnp.concatenate([q * head_mask[h] for h in range(HPG)], axis=0)
        s = lax.dot_general(qblk, kk, _NT, preferred_element_type=F32) + jnp.concatenate([b1] * HPG, axis=0)
        mx = jnp.max(s, axis=-1, keepdims=True)
        e = jnp.exp(s - mx)
        den = jnp.sum(e, axis=-1, keepdims=True)
        pv = _dot(e.astype(BF16), vv) * (1.0 / den)
        lse_rows = mx + jnp.log(den)
        o = pv[0:BAND]
        lse = jnp.broadcast_to(lse_rows[0:BAND], (BAND, GROUP_W))
        for h in range(1, HPG):
            own = lane_head_full == h
            o = jnp.where(own, pv[h * BAND:(h + 1) * BAND], o)
            lse = jnp.where(own, lse_rows[h * BAND:(h + 1) * BAND], lse)
        o_ref[0, j * BAND:(j + 1) * BAND, ls] = o
        lse_ref[0, j * BAND:(j + 1) * BAND, ls] = lse


def _attn_seq(q, k, v, dil, w, w_scale):
    B, nsub, _ = q.shape
    tq = min(ATT_TQ, nsub)
    rpb = min(dil, ATT_TQ // tq)
    g1, g2 = dil // rpb, nsub // tq
    w_rows = w.shape[0] // (B * g1 * g2)
    assert w_rows * B * g1 * g2 == w.shape[0] and w_rows % 16 == 0
    w_spec = pl.BlockSpec((w_rows, w.shape[1]), lambda b, r, i: ((b * g1 + r) * g2 + i, 0))
    wb_spec = pl.BlockSpec((w_rows // 2, w.shape[1]), lambda b, r, i: ((b * g1 + r) * g2 + i, 0))
    in_spec = pl.BlockSpec((1, tq, rpb * GROUP_W), lambda b, r, i: (b, i, r))
    prev_spec = pl.BlockSpec((1, BAND, rpb * GROUP_W), lambda b, r, i: (b, jnp.maximum(i * (tq // BAND) - 1, 0), r))
    out_spec = pl.BlockSpec((1, tq, rpb * GROUP_W), lambda b, r, i: (b, i, r))
    return pl.pallas_call(
        functools.partial(_attn_seq_kernel, tq=tq, rpb=rpb, w_scale=w_scale),
        grid=(B, g1, g2),
        in_specs=[in_spec] * 3 + [prev_spec] * 2 + [w_spec],
        out_specs=[out_spec] * 2 + [wb_spec],
        out_shape=([jax.ShapeDtypeStruct((B, nsub, dil * GROUP_W), F32)] * 2
                   + [jax.ShapeDtypeStruct((w.shape[0] // 2, w.shape[1]), jnp.uint32)]),
        compiler_params=pltpu.CompilerParams(
            dimension_semantics=("arbitrary", "arbitrary", "arbitrary"), vmem_limit_bytes=VMEM_LIMIT),
        name=f"attn_seq_d{dil}",
    )(q, k, v, k, v, w)


N_SIDE_CASTS = 5


def _mlstm_seq_kernel(q_ref, k_ref, v_ref, og_ref, gc_ref, gr_ref, mhn_ref, *rest, L, G, side_scales):
    w_refs = rest[:N_SIDE_CASTS]
    ob_ref, c_out, n_out, m_out = rest[N_SIDE_CASTS:N_SIDE_CASTS + 4]
    wb_refs = rest[N_SIDE_CASTS + 4:2 * N_SIDE_CASTS + 4]
    c_s, n_s, m_s = rest[2 * N_SIDE_CASTS + 4:]
    for w_ref, wb_ref, scale in zip(w_refs, wb_refs, side_scales):
        _cast_scaled(w_ref, wb_ref, scale)
    c = pl.program_id(1)

    @pl.when(c == 0)
    def _():
        c_s[...] = jnp.zeros(c_s.shape, F32)
        n_s[...] = jnp.zeros(n_s.shape, F32)
        m_s[...] = jnp.zeros(m_s.shape, F32)

    ti = lax.broadcasted_iota(jnp.int32, (L, L), 0)
    si = lax.broadcasted_iota(jnp.int32, (L, L), 1)
    tril = ti >= si
    lane = lax.broadcasted_iota(jnp.int32, (1, LANES), 1)
    m_vec = [jnp.zeros((1, LANES), F32)] * G

    for g, h in [(g, h) for g in range(G) for h in range(M_HEADS)]:
        gc = gc_ref[g]
        gr = gr_ref[g]
        hs = slice(h * M_DK, (h + 1) * M_DK)
        b_col = gc[:, M_HEADS + h:M_HEADS + h + 1]
        i_col = gc[:, h:h + 1]
        b_row = gr[M_HEADS + h:M_HEADS + h + 1, :]
        i_row = gr[h:h + 1, :]
        m_prev = m_s[g, h, 0:1, 0:1]
        qh = q_ref[g, :, hs]
        kh = k_ref[g, :, hs]
        vh = v_ref[g, :, hs]
        c_prev = c_s[g, h]
        n_prev = n_s[g, h]

        dmat = jnp.where(tril, b_col - b_row + i_row, -jnp.inf)
        m_inter = b_col + m_prev
        m_t = jnp.maximum(m_inter, jnp.max(dmat, axis=-1, keepdims=True))
        w_intra = lax.dot_general(qh, kh, _NT, preferred_element_type=F32) * jnp.exp(dmat - m_t)
        w_inter = jnp.exp(m_inter - m_t)
        num = _dot(w_intra.astype(BF16), vh) + w_inter * _dot(qh, c_prev.astype(BF16))
        den = (jnp.sum(w_intra, axis=-1, keepdims=True)
               + w_inter * jnp.sum(qh.astype(F32) * n_prev, axis=-1, keepdims=True))
        hh = num / jnp.maximum(jnp.abs(den), jnp.exp(-m_t))
        hn = hh * lax.rsqrt(jnp.mean(hh * hh, axis=-1, keepdims=True) + EPS) * mhn_ref[h:h + 1, :]
        ob_ref[g, :, hs] = (hn * og_ref[g, :, hs].astype(F32)).astype(BF16)

        b_last = b_row[:, L - 1:L]
        m_new = jnp.maximum(b_last + m_prev, jnp.max(b_last - b_row + i_row, axis=-1, keepdims=True))
        wk = jnp.exp(b_last - b_col + i_col - m_new)
        decay = jnp.exp(b_last + m_prev - m_new)
        ks = kh.astype(F32) * wk
        c_new = decay * c_prev + lax.dot_general(ks.astype(BF16), vh, _TN, preferred_element_type=F32)
        n_new = decay * n_prev + jnp.sum(ks, axis=0, keepdims=True)
        c_s[g, h] = c_new
        n_s[g, h] = n_new
        m_s[g, h] = jnp.broadcast_to(m_new, (SUBLANES, LANES))
        c_out[g, h] = c_new
        n_out[g, h:h + 1, :] = n_new
        m_vec[g] = jnp.where(lane == h, m_new, m_vec[g])
    for g in range(G):
        m_out[g] = m_vec[g]


def _mlstm_seq(qm, km, vm, og, gc, gr, mhn, side_w, side_scales):
    B, S, _ = qm.shape
    L = M_CHUNK
    G = MLSTM_G
    nc = S // L
    steps = (B // G) * nc
    assert len(side_w) == N_SIDE_CASTS and all(w.shape[0] % (16 * steps) == 0 for w in side_w)
    w_specs = [pl.BlockSpec((w.shape[0] // steps, w.shape[1]), lambda b, c: (b * nc + c, 0)) for w in side_w]
    wb_specs = [pl.BlockSpec((w.shape[0] // steps // 2, w.shape[1]), lambda b, c: (b * nc + c, 0)) for w in side_w]
    tok = pl.BlockSpec((G, L, M_INNER), lambda b, c: (b, c, 0))
    return pl.pallas_call(
        functools.partial(_mlstm_seq_kernel, L=L, G=G, side_scales=tuple(side_scales)),
        grid=(B // G, nc),
        in_specs=[tok, tok, tok, tok,
                  pl.BlockSpec((G, L, LANES), lambda b, c: (b, c, 0)),
                  pl.BlockSpec((G, SUBLANES, L), lambda b, c: (b, 0, c)),
                  pl.BlockSpec((M_HEADS, M_DK), lambda b, c: (0, 0))] + w_specs,
        out_specs=[tok,
                   pl.BlockSpec((G, M_HEADS, M_DK, M_DK), lambda b, c: (b, 0, 0, 0)),
                   pl.BlockSpec((G, M_HEADS, M_DK), lambda b, c: (b, 0, 0)),
                   pl.BlockSpec((G, 1, LANES), lambda b, c: (b, 0, 0))] + wb_specs,
        out_shape=[jax.ShapeDtypeStruct((B, S, M_INNER), BF16),
                   jax.ShapeDtypeStruct((B, M_HEADS, M_DK, M_DK), F32),
                   jax.ShapeDtypeStruct((B, M_HEADS, M_DK), F32),
                   jax.ShapeDtypeStruct((B, 1, LANES), F32)]
                  + [jax.ShapeDtypeStruct((w.shape[0] // 2, w.shape[1]), jnp.uint32) for w in side_w],
        scratch_shapes=[pltpu.VMEM((G, M_HEADS, M_DK, M_DK), F32),
                        pltpu.VMEM((G, M_HEADS, 1, M_DK), F32),
                        pltpu.VMEM((G, M_HEADS, SUBLANES, LANES), F32)],
        compiler_params=pltpu.CompilerParams(
            dimension_semantics=("arbitrary", "arbitrary"), vmem_limit_bytes=VMEM_LIMIT),
        name="mlstm_seq",
    )(qm, km, vm, og, gc, gr, mhn, *side_w)


def _post_kernel(x_ref, o0_ref, o1_ref, o2_ref, l0_ref, l1_ref, l2_ref, ob_ref, ga_ref, gb_ref, p_ref,
                 wpa_ref, wpb_ref, wout_ref, nffn_ref, wg_ref, wu_ref, wd_ref, nple_ref, wpg_ref, wpp_ref,
                 nfin_ref, y_ref, *ubuf, dils):
    lead = (0,) * (len(x_ref.shape) - 2)
    rd = lambda r: r[lead] if lead else r[...]
    tm = x_ref.shape[-2]

    def undilate(ref, dil, slot):
        if dil == 1:
            return rd(ref)
        halves = range(GROUP_W // LANES)
        for r in range(dil):
            for c in halves:
                lo = r * GROUP_W + c * LANES
                ubuf[0][slot, c, pl.ds(r, tm // dil, stride=dil), :] = ref[0, :, lo:lo + LANES]
        return jnp.concatenate([ubuf[0][slot, c] for c in halves], axis=1)

    l0, l1, l2 = (undilate(r, d, 2 * g) for g, (r, d) in enumerate(zip((l0_ref, l1_ref, l2_ref), dils)))
    o0, o1, o2 = (undilate(r, d, 2 * g + 1) for g, (r, d) in enumerate(zip((o0_ref, o1_ref, o2_ref), dils)))
    lmax = jnp.maximum(jnp.maximum(l0, l1), l2)
    e0, e1, e2 = jnp.exp(l0 - lmax), jnp.exp(l1 - lmax), jnp.exp(l2 - lmax)
    o_a = (e0 * o0 + e1 * o1 + e2 * o2) / (e0 + e1 + e2)

    mix = (rd(ga_ref).astype(F32) * _dot(o_a.astype(BF16), _unpack(wpa_ref[...]))
           + rd(gb_ref).astype(F32) * _dot(rd(ob_ref).astype(BF16), _unpack(wpb_ref[...])))
    h = rd(x_ref) + _dot(mix.astype(BF16), _unpack(wout_ref[...]))

    xf = _rms(h, nffn_ref[...]).astype(BF16)
    for lo, hi in FFN_CHUNKS:
        cs = slice(lo, hi)
        gate = _dot(xf, _unpack(wg_ref[:, cs]))
        act = _silu_h(gate) * _dot(xf, _unpack(wu_ref[:, cs]))
        h = h + _dot(act.astype(BF16), _unpack(wd_ref[lo // 2:hi // 2, :]))

    xp = _rms(h, nple_ref[...]).astype(BF16)
    h = h + _sigmoid_h(_dot(xp, _unpack(wpg_ref[...]))) * _dot(rd(p_ref).astype(BF16), _unpack(wpp_ref[...]))
    y = _rms(h, nfin_ref[...])
    if lead:
        y_ref[lead] = y
    else:
        y_ref[...] = y


def _post_step(x, os_, ls_, ob, ga, gb, p, wts):
    return pl.pallas_call(
        functools.partial(_post_kernel, dils=(1,) * N_GROUPS),
        out_shape=jax.ShapeDtypeStruct(x.shape, F32),
        compiler_params=pltpu.CompilerParams(vmem_limit_bytes=VMEM_LIMIT),
        name="post_step",
    )(x, *os_, *ls_, ob, ga, gb, p, *wts)


SM_ROWS = 16
SM_LANES = 512
T_QA, T_KN, T_VN, T_QM, T_KM, T_VM, T_OG, T_N, T_GATE = range(9)
SO_ROWS = 8
SO_OB, SO_N, SO_M, SO_LSE, SO_O = range(5)
HALF_HEADS = HPG // 2
HALF_W = HALF_HEADS * HEAD_DIM


def _eye(n):
    return lax.broadcasted_iota(jnp.int32, (n, n), 0) == lax.broadcasted_iota(jnp.int32, (n, n), 1)


def _col_of(row, eye):
    return jnp.sum(jnp.where(eye, jnp.broadcast_to(row, eye.shape), 0.0), axis=1, keepdims=True)


def _row_of(col, eye):
    return jnp.sum(jnp.where(eye, jnp.broadcast_to(col, eye.shape), 0.0), axis=0, keepdims=True)


def _round_bf16(x):
    return x.astype(BF16).astype(F32)


def _sample_step(tile, kv_refs, cst_ref, mhn_ref, so_ref, kvn_refs, cso_ref):
    so_ref[0] = jnp.zeros((SO_ROWS, SM_LANES), F32)
    eye_a = _eye(HALF_W)
    eye_m = _eye(M_DK)

    for g, (w, dil) in enumerate(DSWA_GROUPS):
        ls = slice(g * HALF_W, (g + 1) * HALF_W)
        k_t = kv_refs[g][0, 0, 0]
        v_t = kv_refs[g][0, 1, 0]
        k_row = tile[T_KN:T_KN + 1, ls]
        v_row = tile[T_VN:T_VN + 1, ls]
        qb = _round_bf16(tile[T_QA:T_QA + 1, ls])
        prod = k_t * _col_of(qb, eye_a)
        qk_new = qb * _round_bf16(k_row)
        pos = lax.broadcasted_iota(jnp.int32, (1, w), 1)
        pe, pn, lse = [], [], []
        for hl in range(HALF_HEADS):
            hd = slice(hl * HEAD_DIM, (hl + 1) * HEAD_DIM)
            s = jnp.where(pos % dil == 0, jnp.sum(prod[hd, :], axis=0, keepdims=True), NEG)
            s_new = jnp.sum(qk_new[:, hd], axis=-1, keepdims=True)
            mx = jnp.maximum(jnp.max(s, axis=-1, keepdims=True), s_new)
            e = jnp.exp(s - mx)
            e_new = jnp.exp(s_new - mx)
            den = jnp.sum(e, axis=-1, keepdims=True) + e_new
            pe.append(jnp.broadcast_to(e / den, (HEAD_DIM, w)))
            pn.append(jnp.broadcast_to(e_new / den, (1, HEAD_DIM)))
            lse.append(jnp.broadcast_to(mx + jnp.log(den), (1, HEAD_DIM)))
        o_col = jnp.sum(v_t * jnp.concatenate(pe, axis=0), axis=-1, keepdims=True)
        so_ref[0, SO_O:SO_O + 1, ls] = _row_of(o_col, eye_a) + jnp.concatenate(pn, axis=1) * _round_bf16(v_row)
        so_ref[0, SO_LSE:SO_LSE + 1, ls] = jnp.concatenate(lse, axis=1)
        last = lax.broadcasted_iota(jnp.int32, (HALF_W, w), 1) == w - 1
        kvn_refs[g][0, 0, 0] = jnp.where(last, _col_of(k_row, eye_a), pltpu.roll(k_t, w - 1, 1))
        kvn_refs[g][0, 1, 0] = jnp.where(last, _col_of(v_row, eye_a), pltpu.roll(v_t, w - 1, 1))

    eye = eye_m
    for j in range(M_HEADS // 2):
        hs = slice(j * M_DK, (j + 1) * M_DK)
        ig = tile[T_GATE:T_GATE + 1, j:j + 1]
        lf = tile[T_GATE:T_GATE + 1, 2 + j:3 + j]
        m_prev = tile[T_GATE:T_GATE + 1, 4 + j:5 + j]
        qb = _round_bf16(tile[T_QM:T_QM + 1, hs])
        kb = _round_bf16(tile[T_KM:T_KM + 1, hs])
        vb = _round_bf16(tile[T_VM:T_VM + 1, hs])
        n_prev = tile[T_N:T_N + 1, hs]
        c_prev = cst_ref[0, j]

        m_inter = lf + m_prev
        m_t = jnp.maximum(m_inter, ig)
        w_intra = jnp.sum(qb * kb, axis=-1, keepdims=True) * jnp.exp(ig - m_t)
        w_inter = jnp.exp(m_inter - m_t)
        q_c = jnp.sum(c_prev * _col_of(qb, eye), axis=0, keepdims=True)
        num = w_intra * vb + w_inter * q_c
        den = w_intra + w_inter * jnp.sum(qb * n_prev, axis=-1, keepdims=True)
        hh = num / jnp.maximum(jnp.abs(den), jnp.exp(-m_t))
        hn = hh * lax.rsqrt(jnp.mean(hh * hh, axis=-1, keepdims=True) + EPS) * mhn_ref[0, j:j + 1, :]
        ks = kb * jnp.exp(ig - m_t)
        so_ref[0, SO_OB:SO_OB + 1, hs] = hn * tile[T_OG:T_OG + 1, hs]
        so_ref[0, SO_N:SO_N + 1, hs] = w_inter * n_prev + ks
        so_ref[0, SO_M:SO_M + 1, hs] = jnp.broadcast_to(m_t, (1, M_DK))
        cso_ref[0, j] = w_inter * c_prev + _col_of(_round_bf16(ks), eye) * vb


N_POST_IN = 22


def _post_fused_kernel(*refs, dils, ns):
    post_in = refs[:N_POST_IN]
    sm_ref, c0_ref, c1_ref, c2_ref, cst_ref, mhn_ref = refs[N_POST_IN:N_POST_IN + 6]
    y_ref, so_ref, cn0_ref, cn1_ref, cn2_ref, cso_ref = refs[N_POST_IN + 6:N_POST_IN + 12]
    (ubuf,) = refs[N_POST_IN + 12:]
    _post_kernel(*post_in, y_ref, ubuf, dils=dils)
    _sample_step(sm_ref[0], (c0_ref, c1_ref, c2_ref), cst_ref, mhn_ref, so_ref,
                 (cn0_ref, cn1_ref, cn2_ref), cso_ref)


def _post_fused(x, os_, ls_, ob, ga, gb, p, wts, sm_in, caches_t, c_state, mhn):
    B, S, _ = x.shape
    NB = c_state.shape[0]
    tm = TM_POST
    ns = S // tm
    assert B * ns == 2 * NB, "one half of a sample batch element per post-mixer grid step"
    tok = lambda w: pl.BlockSpec((1, tm, w), lambda b, i: (b, i, 0))
    dils = tuple(dil for _, dil in DSWA_GROUPS)
    dil_specs = [pl.BlockSpec((1, tm // dil, dil * GROUP_W), lambda b, i: (b, i, 0)) for dil in dils]
    step = lambda b, i: b * ns + i
    cache_specs = [pl.BlockSpec((1, 2, 1, HALF_W, c.shape[-1]), lambda b, i: (step(b, i) // 2, 0, step(b, i) % 2, 0, 0))
                   for c in caches_t]
    cst_spec = pl.BlockSpec((1, M_HEADS // 2, M_DK, M_DK), lambda b, i: (step(b, i) // 2, step(b, i) % 2, 0, 0))
    in_specs = ([tok(D_MODEL)] + dil_specs * 2 + [tok(M_INNER)] * 3 + [tok(PLE_DIM)]
                + [_const_spec(w.shape) for w in wts]
                + [pl.BlockSpec((1, SM_ROWS, SM_LANES), lambda b, i: (step(b, i), 0, 0))]
                + cache_specs
                + [cst_spec, pl.BlockSpec((1, M_HEADS // 2, M_DK), lambda b, i: (step(b, i) % 2, 0, 0))])
    out_specs = ([tok(D_MODEL), pl.BlockSpec((1, SO_ROWS, SM_LANES), lambda b, i: (step(b, i), 0, 0))]
                 + cache_specs + [cst_spec])
    out_shape = ([jax.ShapeDtypeStruct((B, S, D_MODEL), F32),
                  jax.ShapeDtypeStruct((2 * NB, SO_ROWS, SM_LANES), F32)]
                 + [jax.ShapeDtypeStruct(c.shape, F32) for c in caches_t]
                 + [jax.ShapeDtypeStruct(c_state.shape, F32)])
    return pl.pallas_call(
        functools.partial(_post_fused_kernel, dils=dils, ns=ns),
        grid=(B, ns),
        in_specs=in_specs,
        out_specs=out_specs,
        out_shape=out_shape,
        scratch_shapes=[pltpu.VMEM((2 * N_GROUPS, GROUP_W // LANES, tm, LANES), F32)],
        compiler_params=pltpu.CompilerParams(
            dimension_semantics=("arbitrary", "arbitrary"), vmem_limit_bytes=VMEM_LIMIT),
        name="post_seq_sample_stream",
    )(x, *os_, *ls_, ob, ga, gb, p, *wts, sm_in, *caches_t, c_state, mhn)


def _rope_tables(pos):
    pos = np.asarray(pos, np.float64)
    t = pos.shape[0]
    inv = np.power(ROPE_THETA, -np.arange(ROPE_HALF, dtype=np.float64) / ROPE_HALF)
    ang = pos[:, None] * inv[None, :]
    cos, sin = np.cos(ang), np.sin(ang)
    one = np.ones((t, HEAD_DIM - ROPE_DIM))
    zero = np.zeros((t, HEAD_DIM - ROPE_DIM))
    z8 = np.zeros((t, ROPE_HALF))
    a = np.concatenate([cos, cos, one], axis=1)
    m = np.concatenate([-sin, z8, zero], axis=1)
    p = np.concatenate([z8, sin, zero], axis=1)
    rep = LANES // HEAD_DIM
    return tuple(jnp.asarray(np.tile(v, (1, rep)).astype(np.float32)) for v in (a, m, p))


def kernel(x_prompt, x_sample, cache_kv_w128, cache_kv_w512, cache_kv_w2048, state_conv, state_C, state_n,
           state_m, p_prompt, p_sample, norm_mix, w_in, conv_w, conv_b, b_igate, b_fgate, mh_norm, w_proj_a,
           w_proj_b, w_out, norm_ffn, w_gate, w_up, w_down, norm_ple, w_ple_gate, w_ple_proj, norm_final):
    depth = w_in.shape[0]
    assert depth == 1, "single trunk layer"
    B, S, _ = x_prompt.shape
    NB = x_sample.shape[0]
    assert x_sample.shape[1] == 1
    caches = (cache_kv_w128, cache_kv_w512, cache_kv_w2048)
    assert S >= DSWA_GROUPS[-1][0] and S % (BAND * DSWA_GROUPS[-1][1]) == 0
    past = 0
    l = 0

    row_scale = np.ones((IN_COLS, 1), np.float32)
    row_scale[OFF_MO:OFF_MI] = 0.5
    row_scale[OFF_GA:IN_COLS] = 0.5
    wt = (w_in[l].T * row_scale).astype(BF16)
    gate_wt = wt[OFF_MI:OFF_GA]
    gate_b = jnp.concatenate([b_igate[l], b_fgate[l]])
    in_wts = (
        norm_mix[l][None, :],
        wt[OFF_AQ:OFF_MQ],
        wt[OFF_MQ:OFF_MV],
        wt[OFF_MV:OFF_MO],
        wt[OFF_MO:OFF_MI],
        wt[OFF_GA:OFF_GB],
        wt[OFF_GB:IN_COLS],
        jnp.pad(gate_wt, ((0, LANES - 2 * M_HEADS), (0, 0))),
        gate_wt,
        jnp.pad(gate_b, (0, LANES - 2 * M_HEADS))[None, :],
        gate_b[:, None],
        0.5 * conv_w[l],
        0.5 * conv_b[l][None, :],
    )
    mhn = mh_norm[l]

    c0 = jnp.zeros((B, SUBLANES, 2 * M_INNER), F32)
    tabs_p = _rope_tables(past + np.arange(S))
    (*qkv, qm, km, vm, og, ga, gb, gc, gr, kv0, kv1, kv2, cn) = _inproj_seq(x_prompt, c0, tabs_p, in_wts)
    os_p, ls_p, ffn_bf16 = [], [], []
    ffn_w = ((w_gate[l], 0.5), (w_up[l], 1.0), (w_down[l], 1.0))
    for g, ((_, dil), (w_ffn, w_scale)) in enumerate(zip(DSWA_GROUPS, ffn_w)):
        o, lse, wb = _attn_seq(*qkv[3 * g:3 * g + 3], dil, w_ffn, w_scale)
        os_p.append(o)
        ls_p.append(lse)
        ffn_bf16.append(wb)
    ob, c_p, n_p, m_p, wpa, wpb, wo, wpg, wpp = _mlstm_seq(
        qm, km, vm, og, gc, gr, mhn, (w_proj_a[l], w_proj_b[l], w_out[l], w_ple_gate[l], w_ple_proj[l]),
        (1.0, 1.0, 1.0, 0.5, 1.0))
    post_wts = (wpa, wpb, wo, norm_ffn[l][None, :], *ffn_bf16, norm_ple[l][None, :], wpg, wpp, norm_final[None, :])

    tabs_s = _rope_tables(np.full((NB,), PAST_LEN))
    conv_s = jnp.swapaxes(state_conv[l], 0, 1)
    (qa_s, kvn0, kvn1, kvn2, qm_s, km_s, vm_s, og_s, ga_s, gb_s, gc_s, cn_s) = _inproj_step(
        x_sample[:, 0, :], conv_s, tabs_s, in_wts)
    pad_row = lambda t: jnp.pad(t, ((0, 0), (0, SM_LANES - t.shape[1])))
    att_half = lambda t: pad_row(t.reshape(NB, N_GROUPS, 2, HALF_W).transpose(0, 2, 1, 3).reshape(2 * NB, -1))
    mem_half = lambda t: t.reshape(2 * NB, SM_LANES)
    k_new = jnp.concatenate([kv[:, :GROUP_W] for kv in (kvn0, kvn1, kvn2)], axis=1)
    v_new = jnp.concatenate([kv[:, GROUP_W:] for kv in (kvn0, kvn1, kvn2)], axis=1)
    gates = jnp.concatenate([t.reshape(2 * NB, M_HEADS // 2) for t in
                             (gc_s[:, :M_HEADS], gc_s[:, M_HEADS:2 * M_HEADS], state_m[l])], axis=1)
    sm_rows = [att_half(qa_s), att_half(k_new), att_half(v_new), mem_half(qm_s), mem_half(km_s), mem_half(vm_s),
               mem_half(og_s), mem_half(state_n[l].reshape(NB, M_INNER)), pad_row(gates)]
    sm_rows += [jnp.zeros((2 * NB, SM_LANES), F32)] * (SM_ROWS - len(sm_rows))
    sm_in = jnp.stack(sm_rows, axis=1)
    caches_t = [jnp.transpose(c[l], (0, 2, 3, 4, 1)).reshape(NB, 2, 2, HALF_W, c.shape[2]) for c in caches]
    for c, (w, dil) in zip(caches_t, DSWA_GROUPS):
        assert c.shape[-1] == w == BAND * dil, "each cache must hold exactly one full window"
    y_prompt, so, cn0, cn1, cn2, c_s = _post_fused(x_prompt, os_p, ls_p, ob, ga, gb, p_prompt[l], post_wts,
                                                   sm_in, caches_t, state_C[l], mhn.reshape(2, M_HEADS // 2, M_DK))
    kv_s = [jnp.transpose(c.reshape(NB, 2, HPG, HEAD_DIM, -1), (0, 4, 1, 2, 3)) for c in (cn0, cn1, cn2)]
    att_full = lambda t: t[:, :N_GROUPS * HALF_W].reshape(NB, 2, N_GROUPS, HALF_W).transpose(0, 2, 1, 3).reshape(
        NB, N_GROUPS, GROUP_W)
    o_s, l_s = att_full(so[:, SO_O, :]), att_full(so[:, SO_LSE, :])
    os_s = [o_s[:, g, :] for g in range(N_GROUPS)]
    ls_s = [l_s[:, g, :] for g in range(N_GROUPS)]
    ob_s = so[:, SO_OB, :].reshape(NB, M_INNER)
    n_s = so[:, SO_N, :].reshape(NB, M_HEADS, M_DK)
    m_s = so[:, SO_M, ::M_DK].reshape(NB, M_HEADS)
    y_sample = _post_step(x_sample[:, 0, :], os_s, ls_s, ob_s, ga_s, gb_s, p_sample[l][:, 0, :], post_wts)

    kv_shape = lambda n, w: (1, n, w, 2, HPG, HEAD_DIM)
    return (
        y_prompt, y_sample[:, None, :],
        kv0.reshape(kv_shape(B, DSWA_GROUPS[0][0])), kv_s[0].reshape(kv_shape(NB, DSWA_GROUPS[0][0])),
        kv1.reshape(kv_shape(B, DSWA_GROUPS[1][0])), kv_s[1].reshape(kv_shape(NB, DSWA_GROUPS[1][0])),
        kv2.reshape(kv_shape(B, DSWA_GROUPS[2][0])), kv_s[2].reshape(kv_shape(NB, DSWA_GROUPS[2][0])),
        cn[:, SUBLANES - (CONV_W - 1):, :][None], jnp.swapaxes(cn_s, 0, 1)[None],
        c_p[None], c_s[None],
        n_p[None], n_s[None],
        m_p[:, 0, :M_HEADS][None], m_s[None],
    )
```

```python
import functools

import jax
import jax.numpy as jnp
import numpy as np
from jax import lax
from jax.experimental import pallas as pl
from jax.experimental.pallas import tpu as pltpu

F32 = jnp.float32
BF16 = jnp.bfloat16

D_MODEL = 1024
PAST_LEN = 16384
HEAD_DIM = 64
ROPE_DIM = HEAD_DIM // 4
ROPE_HALF = ROPE_DIM // 2
ROPE_THETA = 500000.0
DSWA_GROUPS = ((128, 1), (512, 4), (2048, 16))
HPG = 4
N_GROUPS = len(DSWA_GROUPS)
GROUP_W = HPG * HEAD_DIM
A_QKV = N_GROUPS * GROUP_W
BAND = 128
M_HEADS = 4
M_INNER = D_MODEL
M_DK = M_INNER // M_HEADS
CONV_W = 4
D_FF = 2816
PLE_DIM = 256
EPS = 1e-6
NEG = -1e30

OFF_AQ = 0
OFF_AK = OFF_AQ + A_QKV
OFF_AV = OFF_AK + A_QKV
OFF_MQ = OFF_AV + A_QKV
OFF_MK = OFF_MQ + M_INNER
OFF_MV = OFF_MK + M_INNER
OFF_MO = OFF_MV + M_INNER
OFF_MI = OFF_MO + M_INNER
OFF_MF = OFF_MI + M_HEADS
OFF_GA = OFF_MF + M_HEADS
OFF_GB = OFF_GA + D_MODEL
IN_COLS = OFF_GB + D_MODEL

LANES = 128
SUBLANES = 8
VMEM_LIMIT = 56 * 1024 * 1024

TM_IN = 512
TM_POST = 256
M_CHUNK = 512
CUM_BLOCK = 256
MLSTM_G = 2
ATT_TQ = 4096
MXU_DIM = 256
FFN_CHUNKS = ((0, 4 * MXU_DIM), (4 * MXU_DIM, 8 * MXU_DIM), (8 * MXU_DIM, D_FF))

_NT = (((1,), (1,)), ((), ()))
_TN = (((0,), (0,)), ((), ()))


def _dot(a, b):
    return jnp.dot(a, b, preferred_element_type=F32)


def _wdot(a, wt_ref, lo, hi):
    return lax.dot_general(a, wt_ref[lo:hi, :], _NT, preferred_element_type=F32)


def _rms(x, g):
    return x * lax.rsqrt(jnp.mean(x * x, axis=-1, keepdims=True) + EPS) * g


def _bf16_pieces(x):
    hi = x.astype(BF16)
    r1 = x - hi.astype(F32)
    mid = r1.astype(BF16)
    lo = (r1 - mid.astype(F32)).astype(BF16)
    return hi, mid, lo


def _sigmoid_h(xh):
    return 0.5 * jnp.tanh(xh) + 0.5


def _silu_h(xh):
    return xh * (jnp.tanh(xh) + 1.0)


def _log_sigmoid(x):
    return jnp.minimum(x, 0.0) - jnp.log1p(jnp.exp(-jnp.abs(x)))


def _rope(t, ra, rm, rp):
    return t * ra + pltpu.roll(t, LANES - ROPE_HALF, 1) * rm + pltpu.roll(t, ROPE_HALF, 1) * rp


def _rope2(t, ra, rm, rp):
    return jnp.concatenate([_rope(t[:, :LANES], ra, rm, rp), _rope(t[:, LANES:], ra, rm, rp)], axis=1)


def _gate_cols(z):
    lane = lax.broadcasted_iota(jnp.int32, z.shape, 1)
    return jnp.where(lane < M_HEADS, z, _log_sigmoid(z))


def _inproj_seq_kernel(x_ref, nmix_ref, waqkv_ref, wmqk_ref, wmv_ref, wmo_ref, wga_ref, wgb_ref,
                       wgc_ref, wgr_ref, bc_ref, br_ref, cw_ref, cb_ref, c0_ref, ra_ref, rm_ref, rp_ref,
                       q0_ref, k0_ref, v0_ref, q1_ref, k1_ref, v1_ref, q2_ref, k2_ref, v2_ref,
                       qm_ref, km_ref, vm_ref, og_ref, ga_ref, gb_ref,
                       gc_ref, gr_ref, kv0_ref, kv1_ref, kv2_ref, cn_ref, cbuf, dbuf, *, tm, ns):
    i = pl.program_id(1)
    xn = _rms(x_ref[0], nmix_ref[...]).astype(BF16)
    ra, rm, rp = ra_ref[...], rm_ref[...], rp_ref[...]
    kv_refs = (kv0_ref, kv1_ref, kv2_ref)
    qkv_refs = ((q0_ref, k0_ref, v0_ref), (q1_ref, k1_ref, v1_ref), (q2_ref, k2_ref, v2_ref))

    def put_dilated(ref, val, dil, slot):
        if dil == 1:
            ref[0] = val.astype(BF16)
            return
        for c in range(GROUP_W // LANES):
            dbuf[slot, c] = val[:, c * LANES:(c + 1) * LANES]
        for r in range(dil):
            for c in range(GROUP_W // LANES):
                lo = r * GROUP_W + c * LANES
                ref[0, :, lo:lo + LANES] = dbuf[slot, c, pl.ds(r, tm // dil, stride=dil), :].astype(BF16)

    for g, (win, dil) in enumerate(DSWA_GROUPS):
        zq = _wdot(xn, waqkv_ref, OFF_AQ + g * GROUP_W, OFF_AQ + (g + 1) * GROUP_W)
        put_dilated(qkv_refs[g][0], _rope2(zq, ra, rm, rp) * (HEAD_DIM ** -0.5), dil, 3 * g)
        zk = _wdot(xn, waqkv_ref, OFF_AK + g * GROUP_W, OFF_AK + (g + 1) * GROUP_W)
        kr = _rope2(zk, ra, rm, rp)
        put_dilated(qkv_refs[g][1], kr, dil, 3 * g + 1)
        zv = _wdot(xn, waqkv_ref, OFF_AV + g * GROUP_W, OFF_AV + (g + 1) * GROUP_W)
        put_dilated(qkv_refs[g][2], zv, dil, 3 * g + 2)
        rows = min(win, tm)
        nblk = win // rows

        @pl.when(i >= ns - nblk)
        def _():
            kv_refs[g][0, :, :GROUP_W] = kr[tm - rows:, :]
            kv_refs[g][0, :, GROUP_W:] = zv[tm - rows:, :]

    @pl.when(i == 0)
    def _():
        for blk in range(2 * M_INNER // LANES):
            cbuf[blk, 0:SUBLANES, :] = c0_ref[0, :, blk * LANES:(blk + 1) * LANES]

    for c in range(2 * M_INNER // 256):
        pre = _wdot(xn, wmqk_ref, c * 256, (c + 1) * 256)
        halves = []
        for hf in range(256 // LANES):
            blk = c * (256 // LANES) + hf
            ls = slice(blk * LANES, (blk + 1) * LANES)
            pre_h = pre[:, hf * LANES:(hf + 1) * LANES]
            cbuf[blk, SUBLANES:SUBLANES + tm, :] = pre_h
            y = cb_ref[:, ls] + cw_ref[CONV_W - 1:CONV_W, ls] * pre_h
            for d in range(1, CONV_W):
                y = y + cw_ref[CONV_W - 1 - d:CONV_W - d, ls] * cbuf[blk, pl.ds(SUBLANES - d, tm), :]
            tail = pre_h[tm - SUBLANES:, :]
            cbuf[blk, 0:SUBLANES, :] = tail
            cn_ref[0, :, ls] = tail
            halves.append(_silu_h(y))
        s = jnp.concatenate(halves, axis=1)
        if c < M_INNER // 256:
            qm_ref[0, :, c * 256:(c + 1) * 256] = s.astype(BF16)
        else:
            km_ref[0, :, c * 256 - M_INNER:(c + 1) * 256 - M_INNER] = (s * (M_DK ** -0.5)).astype(BF16)
        cs = slice((c % 2) * 512, (c % 2 + 1) * 512)
        w_ref, o_ref = ((wmv_ref, vm_ref), (wmo_ref, og_ref), (wga_ref, ga_ref), (wgb_ref, gb_ref))[c // 2]
        z = _wdot(xn, w_ref, cs.start, cs.stop)
        o_ref[0, :, cs] = (z if o_ref is vm_ref else _sigmoid_h(z)).astype(BF16)

    gcv = _gate_cols(_wdot(xn, wgc_ref, 0, LANES) + bc_ref[...])
    zr = lax.dot_general(wgr_ref[...], xn, _NT, preferred_element_type=F32) + br_ref[...]
    sub = lax.broadcasted_iota(jnp.int32, zr.shape, 0)
    grv = jnp.where(sub < M_HEADS, zr, _log_sigmoid(zr))
    L = CUM_BLOCK
    ti = lax.broadcasted_iota(jnp.int32, (L, L), 0)
    si = lax.broadcasted_iota(jnp.int32, (L, L), 1)
    lower = jnp.where(ti >= si, 1.0, 0.0).astype(BF16)
    upper = jnp.where(ti <= si, 1.0, 0.0).astype(BF16)
    lane = lax.broadcasted_iota(jnp.int32, (L, LANES), 1)
    sub_l = lax.broadcasted_iota(jnp.int32, (SUBLANES, L), 0)
    for blk in range(tm // L):
        rs = slice(blk * L, (blk + 1) * L)
        cum_c = sum(_dot(lower, p) for p in _bf16_pieces(gcv[rs, :]))
        cum_r = sum(_dot(p, upper) for p in _bf16_pieces(grv[:, rs]))
        if (blk * L) % M_CHUNK:
            cum_c = cum_c + carry_c
            cum_r = cum_r + carry_r
        carry_c, carry_r = cum_c[L - 1:L, :], cum_r[:, L - 1:L]
        gc_ref[0, rs, :] = jnp.where(lane < M_HEADS, gcv[rs, :], cum_c)
        gr_ref[0, :, rs] = jnp.where(sub_l < M_HEADS, grv[:, rs], cum_r)


def _const_spec(shape):
    nd = len(shape)
    return pl.BlockSpec(shape, lambda *_: (0,) * nd, pipeline_mode=pl.Buffered(1))


def _inproj_seq(x, c0, tables, wts):
    B, S, _ = x.shape
    tm = TM_IN
    ns = S // tm
    ra, rm, rp = tables

    def kv_spec(win):
        rows = min(win, tm)
        nblk = win // rows
        return pl.BlockSpec((1, rows, 2 * GROUP_W), lambda b, i: (b, jnp.maximum(i - (ns - nblk), 0), 0))

    tok = lambda w: pl.BlockSpec((1, tm, w), lambda b, i: (b, i, 0))
    tab = pl.BlockSpec((tm, LANES), lambda b, i: (i, 0))
    in_specs = ([tok(D_MODEL)] + [_const_spec(w.shape) for w in wts]
                + [pl.BlockSpec((1, SUBLANES, 2 * M_INNER), lambda b, i: (b, 0, 0)), tab, tab, tab])
    dil_shape = [jax.ShapeDtypeStruct((B, S // dil, dil * GROUP_W), BF16) for _, dil in DSWA_GROUPS for _ in range(3)]
    dil_specs = [pl.BlockSpec((1, tm // dil, dil * GROUP_W), lambda b, i: (b, i, 0))
                 for _, dil in DSWA_GROUPS for _ in range(3)]
    out_shape = (dil_shape
                 + [jax.ShapeDtypeStruct((B, S, M_INNER), BF16)] * 6
                 + [jax.ShapeDtypeStruct((B, S, LANES), F32), jax.ShapeDtypeStruct((B, SUBLANES, S), F32)]
                 + [jax.ShapeDtypeStruct((B, win, 2 * GROUP_W), F32) for win, _ in DSWA_GROUPS]
                 + [jax.ShapeDtypeStruct((B, SUBLANES, 2 * M_INNER), F32)])
    out_specs = (dil_specs + [tok(M_INNER)] * 6
                 + [tok(LANES), pl.BlockSpec((1, SUBLANES, tm), lambda b, i: (b, 0, i))]
                 + [kv_spec(win) for win, _ in DSWA_GROUPS]
                 + [pl.BlockSpec((1, SUBLANES, 2 * M_INNER), lambda b, i: (b, 0, 0))])
    return pl.pallas_call(
        functools.partial(_inproj_seq_kernel, tm=tm, ns=ns),
        grid=(B, ns),
        in_specs=in_specs,
        out_specs=out_specs,
        out_shape=out_shape,
        scratch_shapes=[pltpu.VMEM((2 * M_INNER // LANES, SUBLANES + tm, LANES), F32),
                        pltpu.VMEM((3 * N_GROUPS, GROUP_W // LANES, tm, LANES), F32)],
        compiler_params=pltpu.CompilerParams(
            dimension_semantics=("arbitrary", "arbitrary"), vmem_limit_bytes=VMEM_LIMIT),
        name="inproj_seq",
    )(x, *wts, c0, ra, rm, rp)


def _inproj_step_kernel(x_ref, nmix_ref, waqkv_ref, wmqk_ref, wmv_ref, wmo_ref, wga_ref, wgb_ref,
                        wgc_ref, wgr_ref, bc_ref, br_ref, cw_ref, cb_ref, cs_ref, ra_ref, rm_ref, rp_ref,
                        qa_ref, kv0_ref, kv1_ref, kv2_ref, qm_ref, km_ref, vm_ref, og_ref, ga_ref, gb_ref,
                        gc_ref, cn_ref):
    del wgr_ref, br_ref
    xn = _rms(x_ref[...], nmix_ref[...]).astype(BF16)
    ra, rm, rp = ra_ref[...], rm_ref[...], rp_ref[...]
    kv_refs = (kv0_ref, kv1_ref, kv2_ref)
    for g in range(N_GROUPS):
        cs = slice(g * GROUP_W, (g + 1) * GROUP_W)
        zq = _wdot(xn, waqkv_ref, OFF_AQ + g * GROUP_W, OFF_AQ + (g + 1) * GROUP_W)
        qa_ref[:, cs] = _rope2(zq, ra, rm, rp) * (HEAD_DIM ** -0.5)
        zk = _wdot(xn, waqkv_ref, OFF_AK + g * GROUP_W, OFF_AK + (g + 1) * GROUP_W)
        kv_refs[g][:, :GROUP_W] = _rope2(zk, ra, rm, rp)
        kv_refs[g][:, GROUP_W:] = _wdot(xn, waqkv_ref, OFF_AV + g * GROUP_W, OFF_AV + (g + 1) * GROUP_W)

    for c in range(2 * M_INNER // 256):
        cs = slice(c * 256, (c + 1) * 256)
        pre = _wdot(xn, wmqk_ref, cs.start, cs.stop)
        y = cb_ref[:, cs] + cw_ref[CONV_W - 1:CONV_W, cs] * pre
        for j in range(CONV_W - 1):
            y = y + cw_ref[j:j + 1, cs] * cs_ref[j, :, cs]
        for j in range(CONV_W - 2):
            cn_ref[j, :, cs] = cs_ref[j + 1, :, cs]
        cn_ref[CONV_W - 2, :, cs] = pre
        s = _silu_h(y)
        if c < M_INNER // 256:
            qm_ref[:, cs] = s
        else:
            km_ref[:, c * 256 - M_INNER:(c + 1) * 256 - M_INNER] = s * (M_DK ** -0.5)

    for c in range(M_INNER // 256):
        cs = slice(c * 256, (c + 1) * 256)
        vm_ref[:, cs] = _wdot(xn, wmv_ref, cs.start, cs.stop)
        og_ref[:, cs] = _sigmoid_h(_wdot(xn, wmo_ref, cs.start, cs.stop))
        ga_ref[:, cs] = _sigmoid_h(_wdot(xn, wga_ref, cs.start, cs.stop))
        gb_ref[:, cs] = _sigmoid_h(_wdot(xn, wgb_ref, cs.start, cs.stop))
    gc_ref[...] = _gate_cols(_wdot(xn, wgc_ref, 0, LANES) + bc_ref[...])


def _inproj_step(x, conv_state, tables, wts):
    nb = x.shape[0]
    sds = lambda w: jax.ShapeDtypeStruct((nb, w), F32)
    out_shape = ([sds(A_QKV)] + [sds(2 * GROUP_W)] * N_GROUPS + [sds(M_INNER)] * 6 + [sds(LANES)]
                 + [jax.ShapeDtypeStruct((CONV_W - 1, nb, 2 * M_INNER), F32)])
    return pl.pallas_call(
        _inproj_step_kernel,
        out_shape=out_shape,
        compiler_params=pltpu.CompilerParams(vmem_limit_bytes=VMEM_LIMIT),
        name="inproj_step",
    )(x, *wts, conv_state, *tables)


def _cast_scaled(w_ref, wb_ref, scale):
    w = w_ref[...]
    wb_ref[...] = (w if scale == 1.0 else w * scale).astype(BF16)


def _attn_seq_kernel(q_ref, k_ref, v_ref, kp_ref, vp_ref, w_ref, o_ref, lse_ref, wb_ref, *, tq, rpb, w_scale):
    _cast_scaled(w_ref, wb_ref, w_scale)
    i = pl.program_id(2)
    lane_head = lax.broadcasted_iota(jnp.int32, (1, GROUP_W), 1) // HEAD_DIM
    head_mask = [jnp.where(lane_head == h, 1.0, 0.0).astype(BF16) for h in range(HPG)]
    lane_head_full = lax.broadcasted_iota(jnp.int32, (BAND, GROUP_W), 1) // HEAD_DIM
    qi = lax.broadcasted_iota(jnp.int32, (BAND, 2 * BAND), 0)
    kj = lax.broadcasted_iota(jnp.int32, (BAND, 2 * BAND), 1)
    band = (kj >= qi) & (kj <= qi + BAND)
    first_lo = jnp.where(i > 0, 0, BAND)
    bias = jnp.where(band, 0.0, NEG)
    bias_first = jnp.where(band & (kj >= first_lo), 0.0, NEG)

    for rr, j in [(rr, j) for rr in range(rpb) for j in range(tq // BAND)]:
        ls = slice(rr * GROUP_W, (rr + 1) * GROUP_W)
        b1 = bias_first if j == 0 else bias
        q = q_ref[0, j * BAND:(j + 1) * BAND, ls]
        if j == 0:
            kk = jnp.concatenate([kp_ref[0, :, ls], k_ref[0, 0:BAND, ls]], axis=0)
            vv = jnp.concatenate([vp_ref[0, :, ls], v_ref[0, 0:BAND, ls]], axis=0)
        else:
            kk = k_ref[0, (j - 1) * BAND:(j + 1) * BAND, ls]
            vv = v_ref[0, (j - 1) * BAND:(j + 1) * BAND, ls]
        qblk = jnp.concatenate([q * head_mask[h] for h in range(HPG)], axis=0)
        s = lax.dot_general(qblk, kk, _NT, preferred_element_type=F32) + jnp.concatenate([b1] * HPG, axis=0)
        mx = jnp.max(s, axis=-1, keepdims=True)
        e = jnp.exp(s - mx)
        den = jnp.sum(e, axis=-1, keepdims=True)
        pv = _dot(e.astype(BF16), vv) * (1.0 / den)
        lse_rows = mx + jnp.log(den)
        o = pv[0:BAND]
        lse = jnp.broadcast_to(lse_rows[0:BAND], (BAND, GROUP_W))
        for h in range(1, HPG):
            own = lane_head_full == h
            o = jnp.where(own, pv[h * BAND:(h + 1) * BAND], o)
            lse = jnp.where(own, lse_rows[h * BAND:(h + 1) * BAND], lse)
        o_ref[0, j * BAND:(j + 1) * BAND, ls] = o
        lse_ref[0, j * BAND:(j + 1) * BAND, ls] = lse


def _attn_seq(q, k, v, dil, w, w_scale):
    B, nsub, _ = q.shape
    tq = min(ATT_TQ, nsub)
    rpb = min(dil, ATT_TQ // tq)
    g1, g2 = dil // rpb, nsub // tq
    w_rows = w.shape[0] // (B * g1 * g2)
    assert w_rows * B * g1 * g2 == w.shape[0] and w_rows % 16 == 0
    w_spec = pl.BlockSpec((w_rows, w.shape[1]), lambda b, r, i: ((b * g1 + r) * g2 + i, 0))
    in_spec = pl.BlockSpec((1, tq, rpb * GROUP_W), lambda b, r, i: (b, i, r))
    prev_spec = pl.BlockSpec((1, BAND, rpb * GROUP_W), lambda b, r, i: (b, jnp.maximum(i * (tq // BAND) - 1, 0), r))
    out_spec = pl.BlockSpec((1, tq, rpb * GROUP_W), lambda b, r, i: (b, i, r))
    return pl.pallas_call(
        functools.partial(_attn_seq_kernel, tq=tq, rpb=rpb, w_scale=w_scale),
        grid=(B, g1, g2),
        in_specs=[in_spec] * 3 + [prev_spec] * 2 + [w_spec],
        out_specs=[out_spec] * 2 + [w_spec],
        out_shape=[jax.ShapeDtypeStruct((B, nsub, dil * GROUP_W), F32)] * 2 + [jax.ShapeDtypeStruct(w.shape, BF16)],
        compiler_params=pltpu.CompilerParams(
            dimension_semantics=("arbitrary", "arbitrary", "arbitrary"), vmem_limit_bytes=VMEM_LIMIT),
        name=f"attn_seq_d{dil}",
    )(q, k, v, k, v, w)


N_SIDE_CASTS = 5


def _mlstm_seq_kernel(q_ref, k_ref, v_ref, og_ref, gc_ref, gr_ref, mhn_ref, *rest, L, G, side_scales):
    w_refs = rest[:N_SIDE_CASTS]
    ob_ref, c_out, n_out, m_out = rest[N_SIDE_CASTS:N_SIDE_CASTS + 4]
    wb_refs = rest[N_SIDE_CASTS + 4:2 * N_SIDE_CASTS + 4]
    c_s, n_s, m_s = rest[2 * N_SIDE_CASTS + 4:]
    for w_ref, wb_ref, scale in zip(w_refs, wb_refs, side_scales):
        _cast_scaled(w_ref, wb_ref, scale)
    c = pl.program_id(1)

    @pl.when(c == 0)
    def _():
        c_s[...] = jnp.zeros(c_s.shape, F32)
        n_s[...] = jnp.zeros(n_s.shape, F32)
        m_s[...] = jnp.zeros(m_s.shape, F32)

    ti = lax.broadcasted_iota(jnp.int32, (L, L), 0)
    si = lax.broadcasted_iota(jnp.int32, (L, L), 1)
    tril = ti >= si
    lane = lax.broadcasted_iota(jnp.int32, (1, LANES), 1)
    m_vec = [jnp.zeros((1, LANES), F32)] * G

    for g, h in [(g, h) for g in range(G) for h in range(M_HEADS)]:
        gc = gc_ref[g]
        gr = gr_ref[g]
        hs = slice(h * M_DK, (h + 1) * M_DK)
        b_col = gc[:, M_HEADS + h:M_HEADS + h + 1]
        i_col = gc[:, h:h + 1]
        b_row = gr[M_HEADS + h:M_HEADS + h + 1, :]
        i_row = gr[h:h + 1, :]
        m_prev = m_s[g, h, 0:1, 0:1]
        qh = q_ref[g, :, hs]
        kh = k_ref[g, :, hs]
        vh = v_ref[g, :, hs]
        c_prev = c_s[g, h]
        n_prev = n_s[g, h]

        dmat = jnp.where(tril, b_col - b_row + i_row, -jnp.inf)
        m_inter = b_col + m_prev
        m_t = jnp.maximum(m_inter, jnp.max(dmat, axis=-1, keepdims=True))
        w_intra = lax.dot_general(qh, kh, _NT, preferred_element_type=F32) * jnp.exp(dmat - m_t)
        w_inter = jnp.exp(m_inter - m_t)
        num = _dot(w_intra.astype(BF16), vh) + w_inter * _dot(qh, c_prev.astype(BF16))
        den = (jnp.sum(w_intra, axis=-1, keepdims=True)
               + w_inter * jnp.sum(qh.astype(F32) * n_prev, axis=-1, keepdims=True))
        hh = num / jnp.maximum(jnp.abs(den), jnp.exp(-m_t))
        hn = hh * lax.rsqrt(jnp.mean(hh * hh, axis=-1, keepdims=True) + EPS) * mhn_ref[h:h + 1, :]
        ob_ref[g, :, hs] = (hn * og_ref[g, :, hs].astype(F32)).astype(BF16)

        b_last = b_row[:, L - 1:L]
        m_new = jnp.maximum(b_last + m_prev, jnp.max(b_last - b_row + i_row, axis=-1, keepdims=True))
        wk = jnp.exp(b_last - b_col + i_col - m_new)
        decay = jnp.exp(b_last + m_prev - m_new)
        ks = kh.astype(F32) * wk
        c_new = decay * c_prev + lax.dot_general(ks.astype(BF16), vh, _TN, preferred_element_type=F32)
        n_new = decay * n_prev + jnp.sum(ks, axis=0, keepdims=True)
        c_s[g, h] = c_new
        n_s[g, h] = n_new
        m_s[g, h] = jnp.broadcast_to(m_new, (SUBLANES, LANES))
        c_out[g, h] = c_new
        n_out[g, h:h + 1, :] = n_new
        m_vec[g] = jnp.where(lane == h, m_new, m_vec[g])
    for g in range(G):
        m_out[g] = m_vec[g]


def _mlstm_seq(qm, km, vm, og, gc, gr, mhn, side_w, side_scales):
    B, S, _ = qm.shape
    L = M_CHUNK
    G = MLSTM_G
    nc = S // L
    steps = (B // G) * nc
    assert len(side_w) == N_SIDE_CASTS and all(w.shape[0] % (16 * steps) == 0 for w in side_w)
    w_specs = [pl.BlockSpec((w.shape[0] // steps, w.shape[1]), lambda b, c: (b * nc + c, 0)) for w in side_w]
    tok = pl.BlockSpec((G, L, M_INNER), lambda b, c: (b, c, 0))
    return pl.pallas_call(
        functools.partial(_mlstm_seq_kernel, L=L, G=G, side_scales=tuple(side_scales)),
        grid=(B // G, nc),
        in_specs=[tok, tok, tok, tok,
                  pl.BlockSpec((G, L, LANES), lambda b, c: (b, c, 0)),
                  pl.BlockSpec((G, SUBLANES, L), lambda b, c: (b, 0, c)),
                  pl.BlockSpec((M_HEADS, M_DK), lambda b, c: (0, 0))] + w_specs,
        out_specs=[tok,
                   pl.BlockSpec((G, M_HEADS, M_DK, M_DK), lambda b, c: (b, 0, 0, 0)),
                   pl.BlockSpec((G, M_HEADS, M_DK), lambda b, c: (b, 0, 0)),
                   pl.BlockSpec((G, 1, LANES), lambda b, c: (b, 0, 0))] + w_specs,
        out_shape=[jax.ShapeDtypeStruct((B, S, M_INNER), BF16),
                   jax.ShapeDtypeStruct((B, M_HEADS, M_DK, M_DK), F32),
                   jax.ShapeDtypeStruct((B, M_HEADS, M_DK), F32),
                   jax.ShapeDtypeStruct((B, 1, LANES), F32)]
                  + [jax.ShapeDtypeStruct(w.shape, BF16) for w in side_w],
        scratch_shapes=[pltpu.VMEM((G, M_HEADS, M_DK, M_DK), F32),
                        pltpu.VMEM((G, M_HEADS, 1, M_DK), F32),
                        pltpu.VMEM((G, M_HEADS, SUBLANES, LANES), F32)],
        compiler_params=pltpu.CompilerParams(
            dimension_semantics=("arbitrary", "arbitrary"), vmem_limit_bytes=VMEM_LIMIT),
        name="mlstm_seq",
    )(qm, km, vm, og, gc, gr, mhn, *side_w)


def _post_kernel(x_ref, o0_ref, o1_ref, o2_ref, l0_ref, l1_ref, l2_ref, ob_ref, ga_ref, gb_ref, p_ref,
                 wpa_ref, wpb_ref, wout_ref, nffn_ref, wg_ref, wu_ref, wd_ref, nple_ref, wpg_ref, wpp_ref,
                 nfin_ref, y_ref, *ubuf, dils):
    lead = (0,) * (len(x_ref.shape) - 2)
    rd = lambda r: r[lead] if lead else r[...]
    tm = x_ref.shape[-2]

    def undilate(ref, dil, slot):
        if dil == 1:
            return rd(ref)
        halves = range(GROUP_W // LANES)
        for r in range(dil):
            for c in halves:
                lo = r * GROUP_W + c * LANES
                ubuf[0][slot, c, pl.ds(r, tm // dil, stride=dil), :] = ref[0, :, lo:lo + LANES]
        return jnp.concatenate([ubuf[0][slot, c] for c in halves], axis=1)

    l0, l1, l2 = (undilate(r, d, 2 * g) for g, (r, d) in enumerate(zip((l0_ref, l1_ref, l2_ref), dils)))
    o0, o1, o2 = (undilate(r, d, 2 * g + 1) for g, (r, d) in enumerate(zip((o0_ref, o1_ref, o2_ref), dils)))
    lmax = jnp.maximum(jnp.maximum(l0, l1), l2)
    e0, e1, e2 = jnp.exp(l0 - lmax), jnp.exp(l1 - lmax), jnp.exp(l2 - lmax)
    o_a = (e0 * o0 + e1 * o1 + e2 * o2) / (e0 + e1 + e2)

    mix = (rd(ga_ref).astype(F32) * _dot(o_a.astype(BF16), wpa_ref[...])
           + rd(gb_ref).astype(F32) * _dot(rd(ob_ref).astype(BF16), wpb_ref[...]))
    h = rd(x_ref) + _dot(mix.astype(BF16), wout_ref[...])

    xf = _rms(h, nffn_ref[...]).astype(BF16)
    for lo, hi in FFN_CHUNKS:
        cs = slice(lo, hi)
        gate = _dot(xf, wg_ref[:, cs])
        act = _silu_h(gate) * _dot(xf, wu_ref[:, cs])
        h = h + _dot(act.astype(BF16), wd_ref[cs, :])

    xp = _rms(h, nple_ref[...]).astype(BF16)
    h = h + _sigmoid_h(_dot(xp, wpg_ref[...])) * _dot(rd(p_ref).astype(BF16), wpp_ref[...])
    y = _rms(h, nfin_ref[...])
    if lead:
        y_ref[lead] = y
    else:
        y_ref[...] = y


def _post_step(x, os_, ls_, ob, ga, gb, p, wts):
    return pl.pallas_call(
        functools.partial(_post_kernel, dils=(1,) * N_GROUPS),
        out_shape=jax.ShapeDtypeStruct(x.shape, F32),
        compiler_params=pltpu.CompilerParams(vmem_limit_bytes=VMEM_LIMIT),
        name="post_step",
    )(x, *os_, *ls_, ob, ga, gb, p, *wts)


SM_ROWS = 16
SM_LANES = 512
T_QA, T_KN, T_VN, T_QM, T_KM, T_VM, T_OG, T_N, T_GATE = range(9)
SO_ROWS = 8
SO_OB, SO_N, SO_M, SO_LSE, SO_O = range(5)
HALF_HEADS = HPG // 2
HALF_W = HALF_HEADS * HEAD_DIM


def _eye(n):
    return lax.broadcasted_iota(jnp.int32, (n, n), 0) == lax.broadcasted_iota(jnp.int32, (n, n), 1)


def _col_of(row, eye):
    return jnp.sum(jnp.where(eye, jnp.broadcast_to(row, eye.shape), 0.0), axis=1, keepdims=True)


def _row_of(col, eye):
    return jnp.sum(jnp.where(eye, jnp.broadcast_to(col, eye.shape), 0.0), axis=0, keepdims=True)


def _round_bf16(x):
    return x.astype(BF16).astype(F32)


def _sample_step(tile, kv_refs, cst_ref, mhn_ref, so_ref, kvn_refs, cso_ref):
    so_ref[0] = jnp.zeros((SO_ROWS, SM_LANES), F32)
    eye_a = _eye(HALF_W)
    eye_m = _eye(M_DK)

    for g, (w, dil) in enumerate(DSWA_GROUPS):
        ls = slice(g * HALF_W, (g + 1) * HALF_W)
        k_t = kv_refs[g][0, 0, 0]
        v_t = kv_refs[g][0, 1, 0]
        k_row = tile[T_KN:T_KN + 1, ls]
        v_row = tile[T_VN:T_VN + 1, ls]
        qb = _round_bf16(tile[T_QA:T_QA + 1, ls])
        prod = k_t * _col_of(qb, eye_a)
        qk_new = qb * _round_bf16(k_row)
        pos = lax.broadcasted_iota(jnp.int32, (1, w), 1)
        pe, pn, lse = [], [], []
        for hl in range(HALF_HEADS):
            hd = slice(hl * HEAD_DIM, (hl + 1) * HEAD_DIM)
            s = jnp.where(pos % dil == 0, jnp.sum(prod[hd, :], axis=0, keepdims=True), NEG)
            s_new = jnp.sum(qk_new[:, hd], axis=-1, keepdims=True)
            mx = jnp.maximum(jnp.max(s, axis=-1, keepdims=True), s_new)
            e = jnp.exp(s - mx)
            e_new = jnp.exp(s_new - mx)
            den = jnp.sum(e, axis=-1, keepdims=True) + e_new
            pe.append(jnp.broadcast_to(e / den, (HEAD_DIM, w)))
            pn.append(jnp.broadcast_to(e_new / den, (1, HEAD_DIM)))
            lse.append(jnp.broadcast_to(mx + jnp.log(den), (1, HEAD_DIM)))
        o_col = jnp.sum(v_t * jnp.concatenate(pe, axis=0), axis=-1, keepdims=True)
        so_ref[0, SO_O:SO_O + 1, ls] = _row_of(o_col, eye_a) + jnp.concatenate(pn, axis=1) * _round_bf16(v_row)
        so_ref[0, SO_LSE:SO_LSE + 1, ls] = jnp.concatenate(lse, axis=1)
        last = lax.broadcasted_iota(jnp.int32, (HALF_W, w), 1) == w - 1
        kvn_refs[g][0, 0, 0] = jnp.where(last, _col_of(k_row, eye_a), pltpu.roll(k_t, w - 1, 1))
        kvn_refs[g][0, 1, 0] = jnp.where(last, _col_of(v_row, eye_a), pltpu.roll(v_t, w - 1, 1))

    eye = eye_m
    for j in range(M_HEADS // 2):
        hs = slice(j * M_DK, (j + 1) * M_DK)
        ig = tile[T_GATE:T_GATE + 1, j:j + 1]
        lf = tile[T_GATE:T_GATE + 1, 2 + j:3 + j]
        m_prev = tile[T_GATE:T_GATE + 1, 4 + j:5 + j]
        qb = _round_bf16(tile[T_QM:T_QM + 1, hs])
        kb = _round_bf16(tile[T_KM:T_KM + 1, hs])
        vb = _round_bf16(tile[T_VM:T_VM + 1, hs])
        n_prev = tile[T_N:T_N + 1, hs]
        c_prev = cst_ref[0, j]

        m_inter = lf + m_prev
        m_t = jnp.maximum(m_inter, ig)
        w_intra = jnp.sum(qb * kb, axis=-1, keepdims=True) * jnp.exp(ig - m_t)
        w_inter = jnp.exp(m_inter - m_t)
        q_c = jnp.sum(c_prev * _col_of(qb, eye), axis=0, keepdims=True)
        num = w_intra * vb + w_inter * q_c
        den = w_intra + w_inter * jnp.sum(qb * n_prev, axis=-1, keepdims=True)
        hh = num / jnp.maximum(jnp.abs(den), jnp.exp(-m_t))
        hn = hh * lax.rsqrt(jnp.mean(hh * hh, axis=-1, keepdims=True) + EPS) * mhn_ref[0, j:j + 1, :]
        ks = kb * jnp.exp(ig - m_t)
        so_ref[0, SO_OB:SO_OB + 1, hs] = hn * tile[T_OG:T_OG + 1, hs]
        so_ref[0, SO_N:SO_N + 1, hs] = w_inter * n_prev + ks
        so_ref[0, SO_M:SO_M + 1, hs] = jnp.broadcast_to(m_t, (1, M_DK))
        cso_ref[0, j] = w_inter * c_prev + _col_of(_round_bf16(ks), eye) * vb


N_POST_IN = 22


def _post_fused_kernel(*refs, dils, ns):
    post_in = refs[:N_POST_IN]
    sm_ref, c0_ref, c1_ref, c2_ref, cst_ref, mhn_ref = refs[N_POST_IN:N_POST_IN + 6]
    y_ref, so_ref, cn0_ref, cn1_ref, cn2_ref, cso_ref = refs[N_POST_IN + 6:N_POST_IN + 12]
    (ubuf,) = refs[N_POST_IN + 12:]
    _post_kernel(*post_in, y_ref, ubuf, dils=dils)
    _sample_step(sm_ref[0], (c0_ref, c1_ref, c2_ref), cst_ref, mhn_ref, so_ref,
                 (cn0_ref, cn1_ref, cn2_ref), cso_ref)


def _post_fused(x, os_, ls_, ob, ga, gb, p, wts, sm_in, caches_t, c_state, mhn):
    B, S, _ = x.shape
    NB = c_state.shape[0]
    tm = TM_POST
    ns = S // tm
    assert B * ns == 2 * NB, "one half of a sample batch element per post-mixer grid step"
    tok = lambda w: pl.BlockSpec((1, tm, w), lambda b, i: (b, i, 0))
    dils = tuple(dil for _, dil in DSWA_GROUPS)
    dil_specs = [pl.BlockSpec((1, tm // dil, dil * GROUP_W), lambda b, i: (b, i, 0)) for dil in dils]
    step = lambda b, i: b * ns + i
    cache_specs = [pl.BlockSpec((1, 2, 1, HALF_W, c.shape[-1]), lambda b, i: (step(b, i) // 2, 0, step(b, i) % 2, 0, 0))
                   for c in caches_t]
    cst_spec = pl.BlockSpec((1, M_HEADS // 2, M_DK, M_DK), lambda b, i: (step(b, i) // 2, step(b, i) % 2, 0, 0))
    in_specs = ([tok(D_MODEL)] + dil_specs * 2 + [tok(M_INNER)] * 3 + [tok(PLE_DIM)]
                + [_const_spec(w.shape) for w in wts]
                + [pl.BlockSpec((1, SM_ROWS, SM_LANES), lambda b, i: (step(b, i), 0, 0))]
                + cache_specs
                + [cst_spec, pl.BlockSpec((1, M_HEADS // 2, M_DK), lambda b, i: (step(b, i) % 2, 0, 0))])
    out_specs = ([tok(D_MODEL), pl.BlockSpec((1, SO_ROWS, SM_LANES), lambda b, i: (step(b, i), 0, 0))]
                 + cache_specs + [cst_spec])
    out_shape = ([jax.ShapeDtypeStruct((B, S, D_MODEL), F32),
                  jax.ShapeDtypeStruct((2 * NB, SO_ROWS, SM_LANES), F32)]
                 + [jax.ShapeDtypeStruct(c.shape, F32) for c in caches_t]
                 + [jax.ShapeDtypeStruct(c_state.shape, F32)])
    return pl.pallas_call(
        functools.partial(_post_fused_kernel, dils=dils, ns=ns),
        grid=(B, ns),
        in_specs=in_specs,
        out_specs=out_specs,
        out_shape=out_shape,
        scratch_shapes=[pltpu.VMEM((2 * N_GROUPS, GROUP_W // LANES, tm, LANES), F32)],
        compiler_params=pltpu.CompilerParams(
            dimension_semantics=("arbitrary", "arbitrary"), vmem_limit_bytes=VMEM_LIMIT),
        name="post_seq_sample_stream",
    )(x, *os_, *ls_, ob, ga, gb, p, *wts, sm_in, *caches_t, c_state, mhn)


def _rope_tables(pos):
    pos = np.asarray(pos, np.float64)
    t = pos.shape[0]
    inv = np.power(ROPE_THETA, -np.arange(ROPE_HALF, dtype=np.float64) / ROPE_HALF)
    ang = pos[:, None] * inv[None, :]
    cos, sin = np.cos(ang), np.sin(ang)
    one = np.ones((t, HEAD_DIM - ROPE_DIM))
    zero = np.zeros((t, HEAD_DIM - ROPE_DIM))
    z8 = np.zeros((t, ROPE_HALF))
    a = np.concatenate([cos, cos, one], axis=1)
    m = np.concatenate([-sin, z8, zero], axis=1)
    p = np.concatenate([z8, sin, zero], axis=1)
    rep = LANES // HEAD_DIM
    return tuple(jnp.asarray(np.tile(v, (1, rep)).astype(np.float32)) for v in (a, m, p))


def kernel(x_prompt, x_sample, cache_kv_w128, cache_kv_w512, cache_kv_w2048, state_conv, state_C, state_n,
           state_m, p_prompt, p_sample, norm_mix, w_in, conv_w, conv_b, b_igate, b_fgate, mh_norm, w_proj_a,
           w_proj_b, w_out, norm_ffn, w_gate, w_up, w_down, norm_ple, w_ple_gate, w_ple_proj, norm_final):
    depth = w_in.shape[0]
    assert depth == 1, "single trunk layer"
    B, S, _ = x_prompt.shape
    NB = x_sample.shape[0]
    assert x_sample.shape[1] == 1
    caches = (cache_kv_w128, cache_kv_w512, cache_kv_w2048)
    assert S >= DSWA_GROUPS[-1][0] and S % (BAND * DSWA_GROUPS[-1][1]) == 0
    past = 0
    l = 0

    row_scale = np.ones((IN_COLS, 1), np.float32)
    row_scale[OFF_MO:OFF_MI] = 0.5
    row_scale[OFF_GA:IN_COLS] = 0.5
    wt = (w_in[l].T * row_scale).astype(BF16)
    gate_wt = wt[OFF_MI:OFF_GA]
    gate_b = jnp.concatenate([b_igate[l], b_fgate[l]])
    in_wts = (
        norm_mix[l][None, :],
        wt[OFF_AQ:OFF_MQ],
        wt[OFF_MQ:OFF_MV],
        wt[OFF_MV:OFF_MO],
        wt[OFF_MO:OFF_MI],
        wt[OFF_GA:OFF_GB],
        wt[OFF_GB:IN_COLS],
        jnp.pad(gate_wt, ((0, LANES - 2 * M_HEADS), (0, 0))),
        gate_wt,
        jnp.pad(gate_b, (0, LANES - 2 * M_HEADS))[None, :],
        gate_b[:, None],
        0.5 * conv_w[l],
        0.5 * conv_b[l][None, :],
    )
    mhn = mh_norm[l]

    c0 = jnp.zeros((B, SUBLANES, 2 * M_INNER), F32)
    tabs_p = _rope_tables(past + np.arange(S))
    (*qkv, qm, km, vm, og, ga, gb, gc, gr, kv0, kv1, kv2, cn) = _inproj_seq(x_prompt, c0, tabs_p, in_wts)
    os_p, ls_p, ffn_bf16 = [], [], []
    ffn_w = ((w_gate[l], 0.5), (w_up[l], 1.0), (w_down[l], 1.0))
    for g, ((_, dil), (w_ffn, w_scale)) in enumerate(zip(DSWA_GROUPS, ffn_w)):
        o, lse, wb = _attn_seq(*qkv[3 * g:3 * g + 3], dil, w_ffn, w_scale)
        os_p.append(o)
        ls_p.append(lse)
        ffn_bf16.append(wb)
    ob, c_p, n_p, m_p, wpa, wpb, wo, wpg, wpp = _mlstm_seq(
        qm, km, vm, og, gc, gr, mhn, (w_proj_a[l], w_proj_b[l], w_out[l], w_ple_gate[l], w_ple_proj[l]),
        (1.0, 1.0, 1.0, 0.5, 1.0))
    post_wts = (wpa, wpb, wo, norm_ffn[l][None, :], *ffn_bf16, norm_ple[l][None, :], wpg, wpp, norm_final[None, :])

    tabs_s = _rope_tables(np.full((NB,), PAST_LEN))
    conv_s = jnp.swapaxes(state_conv[l], 0, 1)
    (qa_s, kvn0, kvn1, kvn2, qm_s, km_s, vm_s, og_s, ga_s, gb_s, gc_s, cn_s) = _inproj_step(
        x_sample[:, 0, :], conv_s, tabs_s, in_wts)
    pad_row = lambda t: jnp.pad(t, ((0, 0), (0, SM_LANES - t.shape[1])))
    att_half = lambda t: pad_row(t.reshape(NB, N_GROUPS, 2, HALF_W).transpose(0, 2, 1, 3).reshape(2 * NB, -1))
    mem_half = lambda t: t.reshape(2 * NB, SM_LANES)
    k_new = jnp.concatenate([kv[:, :GROUP_W] for kv in (kvn0, kvn1, kvn2)], axis=1)
    v_new = jnp.concatenate([kv[:, GROUP_W:] for kv in (kvn0, kvn1, kvn2)], axis=1)
    gates = jnp.concatenate([t.reshape(2 * NB, M_HEADS // 2) for t in
                             (gc_s[:, :M_HEADS], gc_s[:, M_HEADS:2 * M_HEADS], state_m[l])], axis=1)
    sm_rows = [att_half(qa_s), att_half(k_new), att_half(v_new), mem_half(qm_s), mem_half(km_s), mem_half(vm_s),
               mem_half(og_s), mem_half(state_n[l].reshape(NB, M_INNER)), pad_row(gates)]
    sm_rows += [jnp.zeros((2 * NB, SM_LANES), F32)] * (SM_ROWS - len(sm_rows))
    sm_in = jnp.stack(sm_rows, axis=1)
    caches_t = [jnp.transpose(c[l], (0, 2, 3, 4, 1)).reshape(NB, 2, 2, HALF_W, c.shape[2]) for c in caches]
    for c, (w, dil) in zip(caches_t, DSWA_GROUPS):
        assert c.shape[-1] == w == BAND * dil, "each cache must hold exactly one full window"
    y_prompt, so, cn0, cn1, cn2, c_s = _post_fused(x_prompt, os_p, ls_p, ob, ga, gb, p_prompt[l], post_wts,
                                                   sm_in, caches_t, state_C[l], mhn.reshape(2, M_HEADS // 2, M_DK))
    kv_s = [jnp.transpose(c.reshape(NB, 2, HPG, HEAD_DIM, -1), (0, 4, 1, 2, 3)) for c in (cn0, cn1, cn2)]
    att_full = lambda t: t[:, :N_GROUPS * HALF_W].reshape(NB, 2, N_GROUPS, HALF_W).transpose(0, 2, 1, 3).reshape(
        NB, N_GROUPS, GROUP_W)
    o_s, l_s = att_full(so[:, SO_O, :]), att_full(so[:, SO_LSE, :])
    os_s = [o_s[:, g, :] for g in range(N_GROUPS)]
    ls_s = [l_s[:, g, :] for g in range(N_GROUPS)]
    ob_s = so[:, SO_OB, :].reshape(NB, M_INNER)
    n_s = so[:, SO_N, :].reshape(NB, M_HEADS, M_DK)
    m_s = so[:, SO_M, ::M_DK].reshape(NB, M_HEADS)
    y_sample = _post_step(x_sample[:, 0, :], os_s, ls_s, ob_s, ga_s, gb_s, p_sample[l][:, 0, :], post_wts)

    kv_shape = lambda n, w: (1, n, w, 2, HPG, HEAD_DIM)
    return (
        y_prompt, y_sample[:, None, :],
        kv0.reshape(kv_shape(B, DSWA_GROUPS[0][0])), kv_s[0].reshape(kv_shape(NB, DSWA_GROUPS[0][0])),
        kv1.reshape(kv_shape(B, DSWA_GROUPS[1][0])), kv_s[1].reshape(kv_shape(NB, DSWA_GROUPS[1][0])),
        kv2.reshape(kv_shape(B, DSWA_GROUPS[2][0])), kv_s[2].reshape(kv_shape(NB, DSWA_GROUPS[2][0])),
        cn[:, SUBLANES - (CONV_W - 1):, :][None], jnp.swapaxes(cn_s, 0, 1)[None],
        c_p[None], c_s[None],
        n_p[None], n_s[None],
        m_p[:, 0, :M_HEADS][None], m_s[None],
    )
```

```python
import functools

import jax
import jax.numpy as jnp
import numpy as np
from jax import lax
from jax.experimental import pallas as pl
from jax.experimental.pallas import tpu as pltpu

F32 = jnp.float32
BF16 = jnp.bfloat16

D_MODEL = 1024
PAST_LEN = 16384
HEAD_DIM = 64
ROPE_DIM = HEAD_DIM // 4
ROPE_HALF = ROPE_DIM // 2
ROPE_THETA = 500000.0
DSWA_GROUPS = ((128, 1), (512, 4), (2048, 16))
HPG = 4
N_GROUPS = len(DSWA_GROUPS)
GROUP_W = HPG * HEAD_DIM
A_QKV = N_GROUPS * GROUP_W
BAND = 128
M_HEADS = 4
M_INNER = D_MODEL
M_DK = M_INNER // M_HEADS
CONV_W = 4
D_FF = 2816
PLE_DIM = 256
EPS = 1e-6
NEG = -1e30

OFF_AQ = 0
OFF_AK = OFF_AQ + A_QKV
OFF_AV = OFF_AK + A_QKV
OFF_MQ = OFF_AV + A_QKV
OFF_MK = OFF_MQ + M_INNER
OFF_MV = OFF_MK + M_INNER
OFF_MO = OFF_MV + M_INNER
OFF_MI = OFF_MO + M_INNER
OFF_MF = OFF_MI + M_HEADS
OFF_GA = OFF_MF + M_HEADS
OFF_GB = OFF_GA + D_MODEL
IN_COLS = OFF_GB + D_MODEL

LANES = 128
SUBLANES = 8
VMEM_LIMIT = 56 * 1024 * 1024

TM_IN = 512
TM_POST = 256
M_CHUNK = 512
CUM_BLOCK = 256
MLSTM_G = 2
ATT_TQ = 2048
MXU_DIM = 256
FFN_CHUNKS = ((0, 4 * MXU_DIM), (4 * MXU_DIM, 8 * MXU_DIM), (8 * MXU_DIM, D_FF))

_NT = (((1,), (1,)), ((), ()))
_TN = (((0,), (0,)), ((), ()))


def _dot(a, b):
    return jnp.dot(a, b, preferred_element_type=F32)


def _wdot(a, wt_ref, lo, hi):
    return lax.dot_general(a, wt_ref[lo:hi, :], _NT, preferred_element_type=F32)


def _rms(x, g):
    return x * lax.rsqrt(jnp.mean(x * x, axis=-1, keepdims=True) + EPS) * g


def _bf16_pieces(x):
    hi = x.astype(BF16)
    r1 = x - hi.astype(F32)
    mid = r1.astype(BF16)
    lo = (r1 - mid.astype(F32)).astype(BF16)
    return hi, mid, lo


def _sigmoid_h(xh):
    return 0.5 * jnp.tanh(xh) + 0.5


def _silu_h(xh):
    return xh * (jnp.tanh(xh) + 1.0)


def _log_sigmoid(x):
    return jnp.minimum(x, 0.0) - jnp.log1p(jnp.exp(-jnp.abs(x)))


def _rope(t, ra, rm, rp):
    return t * ra + pltpu.roll(t, LANES - ROPE_HALF, 1) * rm + pltpu.roll(t, ROPE_HALF, 1) * rp


def _rope2(t, ra, rm, rp):
    return jnp.concatenate([_rope(t[:, :LANES], ra, rm, rp), _rope(t[:, LANES:], ra, rm, rp)], axis=1)


def _gate_cols(z):
    lane = lax.broadcasted_iota(jnp.int32, z.shape, 1)
    return jnp.where(lane < M_HEADS, z, _log_sigmoid(z))


def _inproj_seq_kernel(x_ref, nmix_ref, waqkv_ref, wmqk_ref, wmv_ref, wmo_ref, wga_ref, wgb_ref,
                       wgc_ref, wgr_ref, bc_ref, br_ref, cw_ref, cb_ref, c0_ref, ra_ref, rm_ref, rp_ref,
                       q0_ref, k0_ref, v0_ref, q1_ref, k1_ref, v1_ref, q2_ref, k2_ref, v2_ref,
                       qm_ref, km_ref, vm_ref, og_ref, ga_ref, gb_ref,
                       gc_ref, gr_ref, kv0_ref, kv1_ref, kv2_ref, cn_ref, cbuf, dbuf, *, tm, ns):
    i = pl.program_id(1)
    xn = _rms(x_ref[0], nmix_ref[...]).astype(BF16)
    ra, rm, rp = ra_ref[...], rm_ref[...], rp_ref[...]
    kv_refs = (kv0_ref, kv1_ref, kv2_ref)
    qkv_refs = ((q0_ref, k0_ref, v0_ref), (q1_ref, k1_ref, v1_ref), (q2_ref, k2_ref, v2_ref))

    def put_dilated(ref, val, dil, slot):
        if dil == 1:
            ref[0] = val.astype(BF16)
            return
        for c in range(GROUP_W // LANES):
            dbuf[slot, c] = val[:, c * LANES:(c + 1) * LANES]
        for r in range(dil):
            for c in range(GROUP_W // LANES):
                lo = r * GROUP_W + c * LANES
                ref[0, :, lo:lo + LANES] = dbuf[slot, c, pl.ds(r, tm // dil, stride=dil), :].astype(BF16)

    for g, (win, dil) in enumerate(DSWA_GROUPS):
        zq = _wdot(xn, waqkv_ref, OFF_AQ + g * GROUP_W, OFF_AQ + (g + 1) * GROUP_W)
        put_dilated(qkv_refs[g][0], _rope2(zq, ra, rm, rp) * (HEAD_DIM ** -0.5), dil, 3 * g)
        zk = _wdot(xn, waqkv_ref, OFF_AK + g * GROUP_W, OFF_AK + (g + 1) * GROUP_W)
        kr = _rope2(zk, ra, rm, rp)
        put_dilated(qkv_refs[g][1], kr, dil, 3 * g + 1)
        zv = _wdot(xn, waqkv_ref, OFF_AV + g * GROUP_W, OFF_AV + (g + 1) * GROUP_W)
        put_dilated(qkv_refs[g][2], zv, dil, 3 * g + 2)
        rows = min(win, tm)
        kv_refs[g][0, :, :GROUP_W] = kr[tm - rows:, :]
        kv_refs[g][0, :, GROUP_W:] = zv[tm - rows:, :]

    for blk in range(2 * M_INNER // LANES):
        cbuf[blk, 0:SUBLANES, :] = jnp.where(i == 0, c0_ref[0, :, blk * LANES:(blk + 1) * LANES],
                                             cbuf[blk, 0:SUBLANES, :])

    for c in range(2 * M_INNER // 256):
        pre = _wdot(xn, wmqk_ref, c * 256, (c + 1) * 256)
        halves = []
        for hf in range(256 // LANES):
            blk = c * (256 // LANES) + hf
            ls = slice(blk * LANES, (blk + 1) * LANES)
            pre_h = pre[:, hf * LANES:(hf + 1) * LANES]
            cbuf[blk, SUBLANES:SUBLANES + tm, :] = pre_h
            y = cb_ref[:, ls] + cw_ref[CONV_W - 1:CONV_W, ls] * pre_h
            for d in range(1, CONV_W):
                y = y + cw_ref[CONV_W - 1 - d:CONV_W - d, ls] * cbuf[blk, pl.ds(SUBLANES - d, tm), :]
            tail = pre_h[tm - SUBLANES:, :]
            cbuf[blk, 0:SUBLANES, :] = tail
            cn_ref[0, :, ls] = tail
            halves.append(_silu_h(y))
        s = jnp.concatenate(halves, axis=1)
        if c < M_INNER // 256:
            qm_ref[0, :, c * 256:(c + 1) * 256] = s.astype(BF16)
        else:
            km_ref[0, :, c * 256 - M_INNER:(c + 1) * 256 - M_INNER] = (s * (M_DK ** -0.5)).astype(BF16)
        cs = slice((c % 2) * 512, (c % 2 + 1) * 512)
        w_ref, o_ref = ((wmv_ref, vm_ref), (wmo_ref, og_ref), (wga_ref, ga_ref), (wgb_ref, gb_ref))[c // 2]
        z = _wdot(xn, w_ref, cs.start, cs.stop)
        o_ref[0, :, cs] = (z if o_ref is vm_ref else _sigmoid_h(z)).astype(BF16)

    gcv = _gate_cols(_wdot(xn, wgc_ref, 0, LANES) + bc_ref[...])
    zr = lax.dot_general(wgr_ref[...], xn, _NT, preferred_element_type=F32) + br_ref[...]
    sub = lax.broadcasted_iota(jnp.int32, zr.shape, 0)
    grv = jnp.where(sub < M_HEADS, zr, _log_sigmoid(zr))
    L = CUM_BLOCK
    ti = lax.broadcasted_iota(jnp.int32, (L, L), 0)
    si = lax.broadcasted_iota(jnp.int32, (L, L), 1)
    lower = jnp.where(ti >= si, 1.0, 0.0).astype(BF16)
    upper = jnp.where(ti <= si, 1.0, 0.0).astype(BF16)
    lane = lax.broadcasted_iota(jnp.int32, (L, LANES), 1)
    sub_l = lax.broadcasted_iota(jnp.int32, (SUBLANES, L), 0)
    for blk in range(tm // L):
        rs = slice(blk * L, (blk + 1) * L)
        cum_c = sum(_dot(lower, p) for p in _bf16_pieces(gcv[rs, :]))
        cum_r = sum(_dot(p, upper) for p in _bf16_pieces(grv[:, rs]))
        if (blk * L) % M_CHUNK:
            cum_c = cum_c + carry_c
            cum_r = cum_r + carry_r
        carry_c, carry_r = cum_c[L - 1:L, :], cum_r[:, L - 1:L]
        gc_ref[0, rs, :] = jnp.where(lane < M_HEADS, gcv[rs, :], cum_c)
        gr_ref[0, :, rs] = jnp.where(sub_l < M_HEADS, grv[:, rs], cum_r)


def _const_spec(shape):
    nd = len(shape)
    return pl.BlockSpec(shape, lambda *_: (0,) * nd, pipeline_mode=pl.Buffered(1))


def _inproj_seq(x, c0, tables, wts):
    B, S, _ = x.shape
    tm = TM_IN
    ns = S // tm
    ra, rm, rp = tables

    def kv_spec(win):
        rows = min(win, tm)
        nblk = win // rows
        return pl.BlockSpec((1, rows, 2 * GROUP_W), lambda b, i: (b, jnp.maximum(i - (ns - nblk), 0), 0))

    tok = lambda w: pl.BlockSpec((1, tm, w), lambda b, i: (b, i, 0))
    tab = pl.BlockSpec((tm, LANES), lambda b, i: (i, 0))
    in_specs = ([tok(D_MODEL)] + [_const_spec(w.shape) for w in wts]
                + [pl.BlockSpec((1, SUBLANES, 2 * M_INNER), lambda b, i: (b, 0, 0)), tab, tab, tab])
    dil_shape = [jax.ShapeDtypeStruct((B, S // dil, dil * GROUP_W), BF16) for _, dil in DSWA_GROUPS for _ in range(3)]
    dil_specs = [pl.BlockSpec((1, tm // dil, dil * GROUP_W), lambda b, i: (b, i, 0))
                 for _, dil in DSWA_GROUPS for _ in range(3)]
    out_shape = (dil_shape
                 + [jax.ShapeDtypeStruct((B, S, M_INNER), BF16)] * 6
                 + [jax.ShapeDtypeStruct((B, S, LANES), F32), jax.ShapeDtypeStruct((B, SUBLANES, S), F32)]
                 + [jax.ShapeDtypeStruct((B, win, 2 * GROUP_W), F32) for win, _ in DSWA_GROUPS]
                 + [jax.ShapeDtypeStruct((B, SUBLANES, 2 * M_INNER), F32)])
    out_specs = (dil_specs + [tok(M_INNER)] * 6
                 + [tok(LANES), pl.BlockSpec((1, SUBLANES, tm), lambda b, i: (b, 0, i))]
                 + [kv_spec(win) for win, _ in DSWA_GROUPS]
                 + [pl.BlockSpec((1, SUBLANES, 2 * M_INNER), lambda b, i: (b, 0, 0))])
    return pl.pallas_call(
        functools.partial(_inproj_seq_kernel, tm=tm, ns=ns),
        grid=(B, ns),
        in_specs=in_specs,
        out_specs=out_specs,
        out_shape=out_shape,
        scratch_shapes=[pltpu.VMEM((2 * M_INNER // LANES, SUBLANES + tm, LANES), F32),
                        pltpu.VMEM((3 * N_GROUPS, GROUP_W // LANES, tm, LANES), F32)],
        compiler_params=pltpu.CompilerParams(
            dimension_semantics=("arbitrary", "arbitrary"), vmem_limit_bytes=VMEM_LIMIT),
        name="inproj_seq",
    )(x, *wts, c0, ra, rm, rp)


def _inproj_step_kernel(x_ref, nmix_ref, waqkv_ref, wmqk_ref, wmv_ref, wmo_ref, wga_ref, wgb_ref,
                        wgc_ref, wgr_ref, bc_ref, br_ref, cw_ref, cb_ref, cs_ref, ra_ref, rm_ref, rp_ref,
                        qa_ref, kv0_ref, kv1_ref, kv2_ref, qm_ref, km_ref, vm_ref, og_ref, ga_ref, gb_ref,
                        gc_ref, cn_ref):
    del wgr_ref, br_ref
    xn = _rms(x_ref[...], nmix_ref[...]).astype(BF16)
    ra, rm, rp = ra_ref[...], rm_ref[...], rp_ref[...]
    kv_refs = (kv0_ref, kv1_ref, kv2_ref)
    for g in range(N_GROUPS):
        cs = slice(g * GROUP_W, (g + 1) * GROUP_W)
        zq = _wdot(xn, waqkv_ref, OFF_AQ + g * GROUP_W, OFF_AQ + (g + 1) * GROUP_W)
        qa_ref[:, cs] = _rope2(zq, ra, rm, rp) * (HEAD_DIM ** -0.5)
        zk = _wdot(xn, waqkv_ref, OFF_AK + g * GROUP_W, OFF_AK + (g + 1) * GROUP_W)
        kv_refs[g][:, :GROUP_W] = _rope2(zk, ra, rm, rp)
        kv_refs[g][:, GROUP_W:] = _wdot(xn, waqkv_ref, OFF_AV + g * GROUP_W, OFF_AV + (g + 1) * GROUP_W)

    for c in range(2 * M_INNER // 256):
        cs = slice(c * 256, (c + 1) * 256)
        pre = _wdot(xn, wmqk_ref, cs.start, cs.stop)
        y = cb_ref[:, cs] + cw_ref[CONV_W - 1:CONV_W, cs] * pre
        for j in range(CONV_W - 1):
            y = y + cw_ref[j:j + 1, cs] * cs_ref[j, :, cs]
        for j in range(CONV_W - 2):
            cn_ref[j, :, cs] = cs_ref[j + 1, :, cs]
        cn_ref[CONV_W - 2, :, cs] = pre
        s = _silu_h(y)
        if c < M_INNER // 256:
            qm_ref[:, cs] = s
        else:
            km_ref[:, c * 256 - M_INNER:(c + 1) * 256 - M_INNER] = s * (M_DK ** -0.5)

    for c in range(M_INNER // 256):
        cs = slice(c * 256, (c + 1) * 256)
        vm_ref[:, cs] = _wdot(xn, wmv_ref, cs.start, cs.stop)
        og_ref[:, cs] = _sigmoid_h(_wdot(xn, wmo_ref, cs.start, cs.stop))
        ga_ref[:, cs] = _sigmoid_h(_wdot(xn, wga_ref, cs.start, cs.stop))
        gb_ref[:, cs] = _sigmoid_h(_wdot(xn, wgb_ref, cs.start, cs.stop))
    gc_ref[...] = _gate_cols(_wdot(xn, wgc_ref, 0, LANES) + bc_ref[...])


def _inproj_step(x, conv_state, tables, wts):
    nb = x.shape[0]
    sds = lambda w: jax.ShapeDtypeStruct((nb, w), F32)
    out_shape = ([sds(A_QKV)] + [sds(2 * GROUP_W)] * N_GROUPS + [sds(M_INNER)] * 6 + [sds(LANES)]
                 + [jax.ShapeDtypeStruct((CONV_W - 1, nb, 2 * M_INNER), F32)])
    return pl.pallas_call(
        _inproj_step_kernel,
        out_shape=out_shape,
        compiler_params=pltpu.CompilerParams(vmem_limit_bytes=VMEM_LIMIT),
        name="inproj_step",
    )(x, *wts, conv_state, *tables)


def _cast_scaled(w_ref, wb_ref, scale):
    w = w_ref[...]
    wb_ref[...] = (w if scale == 1.0 else w * scale).astype(BF16)


def _attn_seq_kernel(q_ref, k_ref, v_ref, kp_ref, vp_ref, w_ref, o_ref, lse_ref, wb_ref, *, tq, rpb, w_scale):
    _cast_scaled(w_ref, wb_ref, w_scale)
    i = pl.program_id(2)
    lane_head = lax.broadcasted_iota(jnp.int32, (1, GROUP_W), 1) // HEAD_DIM
    head_mask = [jnp.where(lane_head == h, 1.0, 0.0).astype(BF16) for h in range(HPG)]
    lane_head_full = lax.broadcasted_iota(jnp.int32, (BAND, GROUP_W), 1) // HEAD_DIM
    qi = lax.broadcasted_iota(jnp.int32, (BAND, 2 * BAND), 0)
    kj = lax.broadcasted_iota(jnp.int32, (BAND, 2 * BAND), 1)
    band = (kj >= qi) & (kj <= qi + BAND)
    first_lo = jnp.where(i > 0, 0, BAND)
    bias = jnp.where(band, 0.0, NEG)
    bias_first = jnp.where(band & (kj >= first_lo), 0.0, NEG)

    for rr, j in [(rr, j) for rr in range(rpb) for j in range(tq // BAND)]:
        ls = slice(rr * GROUP_W, (rr + 1) * GROUP_W)
        b1 = bias_first if j == 0 else bias
        q = q_ref[0, j * BAND:(j + 1) * BAND, ls]
        if j == 0:
            kk = jnp.concatenate([kp_ref[0, :, ls], k_ref[0, 0:BAND, ls]], axis=0)
            vv = jnp.concatenate([vp_ref[0, :, ls], v_ref[0, 0:BAND, ls]], axis=0)
        else:
            kk = k_ref[0, (j - 1) * BAND:(j + 1) * BAND, ls]
            vv = v_ref[0, (j - 1) * BAND:(j + 1) * BAND, ls]
        qblk = jnp.concatenate([q * head_mask[h] for h in range(HPG)], axis=0)
        s = lax.dot_general(qblk, kk, _NT, preferred_element_type=F32) + jnp.concatenate([b1] * HPG, axis=0)
        mx = jnp.max(s, axis=-1, keepdims=True)
        e = jnp.exp(s - mx)
        den = jnp.sum(e, axis=-1, keepdims=True)
        pv = _dot(e.astype(BF16), vv) * (1.0 / den)
        lse_rows = mx + jnp.log(den)
        o = pv[0:BAND]
        lse = jnp.broadcast_to(lse_rows[0:BAND], (BAND, GROUP_W))
        for h in range(1, HPG):
            own = lane_head_full == h
            o = jnp.where(own, pv[h * BAND:(h + 1) * BAND], o)
            lse = jnp.where(own, lse_rows[h * BAND:(h + 1) * BAND], lse)
        o_ref[0, j * BAND:(j + 1) * BAND, ls] = o
        lse_ref[0, j * BAND:(j + 1) * BAND, ls] = lse


def _attn_seq(q, k, v, dil, w, w_scale):
    B, nsub, _ = q.shape
    tq = min(ATT_TQ, nsub)
    rpb = min(dil, ATT_TQ // tq)
    g1, g2 = dil // rpb, nsub // tq
    w_rows = w.shape[0] // (B * g1 * g2)
    assert w_rows * B * g1 * g2 == w.shape[0] and w_rows % 16 == 0
    w_spec = pl.BlockSpec((w_rows, w.shape[1]), lambda b, r, i: ((b * g1 + r) * g2 + i, 0))
    in_spec = pl.BlockSpec((1, tq, rpb * GROUP_W), lambda b, r, i: (b, i, r))
    prev_spec = pl.BlockSpec((1, BAND, rpb * GROUP_W), lambda b, r, i: (b, jnp.maximum(i * (tq // BAND) - 1, 0), r))
    out_spec = pl.BlockSpec((1, tq, rpb * GROUP_W), lambda b, r, i: (b, i, r))
    return pl.pallas_call(
        functools.partial(_attn_seq_kernel, tq=tq, rpb=rpb, w_scale=w_scale),
        grid=(B, g1, g2),
        in_specs=[in_spec] * 3 + [prev_spec] * 2 + [w_spec],
        out_specs=[out_spec] * 2 + [w_spec],
        out_shape=[jax.ShapeDtypeStruct((B, nsub, dil * GROUP_W), F32)] * 2 + [jax.ShapeDtypeStruct(w.shape, BF16)],
        compiler_params=pltpu.CompilerParams(
            dimension_semantics=("arbitrary", "arbitrary", "arbitrary"), vmem_limit_bytes=VMEM_LIMIT),
        name=f"attn_seq_d{dil}",
    )(q, k, v, k, v, w)


N_SIDE_CASTS = 5


def _mlstm_seq_kernel(q_ref, k_ref, v_ref, og_ref, gc_ref, gr_ref, mhn_ref, *rest, L, G, side_scales):
    w_refs = rest[:N_SIDE_CASTS]
    ob_ref, c_out, n_out, m_out = rest[N_SIDE_CASTS:N_SIDE_CASTS + 4]
    wb_refs = rest[N_SIDE_CASTS + 4:2 * N_SIDE_CASTS + 4]
    c_s, n_s, m_s = rest[2 * N_SIDE_CASTS + 4:]
    for w_ref, wb_ref, scale in zip(w_refs, wb_refs, side_scales):
        _cast_scaled(w_ref, wb_ref, scale)
    c = pl.program_id(1)

    @pl.when(c == 0)
    def _():
        c_s[...] = jnp.zeros(c_s.shape, F32)
        n_s[...] = jnp.zeros(n_s.shape, F32)
        m_s[...] = jnp.zeros(m_s.shape, F32)

    ti = lax.broadcasted_iota(jnp.int32, (L, L), 0)
    si = lax.broadcasted_iota(jnp.int32, (L, L), 1)
    tril = ti >= si
    lane = lax.broadcasted_iota(jnp.int32, (1, LANES), 1)
    m_vec = [jnp.zeros((1, LANES), F32)] * G

    for g, h in [(g, h) for g in range(G) for h in range(M_HEADS)]:
        gc = gc_ref[g]
        gr = gr_ref[g]
        hs = slice(h * M_DK, (h + 1) * M_DK)
        b_col = gc[:, M_HEADS + h:M_HEADS + h + 1]
        i_col = gc[:, h:h + 1]
        b_row = gr[M_HEADS + h:M_HEADS + h + 1, :]
        i_row = gr[h:h + 1, :]
        m_prev = m_s[g, h, 0:1, 0:1]
        qh = q_ref[g, :, hs]
        kh = k_ref[g, :, hs]
        vh = v_ref[g, :, hs]
        c_prev = c_s[g, h]
        n_prev = n_s[g, h]

        dmat = jnp.where(tril, b_col - b_row + i_row, -jnp.inf)
        m_inter = b_col + m_prev
        m_t = jnp.maximum(m_inter, jnp.max(dmat, axis=-1, keepdims=True))
        w_intra = lax.dot_general(qh, kh, _NT, preferred_element_type=F32) * jnp.exp(dmat - m_t)
        w_inter = jnp.exp(m_inter - m_t)
        num = _dot(w_intra.astype(BF16), vh) + w_inter * _dot(qh, c_prev.astype(BF16))
        den = (jnp.sum(w_intra, axis=-1, keepdims=True)
               + w_inter * jnp.sum(qh.astype(F32) * n_prev, axis=-1, keepdims=True))
        hh = num / jnp.maximum(jnp.abs(den), jnp.exp(-m_t))
        hn = hh * lax.rsqrt(jnp.mean(hh * hh, axis=-1, keepdims=True) + EPS) * mhn_ref[h:h + 1, :]
        ob_ref[g, :, hs] = (hn * og_ref[g, :, hs].astype(F32)).astype(BF16)

        b_last = b_row[:, L - 1:L]
        m_new = jnp.maximum(b_last + m_prev, jnp.max(b_last - b_row + i_row, axis=-1, keepdims=True))
        wk = jnp.exp(b_last - b_col + i_col - m_new)
        decay = jnp.exp(b_last + m_prev - m_new)
        ks = kh.astype(F32) * wk
        c_new = decay * c_prev + lax.dot_general(ks.astype(BF16), vh, _TN, preferred_element_type=F32)
        n_new = decay * n_prev + jnp.sum(ks, axis=0, keepdims=True)
        c_s[g, h] = c_new
        n_s[g, h] = n_new
        m_s[g, h] = jnp.broadcast_to(m_new, (SUBLANES, LANES))
        c_out[g, h] = c_new
        n_out[g, h:h + 1, :] = n_new
        m_vec[g] = jnp.where(lane == h, m_new, m_vec[g])
    for g in range(G):
        m_out[g] = m_vec[g]


def _mlstm_seq(qm, km, vm, og, gc, gr, mhn, side_w, side_scales):
    B, S, _ = qm.shape
    L = M_CHUNK
    G = MLSTM_G
    nc = S // L
    steps = (B // G) * nc
    assert len(side_w) == N_SIDE_CASTS and all(w.shape[0] % (16 * steps) == 0 for w in side_w)
    w_specs = [pl.BlockSpec((w.shape[0] // steps, w.shape[1]), lambda b, c: (b * nc + c, 0)) for w in side_w]
    tok = pl.BlockSpec((G, L, M_INNER), lambda b, c: (b, c, 0))
    return pl.pallas_call(
        functools.partial(_mlstm_seq_kernel, L=L, G=G, side_scales=tuple(side_scales)),
        grid=(B // G, nc),
        in_specs=[tok, tok, tok, tok,
                  pl.BlockSpec((G, L, LANES), lambda b, c: (b, c, 0)),
                  pl.BlockSpec((G, SUBLANES, L), lambda b, c: (b, 0, c)),
                  pl.BlockSpec((M_HEADS, M_DK), lambda b, c: (0, 0))] + w_specs,
        out_specs=[tok,
                   pl.BlockSpec((G, M_HEADS, M_DK, M_DK), lambda b, c: (b, 0, 0, 0)),
                   pl.BlockSpec((G, M_HEADS, M_DK), lambda b, c: (b, 0, 0)),
                   pl.BlockSpec((G, 1, LANES), lambda b, c: (b, 0, 0))] + w_specs,
        out_shape=[jax.ShapeDtypeStruct((B, S, M_INNER), BF16),
                   jax.ShapeDtypeStruct((B, M_HEADS, M_DK, M_DK), F32),
                   jax.ShapeDtypeStruct((B, M_HEADS, M_DK), F32),
                   jax.ShapeDtypeStruct((B, 1, LANES), F32)]
                  + [jax.ShapeDtypeStruct(w.shape, BF16) for w in side_w],
        scratch_shapes=[pltpu.VMEM((G, M_HEADS, M_DK, M_DK), F32),
                        pltpu.VMEM((G, M_HEADS, 1, M_DK), F32),
                        pltpu.VMEM((G, M_HEADS, SUBLANES, LANES), F32)],
        compiler_params=pltpu.CompilerParams(
            dimension_semantics=("arbitrary", "arbitrary"), vmem_limit_bytes=VMEM_LIMIT),
        name="mlstm_seq",
    )(qm, km, vm, og, gc, gr, mhn, *side_w)


def _post_kernel(x_ref, o0_ref, o1_ref, o2_ref, l0_ref, l1_ref, l2_ref, ob_ref, ga_ref, gb_ref, p_ref,
                 wpa_ref, wpb_ref, wout_ref, nffn_ref, wg_ref, wu_ref, wd_ref, nple_ref, wpg_ref, wpp_ref,
                 nfin_ref, y_ref, *ubuf, dils):
    lead = (0,) * (len(x_ref.shape) - 2)
    rd = lambda r: r[lead] if lead else r[...]
    tm = x_ref.shape[-2]

    def undilate(ref, dil, slot):
        if dil == 1:
            return rd(ref)
        halves = range(GROUP_W // LANES)
        for r in range(dil):
            for c in halves:
                lo = r * GROUP_W + c * LANES
                ubuf[0][slot, c, pl.ds(r, tm // dil, stride=dil), :] = ref[0, :, lo:lo + LANES]
        return jnp.concatenate([ubuf[0][slot, c] for c in halves], axis=1)

    l0, l1, l2 = (undilate(r, d, 2 * g) for g, (r, d) in enumerate(zip((l0_ref, l1_ref, l2_ref), dils)))
    o0, o1, o2 = (undilate(r, d, 2 * g + 1) for g, (r, d) in enumerate(zip((o0_ref, o1_ref, o2_ref), dils)))
    lmax = jnp.maximum(jnp.maximum(l0, l1), l2)
    e0, e1, e2 = jnp.exp(l0 - lmax), jnp.exp(l1 - lmax), jnp.exp(l2 - lmax)
    o_a = (e0 * o0 + e1 * o1 + e2 * o2) / (e0 + e1 + e2)

    mix = (rd(ga_ref).astype(F32) * _dot(o_a.astype(BF16), wpa_ref[...])
           + rd(gb_ref).astype(F32) * _dot(rd(ob_ref).astype(BF16), wpb_ref[...]))
    h = rd(x_ref) + _dot(mix.astype(BF16), wout_ref[...])

    xf = _rms(h, nffn_ref[...]).astype(BF16)
    for lo, hi in FFN_CHUNKS:
        cs = slice(lo, hi)
        gate = _dot(xf, wg_ref[:, cs])
        act = _silu_h(gate) * _dot(xf, wu_ref[:, cs])
        h = h + _dot(act.astype(BF16), wd_ref[cs, :])

    xp = _rms(h, nple_ref[...]).astype(BF16)
    h = h + _sigmoid_h(_dot(xp, wpg_ref[...])) * _dot(rd(p_ref).astype(BF16), wpp_ref[...])
    y = _rms(h, nfin_ref[...])
    if lead:
        y_ref[lead] = y
    else:
        y_ref[...] = y


def _post_step(x, os_, ls_, ob, ga, gb, p, wts):
    return pl.pallas_call(
        functools.partial(_post_kernel, dils=(1,) * N_GROUPS),
        out_shape=jax.ShapeDtypeStruct(x.shape, F32),
        compiler_params=pltpu.CompilerParams(vmem_limit_bytes=VMEM_LIMIT),
        name="post_step",
    )(x, *os_, *ls_, ob, ga, gb, p, *wts)


SM_ROWS = 16
SM_LANES = 512
T_QA, T_KN, T_VN, T_QM, T_KM, T_VM, T_OG, T_N, T_GATE = range(9)
SO_ROWS = 8
SO_OB, SO_N, SO_M, SO_LSE, SO_O = range(5)
HALF_HEADS = HPG // 2
HALF_W = HALF_HEADS * HEAD_DIM


def _eye(n):
    return lax.broadcasted_iota(jnp.int32, (n, n), 0) == lax.broadcasted_iota(jnp.int32, (n, n), 1)


def _col_of(row, eye):
    return jnp.sum(jnp.where(eye, jnp.broadcast_to(row, eye.shape), 0.0), axis=1, keepdims=True)


def _row_of(col, eye):
    return jnp.sum(jnp.where(eye, jnp.broadcast_to(col, eye.shape), 0.0), axis=0, keepdims=True)


def _round_bf16(x):
    return x.astype(BF16).astype(F32)


def _sample_step(tile, kv_refs, cst_ref, mhn_ref, so_ref, kvn_refs, cso_ref):
    so_ref[0] = jnp.zeros((SO_ROWS, SM_LANES), F32)
    eye_a = _eye(HALF_W)
    eye_m = _eye(M_DK)

    for g, (w, dil) in enumerate(DSWA_GROUPS):
        ls = slice(g * HALF_W, (g + 1) * HALF_W)
        k_t = kv_refs[g][0, 0, 0]
        v_t = kv_refs[g][0, 1, 0]
        k_row = tile[T_KN:T_KN + 1, ls]
        v_row = tile[T_VN:T_VN + 1, ls]
        qb = _round_bf16(tile[T_QA:T_QA + 1, ls])
        prod = k_t * _col_of(qb, eye_a)
        qk_new = qb * _round_bf16(k_row)
        pos = lax.broadcasted_iota(jnp.int32, (1, w), 1)
        pe, pn, lse = [], [], []
        for hl in range(HALF_HEADS):
            hd = slice(hl * HEAD_DIM, (hl + 1) * HEAD_DIM)
            s = jnp.where(pos % dil == 0, jnp.sum(prod[hd, :], axis=0, keepdims=True), NEG)
            s_new = jnp.sum(qk_new[:, hd], axis=-1, keepdims=True)
            mx = jnp.maximum(jnp.max(s, axis=-1, keepdims=True), s_new)
            e = jnp.exp(s - mx)
            e_new = jnp.exp(s_new - mx)
            den = jnp.sum(e, axis=-1, keepdims=True) + e_new
            pe.append(jnp.broadcast_to(e / den, (HEAD_DIM, w)))
            pn.append(jnp.broadcast_to(e_new / den, (1, HEAD_DIM)))
            lse.append(jnp.broadcast_to(mx + jnp.log(den), (1, HEAD_DIM)))
        o_col = jnp.sum(v_t * jnp.concatenate(pe, axis=0), axis=-1, keepdims=True)
        so_ref[0, SO_O:SO_O + 1, ls] = _row_of(o_col, eye_a) + jnp.concatenate(pn, axis=1) * _round_bf16(v_row)
        so_ref[0, SO_LSE:SO_LSE + 1, ls] = jnp.concatenate(lse, axis=1)
        last = lax.broadcasted_iota(jnp.int32, (HALF_W, w), 1) == w - 1
        kvn_refs[g][0, 0, 0] = jnp.where(last, _col_of(k_row, eye_a), pltpu.roll(k_t, w - 1, 1))
        kvn_refs[g][0, 1, 0] = jnp.where(last, _col_of(v_row, eye_a), pltpu.roll(v_t, w - 1, 1))

    eye = eye_m
    for j in range(M_HEADS // 2):
        hs = slice(j * M_DK, (j + 1) * M_DK)
        ig = tile[T_GATE:T_GATE + 1, j:j + 1]
        lf = tile[T_GATE:T_GATE + 1, 2 + j:3 + j]
        m_prev = tile[T_GATE:T_GATE + 1, 4 + j:5 + j]
        qb = _round_bf16(tile[T_QM:T_QM + 1, hs])
        kb = _round_bf16(tile[T_KM:T_KM + 1, hs])
        vb = _round_bf16(tile[T_VM:T_VM + 1, hs])
        n_prev = tile[T_N:T_N + 1, hs]
        c_prev = cst_ref[0, j]

        m_inter = lf + m_prev
        m_t = jnp.maximum(m_inter, ig)
        w_intra = jnp.sum(qb * kb, axis=-1, keepdims=True) * jnp.exp(ig - m_t)
        w_inter = jnp.exp(m_inter - m_t)
        q_c = jnp.sum(c_prev * _col_of(qb, eye), axis=0, keepdims=True)
        num = w_intra * vb + w_inter * q_c
        den = w_intra + w_inter * jnp.sum(qb * n_prev, axis=-1, keepdims=True)
        hh = num / jnp.maximum(jnp.abs(den), jnp.exp(-m_t))
        hn = hh * lax.rsqrt(jnp.mean(hh * hh, axis=-1, keepdims=True) + EPS) * mhn_ref[0, j:j + 1, :]
        ks = kb * jnp.exp(ig - m_t)
        so_ref[0, SO_OB:SO_OB + 1, hs] = hn * tile[T_OG:T_OG + 1, hs]
        so_ref[0, SO_N:SO_N + 1, hs] = w_inter * n_prev + ks
        so_ref[0, SO_M:SO_M + 1, hs] = jnp.broadcast_to(m_t, (1, M_DK))
        cso_ref[0, j] = w_inter * c_prev + _col_of(_round_bf16(ks), eye) * vb


N_POST_IN = 22


def _post_fused_kernel(*refs, dils, ns):
    post_in = refs[:N_POST_IN]
    sm_ref, c0_ref, c1_ref, c2_ref, cst_ref, mhn_ref = refs[N_POST_IN:N_POST_IN + 6]
    y_ref, so_ref, cn0_ref, cn1_ref, cn2_ref, cso_ref = refs[N_POST_IN + 6:N_POST_IN + 12]
    (ubuf,) = refs[N_POST_IN + 12:]
    _post_kernel(*post_in, y_ref, ubuf, dils=dils)
    _sample_step(sm_ref[0], (c0_ref, c1_ref, c2_ref), cst_ref, mhn_ref, so_ref,
                 (cn0_ref, cn1_ref, cn2_ref), cso_ref)


def _post_fused(x, os_, ls_, ob, ga, gb, p, wts, sm_in, caches_t, c_state, mhn):
    B, S, _ = x.shape
    NB = c_state.shape[0]
    tm = TM_POST
    ns = S // tm
    assert B * ns == 2 * NB, "one half of a sample batch element per post-mixer grid step"
    tok = lambda w: pl.BlockSpec((1, tm, w), lambda b, i: (b, i, 0))
    dils = tuple(dil for _, dil in DSWA_GROUPS)
    dil_specs = [pl.BlockSpec((1, tm // dil, dil * GROUP_W), lambda b, i: (b, i, 0)) for dil in dils]
    step = lambda b, i: b * ns + i
    cache_specs = [pl.BlockSpec((1, 2, 1, HALF_W, c.shape[-1]), lambda b, i: (step(b, i) // 2, 0, step(b, i) % 2, 0, 0))
                   for c in caches_t]
    cst_spec = pl.BlockSpec((1, M_HEADS // 2, M_DK, M_DK), lambda b, i: (step(b, i) // 2, step(b, i) % 2, 0, 0))
    in_specs = ([tok(D_MODEL)] + dil_specs * 2 + [tok(M_INNER)] * 3 + [tok(PLE_DIM)]
                + [_const_spec(w.shape) for w in wts]
                + [pl.BlockSpec((1, SM_ROWS, SM_LANES), lambda b, i: (step(b, i), 0, 0))]
                + cache_specs
                + [cst_spec, pl.BlockSpec((1, M_HEADS // 2, M_DK), lambda b, i: (step(b, i) % 2, 0, 0))])
    out_specs = ([tok(D_MODEL), pl.BlockSpec((1, SO_ROWS, SM_LANES), lambda b, i: (step(b, i), 0, 0))]
                 + cache_specs + [cst_spec])
    out_shape = ([jax.ShapeDtypeStruct((B, S, D_MODEL), F32),
                  jax.ShapeDtypeStruct((2 * NB, SO_ROWS, SM_LANES), F32)]
                 + [jax.ShapeDtypeStruct(c.shape, F32) for c in caches_t]
                 + [jax.ShapeDtypeStruct(c_state.shape, F32)])
    return pl.pallas_call(
        functools.partial(_post_fused_kernel, dils=dils, ns=ns),
        grid=(B, ns),
        in_specs=in_specs,
        out_specs=out_specs,
        out_shape=out_shape,
        scratch_shapes=[pltpu.VMEM((2 * N_GROUPS, GROUP_W // LANES, tm, LANES), F32)],
        compiler_params=pltpu.CompilerParams(
            dimension_semantics=("arbitrary", "arbitrary"), vmem_limit_bytes=VMEM_LIMIT),
        name="post_seq_sample_stream",
    )(x, *os_, *ls_, ob, ga, gb, p, *wts, sm_in, *caches_t, c_state, mhn)


def _rope_tables(pos):
    pos = np.asarray(pos, np.float64)
    t = pos.shape[0]
    inv = np.power(ROPE_THETA, -np.arange(ROPE_HALF, dtype=np.float64) / ROPE_HALF)
    ang = pos[:, None] * inv[None, :]
    cos, sin = np.cos(ang), np.sin(ang)
    one = np.ones((t, HEAD_DIM - ROPE_DIM))
    zero = np.zeros((t, HEAD_DIM - ROPE_DIM))
    z8 = np.zeros((t, ROPE_HALF))
    a = np.concatenate([cos, cos, one], axis=1)
    m = np.concatenate([-sin, z8, zero], axis=1)
    p = np.concatenate([z8, sin, zero], axis=1)
    rep = LANES // HEAD_DIM
    return tuple(jnp.asarray(np.tile(v, (1, rep)).astype(np.float32)) for v in (a, m, p))


def kernel(x_prompt, x_sample, cache_kv_w128, cache_kv_w512, cache_kv_w2048, state_conv, state_C, state_n,
           state_m, p_prompt, p_sample, norm_mix, w_in, conv_w, conv_b, b_igate, b_fgate, mh_norm, w_proj_a,
           w_proj_b, w_out, norm_ffn, w_gate, w_up, w_down, norm_ple, w_ple_gate, w_ple_proj, norm_final):
    depth = w_in.shape[0]
    assert depth == 1, "single trunk layer"
    B, S, _ = x_prompt.shape
    NB = x_sample.shape[0]
    assert x_sample.shape[1] == 1
    caches = (cache_kv_w128, cache_kv_w512, cache_kv_w2048)
    assert S >= DSWA_GROUPS[-1][0] and S % (BAND * DSWA_GROUPS[-1][1]) == 0
    past = 0
    l = 0

    row_scale = np.ones((IN_COLS, 1), np.float32)
    row_scale[OFF_MO:OFF_MI] = 0.5
    row_scale[OFF_GA:IN_COLS] = 0.5
    wt = (w_in[l].T * row_scale).astype(BF16)
    gate_wt = wt[OFF_MI:OFF_GA]
    gate_b = jnp.concatenate([b_igate[l], b_fgate[l]])
    in_wts = (
        norm_mix[l][None, :],
        wt[OFF_AQ:OFF_MQ],
        wt[OFF_MQ:OFF_MV],
        wt[OFF_MV:OFF_MO],
        wt[OFF_MO:OFF_MI],
        wt[OFF_GA:OFF_GB],
        wt[OFF_GB:IN_COLS],
        jnp.pad(gate_wt, ((0, LANES - 2 * M_HEADS), (0, 0))),
        gate_wt,
        jnp.pad(gate_b, (0, LANES - 2 * M_HEADS))[None, :],
        gate_b[:, None],
        0.5 * conv_w[l],
        0.5 * conv_b[l][None, :],
    )
    mhn = mh_norm[l]

    c0 = jnp.zeros((B, SUBLANES, 2 * M_INNER), F32)
    tabs_p = _rope_tables(past + np.arange(S))
    (*qkv, qm, km, vm, og, ga, gb, gc, gr, kv0, kv1, kv2, cn) = _inproj_seq(x_prompt, c0, tabs_p, in_wts)
    os_p, ls_p, ffn_bf16 = [], [], []
    ffn_w = ((w_gate[l], 0.5), (w_up[l], 1.0), (w_down[l], 1.0))
    for g, ((_, dil), (w_ffn, w_scale)) in enumerate(zip(DSWA_GROUPS, ffn_w)):
        o, lse, wb = _attn_seq(*qkv[3 * g:3 * g + 3], dil, w_ffn, w_scale)
        os_p.append(o)
        ls_p.append(lse)
        ffn_bf16.append(wb)
    ob, c_p, n_p, m_p, wpa, wpb, wo, wpg, wpp = _mlstm_seq(
        qm, km, vm, og, gc, gr, mhn, (w_proj_a[l], w_proj_b[l], w_out[l], w_ple_gate[l], w_ple_proj[l]),
        (1.0, 1.0, 1.0, 0.5, 1.0))
    post_wts = (wpa, wpb, wo, norm_ffn[l][None, :], *ffn_bf16, norm_ple[l][None, :], wpg, wpp, norm_final[None, :])

    tabs_s = _rope_tables(np.full((NB,), PAST_LEN))
    conv_s = jnp.swapaxes(state_conv[l], 0, 1)
    (qa_s, kvn0, kvn1, kvn2, qm_s, km_s, vm_s, og_s, ga_s, gb_s, gc_s, cn_s) = _inproj_step(
        x_sample[:, 0, :], conv_s, tabs_s, in_wts)
    pad_row = lambda t: jnp.pad(t, ((0, 0), (0, SM_LANES - t.shape[1])))
    att_half = lambda t: pad_row(t.reshape(NB, N_GROUPS, 2, HALF_W).transpose(0, 2, 1, 3).reshape(2 * NB, -1))
    mem_half = lambda t: t.reshape(2 * NB, SM_LANES)
    k_new = jnp.concatenate([kv[:, :GROUP_W] for kv in (kvn0, kvn1, kvn2)], axis=1)
    v_new = jnp.concatenate([kv[:, GROUP_W:] for kv in (kvn0, kvn1, kvn2)], axis=1)
    gates = jnp.concatenate([t.reshape(2 * NB, M_HEADS // 2) for t in
                             (gc_s[:, :M_HEADS], gc_s[:, M_HEADS:2 * M_HEADS], state_m[l])], axis=1)
    sm_rows = [att_half(qa_s), att_half(k_new), att_half(v_new), mem_half(qm_s), mem_half(km_s), mem_half(vm_s),
               mem_half(og_s), mem_half(state_n[l].reshape(NB, M_INNER)), pad_row(gates)]
    sm_rows += [jnp.zeros((2 * NB, SM_LANES), F32)] * (SM_ROWS - len(sm_rows))
    sm_in = jnp.stack(sm_rows, axis=1)
    caches_t = [jnp.transpose(c[l], (0, 2, 3, 4, 1)).reshape(NB, 2, 2, HALF_W, c.shape[2]) for c in caches]
    for c, (w, dil) in zip(caches_t, DSWA_GROUPS):
        assert c.shape[-1] == w == BAND * dil, "each cache must hold exactly one full window"
    y_prompt, so, cn0, cn1, cn2, c_s = _post_fused(x_prompt, os_p, ls_p, ob, ga, gb, p_prompt[l], post_wts,
                                                   sm_in, caches_t, state_C[l], mhn.reshape(2, M_HEADS // 2, M_DK))
    kv_s = [jnp.transpose(c.reshape(NB, 2, HPG, HEAD_DIM, -1), (0, 4, 1, 2, 3)) for c in (cn0, cn1, cn2)]
    att_full = lambda t: t[:, :N_GROUPS * HALF_W].reshape(NB, 2, N_GROUPS, HALF_W).transpose(0, 2, 1, 3).reshape(
        NB, N_GROUPS, GROUP_W)
    o_s, l_s = att_full(so[:, SO_O, :]), att_full(so[:, SO_LSE, :])
    os_s = [o_s[:, g, :] for g in range(N_GROUPS)]
    ls_s = [l_s[:, g, :] for g in range(N_GROUPS)]
    ob_s = so[:, SO_OB, :].reshape(NB, M_INNER)
    n_s = so[:, SO_N, :].reshape(NB, M_HEADS, M_DK)
    m_s = so[:, SO_M, ::M_DK].reshape(NB, M_HEADS)
    y_sample = _post_step(x_sample[:, 0, :], os_s, ls_s, ob_s, ga_s, gb_s, p_sample[l][:, 0, :], post_wts)

    kv_shape = lambda n, w: (1, n, w, 2, HPG, HEAD_DIM)
    return (
        y_prompt, y_sample[:, None, :],
        kv0.reshape(kv_shape(B, DSWA_GROUPS[0][0])), kv_s[0].reshape(kv_shape(NB, DSWA_GROUPS[0][0])),
        kv1.reshape(kv_shape(B, DSWA_GROUPS[1][0])), kv_s[1].reshape(kv_shape(NB, DSWA_GROUPS[1][0])),
        kv2.reshape(kv_shape(B, DSWA_GROUPS[2][0])), kv_s[2].reshape(kv_shape(NB, DSWA_GROUPS[2][0])),
        cn[:, SUBLANES - (CONV_W - 1):, :][None], jnp.swapaxes(cn_s, 0, 1)[None],
        c_p[None], c_s[None],
        n_p[None], n_s[None],
        m_p[:, 0, :M_HEADS][None], m_s[None],
    )
```
